```python
import math
import jax, jax.numpy as jnp
from jax import lax
import numpy as np

D_MODEL = 1024
BATCH = 4
SEQ = 4096
DEPTH = 1

FOURIER_WIDTH = D_MODEL // 2
FOURIER_GROUPS = 4
FOURIER_GROUP_CH = FOURIER_WIDTH // FOURIER_GROUPS
SSM_WIDTH = D_MODEL // 2
SSM_GROUP_CH = 16
SSM_GROUPS = SSM_WIDTH // SSM_GROUP_CH
SSM_STATE = 64
DT_MIN = 1e-3
DT_MAX = 1e-1
N_BRANCHES = 2
IN_WIDTH = FOURIER_WIDTH + SSM_WIDTH + N_BRANCHES * D_MODEL
MOE_GROUPS = 8
EXPERTS_PER_GROUP = 8
N_EXPERTS = MOE_GROUPS * EXPERTS_PER_GROUP
MOE_TOP_K = 2
D_EXPERT = D_MODEL // 2
MOE_BLOCK = 128
RMS_EPS = 1e-6

kernel_name = "hybrid_fnet_s5_hiermoe_encoder_block"


def _rmsnorm(x, g):
    xf = x.astype(jnp.float32)
    inv = lax.rsqrt(jnp.mean(xf * xf, axis=-1, keepdims=True) + RMS_EPS)
    return (xf * inv * g.astype(jnp.float32)).astype(x.dtype)


def _fourier_mix(u):
    b, l, _ = u.shape
    ug = u.astype(jnp.float32).reshape(b, l, FOURIER_GROUPS, FOURIER_GROUP_CH)
    f = jnp.fft.fft2(ug, axes=(1, 3), norm='ortho').real
    return f.reshape(b, l, FOURIER_WIDTH).astype(u.dtype)


def _scan_combine(e1, e2):
    a1, b1 = e1
    a2, b2 = e2
    return a1 * a2, a2 * b1 + b2


def _ssm_direction(ug, a_re, a_im, log_dt, b_re, b_im, c_re, c_im, reverse):
    lam = lax.complex(a_re.astype(jnp.float32), a_im.astype(jnp.float32))
    dt = jnp.exp(log_dt.astype(jnp.float32))[:, None]
    a_bar = jnp.exp(lam * dt)
    b_mat = lax.complex(b_re.astype(jnp.float32), b_im.astype(jnp.float32))
    b_bar = ((a_bar - 1.0) / lam)[..., None] * b_mat
    bu = jnp.einsum('blgc,gnc->blgn', ug.astype(jnp.complex64), b_bar)
    a = jnp.broadcast_to(a_bar, bu.shape)
    _, h = lax.associative_scan(_scan_combine, (a, bu), axis=1, reverse=reverse)
    c_mat = lax.complex(c_re.astype(jnp.float32), c_im.astype(jnp.float32))
    return jnp.einsum('blgn,gcn->blgc', h, c_mat).real


def _ssm_mix(u, a_re, a_im, log_dt, b_re, b_im, c_re, c_im, d_skip, w_glu):
    b, l, _ = u.shape
    uf = u.astype(jnp.float32)
    ug = uf.reshape(b, l, SSM_GROUPS, SSM_GROUP_CH)
    y = (_ssm_direction(ug, a_re[0], a_im[0], log_dt[0], b_re[0], b_im[0], c_re[0], c_im[0], False)
         + _ssm_direction(ug, a_re[1], a_im[1], log_dt[1], b_re[1], b_im[1], c_re[1], c_im[1], True))
    y = y.reshape(b, l, SSM_WIDTH) + d_skip.astype(jnp.float32) * uf
    y = jax.nn.gelu(y)
    y = y * jax.nn.sigmoid(y @ w_glu.astype(jnp.float32))
    return y.astype(u.dtype)


def _hier_moe(h, wg, bg, we, be, w_gate, w_up, w_down):
    bsz, l, d = h.shape
    t = bsz * l
    ht = h.reshape(t, d)
    g_prob = jax.nn.softmax((ht @ wg).astype(jnp.float32) + bg.astype(jnp.float32), axis=-1)
    p_g, g_idx = lax.top_k(g_prob, 1)
    e_logits = ((ht @ we).astype(jnp.float32) + be.astype(jnp.float32)).reshape(t, MOE_GROUPS, EXPERTS_PER_GROUP)
    e_sel = jnp.take_along_axis(e_logits, g_idx[:, :, None], axis=1)[:, 0]
    top_l, top_i = lax.top_k(e_sel, MOE_TOP_K)
    gate_w = p_g * jax.nn.softmax(top_l, axis=-1)
    expert = g_idx * EXPERTS_PER_GROUP + top_i

    n_assign = t * MOE_TOP_K
    e_flat = expert.reshape(n_assign)
    w_flat = gate_w.reshape(n_assign)
    tok_flat = jnp.repeat(jnp.arange(t, dtype=jnp.int32), MOE_TOP_K)
    order = jnp.argsort(e_flat)
    e_s = e_flat[order]
    tok_s = tok_flat[order]
    w_s = w_flat[order]
    counts = jnp.bincount(e_flat, length=N_EXPERTS)
    starts = jnp.cumsum(counts) - counts
    padded = ((counts + MOE_BLOCK - 1) // MOE_BLOCK) * MOE_BLOCK
    pstarts = jnp.cumsum(padded) - padded
    pends = pstarts + padded
    dest = pstarts[e_s] + jnp.arange(n_assign, dtype=jnp.int32) - starts[e_s]
    n_blocks = -(-n_assign // MOE_BLOCK) + N_EXPERTS
    rows = n_blocks * MOE_BLOCK
    x_pad = jnp.zeros((rows, d), h.dtype).at[dest].set(ht[tok_s])
    block_start = jnp.arange(n_blocks, dtype=jnp.int32) * MOE_BLOCK
    block_e = jnp.minimum(jnp.searchsorted(pends, block_start, side='right'), N_EXPERTS - 1)

    def expert_block(args):
        xb, e = args
        a = jax.nn.silu(xb @ w_gate[e]) * (xb @ w_up[e])
        return a @ w_down[e]

    y_pad = lax.map(expert_block, (x_pad.reshape(n_blocks, MOE_BLOCK, d), block_e)).reshape(rows, d)
    y = y_pad[dest] * w_s[:, None].astype(h.dtype)
    out = jax.ops.segment_sum(y, tok_s, num_segments=t)
    return out.reshape(bsz, l, d)


def setup_inputs(seed: int = 0) -> dict:
    key = jax.random.key(seed)
    ks = jax.random.split(key, 24)
    f32 = jnp.float32

    def nrm(k, shape, scale):
        return jax.random.normal(k, shape, f32) * scale

    S, F, D = SSM_WIDTH, FOURIER_WIDTH, D_MODEL
    G, N, C = SSM_GROUPS, SSM_STATE, SSM_GROUP_CH
    n_idx = jnp.arange(N, dtype=f32)
    return {
        'x': nrm(ks[0], (BATCH, SEQ, D), 1.0),
        'mix_norm_g': 1.0 + nrm(ks[1], (DEPTH, D), 0.02),
        'w_in': nrm(ks[2], (DEPTH, D, IN_WIDTH), D ** -0.5),
        'w_fourier_out': nrm(ks[3], (DEPTH, F, D), F ** -0.5),
        'ssm_A_re': -0.5 + nrm(ks[4], (DEPTH, 2, G, N), 0.01),
        'ssm_A_im': math.pi * n_idx + nrm(ks[5], (DEPTH, 2, G, N), 0.01),
        'ssm_log_dt': jax.random.uniform(ks[6], (DEPTH, 2, G), f32, math.log(DT_MIN), math.log(DT_MAX)),
        'ssm_B_re': nrm(ks[7], (DEPTH, 2, G, N, C), (2 * C) ** -0.5),
        'ssm_B_im': nrm(ks[8], (DEPTH, 2, G, N, C), (2 * C) ** -0.5),
        'ssm_C_re': nrm(ks[9], (DEPTH, 2, G, C, N), N ** -0.5),
        'ssm_C_im': nrm(ks[10], (DEPTH, 2, G, C, N), N ** -0.5),
        'ssm_D': nrm(ks[11], (DEPTH, S), 1.0),
        'ssm_w_glu': nrm(ks[12], (DEPTH, S, S), S ** -0.5),
        'w_ssm_out': nrm(ks[13], (DEPTH, S, D), S ** -0.5),
        'w_out': nrm(ks[14], (DEPTH, D, D), D ** -0.5),
        'ffn_norm_g': 1.0 + nrm(ks[15], (DEPTH, D), 0.02),
        'router_group_w': nrm(ks[16], (DEPTH, D, MOE_GROUPS), D ** -0.5),
        'router_group_b': nrm(ks[17], (DEPTH, MOE_GROUPS), 0.01),
        'router_expert_w': nrm(ks[18], (DEPTH, D, N_EXPERTS), D ** -0.5),
        'router_expert_b': nrm(ks[19], (DEPTH, N_EXPERTS), 0.01),
        'expert_w_gate': nrm(ks[20], (DEPTH, N_EXPERTS, D, D_EXPERT), D ** -0.5),
        'expert_w_up': nrm(ks[21], (DEPTH, N_EXPERTS, D, D_EXPERT), D ** -0.5),
        'expert_w_down': nrm(ks[22], (DEPTH, N_EXPERTS, D_EXPERT, D), D_EXPERT ** -0.5),
        'final_norm_g': 1.0 + nrm(ks[23], (D,), 0.02),
    }


def reference(x, mix_norm_g, w_in, w_fourier_out, ssm_A_re, ssm_A_im, ssm_log_dt,
              ssm_B_re, ssm_B_im, ssm_C_re, ssm_C_im, ssm_D, ssm_w_glu, w_ssm_out,
              w_out, ffn_norm_g, router_group_w, router_group_b, router_expert_w,
              router_expert_b, expert_w_gate, expert_w_up, expert_w_down, final_norm_g):
    for i in range(DEPTH):
        h = _rmsnorm(x, mix_norm_g[i])
        z = h @ w_in[i]
        u_f = z[..., :FOURIER_WIDTH]
        u_s = z[..., FOURIER_WIDTH:FOURIER_WIDTH + SSM_WIDTH]
        gates = jax.nn.sigmoid(z[..., FOURIER_WIDTH + SSM_WIDTH:].astype(jnp.float32)).astype(x.dtype)
        y_f = _fourier_mix(u_f) @ w_fourier_out[i]
        y_s = _ssm_mix(u_s, ssm_A_re[i], ssm_A_im[i], ssm_log_dt[i], ssm_B_re[i], ssm_B_im[i],
                       ssm_C_re[i], ssm_C_im[i], ssm_D[i], ssm_w_glu[i]) @ w_ssm_out[i]
        merged = gates[..., :D_MODEL] * y_f + gates[..., D_MODEL:] * y_s
        x = x + merged @ w_out[i]
        hn = _rmsnorm(x, ffn_norm_g[i])
        x = x + _hier_moe(hn, router_group_w[i], router_group_b[i], router_expert_w[i],
                          router_expert_b[i], expert_w_gate[i], expert_w_up[i], expert_w_down[i])
    return _rmsnorm(x, final_norm_g)
```

```python
import functools
import math

import numpy as np
import jax
import jax.numpy as jnp
from jax import lax
from jax.experimental import pallas as pl
from jax.experimental.pallas import tpu as pltpu

F32 = jnp.float32
BF16 = jnp.bfloat16

D_MODEL = 1024
BATCH = 4
SEQ = 4096
TOKENS = BATCH * SEQ
FOURIER_WIDTH = 512
FOURIER_GROUP_CH = 128
FOURIER_GROUPS = 4
SSM_WIDTH = 512
SSM_GROUP_CH = 16
SSM_GROUPS = 32
SSM_STATE = 64
MOE_GROUPS = 8
EXPERTS_PER_GROUP = 8
N_EXPERTS = 64
MOE_TOP_K = 2
D_EXPERT = 512
RMS_EPS = 1e-6

LANES = 128
DFT_RADIX = 64
SSM_CHUNK = 16
SSM_LANE_BLOCKS = SSM_WIDTH // LANES
GROUPS_PER_BLOCK = LANES // SSM_GROUP_CH
CHUNK_COLS = SSM_CHUNK * LANES
N_CHUNKS = SEQ // SSM_CHUNK
STATE_COLS = GROUPS_PER_BLOCK * SSM_STATE
MOE_ROWS = 256
MOE_BLOCKS = TOKENS * MOE_TOP_K // MOE_ROWS + N_EXPERTS
ROUTER_COLS = 128
VMEM_LIMIT = 48 * 1024 * 1024


def _cparams(sem, vmem=VMEM_LIMIT):
    return pltpu.CompilerParams(dimension_semantics=sem, vmem_limit_bytes=vmem)


IN_TM = 512


def _inproj_kernel(x_ref, g_ref, w_ref, cdft_ref, vf_ref, us_ref, gates_ref):
    x = x_ref[0]
    inv = lax.rsqrt(jnp.mean(x * x, axis=-1, keepdims=True) + RMS_EPS)
    h = (x * inv * g_ref[...]).astype(BF16)
    zf = jnp.dot(h, w_ref[:, 0:FOURIER_WIDTH], preferred_element_type=F32).astype(BF16)
    for g in range(FOURIER_GROUPS):
        sl = slice(g * LANES, (g + 1) * LANES)
        v = jnp.dot(zf[:, sl], cdft_ref[...], preferred_element_type=F32)
        vf_ref[0, 0, :, sl] = v[:, :LANES].astype(BF16)
        vf_ref[0, 1, :, sl] = v[:, LANES:].astype(BF16)
    zs = jnp.dot(h, w_ref[:, FOURIER_WIDTH:FOURIER_WIDTH + SSM_WIDTH], preferred_element_type=F32)
    for j in range(SSM_LANE_BLOCKS):
        us_ref[j] = zs[:, j * LANES:(j + 1) * LANES].astype(BF16)
    base = FOURIER_WIDTH + SSM_WIDTH
    for n in range(4):
        zg = jnp.dot(h, w_ref[:, base + n * 512: base + (n + 1) * 512], preferred_element_type=F32)
        gates_ref[:, n * 512:(n + 1) * 512] = jax.nn.sigmoid(zg).astype(BF16)


def _inproj(x, g, w_in, cdft):
    nt = SEQ // IN_TM
    return pl.pallas_call(
        _inproj_kernel,
        grid=(BATCH, nt),
        in_specs=[
            pl.BlockSpec((1, IN_TM, D_MODEL), lambda b, i: (b, i, 0)),
            pl.BlockSpec((1, D_MODEL), lambda b, i: (0, 0)),
            pl.BlockSpec(w_in.shape, lambda b, i: (0, 0)),
            pl.BlockSpec(cdft.shape, lambda b, i: (0, 0)),
        ],
        out_specs=[
            pl.BlockSpec((1, 2, IN_TM, FOURIER_WIDTH), lambda b, i: (b, 0, i, 0)),
            pl.BlockSpec((SSM_LANE_BLOCKS, IN_TM, LANES), lambda b, i: (0, b * nt + i, 0)),
            pl.BlockSpec((IN_TM, 2 * D_MODEL), lambda b, i: (b * nt + i, 0)),
        ],
        out_shape=[
            jax.ShapeDtypeStruct((BATCH, 2, SEQ, FOURIER_WIDTH), BF16),
            jax.ShapeDtypeStruct((SSM_LANE_BLOCKS, TOKENS, LANES), BF16),
            jax.ShapeDtypeStruct((TOKENS, 2 * D_MODEL), BF16),
        ],
        compiler_params=_cparams(("parallel", "parallel")),
        name="inproj",
    )(x, g, w_in, cdft)


DFT_S2_PER_STEP = 8


def _dft1_kernel(v_ref, f_ref, twr_ref, twi_ref, a_ref):
    for s in range(DFT_S2_PER_STEP):
        blk = v_ref[0, :, s * FOURIER_WIDTH:(s + 1) * FOURIER_WIDTH]
        r = jnp.dot(f_ref[...], blk, preferred_element_type=F32)
        ar, ai = r[:DFT_RADIX], r[DFT_RADIX:]
        tr = jnp.tile(twr_ref[s], (1, FOURIER_WIDTH // LANES))
        ti = jnp.tile(twi_ref[s], (1, FOURIER_WIDTH // LANES))
        a_ref[0, 0, s] = (ar * tr - ai * ti).astype(BF16)
        a_ref[0, 1, s] = (ar * ti + ai * tr).astype(BF16)


def _dft1(v, f1, twr, twi):
    ns = DFT_RADIX // DFT_S2_PER_STEP
    cols = DFT_S2_PER_STEP * FOURIER_WIDTH
    return pl.pallas_call(
        _dft1_kernel,
        grid=(BATCH, ns),
        in_specs=[
            pl.BlockSpec((1, 2 * DFT_RADIX, cols), lambda b, s: (b, 0, s)),
            pl.BlockSpec(f1.shape, lambda b, s: (0, 0)),
            pl.BlockSpec((DFT_S2_PER_STEP, DFT_RADIX, LANES), lambda b, s: (s, 0, 0)),
            pl.BlockSpec((DFT_S2_PER_STEP, DFT_RADIX, LANES), lambda b, s: (s, 0, 0)),
        ],
        out_specs=pl.BlockSpec((1, 2, DFT_S2_PER_STEP, DFT_RADIX, FOURIER_WIDTH),
                               lambda b, s: (b, 0, s, 0, 0)),
        out_shape=jax.ShapeDtypeStruct((BATCH, 2, DFT_RADIX, DFT_RADIX, FOURIER_WIDTH), BF16),
        compiler_params=_cparams(("parallel", "parallel")),
        name="dft1",
    )(v, f1, twr, twi)


DFT2_COLS = 4096


def _dft2_kernel(a_ref, f_ref, o_ref):
    o_ref[0] = jnp.dot(f_ref[...], a_ref[0], preferred_element_type=F32).astype(BF16)


def _dft2(a, f2):
    n = DFT_RADIX * FOURIER_WIDTH
    return pl.pallas_call(
        _dft2_kernel,
        grid=(BATCH, n // DFT2_COLS),
        in_specs=[
            pl.BlockSpec((1, 2 * DFT_RADIX, DFT2_COLS), lambda b, s: (b, 0, s)),
            pl.BlockSpec(f2.shape, lambda b, s: (0, 0)),
        ],
        out_specs=pl.BlockSpec((1, DFT_RADIX, DFT2_COLS), lambda b, s: (b, 0, s)),
        out_shape=jax.ShapeDtypeStruct((BATCH, DFT_RADIX, n), BF16),
        compiler_params=_cparams(("parallel", "parallel")),
        name="dft2",
    )(a, f2)


def _dft_constants():
    r = DFT_RADIX
    k = np.arange(r)
    ang = 2.0 * np.pi * np.outer(k, k) / r
    c, s = np.cos(ang), np.sin(ang)
    f1 = np.block([[c, s], [-s, c]])
    f2 = np.concatenate([c, s], axis=1)
    tw = 2.0 * np.pi * np.outer(k, k) / SEQ
    scale = 1.0 / math.sqrt(SEQ)
    twr = np.repeat((np.cos(tw) * scale)[:, :, None], LANES, axis=2)
    twi = np.repeat((-np.sin(tw) * scale)[:, :, None], LANES, axis=2)
    kc = np.arange(FOURIER_GROUP_CH)
    angc = 2.0 * np.pi * np.outer(kc, kc) / FOURIER_GROUP_CH
    cs = 1.0 / math.sqrt(FOURIER_GROUP_CH)
    cdft = np.concatenate([np.cos(angc) * cs, -np.sin(angc) * cs], axis=1)
    return (jnp.asarray(f1, BF16), jnp.asarray(f2, BF16), jnp.asarray(twr, F32),
            jnp.asarray(twi, F32), jnp.asarray(cdft, BF16))


def _ssm_matrices(a_re, a_im, log_dt, b_re, b_im, c_re, c_im):
    q_len = SSM_CHUNK
    dt = jnp.exp(log_dt)[..., None]
    lr, li = a_re * dt, a_im * dt
    steps = jnp.arange(q_len + 1, dtype=F32)
    mag = jnp.exp(lr[..., None] * steps)
    ang = li[..., None] * steps
    pr, pi = mag * jnp.cos(ang), mag * jnp.sin(ang)
    ar, ai = pr[..., 1], pi[..., 1]
    den = a_re * a_re + a_im * a_im
    cr = ((ar - 1.0) * a_re + ai * a_im) / den
    ci = (ai * a_re - (ar - 1.0) * a_im) / den
    bbr = cr[..., None] * b_re - ci[..., None] * b_im
    bbi = cr[..., None] * b_im + ci[..., None] * b_re
    cpr = jnp.einsum('dgcn,dgnt->dgtcn', c_re, pr[..., :q_len]) - jnp.einsum('dgcn,dgnt->dgtcn', c_im, pi[..., :q_len])
    cpi = jnp.einsum('dgcn,dgnt->dgtcn', c_re, pi[..., :q_len]) + jnp.einsum('dgcn,dgnt->dgtcn', c_im, pr[..., :q_len])
    kern = jnp.einsum('dgtcn,dgne->dgtce', cpr, bbr) - jnp.einsum('dgtcn,dgne->dgtce', cpi, bbi)
    qi = jnp.arange(q_len)
    lag = qi[None, :] - qi[:, None]
    k_f = kern[0][:, jnp.maximum(lag, 0)] * (lag >= 0)[None, :, :, None, None].astype(F32)
    k_b = kern[1][:, jnp.maximum(-lag, 0)] * (lag <= 0)[None, :, :, None, None].astype(F32)
    kfull = jnp.transpose(k_f + k_b, (0, 1, 4, 2, 3))
    eye = jnp.eye(GROUPS_PER_BLOCK, dtype=F32)
    kfull = kfull.reshape(SSM_LANE_BLOCKS, GROUPS_PER_BLOCK, q_len, SSM_GROUP_CH, q_len, SSM_GROUP_CH)
    m = jnp.einsum('jgpeqc,gh->jpgeqhc', kfull, eye).reshape(SSM_LANE_BLOCKS, CHUNK_COLS, CHUNK_COLS)

    pw_f = jnp.stack([pr[0][..., q_len - 1 - qi], pi[0][..., q_len - 1 - qi]])
    pw_b = jnp.stack([pr[1][..., qi], pi[1][..., qi]])

    def state_in(pw, br, bi):
        wr = jnp.einsum('gnq,gne->gqen', pw[0], br) - jnp.einsum('gnq,gne->gqen', pw[1], bi)
        wi = jnp.einsum('gnq,gne->gqen', pw[0], bi) + jnp.einsum('gnq,gne->gqen', pw[1], br)
        return jnp.stack([wr, wi])

    ws = jnp.stack([state_in(pw_f, bbr[0], bbi[0]), state_in(pw_b, bbr[1], bbi[1])])
    ws = ws.reshape(2, 2, SSM_LANE_BLOCKS, GROUPS_PER_BLOCK, q_len, SSM_GROUP_CH, SSM_STATE)
    ws = jnp.einsum('drjgqen,gh->jqgedrhn', ws, eye).reshape(SSM_LANE_BLOCKS, CHUNK_COLS, 4 * STATE_COLS)

    def state_out(pwr, pwi, c_r, c_i):
        o_r = jnp.einsum('gcn,gnq->gnqc', c_r, pwr) - jnp.einsum('gcn,gnq->gnqc', c_i, pwi)
        o_i = jnp.einsum('gcn,gnq->gnqc', c_r, pwi) + jnp.einsum('gcn,gnq->gnqc', c_i, pwr)
        return jnp.stack([o_r, -o_i])

    wo = jnp.stack([state_out(pr[0][..., qi + 1], pi[0][..., qi + 1], c_re[0], c_im[0]),
                    state_out(pr[1][..., q_len - qi], pi[1][..., q_len - qi], c_re[1], c_im[1])])
    wo = wo.reshape(2, 2, SSM_LANE_BLOCKS, GROUPS_PER_BLOCK, SSM_STATE, q_len, SSM_GROUP_CH)
    wo = jnp.einsum('drjgnqc,gh->jdrgnqhc', wo, eye).reshape(SSM_LANE_BLOCKS, 4 * STATE_COLS, CHUNK_COLS)

    aq = jnp.stack([pr[0][..., q_len], pi[0][..., q_len], pr[1][..., q_len], pi[1][..., q_len]])
    aq = aq.reshape(4, SSM_LANE_BLOCKS, 1, STATE_COLS)
    aq = jnp.broadcast_to(aq, (4, SSM_LANE_BLOCKS, BATCH, STATE_COLS))
    aq = jnp.transpose(aq, (1, 0, 2, 3)).reshape(SSM_LANE_BLOCKS, 4, BATCH * STATE_COLS)
    return m.astype(BF16), ws.astype(BF16), wo.astype(BF16), aq


def _ssm_state_kernel(x_ref, ws_ref, sre_ref, sim_ref):
    s = jnp.dot(x_ref[0], ws_ref[0], preferred_element_type=F32)
    sre_ref[0] = s[:, :STATE_COLS]
    sim_ref[0] = s[:, STATE_COLS:]


def _ssm_states(xr, ws):
    out_spec = pl.BlockSpec((1, N_CHUNKS, STATE_COLS), lambda j, d, b: (j, 0, d * BATCH + b))
    shape = jax.ShapeDtypeStruct((SSM_LANE_BLOCKS, N_CHUNKS, 2 * BATCH * STATE_COLS), F32)
    return pl.pallas_call(
        _ssm_state_kernel,
        grid=(SSM_LANE_BLOCKS, 2, BATCH),
        in_specs=[
            pl.BlockSpec((1, N_CHUNKS, CHUNK_COLS), lambda j, d, b: (j, b, 0)),
            pl.BlockSpec((1, CHUNK_COLS, 2 * STATE_COLS), lambda j, d, b: (j, 0, d)),
        ],
        out_specs=[out_spec, out_spec],
        out_shape=[shape, shape],
        compiler_params=_cparams(("parallel", "parallel", "parallel")),
        name="ssm_states",
    )(xr, ws)


def _ssm_scan_kernel(sre_ref, sim_ref, aq_ref, hre_ref, him_ref):
    half = BATCH * STATE_COLS
    fwd, bwd = slice(0, half), slice(half, 2 * half)
    a0r, a0i, a1r, a1i = aq_ref[0, 0:1], aq_ref[0, 1:2], aq_ref[0, 2:3], aq_ref[0, 3:4]

    def body(k, carry):
        fr, fi, br, bi = carry
        kb = N_CHUNKS - 1 - k
        hre_ref[0, pl.ds(k, 1), fwd] = fr
        him_ref[0, pl.ds(k, 1), fwd] = fi
        hre_ref[0, pl.ds(kb, 1), bwd] = br
        him_ref[0, pl.ds(kb, 1), bwd] = bi
        sfr, sfi = sre_ref[0, pl.ds(k, 1), fwd], sim_ref[0, pl.ds(k, 1), fwd]
        sbr, sbi = sre_ref[0, pl.ds(kb, 1), bwd], sim_ref[0, pl.ds(kb, 1), bwd]
        return (a0r * fr - a0i * fi + sfr, a0r * fi + a0i * fr + sfi,
                a1r * br - a1i * bi + sbr, a1r * bi + a1i * br + sbi)

    z = jnp.zeros((1, half), F32)
    lax.fori_loop(0, N_CHUNKS, body, (z, z, z, z))


def _ssm_scan(sre, sim, aq):
    spec = pl.BlockSpec((1, N_CHUNKS, 2 * BATCH * STATE_COLS), lambda j: (j, 0, 0))
    shape = jax.ShapeDtypeStruct(sre.shape, F32)
    return pl.pallas_call(
        _ssm_scan_kernel,
        grid=(SSM_LANE_BLOCKS,),
        in_specs=[spec, spec, pl.BlockSpec((1, 4, BATCH * STATE_COLS), lambda j: (j, 0, 0))],
        out_specs=[spec, spec],
        out_shape=[shape, shape],
        compiler_params=_cparams(("parallel",)),
        name="ssm_scan",
    )(sre, sim, aq)


SSM_OUT_TN = 1024


def _ssm_out_kernel(x_ref, m_ref, wo_ref, h0r_ref, h0i_ref, h1r_ref, h1i_ref, y_ref):
    acc = jnp.dot(x_ref[0], m_ref[0], preferred_element_type=F32)
    for i, h_ref in enumerate((h0r_ref, h0i_ref, h1r_ref, h1i_ref)):
        acc += jnp.dot(h_ref[0].astype(BF16), wo_ref[0, i * STATE_COLS:(i + 1) * STATE_COLS, :],
                       preferred_element_type=F32)
    y_ref[0] = acc.astype(BF16)


def _ssm_out(xr, m, wo, hre, him):
    def hspec(d):
        return pl.BlockSpec((1, N_CHUNKS, STATE_COLS), lambda j, n, b: (j, 0, d * BATCH + b))
    return pl.pallas_call(
        _ssm_out_kernel,
        grid=(SSM_LANE_BLOCKS, CHUNK_COLS // SSM_OUT_TN, BATCH),
        in_specs=[
            pl.BlockSpec((1, N_CHUNKS, CHUNK_COLS), lambda j, n, b: (j, b, 0)),
            pl.BlockSpec((1, CHUNK_COLS, SSM_OUT_TN), lambda j, n, b: (j, 0, n)),
            pl.BlockSpec((1, 4 * STATE_COLS, SSM_OUT_TN), lambda j, n, b: (j, 0, n)),
            hspec(0), hspec(0), hspec(1), hspec(1),
        ],
        out_specs=pl.BlockSpec((1, N_CHUNKS, SSM_OUT_TN), lambda j, n, b: (j, b, n)),
        out_shape=jax.ShapeDtypeStruct((SSM_LANE_BLOCKS, BATCH * N_CHUNKS, CHUNK_COLS), BF16),
        compiler_params=_cparams(("parallel", "parallel", "parallel")),
        name="ssm_out",
    )(xr, m, wo, hre, him, hre, him)


MERGE_TM = 512
GELU_C = math.sqrt(2.0 / math.pi)


def _split_bf16(v):
    hi = v.astype(BF16)
    lo = (v - hi.astype(F32)).astype(BF16)
    return hi, lo


def _merge_kernel(x_ref, fm_ref, yc_ref, us_ref, gates_ref, dskip_ref, wf_ref, wglu_ref, ws_ref, wo_ref,
                  ng_ref, wrh_ref, wrl_ref, rb_ref, x1_ref, hn_ref, logit_ref):
    conv = jnp.concatenate([yc_ref[j].astype(F32) for j in range(SSM_LANE_BLOCKS)], axis=-1)
    u = jnp.concatenate([us_ref[j].astype(F32) for j in range(SSM_LANE_BLOCKS)], axis=-1)
    y = conv + dskip_ref[...] * u
    y = 0.5 * y * (1.0 + jnp.tanh(GELU_C * (y + 0.044715 * (y * y * y))))
    glu = jax.nn.sigmoid(jnp.dot(y.astype(BF16), wglu_ref[...], preferred_element_type=F32))
    y_s = jnp.dot((y * glu).astype(BF16), ws_ref[...], preferred_element_type=F32)
    y_f = jnp.dot(fm_ref[...], wf_ref[...], preferred_element_type=F32)
    merged = (gates_ref[:, :D_MODEL].astype(F32) * y_f + gates_ref[:, D_MODEL:].astype(F32) * y_s)
    x1 = x_ref[...] + jnp.dot(merged.astype(BF16), wo_ref[...], preferred_element_type=F32)
    x1_ref[...] = x1
    inv = lax.rsqrt(jnp.mean(x1 * x1, axis=-1, keepdims=True) + RMS_EPS)
    hn = x1 * inv * ng_ref[...]
    hn_ref[...] = hn
    hi, lo = _split_bf16(hn)
    logits = (jnp.dot(hi, wrh_ref[...], preferred_element_type=F32)
              + jnp.dot(lo, wrh_ref[...], preferred_element_type=F32)
              + jnp.dot(hi, wrl_ref[...], preferred_element_type=F32))
    logit_ref[...] = logits + rb_ref[...]


def _merge(x, fmix, yconv, us, gates, dskip, wf, wglu, ws, wo, ng, wrh, wrl, rb):
    tm = MERGE_TM
    full = lambda a: pl.BlockSpec(a.shape, lambda i: (0,) * a.ndim)
    return pl.pallas_call(
        _merge_kernel,
        grid=(TOKENS // tm,),
        in_specs=[
            pl.BlockSpec((tm, D_MODEL), lambda i: (i, 0)),
            pl.BlockSpec((tm, FOURIER_WIDTH), lambda i: (i, 0)),
            pl.BlockSpec((SSM_LANE_BLOCKS, tm, LANES), lambda i: (0, i, 0)),
            pl.BlockSpec((SSM_LANE_BLOCKS, tm, LANES), lambda i: (0, i, 0)),
            pl.BlockSpec((tm, 2 * D_MODEL), lambda i: (i, 0)),
            full(dskip), full(wf), full(wglu), full(ws), full(wo), full(ng), full(wrh), full(wrl), full(rb),
        ],
        out_specs=[
            pl.BlockSpec((tm, D_MODEL), lambda i: (i, 0)),
            pl.BlockSpec((tm, D_MODEL), lambda i: (i, 0)),
            pl.BlockSpec((tm, ROUTER_COLS), lambda i: (i, 0)),
        ],
        out_shape=[
            jax.ShapeDtypeStruct((TOKENS, D_MODEL), F32),
            jax.ShapeDtypeStruct((TOKENS, D_MODEL), F32),
            jax.ShapeDtypeStruct((TOKENS, ROUTER_COLS), F32),
        ],
        compiler_params=_cparams(("parallel",)),
        name="merge",
    )(x, fmix, yconv, us, gates, dskip, wf, wglu, ws, wo, ng, wrh, wrl, rb)


def _route(logits):
    t = TOKENS
    g_prob = jax.nn.softmax(logits[:, :MOE_GROUPS], axis=-1)
    p_g, g_idx = lax.top_k(g_prob, 1)
    e_logits = logits[:, MOE_GROUPS:MOE_GROUPS + N_EXPERTS].reshape(t, MOE_GROUPS, EXPERTS_PER_GROUP)
    e_sel = jnp.take_along_axis(e_logits, g_idx[:, :, None], axis=1)[:, 0]
    top_l, top_i = lax.top_k(e_sel, MOE_TOP_K)
    gate_w = p_g * jax.nn.softmax(top_l, axis=-1)
    expert = (g_idx * EXPERTS_PER_GROUP + top_i).astype(jnp.int32)

    n_assign = t * MOE_TOP_K
    e_flat = expert.reshape(n_assign)
    order = jnp.argsort(e_flat)
    e_s = e_flat[order]
    tok_s = (order // MOE_TOP_K).astype(jnp.int32)
    counts = jnp.bincount(e_flat, length=N_EXPERTS).astype(jnp.int32)
    starts = jnp.cumsum(counts) - counts
    padded = ((counts + MOE_ROWS - 1) // MOE_ROWS) * MOE_ROWS
    pstarts = jnp.cumsum(padded) - padded
    pends = pstarts + padded
    dest_s = pstarts[e_s] + jnp.arange(n_assign, dtype=jnp.int32) - starts[e_s]
    dest = jnp.zeros((n_assign,), jnp.int32).at[order].set(dest_s)
    row_tok = jnp.zeros((MOE_BLOCKS * MOE_ROWS,), jnp.int32).at[dest_s].set(tok_s)
    block_start = jnp.arange(MOE_BLOCKS, dtype=jnp.int32) * MOE_ROWS
    n_used = (pends[-1] // MOE_ROWS).astype(jnp.int32)
    block_e = jnp.minimum(jnp.searchsorted(pends, block_start, side='right'), N_EXPERTS - 1).astype(jnp.int32)
    last_e = block_e[jnp.maximum(n_used - 1, 0)]
    block_e = jnp.where(jnp.arange(MOE_BLOCKS) < n_used, block_e, last_e)
    return gate_w, dest, row_tok, block_e, n_used.reshape(1)


def _moe_kernel(tok_ref, be_ref, nu_ref, hn_ref, wg_ref, wu_ref, wd_ref, y_ref, xbuf, sem):
    i = pl.program_id(0)
    n_used = nu_ref[0]

    def gather(block, slot):
        base = block * MOE_ROWS

        def issue(r, c):
            tok = tok_ref[base + r]
            pltpu.make_async_copy(hn_ref.at[pl.ds(tok, 1), :], xbuf.at[slot, pl.ds(r, 1), :], sem.at[slot]).start()
            return c

        lax.fori_loop(0, MOE_ROWS, issue, 0, unroll=8)

    @pl.when(i == 0)
    def _():
        gather(0, 0)

    @pl.when(i + 1 < n_used)
    def _():
        gather(i + 1, (i + 1) % 2)

    @pl.when(i < n_used)
    def _():
        slot = i % 2
        pltpu.make_async_copy(hn_ref.at[pl.ds(0, MOE_ROWS), :], xbuf.at[slot], sem.at[slot]).wait()
        xb = xbuf[slot].astype(BF16)
        a = jnp.dot(xb, wg_ref[0].astype(BF16), preferred_element_type=F32)
        u = jnp.dot(xb, wu_ref[0].astype(BF16), preferred_element_type=F32)
        act = (a * jax.nn.sigmoid(a) * u).astype(BF16)
        y_ref[...] = jnp.dot(act, wd_ref[0].astype(BF16), preferred_element_type=F32)

    @pl.when(i >= n_used)
    def _():
        y_ref[...] = jnp.zeros_like(y_ref)


def _moe(row_tok, block_e, n_used, hn, w_gate, w_up, w_down):
    grid_spec = pltpu.PrefetchScalarGridSpec(
        num_scalar_prefetch=3,
        grid=(MOE_BLOCKS,),
        in_specs=[
            pl.BlockSpec(memory_space=pl.ANY),
            pl.BlockSpec((1, D_MODEL, D_EXPERT), lambda i, tok, be, nu: (be[i], 0, 0)),
            pl.BlockSpec((1, D_MODEL, D_EXPERT), lambda i, tok, be, nu: (be[i], 0, 0)),
            pl.BlockSpec((1, D_EXPERT, D_MODEL), lambda i, tok, be, nu: (be[i], 0, 0)),
        ],
        out_specs=pl.BlockSpec((MOE_ROWS, D_MODEL), lambda i, tok, be, nu: (i, 0)),
        scratch_shapes=[pltpu.VMEM((2, MOE_ROWS, D_MODEL), F32), pltpu.SemaphoreType.DMA((2,))],
    )
    return pl.pallas_call(
        _moe_kernel,
        grid_spec=grid_spec,
        out_shape=jax.ShapeDtypeStruct((MOE_BLOCKS * MOE_ROWS, D_MODEL), F32),
        compiler_params=_cparams(("arbitrary",)),
        name="moe",
    )(row_tok, block_e, n_used, hn, w_gate, w_up, w_down)


COMBINE_TM = 256


def _combine_kernel(dest_ref, x1_ref, gw_ref, g_ref, y_ref, o_ref, ybuf, sem):
    i = pl.program_id(0)
    n = pl.num_programs(0)

    def gather(tile, slot):
        base = tile * (COMBINE_TM * MOE_TOP_K)

        def issue(r, c):
            for k in range(MOE_TOP_K):
                row = dest_ref[base + r * MOE_TOP_K + k]
                pltpu.make_async_copy(y_ref.at[pl.ds(row, 1), :], ybuf.at[slot, k, pl.ds(r, 1), :],
                                      sem.at[slot]).start()
            return c

        lax.fori_loop(0, COMBINE_TM, issue, 0, unroll=4)

    @pl.when(i == 0)
    def _():
        gather(0, 0)

    @pl.when(i + 1 < n)
    def _():
        gather(i + 1, (i + 1) % 2)

    slot = i % 2
    for k in range(MOE_TOP_K):
        pltpu.make_async_copy(y_ref.at[pl.ds(0, COMBINE_TM), :], ybuf.at[slot, k], sem.at[slot]).wait()
    gw = gw_ref[...]
    x2 = x1_ref[...] + gw[:, 0:1] * ybuf[slot, 0] + gw[:, 1:2] * ybuf[slot, 1]
    inv = lax.rsqrt(jnp.mean(x2 * x2, axis=-1, keepdims=True) + RMS_EPS)
    o_ref[...] = x2 * inv * g_ref[...]


def _combine(dest, x1, gate_w, g, y_pad):
    tm = COMBINE_TM
    grid_spec = pltpu.PrefetchScalarGridSpec(
        num_scalar_prefetch=1,
        grid=(TOKENS // tm,),
        in_specs=[
            pl.BlockSpec((tm, D_MODEL), lambda i, d: (i, 0)),
            pl.BlockSpec((tm, MOE_TOP_K), lambda i, d: (i, 0)),
            pl.BlockSpec((1, D_MODEL), lambda i, d: (0, 0)),
            pl.BlockSpec(memory_space=pl.ANY),
        ],
        out_specs=pl.BlockSpec((tm, D_MODEL), lambda i, d: (i, 0)),
        scratch_shapes=[pltpu.VMEM((2, MOE_TOP_K, tm, D_MODEL), F32), pltpu.SemaphoreType.DMA((2,))],
    )
    return pl.pallas_call(
        _combine_kernel,
        grid_spec=grid_spec,
        out_shape=jax.ShapeDtypeStruct((TOKENS, D_MODEL), F32),
        compiler_params=_cparams(("arbitrary",)),
        name="combine",
    )(dest, x1, gate_w, g, y_pad)


def kernel(x, mix_norm_g, w_in, w_fourier_out, ssm_A_re, ssm_A_im, ssm_log_dt, ssm_B_re, ssm_B_im, ssm_C_re,
           ssm_C_im, ssm_D, ssm_w_glu, w_ssm_out, w_out, ffn_norm_g, router_group_w, router_group_b,
           router_expert_w, router_expert_b, expert_w_gate, expert_w_up, expert_w_down, final_norm_g):
    assert x.shape == (BATCH, SEQ, D_MODEL) and w_in.shape[0] == 1
    f1, f2, twr, twi, cdft = _dft_constants()

    vf, us, gates = _inproj(x, mix_norm_g[0][None], w_in[0].astype(BF16), cdft)
    a = _dft1(vf.reshape(BATCH, 2 * DFT_RADIX, DFT_RADIX * FOURIER_WIDTH), f1, twr, twi)
    fmix = _dft2(a.reshape(BATCH, 2 * DFT_RADIX, DFT_RADIX * FOURIER_WIDTH), f2)
    fmix = fmix.reshape(TOKENS, FOURIER_WIDTH)

    m, ws, wo, aq = _ssm_matrices(ssm_A_re[0], ssm_A_im[0], ssm_log_dt[0], ssm_B_re[0], ssm_B_im[0],
                                  ssm_C_re[0], ssm_C_im[0])
    xr = us.reshape(SSM_LANE_BLOCKS, BATCH * N_CHUNKS, CHUNK_COLS)
    sre, sim = _ssm_states(xr, ws)
    hre, him = _ssm_scan(sre, sim, aq)
    yconv = _ssm_out(xr, m, wo, hre, him).reshape(SSM_LANE_BLOCKS, TOKENS, LANES)

    w_router = jnp.concatenate([router_group_w[0], router_expert_w[0]], axis=1)
    w_router = jnp.pad(w_router, ((0, 0), (0, ROUTER_COLS - w_router.shape[1])))
    b_router = jnp.concatenate([router_group_b[0], router_expert_b[0]])
    b_router = jnp.pad(b_router, (0, ROUTER_COLS - b_router.shape[0]))[None]
    wr_hi = w_router.astype(BF16)
    wr_lo = (w_router - wr_hi.astype(F32)).astype(BF16)
    x1, hn, logits = _merge(x.reshape(TOKENS, D_MODEL), fmix, yconv, us, gates, ssm_D[0][None],
                            w_fourier_out[0].astype(BF16), ssm_w_glu[0].astype(BF16), w_ssm_out[0].astype(BF16),
                            w_out[0].astype(BF16), ffn_norm_g[0][None], wr_hi, wr_lo, b_router)

    gate_w, dest, row_tok, block_e, n_used = _route(logits)
    y_pad = _moe(row_tok, block_e, n_used, hn, expert_w_gate[0], expert_w_up[0], expert_w_down[0])
    out = _combine(dest, x1, gate_w, final_norm_g[None], y_pad)
    return out.reshape(BATCH, SEQ, D_MODEL)
```

```python
import math

import numpy as np
import jax
import jax.numpy as jnp
from jax import lax
from jax.experimental import pallas as pl
from jax.experimental.pallas import tpu as pltpu

F32 = jnp.float32
BF16 = jnp.bfloat16

D_MODEL = 1024
BATCH = 4
SEQ = 4096
TOKENS = BATCH * SEQ
FOURIER_WIDTH = 512
FOURIER_GROUP_CH = 128
FOURIER_GROUPS = 4
SSM_WIDTH = 512
SSM_GROUP_CH = 16
SSM_GROUPS = 32
SSM_STATE = 64
MOE_GROUPS = 8
EXPERTS_PER_GROUP = 8
N_EXPERTS = 64
MOE_TOP_K = 2
D_EXPERT = 512
RMS_EPS = 1e-6

LANES = 128
DFT_RADIX = 64
SSM_CHUNK = 16
SSM_LANE_BLOCKS = SSM_WIDTH // LANES
GROUPS_PER_BLOCK = LANES // SSM_GROUP_CH
CHUNK_COLS = SSM_CHUNK * LANES
N_CHUNKS = SEQ // SSM_CHUNK
STATE_COLS = GROUPS_PER_BLOCK * SSM_STATE
MOE_ROWS = 256
MOE_BLOCKS = TOKENS * MOE_TOP_K // MOE_ROWS + N_EXPERTS
ROUTER_COLS = 128
VMEM_LIMIT = 48 * 1024 * 1024


def _cparams(sem, vmem=VMEM_LIMIT):
    return pltpu.CompilerParams(dimension_semantics=sem, vmem_limit_bytes=vmem)


IN_TM = 512


def _inproj_kernel(x_ref, g_ref, w_ref, cdft_ref, vf_ref, us_ref, gates_ref):
    x = x_ref[0]
    inv = lax.rsqrt(jnp.mean(x * x, axis=-1, keepdims=True) + RMS_EPS)
    h = (x * inv * g_ref[...]).astype(BF16)
    zf = jnp.dot(h, w_ref[:, 0:FOURIER_WIDTH], preferred_element_type=F32).astype(BF16)
    cdft = cdft_ref[...].astype(BF16)
    for g in range(FOURIER_GROUPS):
        sl = slice(g * LANES, (g + 1) * LANES)
        v = jnp.dot(zf[:, sl], cdft, preferred_element_type=F32)
        vf_ref[0, 0, :, sl] = v[:, :LANES].astype(BF16)
        vf_ref[0, 1, :, sl] = v[:, LANES:].astype(BF16)
    zs = jnp.dot(h, w_ref[:, FOURIER_WIDTH:FOURIER_WIDTH + SSM_WIDTH], preferred_element_type=F32)
    for j in range(SSM_LANE_BLOCKS):
        us_ref[j] = zs[:, j * LANES:(j + 1) * LANES].astype(BF16)
    base = FOURIER_WIDTH + SSM_WIDTH
    for n in range(4):
        zg = jnp.dot(h, w_ref[:, base + n * 512: base + (n + 1) * 512], preferred_element_type=F32)
        gates_ref[:, n * 512:(n + 1) * 512] = jax.nn.sigmoid(zg).astype(BF16)


def _inproj(x, g, w_in, cdft):
    nt = SEQ // IN_TM
    return pl.pallas_call(
        _inproj_kernel,
        grid=(BATCH, nt),
        in_specs=[
            pl.BlockSpec((1, IN_TM, D_MODEL), lambda b, i: (b, i, 0)),
            pl.BlockSpec((1, D_MODEL), lambda b, i: (0, 0)),
            pl.BlockSpec(w_in.shape, lambda b, i: (0, 0)),
            pl.BlockSpec(cdft.shape, lambda b, i: (0, 0)),
        ],
        out_specs=[
            pl.BlockSpec((1, 2, IN_TM, FOURIER_WIDTH), lambda b, i: (b, 0, i, 0)),
            pl.BlockSpec((SSM_LANE_BLOCKS, IN_TM, LANES), lambda b, i: (0, b * nt + i, 0)),
            pl.BlockSpec((IN_TM, 2 * D_MODEL), lambda b, i: (b * nt + i, 0)),
        ],
        out_shape=[
            jax.ShapeDtypeStruct((BATCH, 2, SEQ, FOURIER_WIDTH), BF16),
            jax.ShapeDtypeStruct((SSM_LANE_BLOCKS, TOKENS, LANES), BF16),
            jax.ShapeDtypeStruct((TOKENS, 2 * D_MODEL), BF16),
        ],
        compiler_params=_cparams(("parallel", "parallel")),
        name="inproj",
    )(x, g, w_in, cdft)


DFT_S2_PER_STEP = 8


def _dft1_kernel(v_ref, f_ref, twr_ref, twi_ref, a_ref):
    f1 = f_ref[...].astype(BF16)
    for s in range(DFT_S2_PER_STEP):
        blk = v_ref[0, :, s * FOURIER_WIDTH:(s + 1) * FOURIER_WIDTH]
        r = jnp.dot(f1, blk, preferred_element_type=F32)
        ar, ai = r[:DFT_RADIX], r[DFT_RADIX:]
        tr = jnp.tile(twr_ref[s], (1, FOURIER_WIDTH // LANES))
        ti = jnp.tile(twi_ref[s], (1, FOURIER_WIDTH // LANES))
        a_ref[0, 0, s] = (ar * tr - ai * ti).astype(BF16)
        a_ref[0, 1, s] = (ar * ti + ai * tr).astype(BF16)


def _dft1(v, f1, twr, twi):
    ns = DFT_RADIX // DFT_S2_PER_STEP
    cols = DFT_S2_PER_STEP * FOURIER_WIDTH
    return pl.pallas_call(
        _dft1_kernel,
        grid=(BATCH, ns),
        in_specs=[
            pl.BlockSpec((1, 2 * DFT_RADIX, cols), lambda b, s: (b, 0, s)),
            pl.BlockSpec(f1.shape, lambda b, s: (0, 0)),
            pl.BlockSpec((DFT_S2_PER_STEP, DFT_RADIX, LANES), lambda b, s: (s, 0, 0)),
            pl.BlockSpec((DFT_S2_PER_STEP, DFT_RADIX, LANES), lambda b, s: (s, 0, 0)),
        ],
        out_specs=pl.BlockSpec((1, 2, DFT_S2_PER_STEP, DFT_RADIX, FOURIER_WIDTH),
                               lambda b, s: (b, 0, s, 0, 0)),
        out_shape=jax.ShapeDtypeStruct((BATCH, 2, DFT_RADIX, DFT_RADIX, FOURIER_WIDTH), BF16),
        compiler_params=_cparams(("parallel", "parallel")),
        name="dft1",
    )(v, f1, twr, twi)


DFT2_COLS = 4096


def _dft2_kernel(a_ref, f_ref, o_ref):
    o_ref[0] = jnp.dot(f_ref[...].astype(BF16), a_ref[0], preferred_element_type=F32).astype(BF16)


def _dft2(a, f2):
    n = DFT_RADIX * FOURIER_WIDTH
    return pl.pallas_call(
        _dft2_kernel,
        grid=(BATCH, n // DFT2_COLS),
        in_specs=[
            pl.BlockSpec((1, 2 * DFT_RADIX, DFT2_COLS), lambda b, s: (b, 0, s)),
            pl.BlockSpec(f2.shape, lambda b, s: (0, 0)),
        ],
        out_specs=pl.BlockSpec((1, DFT_RADIX, DFT2_COLS), lambda b, s: (b, 0, s)),
        out_shape=jax.ShapeDtypeStruct((BATCH, DFT_RADIX, n), BF16),
        compiler_params=_cparams(("parallel", "parallel")),
        name="dft2",
    )(a, f2)


def _dft_constants():
    r = DFT_RADIX
    k = np.arange(r)
    ang = 2.0 * np.pi * np.outer(k, k) / r
    c, s = np.cos(ang), np.sin(ang)
    f1 = np.block([[c, s], [-s, c]])
    f2 = np.concatenate([c, s], axis=1)
    tw = 2.0 * np.pi * np.outer(k, k) / SEQ
    scale = 1.0 / math.sqrt(SEQ)
    twr = np.repeat((np.cos(tw) * scale)[:, :, None], LANES, axis=2)
    twi = np.repeat((-np.sin(tw) * scale)[:, :, None], LANES, axis=2)
    kc = np.arange(FOURIER_GROUP_CH)
    angc = 2.0 * np.pi * np.outer(kc, kc) / FOURIER_GROUP_CH
    cs = 1.0 / math.sqrt(FOURIER_GROUP_CH)
    cdft = np.concatenate([np.cos(angc) * cs, -np.sin(angc) * cs], axis=1)
    return tuple(jnp.asarray(v, F32) for v in (f1, f2, twr, twi, cdft))


def _ssm_operators(a_re, a_im, log_dt, b_re, b_im, c_re, c_im):
    q_len = SSM_CHUNK
    jb, gb, ch, ns = SSM_LANE_BLOCKS, GROUPS_PER_BLOCK, SSM_GROUP_CH, SSM_STATE
    dt = jnp.exp(log_dt)[..., None]
    lr, li = a_re * dt, a_im * dt
    steps = jnp.arange(q_len + 1, dtype=F32)
    mag = jnp.exp(lr[..., None] * steps)
    ang = li[..., None] * steps
    pr, pi = mag * jnp.cos(ang), mag * jnp.sin(ang)
    ar, ai = pr[..., 1], pi[..., 1]
    den = a_re * a_re + a_im * a_im
    cr = ((ar - 1.0) * a_re + ai * a_im) / den
    ci = (ai * a_re - (ar - 1.0) * a_im) / den
    bbr = cr[..., None] * b_re - ci[..., None] * b_im
    bbi = cr[..., None] * b_im + ci[..., None] * b_re
    prq, piq = pr[..., :q_len], pi[..., :q_len]
    cpr = jnp.einsum('dgcn,dgnt->dgtcn', c_re, prq) - jnp.einsum('dgcn,dgnt->dgtcn', c_im, piq)
    cpi = jnp.einsum('dgcn,dgnt->dgtcn', c_re, piq) + jnp.einsum('dgcn,dgnt->dgtcn', c_im, prq)
    kern = jnp.einsum('dgtcn,dgne->dgtce', cpr, bbr) - jnp.einsum('dgtcn,dgne->dgtce', cpi, bbi)
    qi = jnp.arange(q_len)
    lag = qi[None, :] - qi[:, None]
    k_f = kern[0][:, jnp.maximum(lag, 0)] * (lag >= 0)[None, :, :, None, None].astype(F32)
    k_b = kern[1][:, jnp.maximum(-lag, 0)] * (lag <= 0)[None, :, :, None, None].astype(F32)
    kc = (k_f + k_b).reshape(jb, gb, q_len, q_len, ch, ch)
    kc = jnp.transpose(kc, (0, 2, 1, 5, 3, 4)).reshape(jb, CHUNK_COLS, q_len * ch)

    def state_in(pwr, pwi, br, bi):
        wr = jnp.einsum('gnq,gne->gqen', pwr, br) - jnp.einsum('gnq,gne->gqen', pwi, bi)
        wi = jnp.einsum('gnq,gne->gqen', pwr, bi) + jnp.einsum('gnq,gne->gqen', pwi, br)
        return jnp.stack([wr, wi])

    rev = q_len - 1 - qi
    wsc = jnp.stack([state_in(pr[0][..., rev], pi[0][..., rev], bbr[0], bbi[0]),
                     state_in(pr[1][..., qi], pi[1][..., qi], bbr[1], bbi[1])])
    wsc = wsc.reshape(2, 2, jb, gb, q_len, ch, ns)
    wsc = jnp.transpose(wsc, (2, 4, 3, 5, 0, 1, 6)).reshape(jb, CHUNK_COLS, 4 * ns)

    def state_out(pwr, pwi, c_r, c_i):
        o_r = jnp.einsum('gcn,gnq->gnqc', c_r, pwr) - jnp.einsum('gcn,gnq->gnqc', c_i, pwi)
        o_i = jnp.einsum('gcn,gnq->gnqc', c_r, pwi) + jnp.einsum('gcn,gnq->gnqc', c_i, pwr)
        return jnp.stack([o_r, -o_i])

    woc = jnp.stack([state_out(pr[0][..., qi + 1], pi[0][..., qi + 1], c_re[0], c_im[0]),
                     state_out(pr[1][..., q_len - qi], pi[1][..., q_len - qi], c_re[1], c_im[1])])
    woc = woc.reshape(2, 2, jb, gb, ns, q_len * ch)
    woc = jnp.transpose(woc, (2, 0, 1, 3, 4, 5)).reshape(jb, 4 * STATE_COLS, q_len * ch)

    aq = jnp.stack([pr[0][..., q_len], pi[0][..., q_len], pr[1][..., q_len], pi[1][..., q_len]])
    aq = aq.reshape(4, jb, 1, STATE_COLS)
    aq = jnp.broadcast_to(aq, (4, jb, BATCH, STATE_COLS))
    aq = jnp.transpose(aq, (1, 0, 2, 3)).reshape(jb, 4, BATCH * STATE_COLS)
    return kc.astype(BF16), wsc.astype(BF16), woc.astype(BF16), aq


EXPAND_ROWS = 256
GROUP_SHIFT_CH = 4
GROUP_SHIFT_STATE = 6


def _expand_block_diag(compact_ref, sel_ref, out_ref, row_shift, col_shift):
    n_rows, n_cols = out_ref.shape
    col_g = (lax.broadcasted_iota(jnp.int32, (EXPAND_ROWS, n_cols), 1) >> col_shift) & (GROUPS_PER_BLOCK - 1)
    for r in range(n_rows // EXPAND_ROWS):
        rows = slice(r * EXPAND_ROWS, (r + 1) * EXPAND_ROWS)
        t = jnp.dot(compact_ref[rows, :], sel_ref[...], preferred_element_type=F32)
        row_i = lax.broadcasted_iota(jnp.int32, (EXPAND_ROWS, n_cols), 0) + r * EXPAND_ROWS
        row_g = (row_i >> row_shift) & (GROUPS_PER_BLOCK - 1)
        out_ref[rows, :] = jnp.where(row_g == col_g, t, 0.0).astype(BF16)


def _replication_matrices():
    half = SSM_CHUNK // 2
    e_qc = np.zeros((half, SSM_GROUP_CH, half, GROUPS_PER_BLOCK, SSM_GROUP_CH), np.float32)
    for q in range(half):
        for c in range(SSM_GROUP_CH):
            e_qc[q, c, q, :, c] = 1.0
    e_rn = np.zeros((2, SSM_STATE, 2, GROUPS_PER_BLOCK, SSM_STATE), np.float32)
    for r in range(2):
        for n in range(SSM_STATE):
            e_rn[r, n, r, :, n] = 1.0
    return (jnp.asarray(e_qc.reshape(half * SSM_GROUP_CH, -1), BF16),
            jnp.asarray(e_rn.reshape(2 * SSM_STATE, -1), BF16))


def _ssm_state_kernel(x_ref, wsc_ref, sel_ref, sre_ref, sim_ref, ws_scr):
    @pl.when(pl.program_id(2) == 0)
    def _():
        _expand_block_diag(wsc_ref.at[0], sel_ref, ws_scr, GROUP_SHIFT_CH, GROUP_SHIFT_STATE)

    s = jnp.dot(x_ref[0], ws_scr[...], preferred_element_type=F32)
    sre_ref[0] = s[:, :STATE_COLS]
    sim_ref[0] = s[:, STATE_COLS:]


def _ssm_states(xr, wsc, e_rn):
    out_spec = pl.BlockSpec((1, N_CHUNKS, STATE_COLS), lambda j, d, b: (j, 0, d * BATCH + b))
    shape = jax.ShapeDtypeStruct((SSM_LANE_BLOCKS, N_CHUNKS, 2 * BATCH * STATE_COLS), F32)
    return pl.pallas_call(
        _ssm_state_kernel,
        grid=(SSM_LANE_BLOCKS, 2, BATCH),
        in_specs=[
            pl.BlockSpec((1, N_CHUNKS, CHUNK_COLS), lambda j, d, b: (j, b, 0)),
            pl.BlockSpec((1, CHUNK_COLS, 2 * SSM_STATE), lambda j, d, b: (j, 0, d)),
            pl.BlockSpec(e_rn.shape, lambda j, d, b: (0, 0)),
        ],
        out_specs=[out_spec, out_spec],
        out_shape=[shape, shape],
        scratch_shapes=[pltpu.VMEM((CHUNK_COLS, 2 * STATE_COLS), BF16)],
        compiler_params=_cparams(("parallel", "parallel", "arbitrary")),
        name="ssm_states",
    )(xr, wsc, e_rn)


def _ssm_scan_kernel(sre_ref, sim_ref, aq_ref, hre_ref, him_ref):
    half = BATCH * STATE_COLS
    fwd, bwd = slice(0, half), slice(half, 2 * half)
    a0r, a0i, a1r, a1i = aq_ref[0, 0:1], aq_ref[0, 1:2], aq_ref[0, 2:3], aq_ref[0, 3:4]

    def body(k, carry):
        fr, fi, br, bi = carry
        kb = N_CHUNKS - 1 - k
        hre_ref[0, pl.ds(k, 1), fwd] = fr
        him_ref[0, pl.ds(k, 1), fwd] = fi
        hre_ref[0, pl.ds(kb, 1), bwd] = br
        him_ref[0, pl.ds(kb, 1), bwd] = bi
        sfr, sfi = sre_ref[0, pl.ds(k, 1), fwd], sim_ref[0, pl.ds(k, 1), fwd]
        sbr, sbi = sre_ref[0, pl.ds(kb, 1), bwd], sim_ref[0, pl.ds(kb, 1), bwd]
        return (a0r * fr - a0i * fi + sfr, a0r * fi + a0i * fr + sfi,
                a1r * br - a1i * bi + sbr, a1r * bi + a1i * br + sbi)

    z = jnp.zeros((1, half), F32)
    lax.fori_loop(0, N_CHUNKS, body, (z, z, z, z))


def _ssm_scan(sre, sim, aq):
    spec = pl.BlockSpec((1, N_CHUNKS, 2 * BATCH * STATE_COLS), lambda j: (j, 0, 0))
    shape = jax.ShapeDtypeStruct(sre.shape, F32)
    return pl.pallas_call(
        _ssm_scan_kernel,
        grid=(SSM_LANE_BLOCKS,),
        in_specs=[spec, spec, pl.BlockSpec((1, 4, BATCH * STATE_COLS), lambda j: (j, 0, 0))],
        out_specs=[spec, spec],
        out_shape=[shape, shape],
        compiler_params=_cparams(("parallel",)),
        name="ssm_scan",
    )(sre, sim, aq)


SSM_OUT_TN = CHUNK_COLS // 2


def _ssm_out_kernel(x_ref, kc_ref, woc_ref, sel_ref, h0r_ref, h0i_ref, h1r_ref, h1i_ref, y_ref, m_scr, wo_scr):
    @pl.when(pl.program_id(2) == 0)
    def _():
        _expand_block_diag(kc_ref.at[0], sel_ref, m_scr, GROUP_SHIFT_CH, GROUP_SHIFT_CH)
        _expand_block_diag(woc_ref.at[0], sel_ref, wo_scr, GROUP_SHIFT_STATE, GROUP_SHIFT_CH)

    acc = jnp.dot(x_ref[0], m_scr[...], preferred_element_type=F32)
    for i, h_ref in enumerate((h0r_ref, h0i_ref, h1r_ref, h1i_ref)):
        acc += jnp.dot(h_ref[0].astype(BF16), wo_scr[i * STATE_COLS:(i + 1) * STATE_COLS, :],
                       preferred_element_type=F32)
    y_ref[0] = acc.astype(BF16)


def _ssm_out(xr, kc, woc, e_qc, hre, him):
    def hspec(d):
        return pl.BlockSpec((1, N_CHUNKS, STATE_COLS), lambda j, n, b: (j, 0, d * BATCH + b))
    half_cols = kc.shape[2] // 2
    return pl.pallas_call(
        _ssm_out_kernel,
        grid=(SSM_LANE_BLOCKS, CHUNK_COLS // SSM_OUT_TN, BATCH),
        in_specs=[
            pl.BlockSpec((1, N_CHUNKS, CHUNK_COLS), lambda j, n, b: (j, b, 0)),
            pl.BlockSpec((1, CHUNK_COLS, half_cols), lambda j, n, b: (j, 0, n)),
            pl.BlockSpec((1, 4 * STATE_COLS, half_cols), lambda j, n, b: (j, 0, n)),
            pl.BlockSpec(e_qc.shape, lambda j, n, b: (0, 0)),
            hspec(0), hspec(0), hspec(1), hspec(1),
        ],
        out_specs=pl.BlockSpec((1, N_CHUNKS, SSM_OUT_TN), lambda j, n, b: (j, b, n)),
        out_shape=jax.ShapeDtypeStruct((SSM_LANE_BLOCKS, BATCH * N_CHUNKS, CHUNK_COLS), BF16),
        scratch_shapes=[pltpu.VMEM((CHUNK_COLS, SSM_OUT_TN), BF16), pltpu.VMEM((4 * STATE_COLS, SSM_OUT_TN), BF16)],
        compiler_params=_cparams(("parallel", "parallel", "arbitrary")),
        name="ssm_out",
    )(xr, kc, woc, e_qc, hre, him, hre, him)


MERGE_TM = 512
GELU_C = math.sqrt(2.0 / math.pi)
PACK_SUB = D_MODEL // LANES


def _split_bf16(v):
    hi = v.astype(BF16)
    lo = (v - hi.astype(F32)).astype(BF16)
    return hi, lo


def _pack_rows(v, out_ref):
    for s in range(PACK_SUB):
        out_ref[:, s, :] = v[:, s * LANES:(s + 1) * LANES]


def _unpack_rows(row_ref):
    return jnp.concatenate([row_ref[:, s, :] for s in range(PACK_SUB)], axis=1)


def _merge_kernel(x_ref, fm_ref, yc_ref, us_ref, gates_ref, dskip_ref, wf_ref, wglu_ref, ws_ref, wo_ref,
                  ng_ref, wrh_ref, wrl_ref, rb_ref, x1_ref, hp_ref, logit_ref):
    conv = jnp.concatenate([yc_ref[j].astype(F32) for j in range(SSM_LANE_BLOCKS)], axis=-1)
    u = jnp.concatenate([us_ref[j].astype(F32) for j in range(SSM_LANE_BLOCKS)], axis=-1)
    y = conv + dskip_ref[...] * u
    y = 0.5 * y * (1.0 + jnp.tanh(GELU_C * (y + 0.044715 * (y * y * y))))
    glu = jax.nn.sigmoid(jnp.dot(y.astype(BF16), wglu_ref[...], preferred_element_type=F32))
    y_s = jnp.dot((y * glu).astype(BF16), ws_ref[...], preferred_element_type=F32)
    y_f = jnp.dot(fm_ref[...], wf_ref[...], preferred_element_type=F32)
    merged = (gates_ref[:, :D_MODEL].astype(F32) * y_f + gates_ref[:, D_MODEL:].astype(F32) * y_s)
    x1 = x_ref[...] + jnp.dot(merged.astype(BF16), wo_ref[...], preferred_element_type=F32)
    x1_ref[...] = x1
    inv = lax.rsqrt(jnp.mean(x1 * x1, axis=-1, keepdims=True) + RMS_EPS)
    hn = x1 * inv * ng_ref[...]
    _pack_rows(hn, hp_ref)
    hi, lo = _split_bf16(hn)
    logits = (jnp.dot(hi, wrh_ref[...], preferred_element_type=F32)
              + jnp.dot(lo, wrh_ref[...], preferred_element_type=F32)
              + jnp.dot(hi, wrl_ref[...], preferred_element_type=F32))
    logit_ref[...] = logits + rb_ref[...]


def _merge(x, fmix, yconv, us, gates, dskip, wf, wglu, ws, wo, ng, wrh, wrl, rb):
    tm = MERGE_TM
    full = lambda a: pl.BlockSpec(a.shape, lambda i: (0,) * a.ndim)
    return pl.pallas_call(
        _merge_kernel,
        grid=(TOKENS // tm,),
        in_specs=[
            pl.BlockSpec((tm, D_MODEL), lambda i: (i, 0)),
            pl.BlockSpec((tm, FOURIER_WIDTH), lambda i: (i, 0)),
            pl.BlockSpec((SSM_LANE_BLOCKS, tm, LANES), lambda i: (0, i, 0)),
            pl.BlockSpec((SSM_LANE_BLOCKS, tm, LANES), lambda i: (0, i, 0)),
            pl.BlockSpec((tm, 2 * D_MODEL), lambda i: (i, 0)),
            full(dskip), full(wf), full(wglu), full(ws), full(wo), full(ng), full(wrh), full(wrl), full(rb),
        ],
        out_specs=[
            pl.BlockSpec((tm, D_MODEL), lambda i: (i, 0)),
            pl.BlockSpec((tm, PACK_SUB, LANES), lambda i: (i, 0, 0)),
            pl.BlockSpec((tm, ROUTER_COLS), lambda i: (i, 0)),
        ],
        out_shape=[
            jax.ShapeDtypeStruct((TOKENS, D_MODEL), F32),
            jax.ShapeDtypeStruct((TOKENS, PACK_SUB, LANES), F32),
            jax.ShapeDtypeStruct((TOKENS, ROUTER_COLS), F32),
        ],
        compiler_params=_cparams(("parallel",)),
        name="merge",
    )(x, fmix, yconv, us, gates, dskip, wf, wglu, ws, wo, ng, wrh, wrl, rb)


ROUTE_TM = 512
EXPERT_LANE0 = MOE_GROUPS
INFO_EXPERT, INFO_RANK, INFO_GATE = 0, 2, 4


def _route_kernel(lg_ref, info_ref, cnt_ref, carry):
    @pl.when(pl.program_id(0) == 0)
    def _():
        carry[...] = jnp.zeros_like(carry)

    lg = lg_ref[...]
    tm = lg.shape[0]
    col_i = lax.broadcasted_iota(jnp.int32, lg.shape, 1)
    col = col_i.astype(F32)
    neg = jnp.float32(-jnp.inf)
    none = jnp.float32(ROUTER_COLS)

    def row_max(v):
        return jnp.max(v, axis=-1, keepdims=True)

    def first_at(v, m):
        return jnp.min(jnp.where(v == m, col, none), axis=-1, keepdims=True)

    gl = jnp.where(col_i < MOE_GROUPS, lg, neg)
    gmax = row_max(gl)
    p_g = 1.0 / jnp.sum(jnp.exp(gl - gmax), axis=-1, keepdims=True)
    lo = EXPERT_LANE0 + first_at(gl, gmax) * EXPERTS_PER_GROUP
    el = jnp.where((col >= lo) & (col < lo + EXPERTS_PER_GROUP), lg, neg)
    l1 = row_max(el)
    i1 = first_at(el, l1)
    el2 = jnp.where(col == i1, neg, el)
    l2 = row_max(el2)
    i2 = first_at(el2, l2)
    r = jnp.exp(l2 - l1)
    w1 = p_g / (1.0 + r)
    w2 = w1 * r

    hit1, hit2 = col == i1, col == i2
    onehot = jnp.where(hit1 | hit2, 1.0, 0.0)
    earlier = lax.broadcasted_iota(jnp.int32, (tm, tm), 0) > lax.broadcasted_iota(jnp.int32, (tm, tm), 1)
    before = jnp.dot(jnp.where(earlier, 1.0, 0.0).astype(BF16), onehot.astype(BF16),
                     preferred_element_type=F32) + carry[...]
    rank1 = jnp.sum(jnp.where(hit1, before, 0.0), axis=-1, keepdims=True)
    rank2 = jnp.sum(jnp.where(hit2, before, 0.0), axis=-1, keepdims=True)
    carry[...] += jnp.sum(onehot, axis=0, keepdims=True)
    cnt_ref[...] = carry[...]

    info = jnp.zeros(lg.shape, F32)
    for lane, v in ((INFO_EXPERT, i1 - EXPERT_LANE0), (INFO_EXPERT + 1, i2 - EXPERT_LANE0), (INFO_RANK, rank1),
                    (INFO_RANK + 1, rank2), (INFO_GATE, w1), (INFO_GATE + 1, w2)):
        info = jnp.where(col_i == lane, v, info)
    info_ref[...] = info


def _route(logits):
    return pl.pallas_call(
        _route_kernel,
        grid=(TOKENS // ROUTE_TM,),
        in_specs=[pl.BlockSpec((ROUTE_TM, ROUTER_COLS), lambda i: (i, 0))],
        out_specs=[pl.BlockSpec((ROUTE_TM, ROUTER_COLS), lambda i: (i, 0)),
                   pl.BlockSpec((1, ROUTER_COLS), lambda i: (0, 0))],
        out_shape=[jax.ShapeDtypeStruct((TOKENS, ROUTER_COLS), F32),
                   jax.ShapeDtypeStruct((1, ROUTER_COLS), F32)],
        scratch_shapes=[pltpu.VMEM((1, ROUTER_COLS), F32)],
        compiler_params=_cparams(("arbitrary",)),
        name="route",
    )(logits)


def _dispatch_plan(info, counts):
    expert = info[:, INFO_EXPERT:INFO_EXPERT + MOE_TOP_K].astype(jnp.int32)
    rank = info[:, INFO_RANK:INFO_RANK + MOE_TOP_K].astype(jnp.int32)
    cnt = counts[0, EXPERT_LANE0:EXPERT_LANE0 + N_EXPERTS].astype(jnp.int32)
    padded = ((cnt + MOE_ROWS - 1) // MOE_ROWS) * MOE_ROWS
    pends = jnp.cumsum(padded)
    pstarts = pends - padded
    ids = jnp.arange(N_EXPERTS, dtype=jnp.int32)
    dest = rank + jnp.sum(jnp.where(expert[..., None] == ids, pstarts, 0), axis=-1)
    n_used = pends[-1] // MOE_ROWS
    blocks = jnp.arange(MOE_BLOCKS, dtype=jnp.int32)
    block_e = jnp.sum((pends[None, :] <= (blocks * MOE_ROWS)[:, None]).astype(jnp.int32), axis=1)
    block_e = jnp.minimum(block_e, N_EXPERTS - 1)
    last_e = jnp.sum(jnp.where(blocks == n_used - 1, block_e, 0))
    block_e = jnp.where(blocks < n_used, block_e, last_e)
    return dest.reshape(TOKENS * MOE_TOP_K), block_e, n_used.reshape(1)


def _invert_kernel(dest_ref, tok_ref):
    def clear(i, c):
        tok_ref[i] = 0
        return c

    lax.fori_loop(0, MOE_BLOCKS * MOE_ROWS, clear, 0, unroll=8)

    def place(i, c):
        tok_ref[dest_ref[i]] = lax.shift_right_logical(i, 1)
        return c

    lax.fori_loop(0, TOKENS * MOE_TOP_K, place, 0, unroll=8)


def _invert(dest):
    return pl.pallas_call(
        _invert_kernel,
        in_specs=[pl.BlockSpec(memory_space=pltpu.SMEM)],
        out_specs=pl.BlockSpec(memory_space=pltpu.SMEM),
        out_shape=jax.ShapeDtypeStruct((MOE_BLOCKS * MOE_ROWS,), jnp.int32),
        name="invert",
    )(dest)


def _moe_kernel(tok_ref, be_ref, nu_ref, hp_ref, wg_ref, wu_ref, wd_ref, y_ref, xbuf, sem):
    i = pl.program_id(0)
    n_used = nu_ref[0]

    def gather(block, slot):
        base = block * MOE_ROWS
        for r in range(MOE_ROWS):
            pltpu.make_async_copy(hp_ref.at[tok_ref[base + r]], xbuf.at[slot, r], sem.at[slot]).start()

    @pl.when(i == 0)
    def _():
        gather(0, 0)

    @pl.when(i + 1 < n_used)
    def _():
        gather(i + 1, (i + 1) % 2)

    @pl.when(i < n_used)
    def _():
        slot = i % 2
        pltpu.make_async_copy(hp_ref.at[pl.ds(0, MOE_ROWS)], xbuf.at[slot], sem.at[slot]).wait()
        xb = _unpack_rows(xbuf.at[slot]).astype(BF16)
        a = jnp.dot(xb, wg_ref[0].astype(BF16), preferred_element_type=F32)
        u = jnp.dot(xb, wu_ref[0].astype(BF16), preferred_element_type=F32)
        act = (a * jax.nn.sigmoid(a) * u).astype(BF16)
        _pack_rows(jnp.dot(act, wd_ref[0].astype(BF16), preferred_element_type=F32), y_ref)

    @pl.when(i >= n_used)
    def _():
        y_ref[...] = jnp.zeros_like(y_ref)


def _moe(row_tok, block_e, n_used, hp, w_gate, w_up, w_down):
    grid_spec = pltpu.PrefetchScalarGridSpec(
        num_scalar_prefetch=3,
        grid=(MOE_BLOCKS,),
        in_specs=[
            pl.BlockSpec(memory_space=pl.ANY),
            pl.BlockSpec((1, D_MODEL, D_EXPERT), lambda i, tok, be, nu: (be[i], 0, 0)),
            pl.BlockSpec((1, D_MODEL, D_EXPERT), lambda i, tok, be, nu: (be[i], 0, 0)),
            pl.BlockSpec((1, D_EXPERT, D_MODEL), lambda i, tok, be, nu: (be[i], 0, 0)),
        ],
        out_specs=pl.BlockSpec((MOE_ROWS, PACK_SUB, LANES), lambda i, tok, be, nu: (i, 0, 0)),
        scratch_shapes=[pltpu.VMEM((2, MOE_ROWS, PACK_SUB, LANES), F32), pltpu.SemaphoreType.DMA((2,))],
    )
    return pl.pallas_call(
        _moe_kernel,
        grid_spec=grid_spec,
        out_shape=jax.ShapeDtypeStruct((MOE_BLOCKS * MOE_ROWS, PACK_SUB, LANES), F32),
        compiler_params=_cparams(("arbitrary",)),
        name="moe",
    )(row_tok, block_e, n_used, hp, w_gate, w_up, w_down)


COMBINE_TM = 256


def _combine_kernel(dest_ref, x1_ref, info_ref, g_ref, y_ref, o_ref, ybuf, sem):
    i = pl.program_id(0)
    n = pl.num_programs(0)

    def gather(tile, slot):
        base = tile * (COMBINE_TM * MOE_TOP_K)
        for r in range(COMBINE_TM):
            for k in range(MOE_TOP_K):
                row = dest_ref[base + r * MOE_TOP_K + k]
                pltpu.make_async_copy(y_ref.at[row], ybuf.at[slot, k, r], sem.at[slot]).start()

    @pl.when(i == 0)
    def _():
        gather(0, 0)

    @pl.when(i + 1 < n)
    def _():
        gather(i + 1, (i + 1) % 2)

    slot = i % 2
    for k in range(MOE_TOP_K):
        pltpu.make_async_copy(y_ref.at[pl.ds(0, COMBINE_TM)], ybuf.at[slot, k], sem.at[slot]).wait()
    x2 = x1_ref[...]
    for k in range(MOE_TOP_K):
        x2 = x2 + info_ref[:, INFO_GATE + k:INFO_GATE + k + 1] * _unpack_rows(ybuf.at[slot, k])
    inv = lax.rsqrt(jnp.mean(x2 * x2, axis=-1, keepdims=True) + RMS_EPS)
    o_ref[...] = x2 * inv * g_ref[...]


def _combine(dest, x1, info, g, y_pad):
    tm = COMBINE_TM
    grid_spec = pltpu.PrefetchScalarGridSpec(
        num_scalar_prefetch=1,
        grid=(TOKENS // tm,),
        in_specs=[
            pl.BlockSpec((tm, D_MODEL), lambda i, d: (i, 0)),
            pl.BlockSpec((tm, ROUTER_COLS), lambda i, d: (i, 0)),
            pl.BlockSpec((1, D_MODEL), lambda i, d: (0, 0)),
            pl.BlockSpec(memory_space=pl.ANY),
        ],
        out_specs=pl.BlockSpec((tm, D_MODEL), lambda i, d: (i, 0)),
        scratch_shapes=[pltpu.VMEM((2, MOE_TOP_K, tm, PACK_SUB, LANES), F32), pltpu.SemaphoreType.DMA((2,))],
    )
    return pl.pallas_call(
        _combine_kernel,
        grid_spec=grid_spec,
        out_shape=jax.ShapeDtypeStruct((TOKENS, D_MODEL), F32),
        compiler_params=_cparams(("arbitrary",)),
        name="combine",
    )(dest, x1, info, g, y_pad)


def kernel(x, mix_norm_g, w_in, w_fourier_out, ssm_A_re, ssm_A_im, ssm_log_dt, ssm_B_re, ssm_B_im, ssm_C_re,
           ssm_C_im, ssm_D, ssm_w_glu, w_ssm_out, w_out, ffn_norm_g, router_group_w, router_group_b,
           router_expert_w, router_expert_b, expert_w_gate, expert_w_up, expert_w_down, final_norm_g):
    assert x.shape == (BATCH, SEQ, D_MODEL) and w_in.shape[0] == 1
    f1, f2, twr, twi, cdft = _dft_constants()
    e_qc, e_rn = _replication_matrices()

    vf, us, gates = _inproj(x, mix_norm_g[0][None], w_in[0].astype(BF16), cdft)
    a = _dft1(vf.reshape(BATCH, 2 * DFT_RADIX, DFT_RADIX * FOURIER_WIDTH), f1, twr, twi)
    fmix = _dft2(a.reshape(BATCH, 2 * DFT_RADIX, DFT_RADIX * FOURIER_WIDTH), f2)
    fmix = fmix.reshape(TOKENS, FOURIER_WIDTH)

    kc, wsc, woc, aq = _ssm_operators(ssm_A_re[0], ssm_A_im[0], ssm_log_dt[0], ssm_B_re[0], ssm_B_im[0],
                                      ssm_C_re[0], ssm_C_im[0])
    xr = us.reshape(SSM_LANE_BLOCKS, BATCH * N_CHUNKS, CHUNK_COLS)
    sre, sim = _ssm_states(xr, wsc, e_rn)
    hre, him = _ssm_scan(sre, sim, aq)
    yconv = _ssm_out(xr, kc, woc, e_qc, hre, him).reshape(SSM_LANE_BLOCKS, TOKENS, LANES)

    w_router = jnp.concatenate([router_group_w[0], router_expert_w[0]], axis=1)
    w_router = jnp.pad(w_router, ((0, 0), (0, ROUTER_COLS - w_router.shape[1])))
    b_router = jnp.concatenate([router_group_b[0], router_expert_b[0]])
    b_router = jnp.pad(b_router, (0, ROUTER_COLS - b_router.shape[0]))[None]
    wr_hi = w_router.astype(BF16)
    wr_lo = (w_router - wr_hi.astype(F32)).astype(BF16)
    x1, hp, logits = _merge(x.reshape(TOKENS, D_MODEL), fmix, yconv, us, gates, ssm_D[0][None],
                            w_fourier_out[0].astype(BF16), ssm_w_glu[0].astype(BF16), w_ssm_out[0].astype(BF16),
                            w_out[0].astype(BF16), ffn_norm_g[0][None], wr_hi, wr_lo, b_router)

    info, counts = _route(logits)
    dest, block_e, n_used = _dispatch_plan(info, counts)
    row_tok = _invert(dest)
    y_pad = _moe(row_tok, block_e, n_used, hp, expert_w_gate[0], expert_w_up[0], expert_w_down[0])
    out = _combine(dest, x1, info, final_norm_g[None], y_pad)
    return out.reshape(BATCH, SEQ, D_MODEL)
```

```python
import math

import numpy as np
import jax
import jax.numpy as jnp
from jax import lax
from jax.experimental import pallas as pl
from jax.experimental.pallas import tpu as pltpu

F32 = jnp.float32
BF16 = jnp.bfloat16

D_MODEL = 1024
BATCH = 4
SEQ = 4096
TOKENS = BATCH * SEQ
FOURIER_WIDTH = 512
FOURIER_GROUP_CH = 128
FOURIER_GROUPS = 4
SSM_WIDTH = 512
SSM_GROUP_CH = 16
SSM_GROUPS = 32
SSM_STATE = 64
MOE_GROUPS = 8
EXPERTS_PER_GROUP = 8
N_EXPERTS = 64
MOE_TOP_K = 2
D_EXPERT = 512
RMS_EPS = 1e-6

LANES = 128
DFT_RADIX = 64
SSM_CHUNK = 16
SSM_LANE_BLOCKS = SSM_WIDTH // LANES
GROUPS_PER_BLOCK = LANES // SSM_GROUP_CH
CHUNK_COLS = SSM_CHUNK * LANES
N_CHUNKS = SEQ // SSM_CHUNK
STATE_COLS = GROUPS_PER_BLOCK * SSM_STATE
MOE_ROWS = 256
MOE_BLOCKS = TOKENS * MOE_TOP_K // MOE_ROWS + N_EXPERTS
ROUTER_COLS = 128
VMEM_LIMIT = 48 * 1024 * 1024


def _cparams(sem, vmem=VMEM_LIMIT):
    return pltpu.CompilerParams(dimension_semantics=sem, vmem_limit_bytes=vmem)


IN_TM = 512


def _inproj_kernel(x_ref, g_ref, w_ref, cdft_ref, vf_ref, us_ref, gates_ref):
    x = x_ref[0]
    inv = lax.rsqrt(jnp.mean(x * x, axis=-1, keepdims=True) + RMS_EPS)
    h = (x * inv * g_ref[...]).astype(BF16)
    zf = jnp.dot(h, w_ref[:, 0:FOURIER_WIDTH], preferred_element_type=F32).astype(BF16)
    cdft = cdft_ref[...].astype(BF16)
    for g in range(FOURIER_GROUPS):
        sl = slice(g * LANES, (g + 1) * LANES)
        v = jnp.dot(zf[:, sl], cdft, preferred_element_type=F32)
        vf_ref[0, 0, :, sl] = v[:, :LANES].astype(BF16)
        vf_ref[0, 1, :, sl] = v[:, LANES:].astype(BF16)
    zs = jnp.dot(h, w_ref[:, FOURIER_WIDTH:FOURIER_WIDTH + SSM_WIDTH], preferred_element_type=F32)
    for j in range(SSM_LANE_BLOCKS):
        us_ref[j] = zs[:, j * LANES:(j + 1) * LANES].astype(BF16)
    base = FOURIER_WIDTH + SSM_WIDTH
    for n in range(4):
        zg = jnp.dot(h, w_ref[:, base + n * 512: base + (n + 1) * 512], preferred_element_type=F32)
        gates_ref[:, n * 512:(n + 1) * 512] = jax.nn.sigmoid(zg).astype(BF16)


def _inproj(x, g, w_in, cdft):
    nt = SEQ // IN_TM
    return pl.pallas_call(
        _inproj_kernel,
        grid=(BATCH, nt),
        in_specs=[
            pl.BlockSpec((1, IN_TM, D_MODEL), lambda b, i: (b, i, 0)),
            pl.BlockSpec((1, D_MODEL), lambda b, i: (0, 0)),
            pl.BlockSpec(w_in.shape, lambda b, i: (0, 0)),
            pl.BlockSpec(cdft.shape, lambda b, i: (0, 0)),
        ],
        out_specs=[
            pl.BlockSpec((1, 2, IN_TM, FOURIER_WIDTH), lambda b, i: (b, 0, i, 0)),
            pl.BlockSpec((SSM_LANE_BLOCKS, IN_TM, LANES), lambda b, i: (0, b * nt + i, 0)),
            pl.BlockSpec((IN_TM, 2 * D_MODEL), lambda b, i: (b * nt + i, 0)),
        ],
        out_shape=[
            jax.ShapeDtypeStruct((BATCH, 2, SEQ, FOURIER_WIDTH), BF16),
            jax.ShapeDtypeStruct((SSM_LANE_BLOCKS, TOKENS, LANES), BF16),
            jax.ShapeDtypeStruct((TOKENS, 2 * D_MODEL), BF16),
        ],
        compiler_params=_cparams(("parallel", "parallel")),
        name="inproj",
    )(x, g, w_in, cdft)


DFT_S2_PER_STEP = 8


def _dft1_kernel(v_ref, f_ref, twr_ref, twi_ref, a_ref):
    f1 = f_ref[...].astype(BF16)
    for s in range(DFT_S2_PER_STEP):
        blk = v_ref[0, :, s * FOURIER_WIDTH:(s + 1) * FOURIER_WIDTH]
        r = jnp.dot(f1, blk, preferred_element_type=F32)
        ar, ai = r[:DFT_RADIX], r[DFT_RADIX:]
        tr = jnp.tile(twr_ref[s], (1, FOURIER_WIDTH // LANES))
        ti = jnp.tile(twi_ref[s], (1, FOURIER_WIDTH // LANES))
        a_ref[0, 0, s] = (ar * tr - ai * ti).astype(BF16)
        a_ref[0, 1, s] = (ar * ti + ai * tr).astype(BF16)


def _dft1(v, f1, twr, twi):
    ns = DFT_RADIX // DFT_S2_PER_STEP
    cols = DFT_S2_PER_STEP * FOURIER_WIDTH
    return pl.pallas_call(
        _dft1_kernel,
        grid=(BATCH, ns),
        in_specs=[
            pl.BlockSpec((1, 2 * DFT_RADIX, cols), lambda b, s: (b, 0, s)),
            pl.BlockSpec(f1.shape, lambda b, s: (0, 0)),
            pl.BlockSpec((DFT_S2_PER_STEP, DFT_RADIX, LANES), lambda b, s: (s, 0, 0)),
            pl.BlockSpec((DFT_S2_PER_STEP, DFT_RADIX, LANES), lambda b, s: (s, 0, 0)),
        ],
        out_specs=pl.BlockSpec((1, 2, DFT_S2_PER_STEP, DFT_RADIX, FOURIER_WIDTH),
                               lambda b, s: (b, 0, s, 0, 0)),
        out_shape=jax.ShapeDtypeStruct((BATCH, 2, DFT_RADIX, DFT_RADIX, FOURIER_WIDTH), BF16),
        compiler_params=_cparams(("parallel", "parallel")),
        name="dft1",
    )(v, f1, twr, twi)


DFT2_COLS = 4096


def _dft2_kernel(a_ref, f_ref, o_ref):
    o_ref[0] = jnp.dot(f_ref[...].astype(BF16), a_ref[0], preferred_element_type=F32).astype(BF16)


def _dft2(a, f2):
    n = DFT_RADIX * FOURIER_WIDTH
    return pl.pallas_call(
        _dft2_kernel,
        grid=(BATCH, n // DFT2_COLS),
        in_specs=[
            pl.BlockSpec((1, 2 * DFT_RADIX, DFT2_COLS), lambda b, s: (b, 0, s)),
            pl.BlockSpec(f2.shape, lambda b, s: (0, 0)),
        ],
        out_specs=pl.BlockSpec((1, DFT_RADIX, DFT2_COLS), lambda b, s: (b, 0, s)),
        out_shape=jax.ShapeDtypeStruct((BATCH, DFT_RADIX, n), BF16),
        compiler_params=_cparams(("parallel", "parallel")),
        name="dft2",
    )(a, f2)


def _dft_constants():
    r = DFT_RADIX
    k = np.arange(r)
    ang = 2.0 * np.pi * np.outer(k, k) / r
    c, s = np.cos(ang), np.sin(ang)
    f1 = np.block([[c, s], [-s, c]])
    f2 = np.concatenate([c, s], axis=1)
    tw = 2.0 * np.pi * np.outer(k, k) / SEQ
    scale = 1.0 / math.sqrt(SEQ)
    twr = np.repeat((np.cos(tw) * scale)[:, :, None], LANES, axis=2)
    twi = np.repeat((-np.sin(tw) * scale)[:, :, None], LANES, axis=2)
    kc = np.arange(FOURIER_GROUP_CH)
    angc = 2.0 * np.pi * np.outer(kc, kc) / FOURIER_GROUP_CH
    cs = 1.0 / math.sqrt(FOURIER_GROUP_CH)
    cdft = np.concatenate([np.cos(angc) * cs, -np.sin(angc) * cs], axis=1)
    return tuple(jnp.asarray(v, F32) for v in (f1, f2, twr, twi, cdft))


def _ssm_operators(a_re, a_im, log_dt, b_re, b_im, c_re, c_im):
    q_len = SSM_CHUNK
    jb, gb, ch, ns = SSM_LANE_BLOCKS, GROUPS_PER_BLOCK, SSM_GROUP_CH, SSM_STATE
    dt = jnp.exp(log_dt)[..., None]
    lr, li = a_re * dt, a_im * dt
    steps = jnp.arange(q_len + 1, dtype=F32)
    mag = jnp.exp(lr[..., None] * steps)
    ang = li[..., None] * steps
    pr, pi = mag * jnp.cos(ang), mag * jnp.sin(ang)
    ar, ai = pr[..., 1], pi[..., 1]
    den = a_re * a_re + a_im * a_im
    cr = ((ar - 1.0) * a_re + ai * a_im) / den
    ci = (ai * a_re - (ar - 1.0) * a_im) / den
    bbr = cr[..., None] * b_re - ci[..., None] * b_im
    bbi = cr[..., None] * b_im + ci[..., None] * b_re
    prq, piq = pr[..., :q_len], pi[..., :q_len]
    cpr = jnp.einsum('dgcn,dgnt->dgtcn', c_re, prq) - jnp.einsum('dgcn,dgnt->dgtcn', c_im, piq)
    cpi = jnp.einsum('dgcn,dgnt->dgtcn', c_re, piq) + jnp.einsum('dgcn,dgnt->dgtcn', c_im, prq)
    kern = jnp.einsum('dgtcn,dgne->dgtce', cpr, bbr) - jnp.einsum('dgtcn,dgne->dgtce', cpi, bbi)
    qi = jnp.arange(q_len)
    lag = qi[None, :] - qi[:, None]
    k_f = kern[0][:, jnp.maximum(lag, 0)] * (lag >= 0)[None, :, :, None, None].astype(F32)
    k_b = kern[1][:, jnp.maximum(-lag, 0)] * (lag <= 0)[None, :, :, None, None].astype(F32)
    kc = (k_f + k_b).reshape(jb, gb, q_len, q_len, ch, ch)
    kc = jnp.transpose(kc, (0, 2, 1, 5, 3, 4)).reshape(jb, CHUNK_COLS, q_len * ch)

    def state_in(pwr, pwi, br, bi):
        wr = jnp.einsum('gnq,gne->gqen', pwr, br) - jnp.einsum('gnq,gne->gqen', pwi, bi)
        wi = jnp.einsum('gnq,gne->gqen', pwr, bi) + jnp.einsum('gnq,gne->gqen', pwi, br)
        return jnp.stack([wr, wi])

    rev = q_len - 1 - qi
    wsc = jnp.stack([state_in(pr[0][..., rev], pi[0][..., rev], bbr[0], bbi[0]),
                     state_in(pr[1][..., qi], pi[1][..., qi], bbr[1], bbi[1])])
    wsc = wsc.reshape(2, 2, jb, gb, q_len, ch, ns)
    wsc = jnp.transpose(wsc, (2, 4, 3, 5, 0, 1, 6)).reshape(jb, CHUNK_COLS, 4 * ns)

    def state_out(pwr, pwi, c_r, c_i):
        o_r = jnp.einsum('gcn,gnq->gnqc', c_r, pwr) - jnp.einsum('gcn,gnq->gnqc', c_i, pwi)
        o_i = jnp.einsum('gcn,gnq->gnqc', c_r, pwi) + jnp.einsum('gcn,gnq->gnqc', c_i, pwr)
        return jnp.stack([o_r, -o_i])

    woc = jnp.stack([state_out(pr[0][..., qi + 1], pi[0][..., qi + 1], c_re[0], c_im[0]),
                     state_out(pr[1][..., q_len - qi], pi[1][..., q_len - qi], c_re[1], c_im[1])])
    woc = woc.reshape(2, 2, jb, gb, ns, q_len * ch)
    woc = jnp.transpose(woc, (2, 0, 1, 3, 4, 5)).reshape(jb, 4 * STATE_COLS, q_len * ch)

    aq = jnp.stack([pr[0][..., q_len], pi[0][..., q_len], pr[1][..., q_len], pi[1][..., q_len]])
    aq = aq.reshape(4, jb, 1, STATE_COLS)
    aq = jnp.broadcast_to(aq, (4, jb, BATCH, STATE_COLS))
    aq = jnp.transpose(aq, (1, 0, 2, 3)).reshape(jb, 4, BATCH * STATE_COLS)
    return kc.astype(BF16), wsc.astype(BF16), woc.astype(BF16), aq


EXPAND_ROWS = 256
GROUP_SHIFT_CH = 4
GROUP_SHIFT_STATE = 6


def _expand_block_diag(compact_ref, sel_ref, out_ref, row_shift, col_shift):
    n_rows, n_cols = out_ref.shape
    col_g = (lax.broadcasted_iota(jnp.int32, (EXPAND_ROWS, n_cols), 1) >> col_shift) & (GROUPS_PER_BLOCK - 1)
    for r in range(n_rows // EXPAND_ROWS):
        rows = slice(r * EXPAND_ROWS, (r + 1) * EXPAND_ROWS)
        t = jnp.dot(compact_ref[rows, :], sel_ref[...], preferred_element_type=F32)
        row_i = lax.broadcasted_iota(jnp.int32, (EXPAND_ROWS, n_cols), 0) + r * EXPAND_ROWS
        row_g = (row_i >> row_shift) & (GROUPS_PER_BLOCK - 1)
        out_ref[rows, :] = jnp.where(row_g == col_g, t, 0.0).astype(BF16)


def _replication_matrices():
    half = SSM_CHUNK // 2
    e_qc = np.zeros((half, SSM_GROUP_CH, half, GROUPS_PER_BLOCK, SSM_GROUP_CH), np.float32)
    for q in range(half):
        for c in range(SSM_GROUP_CH):
            e_qc[q, c, q, :, c] = 1.0
    e_rn = np.zeros((2, SSM_STATE, 2, GROUPS_PER_BLOCK, SSM_STATE), np.float32)
    for r in range(2):
        for n in range(SSM_STATE):
            e_rn[r, n, r, :, n] = 1.0
    return (jnp.asarray(e_qc.reshape(half * SSM_GROUP_CH, -1), BF16),
            jnp.asarray(e_rn.reshape(2 * SSM_STATE, -1), BF16))


def _ssm_state_kernel(x_ref, wsc_ref, sel_ref, sre_ref, sim_ref, ws_scr):
    @pl.when(pl.program_id(2) == 0)
    def _():
        _expand_block_diag(wsc_ref.at[0], sel_ref, ws_scr, GROUP_SHIFT_CH, GROUP_SHIFT_STATE)

    s = jnp.dot(x_ref[0], ws_scr[...], preferred_element_type=F32)
    sre_ref[0] = s[:, :STATE_COLS]
    sim_ref[0] = s[:, STATE_COLS:]


def _ssm_states(xr, wsc, e_rn):
    out_spec = pl.BlockSpec((1, N_CHUNKS, STATE_COLS), lambda j, d, b: (j, 0, d * BATCH + b))
    shape = jax.ShapeDtypeStruct((SSM_LANE_BLOCKS, N_CHUNKS, 2 * BATCH * STATE_COLS), F32)
    return pl.pallas_call(
        _ssm_state_kernel,
        grid=(SSM_LANE_BLOCKS, 2, BATCH),
        in_specs=[
            pl.BlockSpec((1, N_CHUNKS, CHUNK_COLS), lambda j, d, b: (j, b, 0)),
            pl.BlockSpec((1, CHUNK_COLS, 2 * SSM_STATE), lambda j, d, b: (j, 0, d)),
            pl.BlockSpec(e_rn.shape, lambda j, d, b: (0, 0)),
        ],
        out_specs=[out_spec, out_spec],
        out_shape=[shape, shape],
        scratch_shapes=[pltpu.VMEM((CHUNK_COLS, 2 * STATE_COLS), BF16)],
        compiler_params=_cparams(("parallel", "parallel", "arbitrary")),
        name="ssm_states",
    )(xr, wsc, e_rn)


def _ssm_scan_kernel(sre_ref, sim_ref, aq_ref, hre_ref, him_ref):
    half = BATCH * STATE_COLS
    fwd, bwd = slice(0, half), slice(half, 2 * half)
    a0r, a0i, a1r, a1i = aq_ref[0, 0:1], aq_ref[0, 1:2], aq_ref[0, 2:3], aq_ref[0, 3:4]

    def body(k, carry):
        fr, fi, br, bi = carry
        kb = N_CHUNKS - 1 - k
        hre_ref[0, pl.ds(k, 1), fwd] = fr
        him_ref[0, pl.ds(k, 1), fwd] = fi
        hre_ref[0, pl.ds(kb, 1), bwd] = br
        him_ref[0, pl.ds(kb, 1), bwd] = bi
        sfr, sfi = sre_ref[0, pl.ds(k, 1), fwd], sim_ref[0, pl.ds(k, 1), fwd]
        sbr, sbi = sre_ref[0, pl.ds(kb, 1), bwd], sim_ref[0, pl.ds(kb, 1), bwd]
        return (a0r * fr - a0i * fi + sfr, a0r * fi + a0i * fr + sfi,
                a1r * br - a1i * bi + sbr, a1r * bi + a1i * br + sbi)

    z = jnp.zeros((1, half), F32)
    lax.fori_loop(0, N_CHUNKS, body, (z, z, z, z))


def _ssm_scan(sre, sim, aq):
    spec = pl.BlockSpec((1, N_CHUNKS, 2 * BATCH * STATE_COLS), lambda j: (j, 0, 0))
    shape = jax.ShapeDtypeStruct(sre.shape, F32)
    return pl.pallas_call(
        _ssm_scan_kernel,
        grid=(SSM_LANE_BLOCKS,),
        in_specs=[spec, spec, pl.BlockSpec((1, 4, BATCH * STATE_COLS), lambda j: (j, 0, 0))],
        out_specs=[spec, spec],
        out_shape=[shape, shape],
        compiler_params=_cparams(("parallel",)),
        name="ssm_scan",
    )(sre, sim, aq)


SSM_OUT_TN = CHUNK_COLS // 2


def _ssm_out_kernel(x_ref, kc_ref, woc_ref, sel_ref, h0r_ref, h0i_ref, h1r_ref, h1i_ref, y_ref, m_scr, wo_scr):
    @pl.when(pl.program_id(2) == 0)
    def _():
        _expand_block_diag(kc_ref.at[0], sel_ref, m_scr, GROUP_SHIFT_CH, GROUP_SHIFT_CH)
        _expand_block_diag(woc_ref.at[0], sel_ref, wo_scr, GROUP_SHIFT_STATE, GROUP_SHIFT_CH)

    acc = jnp.dot(x_ref[0], m_scr[...], preferred_element_type=F32)
    for i, h_ref in enumerate((h0r_ref, h0i_ref, h1r_ref, h1i_ref)):
        acc += jnp.dot(h_ref[0].astype(BF16), wo_scr[i * STATE_COLS:(i + 1) * STATE_COLS, :],
                       preferred_element_type=F32)
    y_ref[0] = acc.astype(BF16)


def _ssm_out(xr, kc, woc, e_qc, hre, him):
    def hspec(d):
        return pl.BlockSpec((1, N_CHUNKS, STATE_COLS), lambda j, n, b: (j, 0, d * BATCH + b))
    half_cols = kc.shape[2] // 2
    return pl.pallas_call(
        _ssm_out_kernel,
        grid=(SSM_LANE_BLOCKS, CHUNK_COLS // SSM_OUT_TN, BATCH),
        in_specs=[
            pl.BlockSpec((1, N_CHUNKS, CHUNK_COLS), lambda j, n, b: (j, b, 0)),
            pl.BlockSpec((1, CHUNK_COLS, half_cols), lambda j, n, b: (j, 0, n)),
            pl.BlockSpec((1, 4 * STATE_COLS, half_cols), lambda j, n, b: (j, 0, n)),
            pl.BlockSpec(e_qc.shape, lambda j, n, b: (0, 0)),
            hspec(0), hspec(0), hspec(1), hspec(1),
        ],
        out_specs=pl.BlockSpec((1, N_CHUNKS, SSM_OUT_TN), lambda j, n, b: (j, b, n)),
        out_shape=jax.ShapeDtypeStruct((SSM_LANE_BLOCKS, BATCH * N_CHUNKS, CHUNK_COLS), BF16),
        scratch_shapes=[pltpu.VMEM((CHUNK_COLS, SSM_OUT_TN), BF16), pltpu.VMEM((4 * STATE_COLS, SSM_OUT_TN), BF16)],
        compiler_params=_cparams(("parallel", "parallel", "arbitrary")),
        name="ssm_out",
    )(xr, kc, woc, e_qc, hre, him, hre, him)


MERGE_TM = 512
GELU_C = math.sqrt(2.0 / math.pi)
PACK_SUB = D_MODEL // LANES


def _split_bf16(v):
    hi = v.astype(BF16)
    lo = (v - hi.astype(F32)).astype(BF16)
    return hi, lo


def _pack_rows(v, out_ref):
    for s in range(PACK_SUB):
        out_ref[pl.ds(s, v.shape[0], stride=PACK_SUB), :] = v[:, s * LANES:(s + 1) * LANES]


def _unpack_rows(buf_ref, start, rows):
    return jnp.concatenate([buf_ref[pl.ds(start + s, rows, stride=PACK_SUB), :] for s in range(PACK_SUB)], axis=1)


def _merge_kernel(x_ref, fm_ref, yc_ref, us_ref, gates_ref, dskip_ref, wf_ref, wglu_ref, ws_ref, wo_ref,
                  ng_ref, wrh_ref, wrl_ref, rb_ref, x1_ref, hp_ref, logit_ref):
    conv = jnp.concatenate([yc_ref[j].astype(F32) for j in range(SSM_LANE_BLOCKS)], axis=-1)
    u = jnp.concatenate([us_ref[j].astype(F32) for j in range(SSM_LANE_BLOCKS)], axis=-1)
    y = conv + dskip_ref[...] * u
    y = 0.5 * y * (1.0 + jnp.tanh(GELU_C * (y + 0.044715 * (y * y * y))))
    glu = jax.nn.sigmoid(jnp.dot(y.astype(BF16), wglu_ref[...], preferred_element_type=F32))
    y_s = jnp.dot((y * glu).astype(BF16), ws_ref[...], preferred_element_type=F32)
    y_f = jnp.dot(fm_ref[...], wf_ref[...], preferred_element_type=F32)
    merged = (gates_ref[:, :D_MODEL].astype(F32) * y_f + gates_ref[:, D_MODEL:].astype(F32) * y_s)
    x1 = x_ref[...] + jnp.dot(merged.astype(BF16), wo_ref[...], preferred_element_type=F32)
    x1_ref[...] = x1
    inv = lax.rsqrt(jnp.mean(x1 * x1, axis=-1, keepdims=True) + RMS_EPS)
    hn = x1 * inv * ng_ref[...]
    _pack_rows(hn, hp_ref)
    hi, lo = _split_bf16(hn)
    logits = (jnp.dot(hi, wrh_ref[...], preferred_element_type=F32)
              + jnp.dot(lo, wrh_ref[...], preferred_element_type=F32)
              + jnp.dot(hi, wrl_ref[...], preferred_element_type=F32))
    logit_ref[...] = logits + rb_ref[...]


def _merge(x, fmix, yconv, us, gates, dskip, wf, wglu, ws, wo, ng, wrh, wrl, rb):
    tm = MERGE_TM
    full = lambda a: pl.BlockSpec(a.shape, lambda i: (0,) * a.ndim)
    return pl.pallas_call(
        _merge_kernel,
        grid=(TOKENS // tm,),
        in_specs=[
            pl.BlockSpec((tm, D_MODEL), lambda i: (i, 0)),
            pl.BlockSpec((tm, FOURIER_WIDTH), lambda i: (i, 0)),
            pl.BlockSpec((SSM_LANE_BLOCKS, tm, LANES), lambda i: (0, i, 0)),
            pl.BlockSpec((SSM_LANE_BLOCKS, tm, LANES), lambda i: (0, i, 0)),
            pl.BlockSpec((tm, 2 * D_MODEL), lambda i: (i, 0)),
            full(dskip), full(wf), full(wglu), full(ws), full(wo), full(ng), full(wrh), full(wrl), full(rb),
        ],
        out_specs=[
            pl.BlockSpec((tm, D_MODEL), lambda i: (i, 0)),
            pl.BlockSpec((tm * PACK_SUB, LANES), lambda i: (i, 0)),
            pl.BlockSpec((tm, ROUTER_COLS), lambda i: (i, 0)),
        ],
        out_shape=[
            jax.ShapeDtypeStruct((TOKENS, D_MODEL), F32),
            jax.ShapeDtypeStruct((TOKENS * PACK_SUB, LANES), F32),
            jax.ShapeDtypeStruct((TOKENS, ROUTER_COLS), F32),
        ],
        compiler_params=_cparams(("parallel",)),
        name="merge",
    )(x, fmix, yconv, us, gates, dskip, wf, wglu, ws, wo, ng, wrh, wrl, rb)


ROUTE_TM = 512
EXPERT_LANE0 = MOE_GROUPS
INFO_EXPERT, INFO_RANK, INFO_GATE = 0, 2, 4


def _route_kernel(lg_ref, info_ref, cnt_ref, carry):
    @pl.when(pl.program_id(0) == 0)
    def _():
        carry[...] = jnp.zeros_like(carry)

    lg = lg_ref[...]
    tm = lg.shape[0]
    col_i = lax.broadcasted_iota(jnp.int32, lg.shape, 1)
    col = col_i.astype(F32)
    neg = jnp.float32(-jnp.inf)
    none = jnp.float32(ROUTER_COLS)

    def row_max(v):
        return jnp.max(v, axis=-1, keepdims=True)

    def first_at(v, m):
        return jnp.min(jnp.where(v == m, col, none), axis=-1, keepdims=True)

    gl = jnp.where(col_i < MOE_GROUPS, lg, neg)
    gmax = row_max(gl)
    p_g = 1.0 / jnp.sum(jnp.exp(gl - gmax), axis=-1, keepdims=True)
    lo = EXPERT_LANE0 + first_at(gl, gmax) * EXPERTS_PER_GROUP
    el = jnp.where((col >= lo) & (col < lo + EXPERTS_PER_GROUP), lg, neg)
    l1 = row_max(el)
    i1 = first_at(el, l1)
    el2 = jnp.where(col == i1, neg, el)
    l2 = row_max(el2)
    i2 = first_at(el2, l2)
    r = jnp.exp(l2 - l1)
    w1 = p_g / (1.0 + r)
    w2 = w1 * r

    hit1, hit2 = col == i1, col == i2
    onehot = jnp.where(hit1 | hit2, 1.0, 0.0)
    earlier = lax.broadcasted_iota(jnp.int32, (tm, tm), 0) > lax.broadcasted_iota(jnp.int32, (tm, tm), 1)
    before = jnp.dot(jnp.where(earlier, 1.0, 0.0).astype(BF16), onehot.astype(BF16),
                     preferred_element_type=F32) + carry[...]
    rank1 = jnp.sum(jnp.where(hit1, before, 0.0), axis=-1, keepdims=True)
    rank2 = jnp.sum(jnp.where(hit2, before, 0.0), axis=-1, keepdims=True)
    carry[...] += jnp.sum(onehot, axis=0, keepdims=True)
    cnt_ref[...] = carry[...]

    info = jnp.zeros(lg.shape, F32)
    for lane, v in ((INFO_EXPERT, i1 - EXPERT_LANE0), (INFO_EXPERT + 1, i2 - EXPERT_LANE0), (INFO_RANK, rank1),
                    (INFO_RANK + 1, rank2), (INFO_GATE, w1), (INFO_GATE + 1, w2)):
        info = jnp.where(col_i == lane, v, info)
    info_ref[...] = info


def _route(logits):
    return pl.pallas_call(
        _route_kernel,
        grid=(TOKENS // ROUTE_TM,),
        in_specs=[pl.BlockSpec((ROUTE_TM, ROUTER_COLS), lambda i: (i, 0))],
        out_specs=[pl.BlockSpec((ROUTE_TM, ROUTER_COLS), lambda i: (i, 0)),
                   pl.BlockSpec((1, ROUTER_COLS), lambda i: (0, 0))],
        out_shape=[jax.ShapeDtypeStruct((TOKENS, ROUTER_COLS), F32),
                   jax.ShapeDtypeStruct((1, ROUTER_COLS), F32)],
        scratch_shapes=[pltpu.VMEM((1, ROUTER_COLS), F32)],
        compiler_params=_cparams(("arbitrary",)),
        name="route",
    )(logits)


def _dispatch_plan(info, counts):
    expert = info[:, INFO_EXPERT:INFO_EXPERT + MOE_TOP_K].astype(jnp.int32)
    rank = info[:, INFO_RANK:INFO_RANK + MOE_TOP_K].astype(jnp.int32)
    cnt = counts[0, EXPERT_LANE0:EXPERT_LANE0 + N_EXPERTS].astype(jnp.int32)
    padded = ((cnt + MOE_ROWS - 1) // MOE_ROWS) * MOE_ROWS
    pends = jnp.cumsum(padded)
    pstarts = pends - padded
    ids = jnp.arange(N_EXPERTS, dtype=jnp.int32)
    dest = rank + jnp.sum(jnp.where(expert[..., None] == ids, pstarts, 0), axis=-1)
    n_used = pends[-1] // MOE_ROWS
    blocks = jnp.arange(MOE_BLOCKS, dtype=jnp.int32)
    block_e = jnp.sum((pends[None, :] <= (blocks * MOE_ROWS)[:, None]).astype(jnp.int32), axis=1)
    block_e = jnp.minimum(block_e, N_EXPERTS - 1)
    last_e = jnp.sum(jnp.where(blocks == n_used - 1, block_e, 0))
    block_e = jnp.where(blocks < n_used, block_e, last_e)
    return dest.reshape(TOKENS * MOE_TOP_K), block_e, n_used.reshape(1)


TOK_BITS = 16
ROW_TOK_WORDS = MOE_BLOCKS * MOE_ROWS // 2


def _invert_kernel(dest_ref, tok_ref):
    def clear(i, c):
        tok_ref[i] = 0
        return c

    lax.fori_loop(0, ROW_TOK_WORDS, clear, 0, unroll=8)

    def place(i, c):
        slot = dest_ref[i]
        word = lax.shift_right_logical(slot, 1)
        token = lax.shift_right_logical(i, 1)
        tok_ref[word] = tok_ref[word] | lax.shift_left(token, (slot & 1) * TOK_BITS)
        return c

    lax.fori_loop(0, TOKENS * MOE_TOP_K, place, 0, unroll=8)


def _invert(dest):
    return pl.pallas_call(
        _invert_kernel,
        in_specs=[pl.BlockSpec(memory_space=pltpu.SMEM)],
        out_specs=pl.BlockSpec(memory_space=pltpu.SMEM),
        out_shape=jax.ShapeDtypeStruct((ROW_TOK_WORDS,), jnp.int32),
        name="invert",
    )(dest)


MOE_SLOT_ROWS = MOE_ROWS * PACK_SUB


def _moe_kernel(tok_ref, be_ref, nu_ref, hp_ref, wg_ref, wu_ref, wd_ref, y_ref, xbuf, sem):
    i = pl.program_id(0)
    n_used = nu_ref[0]

    def slot_rows(slot):
        return xbuf.at[pl.ds(pl.multiple_of(slot * MOE_SLOT_ROWS, MOE_SLOT_ROWS), MOE_SLOT_ROWS), :]

    def gather(block, slot):
        first_word = block * (MOE_ROWS // 2)
        for r in range(MOE_ROWS):
            if r % 2 == 0:
                word = tok_ref[first_word + r // 2]
                tok = word & ((1 << TOK_BITS) - 1)
            else:
                tok = lax.shift_right_logical(word, TOK_BITS)
            src = hp_ref.at[pl.ds(pl.multiple_of(tok * PACK_SUB, PACK_SUB), PACK_SUB), :]
            dst = xbuf.at[pl.ds(pl.multiple_of(slot * MOE_SLOT_ROWS + r * PACK_SUB, PACK_SUB), PACK_SUB), :]
            pltpu.make_async_copy(src, dst, sem.at[slot]).start()

    @pl.when(i == 0)
    def _():
        gather(0, 0)

    @pl.when(i + 1 < n_used)
    def _():
        gather(i + 1, (i + 1) % 2)

    @pl.when(i < n_used)
    def _():
        slot = i % 2
        pltpu.make_async_copy(slot_rows(slot), slot_rows(slot), sem.at[slot]).wait()
        xb = _unpack_rows(xbuf, slot * MOE_SLOT_ROWS, MOE_ROWS).astype(BF16)
        a = jnp.dot(xb, wg_ref[0].astype(BF16), preferred_element_type=F32)
        u = jnp.dot(xb, wu_ref[0].astype(BF16), preferred_element_type=F32)
        act = (a * jax.nn.sigmoid(a) * u).astype(BF16)
        _pack_rows(jnp.dot(act, wd_ref[0].astype(BF16), preferred_element_type=F32), y_ref)

    @pl.when(i >= n_used)
    def _():
        y_ref[...] = jnp.zeros_like(y_ref)


def _moe(row_tok, block_e, n_used, hp, w_gate, w_up, w_down):
    grid_spec = pltpu.PrefetchScalarGridSpec(
        num_scalar_prefetch=3,
        grid=(MOE_BLOCKS,),
        in_specs=[
            pl.BlockSpec(memory_space=pl.ANY),
            pl.BlockSpec((1, D_MODEL, D_EXPERT), lambda i, tok, be, nu: (be[i], 0, 0)),
            pl.BlockSpec((1, D_MODEL, D_EXPERT), lambda i, tok, be, nu: (be[i], 0, 0)),
            pl.BlockSpec((1, D_EXPERT, D_MODEL), lambda i, tok, be, nu: (be[i], 0, 0)),
        ],
        out_specs=pl.BlockSpec((MOE_SLOT_ROWS, LANES), lambda i, tok, be, nu: (i, 0)),
        scratch_shapes=[pltpu.VMEM((2 * MOE_SLOT_ROWS, LANES), F32), pltpu.SemaphoreType.DMA((2,))],
    )
    return pl.pallas_call(
        _moe_kernel,
        grid_spec=grid_spec,
        out_shape=jax.ShapeDtypeStruct((MOE_BLOCKS * MOE_SLOT_ROWS, LANES), F32),
        compiler_params=_cparams(("arbitrary",)),
        name="moe",
    )(row_tok, block_e, n_used, hp, w_gate, w_up, w_down)


COMBINE_TM = 256


def _combine_kernel(dest_ref, x1_ref, info_ref, g_ref, y_ref, o_ref, ybuf, sem):
    i = pl.program_id(0)
    n = pl.num_programs(0)
    k_rows = COMBINE_TM * PACK_SUB
    slot_rows = MOE_TOP_K * k_rows

    def slot_ref(slot):
        return ybuf.at[pl.ds(pl.multiple_of(slot * slot_rows, slot_rows), slot_rows), :]

    def gather(tile, slot):
        base = tile * (COMBINE_TM * MOE_TOP_K)
        for r in range(COMBINE_TM):
            for k in range(MOE_TOP_K):
                row = dest_ref[base + r * MOE_TOP_K + k]
                src = y_ref.at[pl.ds(pl.multiple_of(row * PACK_SUB, PACK_SUB), PACK_SUB), :]
                at = slot * slot_rows + k * k_rows + r * PACK_SUB
                dst = ybuf.at[pl.ds(pl.multiple_of(at, PACK_SUB), PACK_SUB), :]
                pltpu.make_async_copy(src, dst, sem.at[slot]).start(priority=k)

    @pl.when(i == 0)
    def _():
        gather(0, 0)

    @pl.when(i + 1 < n)
    def _():
        gather(i + 1, (i + 1) % 2)

    slot = i % 2
    pltpu.make_async_copy(slot_ref(slot), slot_ref(slot), sem.at[slot]).wait()
    x2 = x1_ref[...]
    for k in range(MOE_TOP_K):
        yk = _unpack_rows(ybuf, slot * slot_rows + k * k_rows, COMBINE_TM)
        x2 = x2 + info_ref[:, INFO_GATE + k:INFO_GATE + k + 1] * yk
    inv = lax.rsqrt(jnp.mean(x2 * x2, axis=-1, keepdims=True) + RMS_EPS)
    o_ref[...] = x2 * inv * g_ref[...]


def _combine(dest, x1, info, g, y_pad):
    tm = COMBINE_TM
    grid_spec = pltpu.PrefetchScalarGridSpec(
        num_scalar_prefetch=1,
        grid=(TOKENS // tm,),
        in_specs=[
            pl.BlockSpec((tm, D_MODEL), lambda i, d: (i, 0)),
            pl.BlockSpec((tm, ROUTER_COLS), lambda i, d: (i, 0)),
            pl.BlockSpec((1, D_MODEL), lambda i, d: (0, 0)),
            pl.BlockSpec(memory_space=pl.ANY),
        ],
        out_specs=pl.BlockSpec((tm, D_MODEL), lambda i, d: (i, 0)),
        scratch_shapes=[pltpu.VMEM((2 * MOE_TOP_K * tm * PACK_SUB, LANES), F32), pltpu.SemaphoreType.DMA((2,))],
    )
    return pl.pallas_call(
        _combine_kernel,
        grid_spec=grid_spec,
        out_shape=jax.ShapeDtypeStruct((TOKENS, D_MODEL), F32),
        compiler_params=_cparams(("arbitrary",)),
        name="combine",
    )(dest, x1, info, g, y_pad)


def kernel(x, mix_norm_g, w_in, w_fourier_out, ssm_A_re, ssm_A_im, ssm_log_dt, ssm_B_re, ssm_B_im, ssm_C_re,
           ssm_C_im, ssm_D, ssm_w_glu, w_ssm_out, w_out, ffn_norm_g, router_group_w, router_group_b,
           router_expert_w, router_expert_b, expert_w_gate, expert_w_up, expert_w_down, final_norm_g):
    assert x.shape == (BATCH, SEQ, D_MODEL) and w_in.shape[0] == 1
    f1, f2, twr, twi, cdft = _dft_constants()
    e_qc, e_rn = _replication_matrices()

    vf, us, gates = _inproj(x, mix_norm_g[0][None], w_in[0].astype(BF16), cdft)
    a = _dft1(vf.reshape(BATCH, 2 * DFT_RADIX, DFT_RADIX * FOURIER_WIDTH), f1, twr, twi)
    fmix = _dft2(a.reshape(BATCH, 2 * DFT_RADIX, DFT_RADIX * FOURIER_WIDTH), f2)
    fmix = fmix.reshape(TOKENS, FOURIER_WIDTH)

    kc, wsc, woc, aq = _ssm_operators(ssm_A_re[0], ssm_A_im[0], ssm_log_dt[0], ssm_B_re[0], ssm_B_im[0],
                                      ssm_C_re[0], ssm_C_im[0])
    xr = us.reshape(SSM_LANE_BLOCKS, BATCH * N_CHUNKS, CHUNK_COLS)
    sre, sim = _ssm_states(xr, wsc, e_rn)
    hre, him = _ssm_scan(sre, sim, aq)
    yconv = _ssm_out(xr, kc, woc, e_qc, hre, him).reshape(SSM_LANE_BLOCKS, TOKENS, LANES)

    w_router = jnp.concatenate([router_group_w[0], router_expert_w[0]], axis=1)
    w_router = jnp.pad(w_router, ((0, 0), (0, ROUTER_COLS - w_router.shape[1])))
    b_router = jnp.concatenate([router_group_b[0], router_expert_b[0]])
    b_router = jnp.pad(b_router, (0, ROUTER_COLS - b_router.shape[0]))[None]
    wr_hi = w_router.astype(BF16)
    wr_lo = (w_router - wr_hi.astype(F32)).astype(BF16)
    x1, hp, logits = _merge(x.reshape(TOKENS, D_MODEL), fmix, yconv, us, gates, ssm_D[0][None],
                            w_fourier_out[0].astype(BF16), ssm_w_glu[0].astype(BF16), w_ssm_out[0].astype(BF16),
                            w_out[0].astype(BF16), ffn_norm_g[0][None], wr_hi, wr_lo, b_router)

    info, counts = _route(logits)
    dest, block_e, n_used = _dispatch_plan(info, counts)
    row_tok = _invert(dest)
    y_pad = _moe(row_tok, block_e, n_used, hp, expert_w_gate[0], expert_w_up[0], expert_w_down[0])
    out = _combine(dest, x1, info, final_norm_g[None], y_pad)
    return out.reshape(BATCH, SEQ, D_MODEL)
```

```python
import math

import numpy as np
import jax
import jax.numpy as jnp
from jax import lax
from jax.experimental import pallas as pl
from jax.experimental.pallas import tpu as pltpu

F32 = jnp.float32
BF16 = jnp.bfloat16

D_MODEL = 1024
BATCH = 4
SEQ = 4096
TOKENS = BATCH * SEQ
FOURIER_WIDTH = 512
FOURIER_GROUP_CH = 128
FOURIER_GROUPS = 4
SSM_WIDTH = 512
SSM_GROUP_CH = 16
SSM_GROUPS = 32
SSM_STATE = 64
MOE_GROUPS = 8
EXPERTS_PER_GROUP = 8
N_EXPERTS = 64
MOE_TOP_K = 2
D_EXPERT = 512
RMS_EPS = 1e-6

LANES = 128
DFT_RADIX = 64
SSM_CHUNK = 16
SSM_LANE_BLOCKS = SSM_WIDTH // LANES
GROUPS_PER_BLOCK = LANES // SSM_GROUP_CH
CHUNK_COLS = SSM_CHUNK * LANES
N_CHUNKS = SEQ // SSM_CHUNK
STATE_COLS = GROUPS_PER_BLOCK * SSM_STATE
MOE_ROWS = 256
MOE_BLOCKS = TOKENS * MOE_TOP_K // MOE_ROWS + N_EXPERTS
ROUTER_COLS = 128
VMEM_LIMIT = 48 * 1024 * 1024


def _cparams(sem, vmem=VMEM_LIMIT):
    return pltpu.CompilerParams(dimension_semantics=sem, vmem_limit_bytes=vmem)


IN_TM = 512


def _inproj_kernel(x_ref, g_ref, w_ref, cdft_ref, vf_ref, us_ref, gates_ref):
    x = x_ref[0]
    inv = lax.rsqrt(jnp.mean(x * x, axis=-1, keepdims=True) + RMS_EPS)
    h = (x * inv * g_ref[...]).astype(BF16)
    zf = jnp.dot(h, w_ref[:, 0:FOURIER_WIDTH], preferred_element_type=F32).astype(BF16)
    cdft = cdft_ref[...].astype(BF16)
    for g in range(FOURIER_GROUPS):
        sl = slice(g * LANES, (g + 1) * LANES)
        v = jnp.dot(zf[:, sl], cdft, preferred_element_type=F32)
        vf_ref[0, 0, :, sl] = v[:, :LANES].astype(BF16)
        vf_ref[0, 1, :, sl] = v[:, LANES:].astype(BF16)
    zs = jnp.dot(h, w_ref[:, FOURIER_WIDTH:FOURIER_WIDTH + SSM_WIDTH], preferred_element_type=F32)
    for j in range(SSM_LANE_BLOCKS):
        us_ref[j] = zs[:, j * LANES:(j + 1) * LANES].astype(BF16)
    base = FOURIER_WIDTH + SSM_WIDTH
    for n in range(4):
        zg = jnp.dot(h, w_ref[:, base + n * 512: base + (n + 1) * 512], preferred_element_type=F32)
        gates_ref[:, n * 512:(n + 1) * 512] = jax.nn.sigmoid(zg).astype(BF16)


def _inproj(x, g, w_in, cdft):
    nt = SEQ // IN_TM
    return pl.pallas_call(
        _inproj_kernel,
        grid=(BATCH, nt),
        in_specs=[
            pl.BlockSpec((1, IN_TM, D_MODEL), lambda b, i: (b, i, 0)),
            pl.BlockSpec((1, D_MODEL), lambda b, i: (0, 0)),
            pl.BlockSpec(w_in.shape, lambda b, i: (0, 0)),
            pl.BlockSpec(cdft.shape, lambda b, i: (0, 0)),
        ],
        out_specs=[
            pl.BlockSpec((1, 2, IN_TM, FOURIER_WIDTH), lambda b, i: (b, 0, i, 0)),
            pl.BlockSpec((SSM_LANE_BLOCKS, IN_TM, LANES), lambda b, i: (0, b * nt + i, 0)),
            pl.BlockSpec((IN_TM, 2 * D_MODEL), lambda b, i: (b * nt + i, 0)),
        ],
        out_shape=[
            jax.ShapeDtypeStruct((BATCH, 2, SEQ, FOURIER_WIDTH), BF16),
            jax.ShapeDtypeStruct((SSM_LANE_BLOCKS, TOKENS, LANES), BF16),
            jax.ShapeDtypeStruct((TOKENS, 2 * D_MODEL), BF16),
        ],
        compiler_params=_cparams(("parallel", "parallel")),
        name="inproj",
    )(x, g, w_in, cdft)


DFT_S2_PER_STEP = 8


def _dft1_kernel(v_ref, f_ref, twr_ref, twi_ref, a_ref):
    f1 = f_ref[...].astype(BF16)
    for s in range(DFT_S2_PER_STEP):
        blk = v_ref[0, :, s * FOURIER_WIDTH:(s + 1) * FOURIER_WIDTH]
        r = jnp.dot(f1, blk, preferred_element_type=F32)
        ar, ai = r[:DFT_RADIX], r[DFT_RADIX:]
        tr = jnp.tile(twr_ref[s], (1, FOURIER_WIDTH // LANES))
        ti = jnp.tile(twi_ref[s], (1, FOURIER_WIDTH // LANES))
        a_ref[0, 0, s] = (ar * tr - ai * ti).astype(BF16)
        a_ref[0, 1, s] = (ar * ti + ai * tr).astype(BF16)


def _dft1(v, f1, twr, twi):
    ns = DFT_RADIX // DFT_S2_PER_STEP
    cols = DFT_S2_PER_STEP * FOURIER_WIDTH
    return pl.pallas_call(
        _dft1_kernel,
        grid=(BATCH, ns),
        in_specs=[
            pl.BlockSpec((1, 2 * DFT_RADIX, cols), lambda b, s: (b, 0, s)),
            pl.BlockSpec(f1.shape, lambda b, s: (0, 0)),
            pl.BlockSpec((DFT_S2_PER_STEP, DFT_RADIX, LANES), lambda b, s: (s, 0, 0)),
            pl.BlockSpec((DFT_S2_PER_STEP, DFT_RADIX, LANES), lambda b, s: (s, 0, 0)),
        ],
        out_specs=pl.BlockSpec((1, 2, DFT_S2_PER_STEP, DFT_RADIX, FOURIER_WIDTH),
                               lambda b, s: (b, 0, s, 0, 0)),
        out_shape=jax.ShapeDtypeStruct((BATCH, 2, DFT_RADIX, DFT_RADIX, FOURIER_WIDTH), BF16),
        compiler_params=_cparams(("parallel", "parallel")),
        name="dft1",
    )(v, f1, twr, twi)


DFT2_COLS = 4096


def _dft2_kernel(a_ref, f_ref, o_ref):
    o_ref[0] = jnp.dot(f_ref[...].astype(BF16), a_ref[0], preferred_element_type=F32).astype(BF16)


def _dft2(a, f2):
    n = DFT_RADIX * FOURIER_WIDTH
    return pl.pallas_call(
        _dft2_kernel,
        grid=(BATCH, n // DFT2_COLS),
        in_specs=[
            pl.BlockSpec((1, 2 * DFT_RADIX, DFT2_COLS), lambda b, s: (b, 0, s)),
            pl.BlockSpec(f2.shape, lambda b, s: (0, 0)),
        ],
        out_specs=pl.BlockSpec((1, DFT_RADIX, DFT2_COLS), lambda b, s: (b, 0, s)),
        out_shape=jax.ShapeDtypeStruct((BATCH, DFT_RADIX, n), BF16),
        compiler_params=_cparams(("parallel", "parallel")),
        name="dft2",
    )(a, f2)


def _dft_constants():
    r = DFT_RADIX
    k = np.arange(r)
    ang = 2.0 * np.pi * np.outer(k, k) / r
    c, s = np.cos(ang), np.sin(ang)
    f1 = np.block([[c, s], [-s, c]])
    f2 = np.concatenate([c, s], axis=1)
    tw = 2.0 * np.pi * np.outer(k, k) / SEQ
    scale = 1.0 / math.sqrt(SEQ)
    twr = np.repeat((np.cos(tw) * scale)[:, :, None], LANES, axis=2)
    twi = np.repeat((-np.sin(tw) * scale)[:, :, None], LANES, axis=2)
    kc = np.arange(FOURIER_GROUP_CH)
    angc = 2.0 * np.pi * np.outer(kc, kc) / FOURIER_GROUP_CH
    cs = 1.0 / math.sqrt(FOURIER_GROUP_CH)
    cdft = np.concatenate([np.cos(angc) * cs, -np.sin(angc) * cs], axis=1)
    return tuple(jnp.asarray(v, F32) for v in (f1, f2, twr, twi, cdft))


def _ssm_operators(a_re, a_im, log_dt, b_re, b_im, c_re, c_im):
    q_len = SSM_CHUNK
    jb, gb, ch, ns = SSM_LANE_BLOCKS, GROUPS_PER_BLOCK, SSM_GROUP_CH, SSM_STATE
    dt = jnp.exp(log_dt)[..., None]
    lr, li = a_re * dt, a_im * dt
    steps = jnp.arange(q_len + 1, dtype=F32)
    mag = jnp.exp(lr[..., None] * steps)
    ang = li[..., None] * steps
    pr, pi = mag * jnp.cos(ang), mag * jnp.sin(ang)
    ar, ai = pr[..., 1], pi[..., 1]
    den = a_re * a_re + a_im * a_im
    cr = ((ar - 1.0) * a_re + ai * a_im) / den
    ci = (ai * a_re - (ar - 1.0) * a_im) / den
    bbr = cr[..., None] * b_re - ci[..., None] * b_im
    bbi = cr[..., None] * b_im + ci[..., None] * b_re
    prq, piq = pr[..., :q_len], pi[..., :q_len]
    cpr = jnp.einsum('dgcn,dgnt->dgtcn', c_re, prq) - jnp.einsum('dgcn,dgnt->dgtcn', c_im, piq)
    cpi = jnp.einsum('dgcn,dgnt->dgtcn', c_re, piq) + jnp.einsum('dgcn,dgnt->dgtcn', c_im, prq)
    kern = jnp.einsum('dgtcn,dgne->dgtce', cpr, bbr) - jnp.einsum('dgtcn,dgne->dgtce', cpi, bbi)
    qi = jnp.arange(q_len)
    lag = qi[None, :] - qi[:, None]
    k_f = kern[0][:, jnp.maximum(lag, 0)] * (lag >= 0)[None, :, :, None, None].astype(F32)
    k_b = kern[1][:, jnp.maximum(-lag, 0)] * (lag <= 0)[None, :, :, None, None].astype(F32)
    kc = (k_f + k_b).reshape(jb, gb, q_len, q_len, ch, ch)
    kc = jnp.transpose(kc, (0, 2, 1, 5, 3, 4)).reshape(jb, CHUNK_COLS, q_len * ch)

    def state_in(pwr, pwi, br, bi):
        wr = jnp.einsum('gnq,gne->gqen', pwr, br) - jnp.einsum('gnq,gne->gqen', pwi, bi)
        wi = jnp.einsum('gnq,gne->gqen', pwr, bi) + jnp.einsum('gnq,gne->gqen', pwi, br)
        return jnp.stack([wr, wi])

    rev = q_len - 1 - qi
    wsc = jnp.stack([state_in(pr[0][..., rev], pi[0][..., rev], bbr[0], bbi[0]),
                     state_in(pr[1][..., qi], pi[1][..., qi], bbr[1], bbi[1])])
    wsc = wsc.reshape(2, 2, jb, gb, q_len, ch, ns)
    wsc = jnp.transpose(wsc, (2, 4, 3, 5, 0, 1, 6)).reshape(jb, CHUNK_COLS, 4 * ns)

    def state_out(pwr, pwi, c_r, c_i):
        o_r = jnp.einsum('gcn,gnq->gnqc', c_r, pwr) - jnp.einsum('gcn,gnq->gnqc', c_i, pwi)
        o_i = jnp.einsum('gcn,gnq->gnqc', c_r, pwi) + jnp.einsum('gcn,gnq->gnqc', c_i, pwr)
        return jnp.stack([o_r, -o_i])

    woc = jnp.stack([state_out(pr[0][..., qi + 1], pi[0][..., qi + 1], c_re[0], c_im[0]),
                     state_out(pr[1][..., q_len - qi], pi[1][..., q_len - qi], c_re[1], c_im[1])])
    woc = woc.reshape(2, 2, jb, gb, ns, q_len * ch)
    woc = jnp.transpose(woc, (2, 0, 1, 3, 4, 5)).reshape(jb, 4 * STATE_COLS, q_len * ch)

    aq = jnp.stack([pr[0][..., q_len], pi[0][..., q_len], pr[1][..., q_len], pi[1][..., q_len]])
    aq = aq.reshape(4, jb, 1, STATE_COLS)
    aq = jnp.broadcast_to(aq, (4, jb, BATCH, STATE_COLS))
    aq = jnp.transpose(aq, (1, 0, 2, 3)).reshape(jb, 4, BATCH * STATE_COLS)
    return kc.astype(BF16), wsc.astype(BF16), woc.astype(BF16), aq


EXPAND_ROWS = 256
GROUP_SHIFT_CH = 4
GROUP_SHIFT_STATE = 6


def _expand_block_diag(compact_ref, sel_ref, out_ref, row_shift, col_shift):
    n_rows, n_cols = out_ref.shape
    col_g = (lax.broadcasted_iota(jnp.int32, (EXPAND_ROWS, n_cols), 1) >> col_shift) & (GROUPS_PER_BLOCK - 1)
    for r in range(n_rows // EXPAND_ROWS):
        rows = slice(r * EXPAND_ROWS, (r + 1) * EXPAND_ROWS)
        t = jnp.dot(compact_ref[rows, :], sel_ref[...], preferred_element_type=F32)
        row_i = lax.broadcasted_iota(jnp.int32, (EXPAND_ROWS, n_cols), 0) + r * EXPAND_ROWS
        row_g = (row_i >> row_shift) & (GROUPS_PER_BLOCK - 1)
        out_ref[rows, :] = jnp.where(row_g == col_g, t, 0.0).astype(BF16)


def _replication_matrices():
    half = SSM_CHUNK // 2
    e_qc = np.zeros((half, SSM_GROUP_CH, half, GROUPS_PER_BLOCK, SSM_GROUP_CH), np.float32)
    for q in range(half):
        for c in range(SSM_GROUP_CH):
            e_qc[q, c, q, :, c] = 1.0
    e_rn = np.zeros((2, SSM_STATE, 2, GROUPS_PER_BLOCK, SSM_STATE), np.float32)
    for r in range(2):
        for n in range(SSM_STATE):
            e_rn[r, n, r, :, n] = 1.0
    return (jnp.asarray(e_qc.reshape(half * SSM_GROUP_CH, -1), BF16),
            jnp.asarray(e_rn.reshape(2 * SSM_STATE, -1), BF16))


def _ssm_state_kernel(x_ref, wsc_ref, sel_ref, sre_ref, sim_ref, ws_scr):
    @pl.when(pl.program_id(2) == 0)
    def _():
        _expand_block_diag(wsc_ref.at[0], sel_ref, ws_scr, GROUP_SHIFT_CH, GROUP_SHIFT_STATE)

    s = jnp.dot(x_ref[0], ws_scr[...], preferred_element_type=F32)
    sre_ref[0] = s[:, :STATE_COLS]
    sim_ref[0] = s[:, STATE_COLS:]


def _ssm_states(xr, wsc, e_rn):
    out_spec = pl.BlockSpec((1, N_CHUNKS, STATE_COLS), lambda j, d, b: (j, 0, d * BATCH + b))
    shape = jax.ShapeDtypeStruct((SSM_LANE_BLOCKS, N_CHUNKS, 2 * BATCH * STATE_COLS), F32)
    return pl.pallas_call(
        _ssm_state_kernel,
        grid=(SSM_LANE_BLOCKS, 2, BATCH),
        in_specs=[
            pl.BlockSpec((1, N_CHUNKS, CHUNK_COLS), lambda j, d, b: (j, b, 0)),
            pl.BlockSpec((1, CHUNK_COLS, 2 * SSM_STATE), lambda j, d, b: (j, 0, d)),
            pl.BlockSpec(e_rn.shape, lambda j, d, b: (0, 0)),
        ],
        out_specs=[out_spec, out_spec],
        out_shape=[shape, shape],
        scratch_shapes=[pltpu.VMEM((CHUNK_COLS, 2 * STATE_COLS), BF16)],
        compiler_params=_cparams(("parallel", "parallel", "arbitrary")),
        name="ssm_states",
    )(xr, wsc, e_rn)


def _ssm_scan_kernel(sre_ref, sim_ref, aq_ref, hre_ref, him_ref):
    half = BATCH * STATE_COLS
    fwd, bwd = slice(0, half), slice(half, 2 * half)
    a0r, a0i, a1r, a1i = aq_ref[0, 0:1], aq_ref[0, 1:2], aq_ref[0, 2:3], aq_ref[0, 3:4]

    def body(k, carry):
        fr, fi, br, bi = carry
        kb = N_CHUNKS - 1 - k
        hre_ref[0, pl.ds(k, 1), fwd] = fr
        him_ref[0, pl.ds(k, 1), fwd] = fi
        hre_ref[0, pl.ds(kb, 1), bwd] = br
        him_ref[0, pl.ds(kb, 1), bwd] = bi
        sfr, sfi = sre_ref[0, pl.ds(k, 1), fwd], sim_ref[0, pl.ds(k, 1), fwd]
        sbr, sbi = sre_ref[0, pl.ds(kb, 1), bwd], sim_ref[0, pl.ds(kb, 1), bwd]
        return (a0r * fr - a0i * fi + sfr, a0r * fi + a0i * fr + sfi,
                a1r * br - a1i * bi + sbr, a1r * bi + a1i * br + sbi)

    z = jnp.zeros((1, half), F32)
    lax.fori_loop(0, N_CHUNKS, body, (z, z, z, z))


def _ssm_scan(sre, sim, aq):
    spec = pl.BlockSpec((1, N_CHUNKS, 2 * BATCH * STATE_COLS), lambda j: (j, 0, 0))
    shape = jax.ShapeDtypeStruct(sre.shape, F32)
    return pl.pallas_call(
        _ssm_scan_kernel,
        grid=(SSM_LANE_BLOCKS,),
        in_specs=[spec, spec, pl.BlockSpec((1, 4, BATCH * STATE_COLS), lambda j: (j, 0, 0))],
        out_specs=[spec, spec],
        out_shape=[shape, shape],
        compiler_params=_cparams(("parallel",)),
        name="ssm_scan",
    )(sre, sim, aq)


SSM_OUT_TN = CHUNK_COLS // 2


def _ssm_out_kernel(x_ref, kc_ref, woc_ref, sel_ref, h0r_ref, h0i_ref, h1r_ref, h1i_ref, y_ref, m_scr, wo_scr):
    @pl.when(pl.program_id(2) == 0)
    def _():
        _expand_block_diag(kc_ref.at[0], sel_ref, m_scr, GROUP_SHIFT_CH, GROUP_SHIFT_CH)
        _expand_block_diag(woc_ref.at[0], sel_ref, wo_scr, GROUP_SHIFT_STATE, GROUP_SHIFT_CH)

    acc = jnp.dot(x_ref[0], m_scr[...], preferred_element_type=F32)
    for i, h_ref in enumerate((h0r_ref, h0i_ref, h1r_ref, h1i_ref)):
        acc += jnp.dot(h_ref[0].astype(BF16), wo_scr[i * STATE_COLS:(i + 1) * STATE_COLS, :],
                       preferred_element_type=F32)
    y_ref[0] = acc.astype(BF16)


def _ssm_out(xr, kc, woc, e_qc, hre, him):
    def hspec(d):
        return pl.BlockSpec((1, N_CHUNKS, STATE_COLS), lambda j, n, b: (j, 0, d * BATCH + b))
    half_cols = kc.shape[2] // 2
    return pl.pallas_call(
        _ssm_out_kernel,
        grid=(SSM_LANE_BLOCKS, CHUNK_COLS // SSM_OUT_TN, BATCH),
        in_specs=[
            pl.BlockSpec((1, N_CHUNKS, CHUNK_COLS), lambda j, n, b: (j, b, 0)),
            pl.BlockSpec((1, CHUNK_COLS, half_cols), lambda j, n, b: (j, 0, n)),
            pl.BlockSpec((1, 4 * STATE_COLS, half_cols), lambda j, n, b: (j, 0, n)),
            pl.BlockSpec(e_qc.shape, lambda j, n, b: (0, 0)),
            hspec(0), hspec(0), hspec(1), hspec(1),
        ],
        out_specs=pl.BlockSpec((1, N_CHUNKS, SSM_OUT_TN), lambda j, n, b: (j, b, n)),
        out_shape=jax.ShapeDtypeStruct((SSM_LANE_BLOCKS, BATCH * N_CHUNKS, CHUNK_COLS), BF16),
        scratch_shapes=[pltpu.VMEM((CHUNK_COLS, SSM_OUT_TN), BF16), pltpu.VMEM((4 * STATE_COLS, SSM_OUT_TN), BF16)],
        compiler_params=_cparams(("parallel", "parallel", "arbitrary")),
        name="ssm_out",
    )(xr, kc, woc, e_qc, hre, him, hre, him)


MERGE_TM = 512
GELU_C = math.sqrt(2.0 / math.pi)
PACK_SUB = D_MODEL // LANES


def _split_bf16(v):
    hi = v.astype(BF16)
    lo = (v - hi.astype(F32)).astype(BF16)
    return hi, lo


def _pack_rows(v, out_ref):
    for s in range(PACK_SUB):
        out_ref[pl.ds(s, v.shape[0], stride=PACK_SUB), :] = v[:, s * LANES:(s + 1) * LANES]


def _unpack_rows(buf_ref, start, rows):
    return jnp.concatenate([buf_ref[pl.ds(start + s, rows, stride=PACK_SUB), :] for s in range(PACK_SUB)], axis=1)


def _merge_kernel(x_ref, fm_ref, yc_ref, us_ref, gates_ref, dskip_ref, wf_ref, wglu_ref, ws_ref, wo_ref,
                  ng_ref, wrh_ref, wrl_ref, rb_ref, x1_ref, hp_ref, logit_ref):
    conv = jnp.concatenate([yc_ref[j].astype(F32) for j in range(SSM_LANE_BLOCKS)], axis=-1)
    u = jnp.concatenate([us_ref[j].astype(F32) for j in range(SSM_LANE_BLOCKS)], axis=-1)
    y = conv + dskip_ref[...] * u
    y = 0.5 * y * (1.0 + jnp.tanh(GELU_C * (y + 0.044715 * (y * y * y))))
    glu = jax.nn.sigmoid(jnp.dot(y.astype(BF16), wglu_ref[...], preferred_element_type=F32))
    y_s = jnp.dot((y * glu).astype(BF16), ws_ref[...], preferred_element_type=F32)
    y_f = jnp.dot(fm_ref[...], wf_ref[...], preferred_element_type=F32)
    merged = (gates_ref[:, :D_MODEL].astype(F32) * y_f + gates_ref[:, D_MODEL:].astype(F32) * y_s)
    x1 = x_ref[...] + jnp.dot(merged.astype(BF16), wo_ref[...], preferred_element_type=F32)
    x1_ref[...] = x1
    inv = lax.rsqrt(jnp.mean(x1 * x1, axis=-1, keepdims=True) + RMS_EPS)
    hn = x1 * inv * ng_ref[...]
    _pack_rows(hn, hp_ref)
    hi, lo = _split_bf16(hn)
    logits = (jnp.dot(hi, wrh_ref[...], preferred_element_type=F32)
              + jnp.dot(lo, wrh_ref[...], preferred_element_type=F32)
              + jnp.dot(hi, wrl_ref[...], preferred_element_type=F32))
    logit_ref[...] = logits + rb_ref[...]


def _merge(x, fmix, yconv, us, gates, dskip, wf, wglu, ws, wo, ng, wrh, wrl, rb):
    tm = MERGE_TM
    full = lambda a: pl.BlockSpec(a.shape, lambda i: (0,) * a.ndim)
    return pl.pallas_call(
        _merge_kernel,
        grid=(TOKENS // tm,),
        in_specs=[
            pl.BlockSpec((tm, D_MODEL), lambda i: (i, 0)),
            pl.BlockSpec((tm, FOURIER_WIDTH), lambda i: (i, 0)),
            pl.BlockSpec((SSM_LANE_BLOCKS, tm, LANES), lambda i: (0, i, 0)),
            pl.BlockSpec((SSM_LANE_BLOCKS, tm, LANES), lambda i: (0, i, 0)),
            pl.BlockSpec((tm, 2 * D_MODEL), lambda i: (i, 0)),
            full(dskip), full(wf), full(wglu), full(ws), full(wo), full(ng), full(wrh), full(wrl), full(rb),
        ],
        out_specs=[
            pl.BlockSpec((tm, D_MODEL), lambda i: (i, 0)),
            pl.BlockSpec((tm * PACK_SUB, LANES), lambda i: (i, 0)),
            pl.BlockSpec((tm, ROUTER_COLS), lambda i: (i, 0)),
        ],
        out_shape=[
            jax.ShapeDtypeStruct((TOKENS, D_MODEL), F32),
            jax.ShapeDtypeStruct((TOKENS * PACK_SUB, LANES), F32),
            jax.ShapeDtypeStruct((TOKENS, ROUTER_COLS), F32),
        ],
        compiler_params=_cparams(("parallel",)),
        name="merge",
    )(x, fmix, yconv, us, gates, dskip, wf, wglu, ws, wo, ng, wrh, wrl, rb)


ROUTE_TM = 512
EXPERT_LANE0 = MOE_GROUPS
INFO_EXPERT, INFO_RANK, INFO_GATE = 0, 2, 4


def _route_kernel(lg_ref, info_ref, cnt_ref, carry):
    @pl.when(pl.program_id(0) == 0)
    def _():
        carry[...] = jnp.zeros_like(carry)

    lg = lg_ref[...]
    tm = lg.shape[0]
    col_i = lax.broadcasted_iota(jnp.int32, lg.shape, 1)
    col = col_i.astype(F32)
    neg = jnp.float32(-jnp.inf)
    none = jnp.float32(ROUTER_COLS)

    def row_max(v):
        return jnp.max(v, axis=-1, keepdims=True)

    def first_at(v, m):
        return jnp.min(jnp.where(v == m, col, none), axis=-1, keepdims=True)

    gl = jnp.where(col_i < MOE_GROUPS, lg, neg)
    gmax = row_max(gl)
    p_g = 1.0 / jnp.sum(jnp.exp(gl - gmax), axis=-1, keepdims=True)
    lo = EXPERT_LANE0 + first_at(gl, gmax) * EXPERTS_PER_GROUP
    el = jnp.where((col >= lo) & (col < lo + EXPERTS_PER_GROUP), lg, neg)
    l1 = row_max(el)
    i1 = first_at(el, l1)
    el2 = jnp.where(col == i1, neg, el)
    l2 = row_max(el2)
    i2 = first_at(el2, l2)
    r = jnp.exp(l2 - l1)
    w1 = p_g / (1.0 + r)
    w2 = w1 * r

    hit1, hit2 = col == i1, col == i2
    onehot = jnp.where(hit1 | hit2, 1.0, 0.0)
    earlier = lax.broadcasted_iota(jnp.int32, (tm, tm), 0) > lax.broadcasted_iota(jnp.int32, (tm, tm), 1)
    before = jnp.dot(jnp.where(earlier, 1.0, 0.0).astype(BF16), onehot.astype(BF16),
                     preferred_element_type=F32) + carry[...]
    rank1 = jnp.sum(jnp.where(hit1, before, 0.0), axis=-1, keepdims=True)
    rank2 = jnp.sum(jnp.where(hit2, before, 0.0), axis=-1, keepdims=True)
    carry[...] += jnp.sum(onehot, axis=0, keepdims=True)
    cnt_ref[...] = carry[...]

    info = jnp.zeros(lg.shape, F32)
    for lane, v in ((INFO_EXPERT, i1 - EXPERT_LANE0), (INFO_EXPERT + 1, i2 - EXPERT_LANE0), (INFO_RANK, rank1),
                    (INFO_RANK + 1, rank2), (INFO_GATE, w1), (INFO_GATE + 1, w2)):
        info = jnp.where(col_i == lane, v, info)
    info_ref[...] = info


def _route(logits):
    return pl.pallas_call(
        _route_kernel,
        grid=(TOKENS // ROUTE_TM,),
        in_specs=[pl.BlockSpec((ROUTE_TM, ROUTER_COLS), lambda i: (i, 0))],
        out_specs=[pl.BlockSpec((ROUTE_TM, ROUTER_COLS), lambda i: (i, 0)),
                   pl.BlockSpec((1, ROUTER_COLS), lambda i: (0, 0))],
        out_shape=[jax.ShapeDtypeStruct((TOKENS, ROUTER_COLS), F32),
                   jax.ShapeDtypeStruct((1, ROUTER_COLS), F32)],
        scratch_shapes=[pltpu.VMEM((1, ROUTER_COLS), F32)],
        compiler_params=_cparams(("arbitrary",)),
        name="route",
    )(logits)


def _dispatch_plan(info, counts):
    expert = info[:, INFO_EXPERT:INFO_EXPERT + MOE_TOP_K].astype(jnp.int32)
    rank = info[:, INFO_RANK:INFO_RANK + MOE_TOP_K].astype(jnp.int32)
    cnt = counts[0, EXPERT_LANE0:EXPERT_LANE0 + N_EXPERTS].astype(jnp.int32)
    padded = ((cnt + MOE_ROWS - 1) // MOE_ROWS) * MOE_ROWS
    pends = jnp.cumsum(padded)
    pstarts = pends - padded
    ids = jnp.arange(N_EXPERTS, dtype=jnp.int32)
    dest = rank + jnp.sum(jnp.where(expert[..., None] == ids, pstarts, 0), axis=-1)
    n_used = pends[-1] // MOE_ROWS
    blocks = jnp.arange(MOE_BLOCKS, dtype=jnp.int32)
    block_e = jnp.sum((pends[None, :] <= (blocks * MOE_ROWS)[:, None]).astype(jnp.int32), axis=1)
    block_e = jnp.minimum(block_e, N_EXPERTS - 1)
    last_e = jnp.sum(jnp.where(blocks == n_used - 1, block_e, 0))
    block_e = jnp.where(blocks < n_used, block_e, last_e)
    return dest.reshape(TOKENS * MOE_TOP_K), block_e, n_used.reshape(1)


TOK_BITS = 16
ROW_TOK_WORDS = MOE_BLOCKS * MOE_ROWS // 2


def _invert_kernel(dest_ref, tok_ref):
    tok_mask = (1 << TOK_BITS) - 1

    def fill(w, c):
        lo = (2 * w) & (TOKENS - 1)
        tok_ref[w] = lo | lax.shift_left(lo + 1, TOK_BITS)
        return c

    lax.fori_loop(0, ROW_TOK_WORDS, fill, 0, unroll=8)

    def place(t, c):
        for k in range(MOE_TOP_K):
            slot = dest_ref[MOE_TOP_K * t + k]
            word = lax.shift_right_logical(slot, 1)
            shift = (slot & 1) * TOK_BITS
            tok_ref[word] = (tok_ref[word] & ~lax.shift_left(tok_mask, shift)) | lax.shift_left(t, shift)
        return c

    lax.fori_loop(0, TOKENS, place, 0, unroll=4)


def _invert(dest):
    return pl.pallas_call(
        _invert_kernel,
        in_specs=[pl.BlockSpec(memory_space=pltpu.SMEM)],
        out_specs=pl.BlockSpec(memory_space=pltpu.SMEM),
        out_shape=jax.ShapeDtypeStruct((ROW_TOK_WORDS,), jnp.int32),
        name="invert",
    )(dest)


MOE_SLOT_ROWS = MOE_ROWS * PACK_SUB


def _moe_kernel(tok_ref, be_ref, nu_ref, hp_ref, wg_ref, wu_ref, wd_ref, y_ref, xbuf, sem):
    i = pl.program_id(0)
    n_used = nu_ref[0]

    def slot_rows(slot):
        return xbuf.at[pl.ds(pl.multiple_of(slot * MOE_SLOT_ROWS, MOE_SLOT_ROWS), MOE_SLOT_ROWS), :]

    def gather(block, slot):
        first_word = block * (MOE_ROWS // 2)
        for r in range(MOE_ROWS):
            if r % 2 == 0:
                word = tok_ref[first_word + r // 2]
                tok = word & ((1 << TOK_BITS) - 1)
            else:
                tok = lax.shift_right_logical(word, TOK_BITS)
            src = hp_ref.at[pl.ds(pl.multiple_of(tok * PACK_SUB, PACK_SUB), PACK_SUB), :]
            dst = xbuf.at[pl.ds(pl.multiple_of(slot * MOE_SLOT_ROWS + r * PACK_SUB, PACK_SUB), PACK_SUB), :]
            pltpu.make_async_copy(src, dst, sem.at[slot]).start()

    @pl.when(i == 0)
    def _():
        gather(0, 0)

    @pl.when(i + 1 < n_used)
    def _():
        gather(i + 1, (i + 1) % 2)

    @pl.when(i < n_used)
    def _():
        slot = i % 2
        pltpu.make_async_copy(slot_rows(slot), slot_rows(slot), sem.at[slot]).wait()
        xb = _unpack_rows(xbuf, slot * MOE_SLOT_ROWS, MOE_ROWS).astype(BF16)
        a = jnp.dot(xb, wg_ref[0].astype(BF16), preferred_element_type=F32)
        u = jnp.dot(xb, wu_ref[0].astype(BF16), preferred_element_type=F32)
        act = (a * jax.nn.sigmoid(a) * u).astype(BF16)
        _pack_rows(jnp.dot(act, wd_ref[0].astype(BF16), preferred_element_type=F32), y_ref)

    @pl.when(i >= n_used)
    def _():
        y_ref[...] = jnp.zeros_like(y_ref)


def _moe(row_tok, block_e, n_used, hp, w_gate, w_up, w_down):
    grid_spec = pltpu.PrefetchScalarGridSpec(
        num_scalar_prefetch=3,
        grid=(MOE_BLOCKS,),
        in_specs=[
            pl.BlockSpec(memory_space=pl.ANY),
            pl.BlockSpec((1, D_MODEL, D_EXPERT), lambda i, tok, be, nu: (be[i], 0, 0)),
            pl.BlockSpec((1, D_MODEL, D_EXPERT), lambda i, tok, be, nu: (be[i], 0, 0)),
            pl.BlockSpec((1, D_EXPERT, D_MODEL), lambda i, tok, be, nu: (be[i], 0, 0)),
        ],
        out_specs=pl.BlockSpec((MOE_SLOT_ROWS, LANES), lambda i, tok, be, nu: (i, 0)),
        scratch_shapes=[pltpu.VMEM((2 * MOE_SLOT_ROWS, LANES), F32), pltpu.SemaphoreType.DMA((2,))],
    )
    return pl.pallas_call(
        _moe_kernel,
        grid_spec=grid_spec,
        out_shape=jax.ShapeDtypeStruct((MOE_BLOCKS * MOE_SLOT_ROWS, LANES), F32),
        compiler_params=_cparams(("arbitrary",)),
        name="moe",
    )(row_tok, block_e, n_used, hp, w_gate, w_up, w_down)


COMBINE_TM = 256


def _combine_kernel(dest_ref, x1_ref, info_ref, g_ref, y_ref, o_ref, ybuf, sem):
    i = pl.program_id(0)
    n = pl.num_programs(0)
    k_rows = COMBINE_TM * PACK_SUB
    slot_rows = MOE_TOP_K * k_rows

    def slot_ref(slot):
        return ybuf.at[pl.ds(pl.multiple_of(slot * slot_rows, slot_rows), slot_rows), :]

    def gather(tile, slot):
        base = tile * (COMBINE_TM * MOE_TOP_K)
        for r in range(COMBINE_TM):
            for k in range(MOE_TOP_K):
                row = dest_ref[base + r * MOE_TOP_K + k]
                src = y_ref.at[pl.ds(pl.multiple_of(row * PACK_SUB, PACK_SUB), PACK_SUB), :]
                at = slot * slot_rows + k * k_rows + r * PACK_SUB
                dst = ybuf.at[pl.ds(pl.multiple_of(at, PACK_SUB), PACK_SUB), :]
                pltpu.make_async_copy(src, dst, sem.at[slot]).start(priority=k)

    @pl.when(i == 0)
    def _():
        gather(0, 0)

    @pl.when(i + 1 < n)
    def _():
        gather(i + 1, (i + 1) % 2)

    slot = i % 2
    pltpu.make_async_copy(slot_ref(slot), slot_ref(slot), sem.at[slot]).wait()
    x2 = x1_ref[...]
    for k in range(MOE_TOP_K):
        yk = _unpack_rows(ybuf, slot * slot_rows + k * k_rows, COMBINE_TM)
        x2 = x2 + info_ref[:, INFO_GATE + k:INFO_GATE + k + 1] * yk
    inv = lax.rsqrt(jnp.mean(x2 * x2, axis=-1, keepdims=True) + RMS_EPS)
    o_ref[...] = x2 * inv * g_ref[...]


def _combine(dest, x1, info, g, y_pad):
    tm = COMBINE_TM
    grid_spec = pltpu.PrefetchScalarGridSpec(
        num_scalar_prefetch=1,
        grid=(TOKENS // tm,),
        in_specs=[
            pl.BlockSpec((tm, D_MODEL), lambda i, d: (i, 0)),
            pl.BlockSpec((tm, ROUTER_COLS), lambda i, d: (i, 0)),
            pl.BlockSpec((1, D_MODEL), lambda i, d: (0, 0)),
            pl.BlockSpec(memory_space=pl.ANY),
        ],
        out_specs=pl.BlockSpec((tm, D_MODEL), lambda i, d: (i, 0)),
        scratch_shapes=[pltpu.VMEM((2 * MOE_TOP_K * tm * PACK_SUB, LANES), F32), pltpu.SemaphoreType.DMA((2,))],
    )
    return pl.pallas_call(
        _combine_kernel,
        grid_spec=grid_spec,
        out_shape=jax.ShapeDtypeStruct((TOKENS, D_MODEL), F32),
        compiler_params=_cparams(("arbitrary",)),
        name="combine",
    )(dest, x1, info, g, y_pad)


def kernel(x, mix_norm_g, w_in, w_fourier_out, ssm_A_re, ssm_A_im, ssm_log_dt, ssm_B_re, ssm_B_im, ssm_C_re,
           ssm_C_im, ssm_D, ssm_w_glu, w_ssm_out, w_out, ffn_norm_g, router_group_w, router_group_b,
           router_expert_w, router_expert_b, expert_w_gate, expert_w_up, expert_w_down, final_norm_g):
    assert x.shape == (BATCH, SEQ, D_MODEL) and w_in.shape[0] == 1
    f1, f2, twr, twi, cdft = _dft_constants()
    e_qc, e_rn = _replication_matrices()

    vf, us, gates = _inproj(x, mix_norm_g[0][None], w_in[0].astype(BF16), cdft)
    a = _dft1(vf.reshape(BATCH, 2 * DFT_RADIX, DFT_RADIX * FOURIER_WIDTH), f1, twr, twi)
    fmix = _dft2(a.reshape(BATCH, 2 * DFT_RADIX, DFT_RADIX * FOURIER_WIDTH), f2)
    fmix = fmix.reshape(TOKENS, FOURIER_WIDTH)

    kc, wsc, woc, aq = _ssm_operators(ssm_A_re[0], ssm_A_im[0], ssm_log_dt[0], ssm_B_re[0], ssm_B_im[0],
                                      ssm_C_re[0], ssm_C_im[0])
    xr = us.reshape(SSM_LANE_BLOCKS, BATCH * N_CHUNKS, CHUNK_COLS)
    sre, sim = _ssm_states(xr, wsc, e_rn)
    hre, him = _ssm_scan(sre, sim, aq)
    yconv = _ssm_out(xr, kc, woc, e_qc, hre, him).reshape(SSM_LANE_BLOCKS, TOKENS, LANES)

    w_router = jnp.concatenate([router_group_w[0], router_expert_w[0]], axis=1)
    w_router = jnp.pad(w_router, ((0, 0), (0, ROUTER_COLS - w_router.shape[1])))
    b_router = jnp.concatenate([router_group_b[0], router_expert_b[0]])
    b_router = jnp.pad(b_router, (0, ROUTER_COLS - b_router.shape[0]))[None]
    wr_hi = w_router.astype(BF16)
    wr_lo = (w_router - wr_hi.astype(F32)).astype(BF16)
    x1, hp, logits = _merge(x.reshape(TOKENS, D_MODEL), fmix, yconv, us, gates, ssm_D[0][None],
                            w_fourier_out[0].astype(BF16), ssm_w_glu[0].astype(BF16), w_ssm_out[0].astype(BF16),
                            w_out[0].astype(BF16), ffn_norm_g[0][None], wr_hi, wr_lo, b_router)

    info, counts = _route(logits)
    dest, block_e, n_used = _dispatch_plan(info, counts)
    row_tok = _invert(dest)
    y_pad = _moe(row_tok, block_e, n_used, hp, expert_w_gate[0], expert_w_up[0], expert_w_down[0])
    out = _combine(dest, x1, info, final_norm_g[None], y_pad)
    return out.reshape(BATCH, SEQ, D_MODEL)
```

```python
import math

import numpy as np
import jax
import jax.numpy as jnp
from jax import lax
from jax.experimental import pallas as pl
from jax.experimental.pallas import tpu as pltpu

F32 = jnp.float32
BF16 = jnp.bfloat16

D_MODEL = 1024
BATCH = 4
SEQ = 4096
TOKENS = BATCH * SEQ
FOURIER_WIDTH = 512
FOURIER_GROUP_CH = 128
FOURIER_GROUPS = 4
SSM_WIDTH = 512
SSM_GROUP_CH = 16
SSM_GROUPS = 32
SSM_STATE = 64
MOE_GROUPS = 8
EXPERTS_PER_GROUP = 8
N_EXPERTS = 64
MOE_TOP_K = 2
D_EXPERT = 512
RMS_EPS = 1e-6

LANES = 128
DFT_RADIX = 64
SSM_CHUNK = 16
SSM_LANE_BLOCKS = SSM_WIDTH // LANES
GROUPS_PER_BLOCK = LANES // SSM_GROUP_CH
CHUNK_COLS = SSM_CHUNK * LANES
N_CHUNKS = SEQ // SSM_CHUNK
STATE_COLS = GROUPS_PER_BLOCK * SSM_STATE
MOE_ROWS = 256
MOE_BLOCKS = TOKENS * MOE_TOP_K // MOE_ROWS + N_EXPERTS
ROUTER_COLS = 128
VMEM_LIMIT = 48 * 1024 * 1024


def _cparams(sem, vmem=VMEM_LIMIT):
    return pltpu.CompilerParams(dimension_semantics=sem, vmem_limit_bytes=vmem)


IN_TM = 512


def _inproj_kernel(x_ref, g_ref, w_ref, cdft_ref, vf_ref, us_ref, gates_ref):
    x = x_ref[0]
    inv = lax.rsqrt(jnp.mean(x * x, axis=-1, keepdims=True) + RMS_EPS)
    h = (x * inv * g_ref[...]).astype(BF16)
    zf = jnp.dot(h, w_ref[:, 0:FOURIER_WIDTH], preferred_element_type=F32).astype(BF16)
    cdft = cdft_ref[...].astype(BF16)
    for g in range(FOURIER_GROUPS):
        sl = slice(g * LANES, (g + 1) * LANES)
        v = jnp.dot(zf[:, sl], cdft, preferred_element_type=F32)
        vf_ref[0, 0, :, sl] = v[:, :LANES].astype(BF16)
        vf_ref[0, 1, :, sl] = v[:, LANES:].astype(BF16)
    zs = jnp.dot(h, w_ref[:, FOURIER_WIDTH:FOURIER_WIDTH + SSM_WIDTH], preferred_element_type=F32)
    for j in range(SSM_LANE_BLOCKS):
        us_ref[j] = zs[:, j * LANES:(j + 1) * LANES].astype(BF16)
    base = FOURIER_WIDTH + SSM_WIDTH
    for n in range(4):
        zg = jnp.dot(h, w_ref[:, base + n * 512: base + (n + 1) * 512], preferred_element_type=F32)
        gates_ref[:, n * 512:(n + 1) * 512] = jax.nn.sigmoid(zg).astype(BF16)


def _inproj(x, g, w_in, cdft):
    nt = SEQ // IN_TM
    return pl.pallas_call(
        _inproj_kernel,
        grid=(BATCH, nt),
        in_specs=[
            pl.BlockSpec((1, IN_TM, D_MODEL), lambda b, i: (b, i, 0)),
            pl.BlockSpec((1, D_MODEL), lambda b, i: (0, 0)),
            pl.BlockSpec(w_in.shape, lambda b, i: (0, 0)),
            pl.BlockSpec(cdft.shape, lambda b, i: (0, 0)),
        ],
        out_specs=[
            pl.BlockSpec((1, 2, IN_TM, FOURIER_WIDTH), lambda b, i: (b, 0, i, 0)),
            pl.BlockSpec((SSM_LANE_BLOCKS, IN_TM, LANES), lambda b, i: (0, b * nt + i, 0)),
            pl.BlockSpec((IN_TM, 2 * D_MODEL), lambda b, i: (b * nt + i, 0)),
        ],
        out_shape=[
            jax.ShapeDtypeStruct((BATCH, 2, SEQ, FOURIER_WIDTH), BF16),
            jax.ShapeDtypeStruct((SSM_LANE_BLOCKS, TOKENS, LANES), BF16),
            jax.ShapeDtypeStruct((TOKENS, 2 * D_MODEL), BF16),
        ],
        compiler_params=_cparams(("parallel", "parallel")),
        name="inproj",
    )(x, g, w_in, cdft)


DFT_S2_PER_STEP = 8


def _dft1_kernel(v_ref, f_ref, twr_ref, twi_ref, a_ref):
    f1 = f_ref[...].astype(BF16)
    for s in range(DFT_S2_PER_STEP):
        blk = v_ref[0, :, s * FOURIER_WIDTH:(s + 1) * FOURIER_WIDTH]
        r = jnp.dot(f1, blk, preferred_element_type=F32)
        ar, ai = r[:DFT_RADIX], r[DFT_RADIX:]
        tr = jnp.tile(twr_ref[s], (1, FOURIER_WIDTH // LANES))
        ti = jnp.tile(twi_ref[s], (1, FOURIER_WIDTH // LANES))
        a_ref[0, 0, s] = (ar * tr - ai * ti).astype(BF16)
        a_ref[0, 1, s] = (ar * ti + ai * tr).astype(BF16)


def _dft1(v, f1, twr, twi):
    ns = DFT_RADIX // DFT_S2_PER_STEP
    cols = DFT_S2_PER_STEP * FOURIER_WIDTH
    return pl.pallas_call(
        _dft1_kernel,
        grid=(BATCH, ns),
        in_specs=[
            pl.BlockSpec((1, 2 * DFT_RADIX, cols), lambda b, s: (b, 0, s)),
            pl.BlockSpec(f1.shape, lambda b, s: (0, 0)),
            pl.BlockSpec((DFT_S2_PER_STEP, DFT_RADIX, LANES), lambda b, s: (s, 0, 0)),
            pl.BlockSpec((DFT_S2_PER_STEP, DFT_RADIX, LANES), lambda b, s: (s, 0, 0)),
        ],
        out_specs=pl.BlockSpec((1, 2, DFT_S2_PER_STEP, DFT_RADIX, FOURIER_WIDTH),
                               lambda b, s: (b, 0, s, 0, 0)),
        out_shape=jax.ShapeDtypeStruct((BATCH, 2, DFT_RADIX, DFT_RADIX, FOURIER_WIDTH), BF16),
        compiler_params=_cparams(("parallel", "parallel")),
        name="dft1",
    )(v, f1, twr, twi)


DFT2_COLS = 4096


def _dft2_kernel(a_ref, f_ref, o_ref):
    o_ref[0] = jnp.dot(f_ref[...].astype(BF16), a_ref[0], preferred_element_type=F32).astype(BF16)


def _dft2(a, f2):
    n = DFT_RADIX * FOURIER_WIDTH
    return pl.pallas_call(
        _dft2_kernel,
        grid=(BATCH, n // DFT2_COLS),
        in_specs=[
            pl.BlockSpec((1, 2 * DFT_RADIX, DFT2_COLS), lambda b, s: (b, 0, s)),
            pl.BlockSpec(f2.shape, lambda b, s: (0, 0)),
        ],
        out_specs=pl.BlockSpec((1, DFT_RADIX, DFT2_COLS), lambda b, s: (b, 0, s)),
        out_shape=jax.ShapeDtypeStruct((BATCH, DFT_RADIX, n), BF16),
        compiler_params=_cparams(("parallel", "parallel")),
        name="dft2",
    )(a, f2)


def _dft_constants():
    r = DFT_RADIX
    k = np.arange(r)
    ang = 2.0 * np.pi * np.outer(k, k) / r
    c, s = np.cos(ang), np.sin(ang)
    f1 = np.block([[c, s], [-s, c]])
    f2 = np.concatenate([c, s], axis=1)
    tw = 2.0 * np.pi * np.outer(k, k) / SEQ
    scale = 1.0 / math.sqrt(SEQ)
    twr = np.repeat((np.cos(tw) * scale)[:, :, None], LANES, axis=2)
    twi = np.repeat((-np.sin(tw) * scale)[:, :, None], LANES, axis=2)
    kc = np.arange(FOURIER_GROUP_CH)
    angc = 2.0 * np.pi * np.outer(kc, kc) / FOURIER_GROUP_CH
    cs = 1.0 / math.sqrt(FOURIER_GROUP_CH)
    cdft = np.concatenate([np.cos(angc) * cs, -np.sin(angc) * cs], axis=1)
    return tuple(jnp.asarray(v, F32) for v in (f1, f2, twr, twi, cdft))


def _ssm_operators(a_re, a_im, log_dt, b_re, b_im, c_re, c_im):
    q_len = SSM_CHUNK
    jb, gb, ch, ns = SSM_LANE_BLOCKS, GROUPS_PER_BLOCK, SSM_GROUP_CH, SSM_STATE
    dt = jnp.exp(log_dt)[..., None]
    lr, li = a_re * dt, a_im * dt
    steps = jnp.arange(q_len + 1, dtype=F32)
    mag = jnp.exp(lr[..., None] * steps)
    ang = li[..., None] * steps
    pr, pi = mag * jnp.cos(ang), mag * jnp.sin(ang)
    ar, ai = pr[..., 1], pi[..., 1]
    den = a_re * a_re + a_im * a_im
    cr = ((ar - 1.0) * a_re + ai * a_im) / den
    ci = (ai * a_re - (ar - 1.0) * a_im) / den
    bbr = cr[..., None] * b_re - ci[..., None] * b_im
    bbi = cr[..., None] * b_im + ci[..., None] * b_re
    prq, piq = pr[..., :q_len], pi[..., :q_len]
    cpr = jnp.einsum('dgcn,dgnt->dgtcn', c_re, prq) - jnp.einsum('dgcn,dgnt->dgtcn', c_im, piq)
    cpi = jnp.einsum('dgcn,dgnt->dgtcn', c_re, piq) + jnp.einsum('dgcn,dgnt->dgtcn', c_im, prq)
    kern = jnp.einsum('dgtcn,dgne->dgtce', cpr, bbr) - jnp.einsum('dgtcn,dgne->dgtce', cpi, bbi)
    qi = jnp.arange(q_len)
    lag = qi[None, :] - qi[:, None]
    k_f = kern[0][:, jnp.maximum(lag, 0)] * (lag >= 0)[None, :, :, None, None].astype(F32)
    k_b = kern[1][:, jnp.maximum(-lag, 0)] * (lag <= 0)[None, :, :, None, None].astype(F32)
    kc = (k_f + k_b).reshape(jb, gb, q_len, q_len, ch, ch)
    kc = jnp.transpose(kc, (0, 2, 1, 5, 3, 4)).reshape(jb, CHUNK_COLS, q_len * ch)

    def state_in(pwr, pwi, br, bi):
        wr = jnp.einsum('gnq,gne->gqen', pwr, br) - jnp.einsum('gnq,gne->gqen', pwi, bi)
        wi = jnp.einsum('gnq,gne->gqen', pwr, bi) + jnp.einsum('gnq,gne->gqen', pwi, br)
        return jnp.stack([wr, wi])

    rev = q_len - 1 - qi
    wsc = jnp.stack([state_in(pr[0][..., rev], pi[0][..., rev], bbr[0], bbi[0]),
                     state_in(pr[1][..., qi], pi[1][..., qi], bbr[1], bbi[1])])
    wsc = wsc.reshape(2, 2, jb, gb, q_len, ch, ns)
    wsc = jnp.transpose(wsc, (2, 4, 3, 5, 0, 1, 6)).reshape(jb, CHUNK_COLS, 4 * ns)

    def state_out(pwr, pwi, c_r, c_i):
        o_r = jnp.einsum('gcn,gnq->gnqc', c_r, pwr) - jnp.einsum('gcn,gnq->gnqc', c_i, pwi)
        o_i = jnp.einsum('gcn,gnq->gnqc', c_r, pwi) + jnp.einsum('gcn,gnq->gnqc', c_i, pwr)
        return jnp.stack([o_r, -o_i])

    woc = jnp.stack([state_out(pr[0][..., qi + 1], pi[0][..., qi + 1], c_re[0], c_im[0]),
                     state_out(pr[1][..., q_len - qi], pi[1][..., q_len - qi], c_re[1], c_im[1])])
    woc = woc.reshape(2, 2, jb, gb, ns, q_len * ch)
    woc = jnp.transpose(woc, (2, 0, 1, 3, 4, 5)).reshape(jb, 4 * STATE_COLS, q_len * ch)

    aq = jnp.stack([pr[0][..., q_len], pi[0][..., q_len], pr[1][..., q_len], pi[1][..., q_len]])
    aq = aq.reshape(4, jb, 1, STATE_COLS)
    aq = jnp.broadcast_to(aq, (4, jb, BATCH, STATE_COLS))
    aq = jnp.transpose(aq, (1, 0, 2, 3)).reshape(jb, 4, BATCH * STATE_COLS)
    return kc.astype(BF16), wsc.astype(BF16), woc.astype(BF16), aq


EXPAND_ROWS = 256
GROUP_SHIFT_CH = 4
GROUP_SHIFT_STATE = 6


def _expand_block_diag(compact_ref, sel_ref, out_ref, row_shift, col_shift):
    n_rows, n_cols = out_ref.shape
    col_g = (lax.broadcasted_iota(jnp.int32, (EXPAND_ROWS, n_cols), 1) >> col_shift) & (GROUPS_PER_BLOCK - 1)
    for r in range(n_rows // EXPAND_ROWS):
        rows = slice(r * EXPAND_ROWS, (r + 1) * EXPAND_ROWS)
        t = jnp.dot(compact_ref[rows, :], sel_ref[...], preferred_element_type=F32)
        row_i = lax.broadcasted_iota(jnp.int32, (EXPAND_ROWS, n_cols), 0) + r * EXPAND_ROWS
        row_g = (row_i >> row_shift) & (GROUPS_PER_BLOCK - 1)
        out_ref[rows, :] = jnp.where(row_g == col_g, t, 0.0).astype(BF16)


def _replication_matrices():
    half = SSM_CHUNK // 2
    e_qc = np.zeros((half, SSM_GROUP_CH, half, GROUPS_PER_BLOCK, SSM_GROUP_CH), np.float32)
    for q in range(half):
        for c in range(SSM_GROUP_CH):
            e_qc[q, c, q, :, c] = 1.0
    e_rn = np.zeros((2, SSM_STATE, 2, GROUPS_PER_BLOCK, SSM_STATE), np.float32)
    for r in range(2):
        for n in range(SSM_STATE):
            e_rn[r, n, r, :, n] = 1.0
    return (jnp.asarray(e_qc.reshape(half * SSM_GROUP_CH, -1), BF16),
            jnp.asarray(e_rn.reshape(2 * SSM_STATE, -1), BF16))


def _ssm_state_kernel(x_ref, wsc_ref, sel_ref, sre_ref, sim_ref, ws_scr):
    @pl.when(pl.program_id(2) == 0)
    def _():
        _expand_block_diag(wsc_ref.at[0], sel_ref, ws_scr, GROUP_SHIFT_CH, GROUP_SHIFT_STATE)

    s = jnp.dot(x_ref[0], ws_scr[...], preferred_element_type=F32)
    sre_ref[0] = s[:, :STATE_COLS]
    sim_ref[0] = s[:, STATE_COLS:]


def _ssm_states(xr, wsc, e_rn):
    out_spec = pl.BlockSpec((1, N_CHUNKS, STATE_COLS), lambda j, d, b: (j, 0, d * BATCH + b))
    shape = jax.ShapeDtypeStruct((SSM_LANE_BLOCKS, N_CHUNKS, 2 * BATCH * STATE_COLS), F32)
    return pl.pallas_call(
        _ssm_state_kernel,
        grid=(SSM_LANE_BLOCKS, 2, BATCH),
        in_specs=[
            pl.BlockSpec((1, N_CHUNKS, CHUNK_COLS), lambda j, d, b: (j, b, 0)),
            pl.BlockSpec((1, CHUNK_COLS, 2 * SSM_STATE), lambda j, d, b: (j, 0, d)),
            pl.BlockSpec(e_rn.shape, lambda j, d, b: (0, 0)),
        ],
        out_specs=[out_spec, out_spec],
        out_shape=[shape, shape],
        scratch_shapes=[pltpu.VMEM((CHUNK_COLS, 2 * STATE_COLS), BF16)],
        compiler_params=_cparams(("parallel", "parallel", "arbitrary")),
        name="ssm_states",
    )(xr, wsc, e_rn)


def _ssm_scan_kernel(sre_ref, sim_ref, aq_ref, hre_ref, him_ref):
    half = BATCH * STATE_COLS
    fwd, bwd = slice(0, half), slice(half, 2 * half)
    a0r, a0i, a1r, a1i = aq_ref[0, 0:1], aq_ref[0, 1:2], aq_ref[0, 2:3], aq_ref[0, 3:4]

    def body(k, carry):
        fr, fi, br, bi = carry
        kb = N_CHUNKS - 1 - k
        hre_ref[0, pl.ds(k, 1), fwd] = fr
        him_ref[0, pl.ds(k, 1), fwd] = fi
        hre_ref[0, pl.ds(kb, 1), bwd] = br
        him_ref[0, pl.ds(kb, 1), bwd] = bi
        sfr, sfi = sre_ref[0, pl.ds(k, 1), fwd], sim_ref[0, pl.ds(k, 1), fwd]
        sbr, sbi = sre_ref[0, pl.ds(kb, 1), bwd], sim_ref[0, pl.ds(kb, 1), bwd]
        return (a0r * fr - a0i * fi + sfr, a0r * fi + a0i * fr + sfi,
                a1r * br - a1i * bi + sbr, a1r * bi + a1i * br + sbi)

    z = jnp.zeros((1, half), F32)
    lax.fori_loop(0, N_CHUNKS, body, (z, z, z, z))


def _ssm_scan(sre, sim, aq):
    spec = pl.BlockSpec((1, N_CHUNKS, 2 * BATCH * STATE_COLS), lambda j: (j, 0, 0))
    shape = jax.ShapeDtypeStruct(sre.shape, F32)
    return pl.pallas_call(
        _ssm_scan_kernel,
        grid=(SSM_LANE_BLOCKS,),
        in_specs=[spec, spec, pl.BlockSpec((1, 4, BATCH * STATE_COLS), lambda j: (j, 0, 0))],
        out_specs=[spec, spec],
        out_shape=[shape, shape],
        compiler_params=_cparams(("parallel",)),
        name="ssm_scan",
    )(sre, sim, aq)


SSM_OUT_TN = CHUNK_COLS // 2


def _ssm_out_kernel(x_ref, kc_ref, woc_ref, sel_ref, h0r_ref, h0i_ref, h1r_ref, h1i_ref, y_ref, m_scr, wo_scr):
    @pl.when(pl.program_id(2) == 0)
    def _():
        _expand_block_diag(kc_ref.at[0], sel_ref, m_scr, GROUP_SHIFT_CH, GROUP_SHIFT_CH)
        _expand_block_diag(woc_ref.at[0], sel_ref, wo_scr, GROUP_SHIFT_STATE, GROUP_SHIFT_CH)

    acc = jnp.dot(x_ref[0], m_scr[...], preferred_element_type=F32)
    for i, h_ref in enumerate((h0r_ref, h0i_ref, h1r_ref, h1i_ref)):
        acc += jnp.dot(h_ref[0].astype(BF16), wo_scr[i * STATE_COLS:(i + 1) * STATE_COLS, :],
                       preferred_element_type=F32)
    y_ref[0] = acc.astype(BF16)


def _ssm_out(xr, kc, woc, e_qc, hre, him):
    def hspec(d):
        return pl.BlockSpec((1, N_CHUNKS, STATE_COLS), lambda j, n, b: (j, 0, d * BATCH + b))
    half_cols = kc.shape[2] // 2
    return pl.pallas_call(
        _ssm_out_kernel,
        grid=(SSM_LANE_BLOCKS, CHUNK_COLS // SSM_OUT_TN, BATCH),
        in_specs=[
            pl.BlockSpec((1, N_CHUNKS, CHUNK_COLS), lambda j, n, b: (j, b, 0)),
            pl.BlockSpec((1, CHUNK_COLS, half_cols), lambda j, n, b: (j, 0, n)),
            pl.BlockSpec((1, 4 * STATE_COLS, half_cols), lambda j, n, b: (j, 0, n)),
            pl.BlockSpec(e_qc.shape, lambda j, n, b: (0, 0)),
            hspec(0), hspec(0), hspec(1), hspec(1),
        ],
        out_specs=pl.BlockSpec((1, N_CHUNKS, SSM_OUT_TN), lambda j, n, b: (j, b, n)),
        out_shape=jax.ShapeDtypeStruct((SSM_LANE_BLOCKS, BATCH * N_CHUNKS, CHUNK_COLS), BF16),
        scratch_shapes=[pltpu.VMEM((CHUNK_COLS, SSM_OUT_TN), BF16), pltpu.VMEM((4 * STATE_COLS, SSM_OUT_TN), BF16)],
        compiler_params=_cparams(("parallel", "parallel", "arbitrary")),
        name="ssm_out",
    )(xr, kc, woc, e_qc, hre, him, hre, him)


MERGE_TM = 512
GELU_C = math.sqrt(2.0 / math.pi)
PACK_SUB = D_MODEL // LANES


def _split_bf16(v):
    hi = v.astype(BF16)
    lo = (v - hi.astype(F32)).astype(BF16)
    return hi, lo


def _pack_rows(v, out_ref):
    for s in range(PACK_SUB):
        out_ref[pl.ds(s, v.shape[0], stride=PACK_SUB), :] = v[:, s * LANES:(s + 1) * LANES]


def _unpack_rows(buf_ref, start, rows):
    return jnp.concatenate([buf_ref[pl.ds(start + s, rows, stride=PACK_SUB), :] for s in range(PACK_SUB)], axis=1)


def _merge_kernel(x_ref, fm_ref, yc_ref, us_ref, gates_ref, dskip_ref, wf_ref, wglu_ref, ws_ref, wo_ref,
                  ng_ref, wrh_ref, wrl_ref, rb_ref, x1_ref, hp_ref, logit_ref):
    conv = jnp.concatenate([yc_ref[j].astype(F32) for j in range(SSM_LANE_BLOCKS)], axis=-1)
    u = jnp.concatenate([us_ref[j].astype(F32) for j in range(SSM_LANE_BLOCKS)], axis=-1)
    y = conv + dskip_ref[...] * u
    y = 0.5 * y * (1.0 + jnp.tanh(GELU_C * (y + 0.044715 * (y * y * y))))
    glu = jax.nn.sigmoid(jnp.dot(y.astype(BF16), wglu_ref[...], preferred_element_type=F32))
    y_s = jnp.dot((y * glu).astype(BF16), ws_ref[...], preferred_element_type=F32)
    y_f = jnp.dot(fm_ref[...], wf_ref[...], preferred_element_type=F32)
    merged = (gates_ref[:, :D_MODEL].astype(F32) * y_f + gates_ref[:, D_MODEL:].astype(F32) * y_s)
    x1 = x_ref[...] + jnp.dot(merged.astype(BF16), wo_ref[...], preferred_element_type=F32)
    x1_ref[...] = x1
    inv = lax.rsqrt(jnp.mean(x1 * x1, axis=-1, keepdims=True) + RMS_EPS)
    hn = x1 * inv * ng_ref[...]
    _pack_rows(hn, hp_ref)
    hi, lo = _split_bf16(hn)
    logits = (jnp.dot(hi, wrh_ref[...], preferred_element_type=F32)
              + jnp.dot(lo, wrh_ref[...], preferred_element_type=F32)
              + jnp.dot(hi, wrl_ref[...], preferred_element_type=F32))
    logit_ref[...] = logits + rb_ref[...]


def _merge(x, fmix, yconv, us, gates, dskip, wf, wglu, ws, wo, ng, wrh, wrl, rb):
    tm = MERGE_TM
    full = lambda a: pl.BlockSpec(a.shape, lambda i: (0,) * a.ndim)
    return pl.pallas_call(
        _merge_kernel,
        grid=(TOKENS // tm,),
        in_specs=[
            pl.BlockSpec((tm, D_MODEL), lambda i: (i, 0)),
            pl.BlockSpec((tm, FOURIER_WIDTH), lambda i: (i, 0)),
            pl.BlockSpec((SSM_LANE_BLOCKS, tm, LANES), lambda i: (0, i, 0)),
            pl.BlockSpec((SSM_LANE_BLOCKS, tm, LANES), lambda i: (0, i, 0)),
            pl.BlockSpec((tm, 2 * D_MODEL), lambda i: (i, 0)),
            full(dskip), full(wf), full(wglu), full(ws), full(wo), full(ng), full(wrh), full(wrl), full(rb),
        ],
        out_specs=[
            pl.BlockSpec((tm, D_MODEL), lambda i: (i, 0)),
            pl.BlockSpec((tm * PACK_SUB, LANES), lambda i: (i, 0)),
            pl.BlockSpec((tm, ROUTER_COLS), lambda i: (i, 0)),
        ],
        out_shape=[
            jax.ShapeDtypeStruct((TOKENS, D_MODEL), F32),
            jax.ShapeDtypeStruct((TOKENS * PACK_SUB, LANES), F32),
            jax.ShapeDtypeStruct((TOKENS, ROUTER_COLS), F32),
        ],
        compiler_params=_cparams(("parallel",)),
        name="merge",
    )(x, fmix, yconv, us, gates, dskip, wf, wglu, ws, wo, ng, wrh, wrl, rb)


ROUTE_TM = 512
EXPERT_LANE0 = MOE_GROUPS
INFO_EXPERT, INFO_RANK, INFO_GATE = 0, 2, 4


def _route_kernel(lg_ref, info_ref, cnt_ref, carry):
    @pl.when(pl.program_id(0) == 0)
    def _():
        carry[...] = jnp.zeros_like(carry)

    lg = lg_ref[...]
    tm = lg.shape[0]
    col_i = lax.broadcasted_iota(jnp.int32, lg.shape, 1)
    col = col_i.astype(F32)
    neg = jnp.float32(-jnp.inf)
    none = jnp.float32(ROUTER_COLS)

    def row_max(v):
        return jnp.max(v, axis=-1, keepdims=True)

    def first_at(v, m):
        return jnp.min(jnp.where(v == m, col, none), axis=-1, keepdims=True)

    gl = jnp.where(col_i < MOE_GROUPS, lg, neg)
    gmax = row_max(gl)
    p_g = 1.0 / jnp.sum(jnp.exp(gl - gmax), axis=-1, keepdims=True)
    lo = EXPERT_LANE0 + first_at(gl, gmax) * EXPERTS_PER_GROUP
    el = jnp.where((col >= lo) & (col < lo + EXPERTS_PER_GROUP), lg, neg)
    l1 = row_max(el)
    i1 = first_at(el, l1)
    el2 = jnp.where(col == i1, neg, el)
    l2 = row_max(el2)
    i2 = first_at(el2, l2)
    r = jnp.exp(l2 - l1)
    w1 = p_g / (1.0 + r)
    w2 = w1 * r

    hit1, hit2 = col == i1, col == i2
    onehot = jnp.where(hit1 | hit2, 1.0, 0.0)
    earlier = lax.broadcasted_iota(jnp.int32, (tm, tm), 0) > lax.broadcasted_iota(jnp.int32, (tm, tm), 1)
    before = jnp.dot(jnp.where(earlier, 1.0, 0.0).astype(BF16), onehot.astype(BF16),
                     preferred_element_type=F32) + carry[...]
    rank1 = jnp.sum(jnp.where(hit1, before, 0.0), axis=-1, keepdims=True)
    rank2 = jnp.sum(jnp.where(hit2, before, 0.0), axis=-1, keepdims=True)
    carry[...] += jnp.sum(onehot, axis=0, keepdims=True)
    cnt_ref[...] = carry[...]

    info = jnp.zeros(lg.shape, F32)
    for lane, v in ((INFO_EXPERT, i1 - EXPERT_LANE0), (INFO_EXPERT + 1, i2 - EXPERT_LANE0), (INFO_RANK, rank1),
                    (INFO_RANK + 1, rank2), (INFO_GATE, w1), (INFO_GATE + 1, w2)):
        info = jnp.where(col_i == lane, v, info)
    info_ref[...] = info


def _route(logits):
    return pl.pallas_call(
        _route_kernel,
        grid=(TOKENS // ROUTE_TM,),
        in_specs=[pl.BlockSpec((ROUTE_TM, ROUTER_COLS), lambda i: (i, 0))],
        out_specs=[pl.BlockSpec((ROUTE_TM, ROUTER_COLS), lambda i: (i, 0)),
                   pl.BlockSpec((1, ROUTER_COLS), lambda i: (0, 0))],
        out_shape=[jax.ShapeDtypeStruct((TOKENS, ROUTER_COLS), F32),
                   jax.ShapeDtypeStruct((1, ROUTER_COLS), F32)],
        scratch_shapes=[pltpu.VMEM((1, ROUTER_COLS), F32)],
        compiler_params=_cparams(("arbitrary",)),
        name="route",
    )(logits)


def _dispatch_plan(info, counts):
    expert = info[:, INFO_EXPERT:INFO_EXPERT + MOE_TOP_K].astype(jnp.int32)
    rank = info[:, INFO_RANK:INFO_RANK + MOE_TOP_K].astype(jnp.int32)
    cnt = counts[0, EXPERT_LANE0:EXPERT_LANE0 + N_EXPERTS].astype(jnp.int32)
    padded = ((cnt + MOE_ROWS - 1) // MOE_ROWS) * MOE_ROWS
    pends = jnp.cumsum(padded)
    pstarts = pends - padded
    ids = jnp.arange(N_EXPERTS, dtype=jnp.int32)
    dest = rank + jnp.sum(jnp.where(expert[..., None] == ids, pstarts, 0), axis=-1)
    n_used = pends[-1] // MOE_ROWS
    blocks = jnp.arange(MOE_BLOCKS, dtype=jnp.int32)
    block_e = jnp.sum((pends[None, :] <= (blocks * MOE_ROWS)[:, None]).astype(jnp.int32), axis=1)
    block_e = jnp.minimum(block_e, N_EXPERTS - 1)
    last_e = jnp.sum(jnp.where(blocks == n_used - 1, block_e, 0))
    block_e = jnp.where(blocks < n_used, block_e, last_e)
    total = jnp.full((1,), MOE_BLOCKS * MOE_ROWS, jnp.int32)
    pad_lo = jnp.concatenate([pstarts + cnt, pends[-1:]])
    pad_hi = jnp.concatenate([pends, total])
    return dest.reshape(TOKENS * MOE_TOP_K), block_e, n_used.reshape(1), pad_lo, pad_hi


def _invert_kernel(dest_ref, pad_lo_ref, pad_hi_ref, tok_ref):
    def pad_range(e, c):
        def fill(s, c2):
            tok_ref[s] = s & (TOKENS - 1)
            return c2

        lax.fori_loop(pad_lo_ref[e], pad_hi_ref[e], fill, 0)
        return c

    lax.fori_loop(0, N_EXPERTS + 1, pad_range, 0)

    def place(t, c):
        for k in range(MOE_TOP_K):
            tok_ref[dest_ref[MOE_TOP_K * t + k]] = t
        return c

    lax.fori_loop(0, TOKENS, place, 0, unroll=4)


def _invert(dest, pad_lo, pad_hi):
    smem = pl.BlockSpec(memory_space=pltpu.SMEM)
    return pl.pallas_call(
        _invert_kernel,
        in_specs=[smem, smem, smem],
        out_specs=smem,
        out_shape=jax.ShapeDtypeStruct((MOE_BLOCKS * MOE_ROWS,), jnp.int32),
        name="invert",
    )(dest, pad_lo, pad_hi)


MOE_SLOT_ROWS = MOE_ROWS * PACK_SUB


def _moe_kernel(tok_ref, be_ref, nu_ref, hp_ref, wg_ref, wu_ref, wd_ref, y_ref, xbuf, sem):
    i = pl.program_id(0)
    n_used = nu_ref[0]

    def slot_rows(slot):
        return xbuf.at[pl.ds(pl.multiple_of(slot * MOE_SLOT_ROWS, MOE_SLOT_ROWS), MOE_SLOT_ROWS), :]

    def gather(block, slot):
        base = block * MOE_ROWS
        for r in range(MOE_ROWS):
            src = hp_ref.at[pl.ds(pl.multiple_of(tok_ref[base + r] * PACK_SUB, PACK_SUB), PACK_SUB), :]
            dst = xbuf.at[pl.ds(pl.multiple_of(slot * MOE_SLOT_ROWS + r * PACK_SUB, PACK_SUB), PACK_SUB), :]
            pltpu.make_async_copy(src, dst, sem.at[slot]).start()

    @pl.when(i == 0)
    def _():
        gather(0, 0)

    @pl.when(i + 1 < n_used)
    def _():
        gather(i + 1, (i + 1) % 2)

    @pl.when(i < n_used)
    def _():
        slot = i % 2
        pltpu.make_async_copy(slot_rows(slot), slot_rows(slot), sem.at[slot]).wait()
        xb = _unpack_rows(xbuf, slot * MOE_SLOT_ROWS, MOE_ROWS).astype(BF16)
        a = jnp.dot(xb, wg_ref[0].astype(BF16), preferred_element_type=F32)
        u = jnp.dot(xb, wu_ref[0].astype(BF16), preferred_element_type=F32)
        act = (a * jax.nn.sigmoid(a) * u).astype(BF16)
        _pack_rows(jnp.dot(act, wd_ref[0].astype(BF16), preferred_element_type=F32), y_ref)

    @pl.when(i >= n_used)
    def _():
        y_ref[...] = jnp.zeros_like(y_ref)


def _moe(row_tok, block_e, n_used, hp, w_gate, w_up, w_down):
    grid_spec = pltpu.PrefetchScalarGridSpec(
        num_scalar_prefetch=3,
        grid=(MOE_BLOCKS,),
        in_specs=[
            pl.BlockSpec(memory_space=pl.ANY),
            pl.BlockSpec((1, D_MODEL, D_EXPERT), lambda i, tok, be, nu: (be[i], 0, 0)),
            pl.BlockSpec((1, D_MODEL, D_EXPERT), lambda i, tok, be, nu: (be[i], 0, 0)),
            pl.BlockSpec((1, D_EXPERT, D_MODEL), lambda i, tok, be, nu: (be[i], 0, 0)),
        ],
        out_specs=pl.BlockSpec((MOE_SLOT_ROWS, LANES), lambda i, tok, be, nu: (i, 0)),
        scratch_shapes=[pltpu.VMEM((2 * MOE_SLOT_ROWS, LANES), F32), pltpu.SemaphoreType.DMA((2,))],
    )
    return pl.pallas_call(
        _moe_kernel,
        grid_spec=grid_spec,
        out_shape=jax.ShapeDtypeStruct((MOE_BLOCKS * MOE_SLOT_ROWS, LANES), F32),
        compiler_params=_cparams(("arbitrary",)),
        name="moe",
    )(row_tok, block_e, n_used, hp, w_gate, w_up, w_down)


COMBINE_TM = 256


def _combine_kernel(dest_ref, x1_ref, info_ref, g_ref, y_ref, o_ref, ybuf, sem):
    i = pl.program_id(0)
    n = pl.num_programs(0)
    k_rows = COMBINE_TM * PACK_SUB
    slot_rows = MOE_TOP_K * k_rows

    def slot_ref(slot):
        return ybuf.at[pl.ds(pl.multiple_of(slot * slot_rows, slot_rows), slot_rows), :]

    def gather(tile, slot):
        base = tile * (COMBINE_TM * MOE_TOP_K)
        for r in range(COMBINE_TM):
            for k in range(MOE_TOP_K):
                row = dest_ref[base + r * MOE_TOP_K + k]
                src = y_ref.at[pl.ds(pl.multiple_of(row * PACK_SUB, PACK_SUB), PACK_SUB), :]
                at = slot * slot_rows + k * k_rows + r * PACK_SUB
                dst = ybuf.at[pl.ds(pl.multiple_of(at, PACK_SUB), PACK_SUB), :]
                pltpu.make_async_copy(src, dst, sem.at[slot]).start(priority=k)

    @pl.when(i == 0)
    def _():
        gather(0, 0)

    @pl.when(i + 1 < n)
    def _():
        gather(i + 1, (i + 1) % 2)

    slot = i % 2
    pltpu.make_async_copy(slot_ref(slot), slot_ref(slot), sem.at[slot]).wait()
    x2 = x1_ref[...]
    for k in range(MOE_TOP_K):
        yk = _unpack_rows(ybuf, slot * slot_rows + k * k_rows, COMBINE_TM)
        x2 = x2 + info_ref[:, INFO_GATE + k:INFO_GATE + k + 1] * yk
    inv = lax.rsqrt(jnp.mean(x2 * x2, axis=-1, keepdims=True) + RMS_EPS)
    o_ref[...] = x2 * inv * g_ref[...]


def _combine(dest, x1, info, g, y_pad):
    tm = COMBINE_TM
    grid_spec = pltpu.PrefetchScalarGridSpec(
        num_scalar_prefetch=1,
        grid=(TOKENS // tm,),
        in_specs=[
            pl.BlockSpec((tm, D_MODEL), lambda i, d: (i, 0)),
            pl.BlockSpec((tm, ROUTER_COLS), lambda i, d: (i, 0)),
            pl.BlockSpec((1, D_MODEL), lambda i, d: (0, 0)),
            pl.BlockSpec(memory_space=pl.ANY),
        ],
        out_specs=pl.BlockSpec((tm, D_MODEL), lambda i, d: (i, 0)),
        scratch_shapes=[pltpu.VMEM((2 * MOE_TOP_K * tm * PACK_SUB, LANES), F32), pltpu.SemaphoreType.DMA((2,))],
    )
    return pl.pallas_call(
        _combine_kernel,
        grid_spec=grid_spec,
        out_shape=jax.ShapeDtypeStruct((TOKENS, D_MODEL), F32),
        compiler_params=_cparams(("arbitrary",)),
        name="combine",
    )(dest, x1, info, g, y_pad)


def kernel(x, mix_norm_g, w_in, w_fourier_out, ssm_A_re, ssm_A_im, ssm_log_dt, ssm_B_re, ssm_B_im, ssm_C_re,
           ssm_C_im, ssm_D, ssm_w_glu, w_ssm_out, w_out, ffn_norm_g, router_group_w, router_group_b,
           router_expert_w, router_expert_b, expert_w_gate, expert_w_up, expert_w_down, final_norm_g):
    assert x.shape == (BATCH, SEQ, D_MODEL) and w_in.shape[0] == 1
    f1, f2, twr, twi, cdft = _dft_constants()
    e_qc, e_rn = _replication_matrices()

    vf, us, gates = _inproj(x, mix_norm_g[0][None], w_in[0].astype(BF16), cdft)
    a = _dft1(vf.reshape(BATCH, 2 * DFT_RADIX, DFT_RADIX * FOURIER_WIDTH), f1, twr, twi)
    fmix = _dft2(a.reshape(BATCH, 2 * DFT_RADIX, DFT_RADIX * FOURIER_WIDTH), f2)
    fmix = fmix.reshape(TOKENS, FOURIER_WIDTH)

    kc, wsc, woc, aq = _ssm_operators(ssm_A_re[0], ssm_A_im[0], ssm_log_dt[0], ssm_B_re[0], ssm_B_im[0],
                                      ssm_C_re[0], ssm_C_im[0])
    xr = us.reshape(SSM_LANE_BLOCKS, BATCH * N_CHUNKS, CHUNK_COLS)
    sre, sim = _ssm_states(xr, wsc, e_rn)
    hre, him = _ssm_scan(sre, sim, aq)
    yconv = _ssm_out(xr, kc, woc, e_qc, hre, him).reshape(SSM_LANE_BLOCKS, TOKENS, LANES)

    w_router = jnp.concatenate([router_group_w[0], router_expert_w[0]], axis=1)
    w_router = jnp.pad(w_router, ((0, 0), (0, ROUTER_COLS - w_router.shape[1])))
    b_router = jnp.concatenate([router_group_b[0], router_expert_b[0]])
    b_router = jnp.pad(b_router, (0, ROUTER_COLS - b_router.shape[0]))[None]
    wr_hi = w_router.astype(BF16)
    wr_lo = (w_router - wr_hi.astype(F32)).astype(BF16)
    x1, hp, logits = _merge(x.reshape(TOKENS, D_MODEL), fmix, yconv, us, gates, ssm_D[0][None],
                            w_fourier_out[0].astype(BF16), ssm_w_glu[0].astype(BF16), w_ssm_out[0].astype(BF16),
                            w_out[0].astype(BF16), ffn_norm_g[0][None], wr_hi, wr_lo, b_router)

    info, counts = _route(logits)
    dest, block_e, n_used, pad_lo, pad_hi = _dispatch_plan(info, counts)
    row_tok = _invert(dest, pad_lo, pad_hi)
    y_pad = _moe(row_tok, block_e, n_used, hp, expert_w_gate[0], expert_w_up[0], expert_w_down[0])
    out = _combine(dest, x1, info, final_norm_g[None], y_pad)
    return out.reshape(BATCH, SEQ, D_MODEL)
```

```python
import math

import numpy as np
import jax
import jax.numpy as jnp
from jax import lax
from jax.experimental import pallas as pl
from jax.experimental.pallas import tpu as pltpu

F32 = jnp.float32
BF16 = jnp.bfloat16

D_MODEL = 1024
BATCH = 4
SEQ = 4096
TOKENS = BATCH * SEQ
FOURIER_WIDTH = 512
FOURIER_GROUP_CH = 128
FOURIER_GROUPS = 4
SSM_WIDTH = 512
SSM_GROUP_CH = 16
SSM_GROUPS = 32
SSM_STATE = 64
MOE_GROUPS = 8
EXPERTS_PER_GROUP = 8
N_EXPERTS = 64
MOE_TOP_K = 2
D_EXPERT = 512
RMS_EPS = 1e-6

LANES = 128
SSM_CHUNK = 16
SSM_LANE_BLOCKS = SSM_WIDTH // LANES
GROUPS_PER_BLOCK = LANES // SSM_GROUP_CH
CHUNK_COLS = SSM_CHUNK * LANES
N_CHUNKS = SEQ // SSM_CHUNK
STATE_COLS = GROUPS_PER_BLOCK * SSM_STATE
MOE_ROWS = 256
MOE_BLOCKS = TOKENS * MOE_TOP_K // MOE_ROWS + N_EXPERTS
ROUTER_COLS = 128
VMEM_LIMIT = 48 * 1024 * 1024


def _cparams(sem, vmem=VMEM_LIMIT):
    return pltpu.CompilerParams(dimension_semantics=sem, vmem_limit_bytes=vmem)


IN_TM = 512


def _inproj_kernel(x_ref, g_ref, w_ref, cdft_ref, vf_ref, us_ref, xr_ref, gates_ref, zs_scr):
    x = x_ref[0]
    inv = lax.rsqrt(jnp.mean(x * x, axis=-1, keepdims=True) + RMS_EPS)
    h = (x * inv * g_ref[...]).astype(BF16)
    zf = jnp.dot(h, w_ref[:, 0:FOURIER_WIDTH], preferred_element_type=F32).astype(BF16)
    cdft = cdft_ref[...].astype(BF16)
    for g in range(FOURIER_GROUPS):
        sl = slice(g * LANES, (g + 1) * LANES)
        v = jnp.dot(zf[:, sl], cdft, preferred_element_type=F32)
        vf_ref[0, 0, :, sl] = v[:, :LANES].astype(BF16)
        vf_ref[0, 1, :, sl] = v[:, LANES:].astype(BF16)
    zs = jnp.dot(h, w_ref[:, FOURIER_WIDTH:FOURIER_WIDTH + SSM_WIDTH], preferred_element_type=F32)
    for j in range(SSM_LANE_BLOCKS):
        us_ref[j] = zs[:, j * LANES:(j + 1) * LANES].astype(BF16)
        zs_scr[j] = zs[:, j * LANES:(j + 1) * LANES]
    for j in range(SSM_LANE_BLOCKS):
        for q in range(SSM_CHUNK):
            piece = zs_scr[j, pl.ds(q, IN_TM // SSM_CHUNK, stride=SSM_CHUNK), :]
            xr_ref[j, :, q * LANES:(q + 1) * LANES] = piece.astype(BF16)
    base = FOURIER_WIDTH + SSM_WIDTH
    for n in range(4):
        zg = jnp.dot(h, w_ref[:, base + n * 512: base + (n + 1) * 512], preferred_element_type=F32)
        gates_ref[:, n * 512:(n + 1) * 512] = jax.nn.sigmoid(zg).astype(BF16)


def _inproj(x, g, w_in, cdft):
    nt = SEQ // IN_TM
    return pl.pallas_call(
        _inproj_kernel,
        grid=(BATCH, nt),
        in_specs=[
            pl.BlockSpec((1, IN_TM, D_MODEL), lambda b, i: (b, i, 0)),
            pl.BlockSpec((1, D_MODEL), lambda b, i: (0, 0)),
            pl.BlockSpec(w_in.shape, lambda b, i: (0, 0)),
            pl.BlockSpec(cdft.shape, lambda b, i: (0, 0)),
        ],
        out_specs=[
            pl.BlockSpec((1, 2, IN_TM, FOURIER_WIDTH), lambda b, i: (b, 0, i, 0)),
            pl.BlockSpec((SSM_LANE_BLOCKS, IN_TM, LANES), lambda b, i: (0, b * nt + i, 0)),
            pl.BlockSpec((SSM_LANE_BLOCKS, IN_TM // SSM_CHUNK, CHUNK_COLS), lambda b, i: (0, b * nt + i, 0)),
            pl.BlockSpec((IN_TM, 2 * D_MODEL), lambda b, i: (b * nt + i, 0)),
        ],
        out_shape=[
            jax.ShapeDtypeStruct((BATCH, 2, SEQ, FOURIER_WIDTH), BF16),
            jax.ShapeDtypeStruct((SSM_LANE_BLOCKS, TOKENS, LANES), BF16),
            jax.ShapeDtypeStruct((SSM_LANE_BLOCKS, TOKENS // SSM_CHUNK, CHUNK_COLS), BF16),
            jax.ShapeDtypeStruct((TOKENS, 2 * D_MODEL), BF16),
        ],
        scratch_shapes=[pltpu.VMEM((SSM_LANE_BLOCKS, IN_TM, LANES), F32)],
        compiler_params=_cparams(("parallel", "parallel")),
        name="inproj",
    )(x, g, w_in, cdft)


DFT_R1 = 8
DFT_R2 = SEQ // DFT_R1
DFT_LANES = 2 * LANES
DFT_ROWS = 16


def _cmul_const(z, w):
    re, im = z
    if abs(w.imag) < 1e-12:
        return (re, im) if abs(w.real - 1.0) < 1e-12 else (re * w.real, im * w.real)
    if abs(w.real) < 1e-12:
        return (im, -re) if abs(w.imag + 1.0) < 1e-12 else (-im * w.imag, re * w.imag)
    return re * w.real - im * w.imag, re * w.imag + im * w.real


def _fft_blocks(xs):
    n = len(xs)
    if n == 1:
        return xs
    even, odd = _fft_blocks(xs[0::2]), _fft_blocks(xs[1::2])
    out = [None] * n
    for k in range(n // 2):
        tr, ti = _cmul_const(odd[k], np.exp(-2j * np.pi * k / n))
        out[k] = (even[k][0] + tr, even[k][1] + ti)
        out[k + n // 2] = (even[k][0] - tr, even[k][1] - ti)
    return out


def _dft_kernel(v_ref, tw_ref, f_ref, o_ref, a_scr, o_scr):
    def tile(i, c):
        r0 = pl.multiple_of(i * DFT_ROWS, DFT_ROWS)
        for slab in range(DFT_LANES // LANES):
            lanes = slice(slab * LANES, (slab + 1) * LANES)
            xs = [(v_ref[0, 0, pl.ds(s1 * DFT_R2 + r0, DFT_ROWS), lanes].astype(F32),
                   v_ref[0, 1, pl.ds(s1 * DFT_R2 + r0, DFT_ROWS), lanes].astype(F32)) for s1 in range(DFT_R1)]
            for t1, (ar, ai) in enumerate(_fft_blocks(xs)):
                tr, ti = tw_ref[0, t1, pl.ds(r0, DFT_ROWS), :], tw_ref[1, t1, pl.ds(r0, DFT_ROWS), :]
                a_scr[t1, pl.ds(r0, DFT_ROWS), lanes] = (ar * tr - ai * ti).astype(BF16)
                a_scr[t1, pl.ds(DFT_R2 + r0, DFT_ROWS), lanes] = (ar * ti + ai * tr).astype(BF16)
        return c

    lax.fori_loop(0, DFT_R2 // DFT_ROWS, tile, 0)

    f2 = f_ref[...].astype(BF16)
    for t1 in range(DFT_R1):
        r = jnp.dot(f2, a_scr[t1], preferred_element_type=F32)
        for slab in range(DFT_LANES // LANES):
            o_scr[slab, pl.ds(t1, DFT_R2, stride=DFT_R1), :] = r[:, slab * LANES:(slab + 1) * LANES]
    for slab in range(DFT_LANES // LANES):
        o_ref[:, slab * LANES:(slab + 1) * LANES] = o_scr[slab].astype(BF16)


def _dft(v, tw, f2):
    nh = FOURIER_WIDTH // DFT_LANES
    return pl.pallas_call(
        _dft_kernel,
        grid=(BATCH, nh),
        in_specs=[
            pl.BlockSpec((1, 2, SEQ, DFT_LANES), lambda b, h: (b, 0, 0, h)),
            pl.BlockSpec(tw.shape, lambda b, h: (0, 0, 0, 0)),
            pl.BlockSpec(f2.shape, lambda b, h: (0, 0)),
        ],
        out_specs=pl.BlockSpec((SEQ, DFT_LANES), lambda b, h: (b, h)),
        out_shape=jax.ShapeDtypeStruct((TOKENS, FOURIER_WIDTH), BF16),
        scratch_shapes=[pltpu.VMEM((DFT_R1, 2 * DFT_R2, DFT_LANES), BF16),
                        pltpu.VMEM((DFT_LANES // LANES, SEQ, LANES), F32)],
        compiler_params=_cparams(("parallel", "parallel")),
        name="dft",
    )(v, tw, f2)


def _dft_constants():
    t1 = np.arange(DFT_R1)
    s2 = np.arange(DFT_R2)
    ang = 2.0 * np.pi * np.outer(t1, s2) / SEQ
    scale = 1.0 / math.sqrt(SEQ)
    tw = np.stack([np.cos(ang) * scale, -np.sin(ang) * scale])
    tw = np.repeat(tw[..., None], LANES, axis=-1)
    ang2 = 2.0 * np.pi * np.outer(s2, s2) / DFT_R2
    f2 = np.concatenate([np.cos(ang2), np.sin(ang2)], axis=1)
    kc = np.arange(FOURIER_GROUP_CH)
    angc = 2.0 * np.pi * np.outer(kc, kc) / FOURIER_GROUP_CH
    cs = 1.0 / math.sqrt(FOURIER_GROUP_CH)
    cdft = np.concatenate([np.cos(angc) * cs, -np.sin(angc) * cs], axis=1)
    return tuple(jnp.asarray(v, F32) for v in (tw, f2, cdft))


GEN_KINDS = 8
TOEPLITZ_LAGS = 2 * SSM_CHUNK - 1


def _ssm_factors(a_re, a_im, log_dt, b_re, b_im, c_re, c_im):
    q_len = SSM_CHUNK
    jb, gb, ch, ns = SSM_LANE_BLOCKS, GROUPS_PER_BLOCK, SSM_GROUP_CH, SSM_STATE
    dt = jnp.exp(log_dt)[..., None]
    lr, li = a_re * dt, a_im * dt
    steps = jnp.arange(q_len + 1, dtype=F32)
    mag = jnp.exp(lr[..., None] * steps)
    ang = li[..., None] * steps
    pr, pi = mag * jnp.cos(ang), mag * jnp.sin(ang)
    ar, ai = pr[..., 1], pi[..., 1]
    den = a_re * a_re + a_im * a_im
    cr = ((ar - 1.0) * a_re + ai * a_im) / den
    ci = (ai * a_re - (ar - 1.0) * a_im) / den
    bbr = cr[..., None] * b_re - ci[..., None] * b_im
    bbi = cr[..., None] * b_im + ci[..., None] * b_re

    def per_dir(fwd, bwd):
        return jnp.stack([fwd, bwd])

    pws = [per_dir(p[0][..., q_len - 1::-1][..., :q_len], p[1][..., :q_len]) for p in (pr, pi)]
    pwo = [per_dir(p[0][..., 1:], p[1][..., q_len:0:-1]) for p in (pr, pi)]
    ct = [jnp.swapaxes(c, -1, -2) for c in (c_re, c_im)]
    gen = jnp.stack([pws[0], pws[1], bbr, bbi, pwo[0], pwo[1], ct[0], ct[1]])
    gen = gen.reshape(GEN_KINDS, 2, jb, gb * ns, q_len)
    gen = jnp.transpose(gen, (2, 1, 0, 3, 4))

    prq, piq = pr[..., :q_len], pi[..., :q_len]
    cpr = jnp.einsum('dgcn,dgnt->dgtcn', c_re, prq) - jnp.einsum('dgcn,dgnt->dgtcn', c_im, piq)
    cpi = jnp.einsum('dgcn,dgnt->dgtcn', c_re, piq) + jnp.einsum('dgcn,dgnt->dgtcn', c_im, prq)
    kern = jnp.einsum('dgtcn,dgne->dgtce', cpr, bbr) - jnp.einsum('dgtcn,dgne->dgtce', cpi, bbi)
    lags = jnp.concatenate([kern[1][:, :0:-1], kern[0][:, :1] + kern[1][:, :1], kern[0][:, 1:]], axis=1)
    lags = lags.reshape(jb, gb, TOEPLITZ_LAGS, ch, ch)
    dtoe = jnp.transpose(lags, (0, 2, 1, 4, 3)).reshape(jb, TOEPLITZ_LAGS * LANES, ch)

    aq = jnp.stack([pr[0][..., q_len], pi[0][..., q_len], pr[1][..., q_len], pi[1][..., q_len]])
    aq = aq.reshape(4, jb, 1, STATE_COLS)
    aq = jnp.broadcast_to(aq, (4, jb, BATCH, STATE_COLS))
    aq = jnp.transpose(aq, (1, 0, 2, 3)).reshape(jb, 4, BATCH * STATE_COLS)
    return gen, dtoe, aq


GEN_ROWS = 128
GROUP_SHIFT_CH = 4
GROUP_SHIFT_STATE = 6


def _replication_matrices():
    rq = np.zeros((SSM_CHUNK, SSM_CHUNK, GROUPS_PER_BLOCK, SSM_GROUP_CH), np.float32)
    rc = np.zeros((SSM_GROUP_CH, SSM_CHUNK, GROUPS_PER_BLOCK, SSM_GROUP_CH), np.float32)
    for i in range(SSM_CHUNK):
        rq[i, i] = 1.0
        rc[i, :, :, i] = 1.0
    return jnp.asarray(rq.reshape(SSM_CHUNK, CHUNK_COLS), BF16), jnp.asarray(rc.reshape(SSM_GROUP_CH, CHUNK_COLS), BF16)


def _state_operator(gen_ref, kind0, rq_ref, rc_ref, out_ref, row0, negate_im):
    col_g = (lax.broadcasted_iota(jnp.int32, (GEN_ROWS, CHUNK_COLS), 1) >> GROUP_SHIFT_CH) & (GROUPS_PER_BLOCK - 1)
    for r in range(STATE_COLS // GEN_ROWS):
        rows = slice(r * GEN_ROWS, (r + 1) * GEN_ROWS)
        pr = jnp.dot(gen_ref[kind0, rows, :].astype(BF16), rq_ref[...], preferred_element_type=F32)
        pi = jnp.dot(gen_ref[kind0 + 1, rows, :].astype(BF16), rq_ref[...], preferred_element_type=F32)
        fr = jnp.dot(gen_ref[kind0 + 2, rows, :].astype(BF16), rc_ref[...], preferred_element_type=F32)
        fi = jnp.dot(gen_ref[kind0 + 3, rows, :].astype(BF16), rc_ref[...], preferred_element_type=F32)
        row_g = ((lax.broadcasted_iota(jnp.int32, (GEN_ROWS, CHUNK_COLS), 0) + r * GEN_ROWS) >> GROUP_SHIFT_STATE)
        keep = (row_g & (GROUPS_PER_BLOCK - 1)) == col_g
        w_re = jnp.where(keep, pr * fr - pi * fi, 0.0)
        w_im = jnp.where(keep, pr * fi + pi * fr, 0.0)
        out_ref[row0 + r * GEN_ROWS:row0 + (r + 1) * GEN_ROWS, :] = w_re.astype(BF16)
        im_rows = slice(row0 + STATE_COLS + r * GEN_ROWS, row0 + STATE_COLS + (r + 1) * GEN_ROWS)
        out_ref[im_rows, :] = (-w_im if negate_im else w_im).astype(BF16)


def _ssm_state_kernel(x_ref, gen_ref, rq_ref, rc_ref, sre_ref, sim_ref, wst_scr):
    @pl.when(pl.program_id(2) == 0)
    def _():
        _state_operator(gen_ref.at[0, 0], 0, rq_ref, rc_ref, wst_scr, 0, False)

    s = lax.dot_general(x_ref[0], wst_scr[...], (((1,), (1,)), ((), ())), preferred_element_type=F32)
    sre_ref[0] = s[:, :STATE_COLS]
    sim_ref[0] = s[:, STATE_COLS:]


def _ssm_states(xr, gen, rq, rc):
    out_spec = pl.BlockSpec((1, N_CHUNKS, STATE_COLS), lambda j, d, b: (j, 0, d * BATCH + b))
    shape = jax.ShapeDtypeStruct((SSM_LANE_BLOCKS, N_CHUNKS, 2 * BATCH * STATE_COLS), F32)
    return pl.pallas_call(
        _ssm_state_kernel,
        grid=(SSM_LANE_BLOCKS, 2, BATCH),
        in_specs=[
            pl.BlockSpec((1, N_CHUNKS, CHUNK_COLS), lambda j, d, b: (j, b, 0)),
            pl.BlockSpec((1, 1, GEN_KINDS, STATE_COLS, SSM_CHUNK), lambda j, d, b: (j, d, 0, 0, 0)),
            pl.BlockSpec(rq.shape, lambda j, d, b: (0, 0)),
            pl.BlockSpec(rc.shape, lambda j, d, b: (0, 0)),
        ],
        out_specs=[out_spec, out_spec],
        out_shape=[shape, shape],
        scratch_shapes=[pltpu.VMEM((2 * STATE_COLS, CHUNK_COLS), BF16)],
        compiler_params=_cparams(("parallel", "parallel", "arbitrary")),
        name="ssm_states",
    )(xr, gen, rq, rc)


def _ssm_scan_kernel(sre_ref, sim_ref, aq_ref, hre_ref, him_ref):
    half = BATCH * STATE_COLS
    fwd, bwd = slice(0, half), slice(half, 2 * half)
    a0r, a0i, a1r, a1i = aq_ref[0, 0:1], aq_ref[0, 1:2], aq_ref[0, 2:3], aq_ref[0, 3:4]

    def body(k, carry):
        fr, fi, br, bi = carry
        kb = N_CHUNKS - 1 - k
        hre_ref[0, pl.ds(k, 1), fwd] = fr
        him_ref[0, pl.ds(k, 1), fwd] = fi
        hre_ref[0, pl.ds(kb, 1), bwd] = br
        him_ref[0, pl.ds(kb, 1), bwd] = bi
        sfr, sfi = sre_ref[0, pl.ds(k, 1), fwd], sim_ref[0, pl.ds(k, 1), fwd]
        sbr, sbi = sre_ref[0, pl.ds(kb, 1), bwd], sim_ref[0, pl.ds(kb, 1), bwd]
        return (a0r * fr - a0i * fi + sfr, a0r * fi + a0i * fr + sfi,
                a1r * br - a1i * bi + sbr, a1r * bi + a1i * br + sbi)

    z = jnp.zeros((1, half), F32)
    lax.fori_loop(0, N_CHUNKS, body, (z, z, z, z))


def _ssm_scan(sre, sim, aq):
    spec = pl.BlockSpec((1, N_CHUNKS, 2 * BATCH * STATE_COLS), lambda j: (j, 0, 0))
    shape = jax.ShapeDtypeStruct(sre.shape, F32)
    return pl.pallas_call(
        _ssm_scan_kernel,
        grid=(SSM_LANE_BLOCKS,),
        in_specs=[spec, spec, pl.BlockSpec((1, 4, BATCH * STATE_COLS), lambda j: (j, 0, 0))],
        out_specs=[spec, spec],
        out_shape=[shape, shape],
        compiler_params=_cparams(("parallel",)),
        name="ssm_scan",
    )(sre, sim, aq)


def _ssm_out_kernel(x_ref, gen_ref, dtoe_ref, rq_ref, rc_ref, h0r_ref, h0i_ref, h1r_ref, h1i_ref, y_ref,
                    m_scr, wo_scr, lag_scr, y_scr):
    @pl.when(pl.program_id(1) == 0)
    def _():
        for d in range(2):
            _state_operator(gen_ref.at[0, d], 4, rq_ref, rc_ref, wo_scr, d * 2 * STATE_COLS, True)
        shape = (TOEPLITZ_LAGS * LANES, LANES)
        t = jnp.dot(dtoe_ref[0].astype(BF16), rc_ref[:, :LANES], preferred_element_type=F32)
        row_g = (lax.broadcasted_iota(jnp.int32, shape, 0) >> GROUP_SHIFT_CH) & (GROUPS_PER_BLOCK - 1)
        col_g = (lax.broadcasted_iota(jnp.int32, shape, 1) >> GROUP_SHIFT_CH) & (GROUPS_PER_BLOCK - 1)
        lag_scr[...] = jnp.where(row_g == col_g, t, 0.0).astype(BF16)
        for qi in range(SSM_CHUNK):
            for qo in range(SSM_CHUNK):
                lag = qo - qi + SSM_CHUNK - 1
                m_scr[qi * LANES:(qi + 1) * LANES, qo * LANES:(qo + 1) * LANES] = lag_scr[lag * LANES:(lag + 1) * LANES, :]

    acc = jnp.dot(x_ref[0], m_scr[...], preferred_element_type=F32)
    for i, h_ref in enumerate((h0r_ref, h0i_ref, h1r_ref, h1i_ref)):
        acc += jnp.dot(h_ref[0].astype(BF16), wo_scr[i * STATE_COLS:(i + 1) * STATE_COLS, :],
                       preferred_element_type=F32)
    for q in range(SSM_CHUNK):
        y_scr[pl.ds(q, N_CHUNKS, stride=SSM_CHUNK), :] = acc[:, q * LANES:(q + 1) * LANES]
    y_ref[0] = y_scr[...].astype(BF16)


def _ssm_out(xr, gen, dtoe, rq, rc, hre, him):
    def hspec(d):
        return pl.BlockSpec((1, N_CHUNKS, STATE_COLS), lambda j, b: (j, 0, d * BATCH + b))
    return pl.pallas_call(
        _ssm_out_kernel,
        grid=(SSM_LANE_BLOCKS, BATCH),
        in_specs=[
            pl.BlockSpec((1, N_CHUNKS, CHUNK_COLS), lambda j, b: (j, b, 0)),
            pl.BlockSpec((1, 2, GEN_KINDS, STATE_COLS, SSM_CHUNK), lambda j, b: (j, 0, 0, 0, 0)),
            pl.BlockSpec((1, TOEPLITZ_LAGS * LANES, SSM_GROUP_CH), lambda j, b: (j, 0, 0)),
            pl.BlockSpec(rq.shape, lambda j, b: (0, 0)),
            pl.BlockSpec(rc.shape, lambda j, b: (0, 0)),
            hspec(0), hspec(0), hspec(1), hspec(1),
        ],
        out_specs=pl.BlockSpec((1, SEQ, LANES), lambda j, b: (j, b, 0)),
        out_shape=jax.ShapeDtypeStruct((SSM_LANE_BLOCKS, TOKENS, LANES), BF16),
        scratch_shapes=[pltpu.VMEM((CHUNK_COLS, CHUNK_COLS), BF16), pltpu.VMEM((4 * STATE_COLS, CHUNK_COLS), BF16),
                        pltpu.VMEM((TOEPLITZ_LAGS * LANES, LANES), BF16), pltpu.VMEM((SEQ, LANES), F32)],
        compiler_params=_cparams(("parallel", "arbitrary")),
        name="ssm_out",
    )(xr, gen, dtoe, rq, rc, hre, him, hre, him)


MERGE_TM = 512
GELU_C = math.sqrt(2.0 / math.pi)
PACK_SUB = D_MODEL // LANES


def _split_bf16(v):
    hi = v.astype(BF16)
    lo = (v - hi.astype(F32)).astype(BF16)
    return hi, lo


def _pack_rows(v, out_ref):
    for s in range(PACK_SUB):
        out_ref[pl.ds(s, v.shape[0], stride=PACK_SUB), :] = v[:, s * LANES:(s + 1) * LANES]


def _unpack_rows(buf_ref, start, rows):
    return jnp.concatenate([buf_ref[pl.ds(start + s, rows, stride=PACK_SUB), :] for s in range(PACK_SUB)], axis=1)


def _merge_kernel(x_ref, fm_ref, yc_ref, us_ref, gates_ref, dskip_ref, wf_ref, wglu_ref, ws_ref, wo_ref,
                  ng_ref, wrh_ref, wrl_ref, rb_ref, x1_ref, hp_ref, logit_ref):
    conv = jnp.concatenate([yc_ref[j].astype(F32) for j in range(SSM_LANE_BLOCKS)], axis=-1)
    u = jnp.concatenate([us_ref[j].astype(F32) for j in range(SSM_LANE_BLOCKS)], axis=-1)
    y = conv + dskip_ref[...] * u
    y = 0.5 * y * (1.0 + jnp.tanh(GELU_C * (y + 0.044715 * (y * y * y))))
    glu = jax.nn.sigmoid(jnp.dot(y.astype(BF16), wglu_ref[...], preferred_element_type=F32))
    y_s = jnp.dot((y * glu).astype(BF16), ws_ref[...], preferred_element_type=F32)
    y_f = jnp.dot(fm_ref[...], wf_ref[...], preferred_element_type=F32)
    merged = (gates_ref[:, :D_MODEL].astype(F32) * y_f + gates_ref[:, D_MODEL:].astype(F32) * y_s)
    x1 = x_ref[...] + jnp.dot(merged.astype(BF16), wo_ref[...], preferred_element_type=F32)
    x1_ref[...] = x1
    inv = lax.rsqrt(jnp.mean(x1 * x1, axis=-1, keepdims=True) + RMS_EPS)
    hn = x1 * inv * ng_ref[...]
    _pack_rows(hn, hp_ref)
    hi, lo = _split_bf16(hn)
    logits = (jnp.dot(hi, wrh_ref[...], preferred_element_type=F32)
              + jnp.dot(lo, wrh_ref[...], preferred_element_type=F32)
              + jnp.dot(hi, wrl_ref[...], preferred_element_type=F32))
    logit_ref[...] = logits + rb_ref[...]


def _merge(x, fmix, yconv, us, gates, dskip, wf, wglu, ws, wo, ng, wrh, wrl, rb):
    tm = MERGE_TM
    full = lambda a: pl.BlockSpec(a.shape, lambda i: (0,) * a.ndim)
    return pl.pallas_call(
        _merge_kernel,
        grid=(TOKENS // tm,),
        in_specs=[
            pl.BlockSpec((tm, D_MODEL), lambda i: (i, 0)),
            pl.BlockSpec((tm, FOURIER_WIDTH), lambda i: (i, 0)),
            pl.BlockSpec((SSM_LANE_BLOCKS, tm, LANES), lambda i: (0, i, 0)),
            pl.BlockSpec((SSM_LANE_BLOCKS, tm, LANES), lambda i: (0, i, 0)),
            pl.BlockSpec((tm, 2 * D_MODEL), lambda i: (i, 0)),
            full(dskip), full(wf), full(wglu), full(ws), full(wo), full(ng), full(wrh), full(wrl), full(rb),
        ],
        out_specs=[
            pl.BlockSpec((tm, D_MODEL), lambda i: (i, 0)),
            pl.BlockSpec((tm * PACK_SUB, LANES), lambda i: (i, 0)),
            pl.BlockSpec((tm, ROUTER_COLS), lambda i: (i, 0)),
        ],
        out_shape=[
            jax.ShapeDtypeStruct((TOKENS, D_MODEL), F32),
            jax.ShapeDtypeStruct((TOKENS * PACK_SUB, LANES), F32),
            jax.ShapeDtypeStruct((TOKENS, ROUTER_COLS), F32),
        ],
        compiler_params=_cparams(("parallel",)),
        name="merge",
    )(x, fmix, yconv, us, gates, dskip, wf, wglu, ws, wo, ng, wrh, wrl, rb)


ROUTE_TM = 512
EXPERT_LANE0 = MOE_GROUPS
INFO_EXPERT, INFO_RANK, INFO_GATE = 0, 2, 4


def _route_kernel(lg_ref, info_ref, cnt_ref, carry):
    @pl.when(pl.program_id(0) == 0)
    def _():
        carry[...] = jnp.zeros_like(carry)

    lg = lg_ref[...]
    tm = lg.shape[0]
    col_i = lax.broadcasted_iota(jnp.int32, lg.shape, 1)
    col = col_i.astype(F32)
    neg = jnp.float32(-jnp.inf)
    none = jnp.float32(ROUTER_COLS)

    def row_max(v):
        return jnp.max(v, axis=-1, keepdims=True)

    def first_at(v, m):
        return jnp.min(jnp.where(v == m, col, none), axis=-1, keepdims=True)

    gl = jnp.where(col_i < MOE_GROUPS, lg, neg)
    gmax = row_max(gl)
    p_g = 1.0 / jnp.sum(jnp.exp(gl - gmax), axis=-1, keepdims=True)
    lo = EXPERT_LANE0 + first_at(gl, gmax) * EXPERTS_PER_GROUP
    el = jnp.where((col >= lo) & (col < lo + EXPERTS_PER_GROUP), lg, neg)
    l1 = row_max(el)
    i1 = first_at(el, l1)
    el2 = jnp.where(col == i1, neg, el)
    l2 = row_max(el2)
    i2 = first_at(el2, l2)
    r = jnp.exp(l2 - l1)
    w1 = p_g / (1.0 + r)
    w2 = w1 * r

    hit1, hit2 = col == i1, col == i2
    onehot = jnp.where(hit1 | hit2, 1.0, 0.0)
    earlier = lax.broadcasted_iota(jnp.int32, (tm, tm), 0) > lax.broadcasted_iota(jnp.int32, (tm, tm), 1)
    before = jnp.dot(jnp.where(earlier, 1.0, 0.0).astype(BF16), onehot.astype(BF16),
                     preferred_element_type=F32) + carry[...]
    rank1 = jnp.sum(jnp.where(hit1, before, 0.0), axis=-1, keepdims=True)
    rank2 = jnp.sum(jnp.where(hit2, before, 0.0), axis=-1, keepdims=True)
    carry[...] += jnp.sum(onehot, axis=0, keepdims=True)
    cnt_ref[...] = carry[...]

    info = jnp.zeros(lg.shape, F32)
    for lane, v in ((INFO_EXPERT, i1 - EXPERT_LANE0), (INFO_EXPERT + 1, i2 - EXPERT_LANE0), (INFO_RANK, rank1),
                    (INFO_RANK + 1, rank2), (INFO_GATE, w1), (INFO_GATE + 1, w2)):
        info = jnp.where(col_i == lane, v, info)
    info_ref[...] = info


def _route(logits):
    return pl.pallas_call(
        _route_kernel,
        grid=(TOKENS // ROUTE_TM,),
        in_specs=[pl.BlockSpec((ROUTE_TM, ROUTER_COLS), lambda i: (i, 0))],
        out_specs=[pl.BlockSpec((ROUTE_TM, ROUTER_COLS), lambda i: (i, 0)),
                   pl.BlockSpec((1, ROUTER_COLS), lambda i: (0, 0))],
        out_shape=[jax.ShapeDtypeStruct((TOKENS, ROUTER_COLS), F32),
                   jax.ShapeDtypeStruct((1, ROUTER_COLS), F32)],
        scratch_shapes=[pltpu.VMEM((1, ROUTER_COLS), F32)],
        compiler_params=_cparams(("arbitrary",)),
        name="route",
    )(logits)


def _dispatch_plan(info, counts):
    expert = info[:, INFO_EXPERT:INFO_EXPERT + MOE_TOP_K].astype(jnp.int32)
    rank = info[:, INFO_RANK:INFO_RANK + MOE_TOP_K].astype(jnp.int32)
    cnt = counts[0, EXPERT_LANE0:EXPERT_LANE0 + N_EXPERTS].astype(jnp.int32)
    padded = ((cnt + MOE_ROWS - 1) // MOE_ROWS) * MOE_ROWS
    pends = jnp.cumsum(padded)
    pstarts = pends - padded
    ids = jnp.arange(N_EXPERTS, dtype=jnp.int32)
    dest = rank + jnp.sum(jnp.where(expert[..., None] == ids, pstarts, 0), axis=-1)
    n_used = pends[-1] // MOE_ROWS
    blocks = jnp.arange(MOE_BLOCKS, dtype=jnp.int32)
    block_e = jnp.sum((pends[None, :] <= (blocks * MOE_ROWS)[:, None]).astype(jnp.int32), axis=1)
    block_e = jnp.minimum(block_e, N_EXPERTS - 1)
    last_e = jnp.sum(jnp.where(blocks == n_used - 1, block_e, 0))
    block_e = jnp.where(blocks < n_used, block_e, last_e)
    return dest.reshape(TOKENS * MOE_TOP_K), block_e, n_used.reshape(1), pends


MOE_SLOT_ROWS = MOE_ROWS * PACK_SUB
INVERT_UNROLL = 8


def _moe_kernel(dest_ref, be_ref, nu_ref, pend_ref, hp_ref, wg_ref, wu_ref, wd_ref, y_ref, xbuf, tok_ref, sem):
    i = pl.program_id(0)
    n_used = nu_ref[0]

    def slot_rows(slot):
        return xbuf.at[pl.ds(pl.multiple_of(slot * MOE_SLOT_ROWS, MOE_SLOT_ROWS), MOE_SLOT_ROWS), :]

    def invert_dispatch():
        def fill_expert(e, c):
            first = jnp.maximum(pend_ref[e] - MOE_ROWS, 0)

            def fill(r, c2):
                tok_ref[first + r] = (first + r) & (TOKENS - 1)
                return c2

            lax.fori_loop(0, MOE_ROWS, fill, 0, unroll=INVERT_UNROLL)
            return c

        lax.fori_loop(0, N_EXPERTS, fill_expert, 0)

        def place(t, c):
            for k in range(MOE_TOP_K):
                tok_ref[dest_ref[MOE_TOP_K * t + k]] = t
            return c

        lax.fori_loop(0, TOKENS, place, 0, unroll=INVERT_UNROLL)

    def gather(block, slot):
        base = block * MOE_ROWS
        for r in range(MOE_ROWS):
            src = hp_ref.at[pl.ds(pl.multiple_of(tok_ref[base + r] * PACK_SUB, PACK_SUB), PACK_SUB), :]
            dst = xbuf.at[pl.ds(pl.multiple_of(slot * MOE_SLOT_ROWS + r * PACK_SUB, PACK_SUB), PACK_SUB), :]
            pltpu.make_async_copy(src, dst, sem.at[slot]).start()

    @pl.when(i == 0)
    def _():
        invert_dispatch()
        gather(0, 0)

    @pl.when(i + 1 < n_used)
    def _():
        gather(i + 1, (i + 1) % 2)

    @pl.when(i < n_used)
    def _():
        slot = i % 2
        pltpu.make_async_copy(slot_rows(slot), slot_rows(slot), sem.at[slot]).wait()
        xb = _unpack_rows(xbuf, slot * MOE_SLOT_ROWS, MOE_ROWS).astype(BF16)
        a = jnp.dot(xb, wg_ref[0].astype(BF16), preferred_element_type=F32)
        u = jnp.dot(xb, wu_ref[0].astype(BF16), preferred_element_type=F32)
        act = (a * jax.nn.sigmoid(a) * u).astype(BF16)
        _pack_rows(jnp.dot(act, wd_ref[0].astype(BF16), preferred_element_type=F32), y_ref)

    @pl.when(i >= n_used)
    def _():
        y_ref[...] = jnp.zeros_like(y_ref)


def _moe(dest, block_e, n_used, pends, hp, w_gate, w_up, w_down):
    grid_spec = pltpu.PrefetchScalarGridSpec(
        num_scalar_prefetch=4,
        grid=(MOE_BLOCKS,),
        in_specs=[
            pl.BlockSpec(memory_space=pl.ANY),
            pl.BlockSpec((1, D_MODEL, D_EXPERT), lambda i, dst, be, nu, pe: (be[i], 0, 0)),
            pl.BlockSpec((1, D_MODEL, D_EXPERT), lambda i, dst, be, nu, pe: (be[i], 0, 0)),
            pl.BlockSpec((1, D_EXPERT, D_MODEL), lambda i, dst, be, nu, pe: (be[i], 0, 0)),
        ],
        out_specs=pl.BlockSpec((MOE_SLOT_ROWS, LANES), lambda i, dst, be, nu, pe: (i, 0)),
        scratch_shapes=[pltpu.VMEM((2 * MOE_SLOT_ROWS, LANES), F32), pltpu.SMEM((MOE_BLOCKS * MOE_ROWS,), jnp.int32),
                        pltpu.SemaphoreType.DMA((2,))],
    )
    return pl.pallas_call(
        _moe_kernel,
        grid_spec=grid_spec,
        out_shape=jax.ShapeDtypeStruct((MOE_BLOCKS * MOE_SLOT_ROWS, LANES), F32),
        compiler_params=_cparams(("arbitrary",)),
        name="moe",
    )(dest, block_e, n_used, pends, hp, w_gate, w_up, w_down)


COMBINE_TM = 256


def _combine_kernel(dest_ref, x1_ref, info_ref, g_ref, y_ref, o_ref, ybuf, sem):
    i = pl.program_id(0)
    n = pl.num_programs(0)
    k_rows = COMBINE_TM * PACK_SUB
    slot_rows = MOE_TOP_K * k_rows

    def slot_ref(slot):
        return ybuf.at[pl.ds(pl.multiple_of(slot * slot_rows, slot_rows), slot_rows), :]

    def gather(tile, slot):
        base = tile * (COMBINE_TM * MOE_TOP_K)
        for r in range(COMBINE_TM):
            for k in range(MOE_TOP_K):
                row = dest_ref[base + r * MOE_TOP_K + k]
                src = y_ref.at[pl.ds(pl.multiple_of(row * PACK_SUB, PACK_SUB), PACK_SUB), :]
                at = slot * slot_rows + k * k_rows + r * PACK_SUB
                dst = ybuf.at[pl.ds(pl.multiple_of(at, PACK_SUB), PACK_SUB), :]
                pltpu.make_async_copy(src, dst, sem.at[slot]).start(priority=k)

    @pl.when(i == 0)
    def _():
        gather(0, 0)

    @pl.when(i + 1 < n)
    def _():
        gather(i + 1, (i + 1) % 2)

    slot = i % 2
    pltpu.make_async_copy(slot_ref(slot), slot_ref(slot), sem.at[slot]).wait()
    x2 = x1_ref[...]
    for k in range(MOE_TOP_K):
        yk = _unpack_rows(ybuf, slot * slot_rows + k * k_rows, COMBINE_TM)
        x2 = x2 + info_ref[:, INFO_GATE + k:INFO_GATE + k + 1] * yk
    inv = lax.rsqrt(jnp.mean(x2 * x2, axis=-1, keepdims=True) + RMS_EPS)
    o_ref[...] = x2 * inv * g_ref[...]


def _combine(dest, x1, info, g, y_pad):
    tm = COMBINE_TM
    grid_spec = pltpu.PrefetchScalarGridSpec(
        num_scalar_prefetch=1,
        grid=(TOKENS // tm,),
        in_specs=[
            pl.BlockSpec((tm, D_MODEL), lambda i, d: (i, 0)),
            pl.BlockSpec((tm, ROUTER_COLS), lambda i, d: (i, 0)),
            pl.BlockSpec((1, D_MODEL), lambda i, d: (0, 0)),
            pl.BlockSpec(memory_space=pl.ANY),
        ],
        out_specs=pl.BlockSpec((tm, D_MODEL), lambda i, d: (i, 0)),
        scratch_shapes=[pltpu.VMEM((2 * MOE_TOP_K * tm * PACK_SUB, LANES), F32), pltpu.SemaphoreType.DMA((2,))],
    )
    return pl.pallas_call(
        _combine_kernel,
        grid_spec=grid_spec,
        out_shape=jax.ShapeDtypeStruct((TOKENS, D_MODEL), F32),
        compiler_params=_cparams(("arbitrary",)),
        name="combine",
    )(dest, x1, info, g, y_pad)


def kernel(x, mix_norm_g, w_in, w_fourier_out, ssm_A_re, ssm_A_im, ssm_log_dt, ssm_B_re, ssm_B_im, ssm_C_re,
           ssm_C_im, ssm_D, ssm_w_glu, w_ssm_out, w_out, ffn_norm_g, router_group_w, router_group_b,
           router_expert_w, router_expert_b, expert_w_gate, expert_w_up, expert_w_down, final_norm_g):
    assert x.shape == (BATCH, SEQ, D_MODEL) and w_in.shape[0] == 1
    tw, f2, cdft = _dft_constants()
    rq, rc = _replication_matrices()

    vf, us, xr, gates = _inproj(x, mix_norm_g[0][None], w_in[0].astype(BF16), cdft)
    fmix = _dft(vf, tw, f2)

    gen, dtoe, aq = _ssm_factors(ssm_A_re[0], ssm_A_im[0], ssm_log_dt[0], ssm_B_re[0], ssm_B_im[0],
                                 ssm_C_re[0], ssm_C_im[0])
    sre, sim = _ssm_states(xr, gen, rq, rc)
    hre, him = _ssm_scan(sre, sim, aq)
    yconv = _ssm_out(xr, gen, dtoe, rq, rc, hre, him)

    w_router = jnp.concatenate([router_group_w[0], router_expert_w[0]], axis=1)
    w_router = jnp.pad(w_router, ((0, 0), (0, ROUTER_COLS - w_router.shape[1])))
    b_router = jnp.concatenate([router_group_b[0], router_expert_b[0]])
    b_router = jnp.pad(b_router, (0, ROUTER_COLS - b_router.shape[0]))[None]
    wr_hi = w_router.astype(BF16)
    wr_lo = (w_router - wr_hi.astype(F32)).astype(BF16)
    x1, hp, logits = _merge(x.reshape(TOKENS, D_MODEL), fmix, yconv, us, gates, ssm_D[0][None],
                            w_fourier_out[0].astype(BF16), ssm_w_glu[0].astype(BF16), w_ssm_out[0].astype(BF16),
                            w_out[0].astype(BF16), ffn_norm_g[0][None], wr_hi, wr_lo, b_router)

    info, counts = _route(logits)
    dest, block_e, n_used, pends = _dispatch_plan(info, counts)
    y_pad = _moe(dest, block_e, n_used, pends, hp, expert_w_gate[0], expert_w_up[0], expert_w_down[0])
    out = _combine(dest, x1, info, final_norm_g[None], y_pad)
    return out.reshape(BATCH, SEQ, D_MODEL)
```

```python
import math

import numpy as np
import jax
import jax.numpy as jnp
from jax import lax
from jax.experimental import pallas as pl
from jax.experimental.pallas import tpu as pltpu

F32 = jnp.float32
BF16 = jnp.bfloat16

D_MODEL = 1024
BATCH = 4
SEQ = 4096
TOKENS = BATCH * SEQ
FOURIER_WIDTH = 512
FOURIER_GROUP_CH = 128
FOURIER_GROUPS = 4
SSM_WIDTH = 512
SSM_GROUP_CH = 16
SSM_GROUPS = 32
SSM_STATE = 64
MOE_GROUPS = 8
EXPERTS_PER_GROUP = 8
N_EXPERTS = 64
MOE_TOP_K = 2
D_EXPERT = 512
RMS_EPS = 1e-6

LANES = 128
SSM_CHUNK = 16
SSM_LANE_BLOCKS = SSM_WIDTH // LANES
GROUPS_PER_BLOCK = LANES // SSM_GROUP_CH
CHUNK_COLS = SSM_CHUNK * LANES
N_CHUNKS = SEQ // SSM_CHUNK
STATE_COLS = GROUPS_PER_BLOCK * SSM_STATE
MOE_ROWS = 256
MOE_BLOCKS = TOKENS * MOE_TOP_K // MOE_ROWS + N_EXPERTS
ROUTER_COLS = 128
VMEM_LIMIT = 48 * 1024 * 1024


def _cparams(sem, vmem=VMEM_LIMIT):
    return pltpu.CompilerParams(dimension_semantics=sem, vmem_limit_bytes=vmem)


IN_TM = 512


def _inproj_kernel(x_ref, g_ref, w_ref, cdft_ref, vf_ref, us_ref, xr_ref, gates_ref, zs_scr):
    x = x_ref[0]
    inv = lax.rsqrt(jnp.mean(x * x, axis=-1, keepdims=True) + RMS_EPS)
    h = (x * inv * g_ref[...]).astype(BF16)
    zf = jnp.dot(h, w_ref[:, 0:FOURIER_WIDTH], preferred_element_type=F32).astype(BF16)
    cdft = cdft_ref[...].astype(BF16)
    for g in range(FOURIER_GROUPS):
        sl = slice(g * LANES, (g + 1) * LANES)
        v = jnp.dot(zf[:, sl], cdft, preferred_element_type=F32)
        vf_ref[0, 0, :, sl] = v[:, :LANES].astype(BF16)
        vf_ref[0, 1, :, sl] = v[:, LANES:].astype(BF16)
    zs = jnp.dot(h, w_ref[:, FOURIER_WIDTH:FOURIER_WIDTH + SSM_WIDTH], preferred_element_type=F32)
    for j in range(SSM_LANE_BLOCKS):
        us_ref[j] = zs[:, j * LANES:(j + 1) * LANES].astype(BF16)
        zs_scr[j] = zs[:, j * LANES:(j + 1) * LANES]
    for j in range(SSM_LANE_BLOCKS):
        for q in range(SSM_CHUNK):
            piece = zs_scr[j, pl.ds(q, IN_TM // SSM_CHUNK, stride=SSM_CHUNK), :]
            xr_ref[j, :, q * LANES:(q + 1) * LANES] = piece.astype(BF16)
    base = FOURIER_WIDTH + SSM_WIDTH
    for n in range(4):
        zg = jnp.dot(h, w_ref[:, base + n * 512: base + (n + 1) * 512], preferred_element_type=F32)
        gates_ref[:, n * 512:(n + 1) * 512] = jax.nn.sigmoid(zg).astype(BF16)


def _inproj(x, g, w_in, cdft):
    nt = SEQ // IN_TM
    return pl.pallas_call(
        _inproj_kernel,
        grid=(BATCH, nt),
        in_specs=[
            pl.BlockSpec((1, IN_TM, D_MODEL), lambda b, i: (b, i, 0)),
            pl.BlockSpec((1, D_MODEL), lambda b, i: (0, 0)),
            pl.BlockSpec(w_in.shape, lambda b, i: (0, 0)),
            pl.BlockSpec(cdft.shape, lambda b, i: (0, 0)),
        ],
        out_specs=[
            pl.BlockSpec((1, 2, IN_TM, FOURIER_WIDTH), lambda b, i: (b, 0, i, 0)),
            pl.BlockSpec((SSM_LANE_BLOCKS, IN_TM, LANES), lambda b, i: (0, b * nt + i, 0)),
            pl.BlockSpec((SSM_LANE_BLOCKS, IN_TM // SSM_CHUNK, CHUNK_COLS), lambda b, i: (0, b * nt + i, 0)),
            pl.BlockSpec((IN_TM, 2 * D_MODEL), lambda b, i: (b * nt + i, 0)),
        ],
        out_shape=[
            jax.ShapeDtypeStruct((BATCH, 2, SEQ, FOURIER_WIDTH), BF16),
            jax.ShapeDtypeStruct((SSM_LANE_BLOCKS, TOKENS, LANES), BF16),
            jax.ShapeDtypeStruct((SSM_LANE_BLOCKS, TOKENS // SSM_CHUNK, CHUNK_COLS), BF16),
            jax.ShapeDtypeStruct((TOKENS, 2 * D_MODEL), BF16),
        ],
        scratch_shapes=[pltpu.VMEM((SSM_LANE_BLOCKS, IN_TM, LANES), F32)],
        compiler_params=_cparams(("parallel", "parallel")),
        name="inproj",
    )(x, g, w_in, cdft)


DFT_R1 = 8
DFT_R2 = SEQ // DFT_R1
DFT_LANES = 2 * LANES
DFT_ROWS = 16


def _cmul_const(z, w):
    re, im = z
    if abs(w.imag) < 1e-12:
        return (re, im) if abs(w.real - 1.0) < 1e-12 else (re * w.real, im * w.real)
    if abs(w.real) < 1e-12:
        return (im, -re) if abs(w.imag + 1.0) < 1e-12 else (-im * w.imag, re * w.imag)
    return re * w.real - im * w.imag, re * w.imag + im * w.real


def _fft_blocks(xs):
    n = len(xs)
    if n == 1:
        return xs
    even, odd = _fft_blocks(xs[0::2]), _fft_blocks(xs[1::2])
    out = [None] * n
    for k in range(n // 2):
        tr, ti = _cmul_const(odd[k], np.exp(-2j * np.pi * k / n))
        out[k] = (even[k][0] + tr, even[k][1] + ti)
        out[k + n // 2] = (even[k][0] - tr, even[k][1] - ti)
    return out


def _dft_kernel(v_ref, tw_ref, f_ref, o_ref, a_scr, o_scr):
    def tile(i, c):
        r0 = pl.multiple_of(i * DFT_ROWS, DFT_ROWS)
        for slab in range(DFT_LANES // LANES):
            lanes = slice(slab * LANES, (slab + 1) * LANES)
            xs = [(v_ref[0, 0, pl.ds(s1 * DFT_R2 + r0, DFT_ROWS), lanes].astype(F32),
                   v_ref[0, 1, pl.ds(s1 * DFT_R2 + r0, DFT_ROWS), lanes].astype(F32)) for s1 in range(DFT_R1)]
            for t1, (ar, ai) in enumerate(_fft_blocks(xs)):
                tr, ti = tw_ref[0, t1, pl.ds(r0, DFT_ROWS), :], tw_ref[1, t1, pl.ds(r0, DFT_ROWS), :]
                a_scr[t1, pl.ds(r0, DFT_ROWS), lanes] = (ar * tr - ai * ti).astype(BF16)
                a_scr[t1, pl.ds(DFT_R2 + r0, DFT_ROWS), lanes] = (ar * ti + ai * tr).astype(BF16)
        return c

    lax.fori_loop(0, DFT_R2 // DFT_ROWS, tile, 0)

    f2 = f_ref[...].astype(BF16)
    for t1 in range(DFT_R1):
        r = jnp.dot(f2, a_scr[t1], preferred_element_type=F32)
        for slab in range(DFT_LANES // LANES):
            o_scr[slab, pl.ds(t1, DFT_R2, stride=DFT_R1), :] = r[:, slab * LANES:(slab + 1) * LANES]
    for slab in range(DFT_LANES // LANES):
        o_ref[:, slab * LANES:(slab + 1) * LANES] = o_scr[slab].astype(BF16)


def _dft(v, tw, f2):
    nh = FOURIER_WIDTH // DFT_LANES
    return pl.pallas_call(
        _dft_kernel,
        grid=(BATCH, nh),
        in_specs=[
            pl.BlockSpec((1, 2, SEQ, DFT_LANES), lambda b, h: (b, 0, 0, h)),
            pl.BlockSpec(tw.shape, lambda b, h: (0, 0, 0, 0)),
            pl.BlockSpec(f2.shape, lambda b, h: (0, 0)),
        ],
        out_specs=pl.BlockSpec((SEQ, DFT_LANES), lambda b, h: (b, h)),
        out_shape=jax.ShapeDtypeStruct((TOKENS, FOURIER_WIDTH), BF16),
        scratch_shapes=[pltpu.VMEM((DFT_R1, 2 * DFT_R2, DFT_LANES), BF16),
                        pltpu.VMEM((DFT_LANES // LANES, SEQ, LANES), F32)],
        compiler_params=_cparams(("parallel", "parallel")),
        name="dft",
    )(v, tw, f2)


def _dft_constants():
    t1 = np.arange(DFT_R1)
    s2 = np.arange(DFT_R2)
    ang = 2.0 * np.pi * np.outer(t1, s2) / SEQ
    scale = 1.0 / math.sqrt(SEQ)
    tw = np.stack([np.cos(ang) * scale, -np.sin(ang) * scale])
    tw = np.repeat(tw[..., None], LANES, axis=-1)
    ang2 = 2.0 * np.pi * np.outer(s2, s2) / DFT_R2
    f2 = np.concatenate([np.cos(ang2), np.sin(ang2)], axis=1)
    kc = np.arange(FOURIER_GROUP_CH)
    angc = 2.0 * np.pi * np.outer(kc, kc) / FOURIER_GROUP_CH
    cs = 1.0 / math.sqrt(FOURIER_GROUP_CH)
    cdft = np.concatenate([np.cos(angc) * cs, -np.sin(angc) * cs], axis=1)
    return tuple(jnp.asarray(v, F32) for v in (tw, f2, cdft))


GEN_KINDS = 8
TOEPLITZ_LAGS = 2 * SSM_CHUNK - 1


def _ssm_factors(a_re, a_im, log_dt, b_re, b_im, c_re, c_im):
    q_len = SSM_CHUNK
    jb, gb, ch, ns = SSM_LANE_BLOCKS, GROUPS_PER_BLOCK, SSM_GROUP_CH, SSM_STATE
    dt = jnp.exp(log_dt)[..., None]
    lr, li = a_re * dt, a_im * dt
    steps = jnp.arange(q_len + 1, dtype=F32)
    mag = jnp.exp(lr[..., None] * steps)
    ang = li[..., None] * steps
    pr, pi = mag * jnp.cos(ang), mag * jnp.sin(ang)
    ar, ai = pr[..., 1], pi[..., 1]
    den = a_re * a_re + a_im * a_im
    cr = ((ar - 1.0) * a_re + ai * a_im) / den
    ci = (ai * a_re - (ar - 1.0) * a_im) / den
    bbr = cr[..., None] * b_re - ci[..., None] * b_im
    bbi = cr[..., None] * b_im + ci[..., None] * b_re

    def per_dir(fwd, bwd):
        return jnp.stack([fwd, bwd])

    pws = [per_dir(p[0][..., q_len - 1::-1][..., :q_len], p[1][..., :q_len]) for p in (pr, pi)]
    pwo = [per_dir(p[0][..., 1:], p[1][..., q_len:0:-1]) for p in (pr, pi)]
    ct = [jnp.swapaxes(c, -1, -2) for c in (c_re, c_im)]
    gen = jnp.stack([pws[0], pws[1], bbr, bbi, pwo[0], pwo[1], ct[0], ct[1]])
    gen = gen.reshape(GEN_KINDS, 2, jb, gb * ns, q_len)
    gen = jnp.transpose(gen, (2, 1, 0, 3, 4))

    prq, piq = pr[..., :q_len], pi[..., :q_len]
    cpr = jnp.einsum('dgcn,dgnt->dgtcn', c_re, prq) - jnp.einsum('dgcn,dgnt->dgtcn', c_im, piq)
    cpi = jnp.einsum('dgcn,dgnt->dgtcn', c_re, piq) + jnp.einsum('dgcn,dgnt->dgtcn', c_im, prq)
    kern = jnp.einsum('dgtcn,dgne->dgtce', cpr, bbr) - jnp.einsum('dgtcn,dgne->dgtce', cpi, bbi)
    lags = jnp.concatenate([kern[1][:, :0:-1], kern[0][:, :1] + kern[1][:, :1], kern[0][:, 1:]], axis=1)
    lags = lags.reshape(jb, gb, TOEPLITZ_LAGS, ch, ch)
    dtoe = jnp.transpose(lags, (0, 2, 1, 4, 3)).reshape(jb, TOEPLITZ_LAGS * LANES, ch)

    aq = jnp.stack([pr[0][..., q_len], pi[0][..., q_len], pr[1][..., q_len], pi[1][..., q_len]])
    aq = aq.reshape(4, jb, 1, STATE_COLS)
    aq = jnp.broadcast_to(aq, (4, jb, BATCH, STATE_COLS))
    aq = jnp.transpose(aq, (1, 0, 2, 3)).reshape(jb, 4, BATCH * STATE_COLS)
    return gen, dtoe, aq


GEN_ROWS = 128
GROUP_SHIFT_CH = 4
GROUP_SHIFT_STATE = 6


def _replication_matrices():
    rq = np.zeros((SSM_CHUNK, SSM_CHUNK, GROUPS_PER_BLOCK, SSM_GROUP_CH), np.float32)
    rc = np.zeros((SSM_GROUP_CH, SSM_CHUNK, GROUPS_PER_BLOCK, SSM_GROUP_CH), np.float32)
    for i in range(SSM_CHUNK):
        rq[i, i] = 1.0
        rc[i, :, :, i] = 1.0
    return jnp.asarray(rq.reshape(SSM_CHUNK, CHUNK_COLS), BF16), jnp.asarray(rc.reshape(SSM_GROUP_CH, CHUNK_COLS), BF16)


def _state_operator(gen_ref, kind0, rq_ref, rc_ref, out_ref, row0, negate_im):
    col_g = (lax.broadcasted_iota(jnp.int32, (GEN_ROWS, CHUNK_COLS), 1) >> GROUP_SHIFT_CH) & (GROUPS_PER_BLOCK - 1)
    for r in range(STATE_COLS // GEN_ROWS):
        rows = slice(r * GEN_ROWS, (r + 1) * GEN_ROWS)
        pr = jnp.dot(gen_ref[kind0, rows, :].astype(BF16), rq_ref[...], preferred_element_type=F32)
        pi = jnp.dot(gen_ref[kind0 + 1, rows, :].astype(BF16), rq_ref[...], preferred_element_type=F32)
        fr = jnp.dot(gen_ref[kind0 + 2, rows, :].astype(BF16), rc_ref[...], preferred_element_type=F32)
        fi = jnp.dot(gen_ref[kind0 + 3, rows, :].astype(BF16), rc_ref[...], preferred_element_type=F32)
        row_g = ((lax.broadcasted_iota(jnp.int32, (GEN_ROWS, CHUNK_COLS), 0) + r * GEN_ROWS) >> GROUP_SHIFT_STATE)
        keep = (row_g & (GROUPS_PER_BLOCK - 1)) == col_g
        w_re = jnp.where(keep, pr * fr - pi * fi, 0.0)
        w_im = jnp.where(keep, pr * fi + pi * fr, 0.0)
        out_ref[row0 + r * GEN_ROWS:row0 + (r + 1) * GEN_ROWS, :] = w_re.astype(BF16)
        im_rows = slice(row0 + STATE_COLS + r * GEN_ROWS, row0 + STATE_COLS + (r + 1) * GEN_ROWS)
        out_ref[im_rows, :] = (-w_im if negate_im else w_im).astype(BF16)


def _ssm_state_kernel(x_ref, gen_ref, rq_ref, rc_ref, sre_ref, sim_ref, wst_scr):
    @pl.when(pl.program_id(2) == 0)
    def _():
        _state_operator(gen_ref.at[0, 0], 0, rq_ref, rc_ref, wst_scr, 0, False)

    s = lax.dot_general(x_ref[0], wst_scr[...], (((1,), (1,)), ((), ())), preferred_element_type=F32)
    sre_ref[0] = s[:, :STATE_COLS]
    sim_ref[0] = s[:, STATE_COLS:]


def _ssm_states(xr, gen, rq, rc):
    out_spec = pl.BlockSpec((1, N_CHUNKS, STATE_COLS), lambda j, d, b: (j, 0, d * BATCH + b))
    shape = jax.ShapeDtypeStruct((SSM_LANE_BLOCKS, N_CHUNKS, 2 * BATCH * STATE_COLS), F32)
    return pl.pallas_call(
        _ssm_state_kernel,
        grid=(SSM_LANE_BLOCKS, 2, BATCH),
        in_specs=[
            pl.BlockSpec((1, N_CHUNKS, CHUNK_COLS), lambda j, d, b: (j, b, 0)),
            pl.BlockSpec((1, 1, GEN_KINDS, STATE_COLS, SSM_CHUNK), lambda j, d, b: (j, d, 0, 0, 0)),
            pl.BlockSpec(rq.shape, lambda j, d, b: (0, 0)),
            pl.BlockSpec(rc.shape, lambda j, d, b: (0, 0)),
        ],
        out_specs=[out_spec, out_spec],
        out_shape=[shape, shape],
        scratch_shapes=[pltpu.VMEM((2 * STATE_COLS, CHUNK_COLS), BF16)],
        compiler_params=_cparams(("parallel", "parallel", "arbitrary")),
        name="ssm_states",
    )(xr, gen, rq, rc)


def _ssm_scan_kernel(sre_ref, sim_ref, aq_ref, hre_ref, him_ref):
    half = BATCH * STATE_COLS
    fwd, bwd = slice(0, half), slice(half, 2 * half)
    a0r, a0i, a1r, a1i = aq_ref[0, 0:1], aq_ref[0, 1:2], aq_ref[0, 2:3], aq_ref[0, 3:4]

    def body(k, carry):
        fr, fi, br, bi = carry
        kb = N_CHUNKS - 1 - k
        hre_ref[0, pl.ds(k, 1), fwd] = fr
        him_ref[0, pl.ds(k, 1), fwd] = fi
        hre_ref[0, pl.ds(kb, 1), bwd] = br
        him_ref[0, pl.ds(kb, 1), bwd] = bi
        sfr, sfi = sre_ref[0, pl.ds(k, 1), fwd], sim_ref[0, pl.ds(k, 1), fwd]
        sbr, sbi = sre_ref[0, pl.ds(kb, 1), bwd], sim_ref[0, pl.ds(kb, 1), bwd]
        return (a0r * fr - a0i * fi + sfr, a0r * fi + a0i * fr + sfi,
                a1r * br - a1i * bi + sbr, a1r * bi + a1i * br + sbi)

    z = jnp.zeros((1, half), F32)
    lax.fori_loop(0, N_CHUNKS, body, (z, z, z, z))


def _ssm_scan(sre, sim, aq):
    spec = pl.BlockSpec((1, N_CHUNKS, 2 * BATCH * STATE_COLS), lambda j: (j, 0, 0))
    shape = jax.ShapeDtypeStruct(sre.shape, F32)
    return pl.pallas_call(
        _ssm_scan_kernel,
        grid=(SSM_LANE_BLOCKS,),
        in_specs=[spec, spec, pl.BlockSpec((1, 4, BATCH * STATE_COLS), lambda j: (j, 0, 0))],
        out_specs=[spec, spec],
        out_shape=[shape, shape],
        compiler_params=_cparams(("parallel",)),
        name="ssm_scan",
    )(sre, sim, aq)


def _ssm_out_kernel(x_ref, gen_ref, dtoe_ref, rq_ref, rc_ref, h0r_ref, h0i_ref, h1r_ref, h1i_ref, y_ref,
                    m_scr, wo_scr, lag_scr, y_scr):
    @pl.when(pl.program_id(1) == 0)
    def _():
        for d in range(2):
            _state_operator(gen_ref.at[0, d], 4, rq_ref, rc_ref, wo_scr, d * 2 * STATE_COLS, True)
        shape = (TOEPLITZ_LAGS * LANES, LANES)
        t = jnp.dot(dtoe_ref[0].astype(BF16), rc_ref[:, :LANES], preferred_element_type=F32)
        row_g = (lax.broadcasted_iota(jnp.int32, shape, 0) >> GROUP_SHIFT_CH) & (GROUPS_PER_BLOCK - 1)
        col_g = (lax.broadcasted_iota(jnp.int32, shape, 1) >> GROUP_SHIFT_CH) & (GROUPS_PER_BLOCK - 1)
        lag_scr[...] = jnp.where(row_g == col_g, t, 0.0).astype(BF16)
        for qi in range(SSM_CHUNK):
            for qo in range(SSM_CHUNK):
                lag = qo - qi + SSM_CHUNK - 1
                m_scr[qi * LANES:(qi + 1) * LANES, qo * LANES:(qo + 1) * LANES] = lag_scr[lag * LANES:(lag + 1) * LANES, :]

    acc = jnp.dot(x_ref[0], m_scr[...], preferred_element_type=F32)
    for i, h_ref in enumerate((h0r_ref, h0i_ref, h1r_ref, h1i_ref)):
        acc += jnp.dot(h_ref[0].astype(BF16), wo_scr[i * STATE_COLS:(i + 1) * STATE_COLS, :],
                       preferred_element_type=F32)
    for q in range(SSM_CHUNK):
        y_scr[pl.ds(q, N_CHUNKS, stride=SSM_CHUNK), :] = acc[:, q * LANES:(q + 1) * LANES]
    y_ref[0] = y_scr[...].astype(BF16)


def _ssm_out(xr, gen, dtoe, rq, rc, hre, him):
    def hspec(d):
        return pl.BlockSpec((1, N_CHUNKS, STATE_COLS), lambda j, b: (j, 0, d * BATCH + b))
    return pl.pallas_call(
        _ssm_out_kernel,
        grid=(SSM_LANE_BLOCKS, BATCH),
        in_specs=[
            pl.BlockSpec((1, N_CHUNKS, CHUNK_COLS), lambda j, b: (j, b, 0)),
            pl.BlockSpec((1, 2, GEN_KINDS, STATE_COLS, SSM_CHUNK), lambda j, b: (j, 0, 0, 0, 0)),
            pl.BlockSpec((1, TOEPLITZ_LAGS * LANES, SSM_GROUP_CH), lambda j, b: (j, 0, 0)),
            pl.BlockSpec(rq.shape, lambda j, b: (0, 0)),
            pl.BlockSpec(rc.shape, lambda j, b: (0, 0)),
            hspec(0), hspec(0), hspec(1), hspec(1),
        ],
        out_specs=pl.BlockSpec((1, SEQ, LANES), lambda j, b: (j, b, 0)),
        out_shape=jax.ShapeDtypeStruct((SSM_LANE_BLOCKS, TOKENS, LANES), BF16),
        scratch_shapes=[pltpu.VMEM((CHUNK_COLS, CHUNK_COLS), BF16), pltpu.VMEM((4 * STATE_COLS, CHUNK_COLS), BF16),
                        pltpu.VMEM((TOEPLITZ_LAGS * LANES, LANES), BF16), pltpu.VMEM((SEQ, LANES), F32)],
        compiler_params=_cparams(("parallel", "arbitrary")),
        name="ssm_out",
    )(xr, gen, dtoe, rq, rc, hre, him, hre, him)


MERGE_TM = 512
GELU_C = math.sqrt(2.0 / math.pi)
PACK_SUB = D_MODEL // LANES


def _split_bf16(v):
    hi = v.astype(BF16)
    lo = (v - hi.astype(F32)).astype(BF16)
    return hi, lo


def _pack_rows(v, out_ref):
    for s in range(PACK_SUB):
        out_ref[pl.ds(s, v.shape[0], stride=PACK_SUB), :] = v[:, s * LANES:(s + 1) * LANES]


def _unpack_rows(buf_ref, start, rows):
    return jnp.concatenate([buf_ref[pl.ds(start + s, rows, stride=PACK_SUB), :] for s in range(PACK_SUB)], axis=1)


def _merge_kernel(x_ref, fm_ref, yc_ref, us_ref, gates_ref, dskip_ref, wf_ref, wglu_ref, ws_ref, wo_ref,
                  ng_ref, wrh_ref, wrl_ref, rb_ref, x1_ref, hp_ref, logit_ref):
    conv = jnp.concatenate([yc_ref[j].astype(F32) for j in range(SSM_LANE_BLOCKS)], axis=-1)
    u = jnp.concatenate([us_ref[j].astype(F32) for j in range(SSM_LANE_BLOCKS)], axis=-1)
    y = conv + dskip_ref[...] * u
    y = 0.5 * y * (1.0 + jnp.tanh(GELU_C * (y + 0.044715 * (y * y * y))))
    glu = jax.nn.sigmoid(jnp.dot(y.astype(BF16), wglu_ref[...], preferred_element_type=F32))
    y_s = jnp.dot((y * glu).astype(BF16), ws_ref[...], preferred_element_type=F32)
    y_f = jnp.dot(fm_ref[...], wf_ref[...], preferred_element_type=F32)
    merged = (gates_ref[:, :D_MODEL].astype(F32) * y_f + gates_ref[:, D_MODEL:].astype(F32) * y_s)
    x1 = x_ref[...] + jnp.dot(merged.astype(BF16), wo_ref[...], preferred_element_type=F32)
    x1_ref[...] = x1
    inv = lax.rsqrt(jnp.mean(x1 * x1, axis=-1, keepdims=True) + RMS_EPS)
    hn = x1 * inv * ng_ref[...]
    _pack_rows(hn, hp_ref)
    hi, lo = _split_bf16(hn)
    logits = (jnp.dot(hi, wrh_ref[...], preferred_element_type=F32)
              + jnp.dot(lo, wrh_ref[...], preferred_element_type=F32)
              + jnp.dot(hi, wrl_ref[...], preferred_element_type=F32))
    logit_ref[...] = logits + rb_ref[...]


def _merge(x, fmix, yconv, us, gates, dskip, wf, wglu, ws, wo, ng, wrh, wrl, rb):
    tm = MERGE_TM
    full = lambda a: pl.BlockSpec(a.shape, lambda i: (0,) * a.ndim)
    return pl.pallas_call(
        _merge_kernel,
        grid=(TOKENS // tm,),
        in_specs=[
            pl.BlockSpec((tm, D_MODEL), lambda i: (i, 0)),
            pl.BlockSpec((tm, FOURIER_WIDTH), lambda i: (i, 0)),
            pl.BlockSpec((SSM_LANE_BLOCKS, tm, LANES), lambda i: (0, i, 0)),
            pl.BlockSpec((SSM_LANE_BLOCKS, tm, LANES), lambda i: (0, i, 0)),
            pl.BlockSpec((tm, 2 * D_MODEL), lambda i: (i, 0)),
            full(dskip), full(wf), full(wglu), full(ws), full(wo), full(ng), full(wrh), full(wrl), full(rb),
        ],
        out_specs=[
            pl.BlockSpec((tm, D_MODEL), lambda i: (i, 0)),
            pl.BlockSpec((tm * PACK_SUB, LANES), lambda i: (i, 0)),
            pl.BlockSpec((tm, ROUTER_COLS), lambda i: (i, 0)),
        ],
        out_shape=[
            jax.ShapeDtypeStruct((TOKENS, D_MODEL), F32),
            jax.ShapeDtypeStruct((TOKENS * PACK_SUB, LANES), F32),
            jax.ShapeDtypeStruct((TOKENS, ROUTER_COLS), F32),
        ],
        compiler_params=_cparams(("parallel",)),
        name="merge",
    )(x, fmix, yconv, us, gates, dskip, wf, wglu, ws, wo, ng, wrh, wrl, rb)


ROUTE_TM = 512
EXPERT_LANE0 = MOE_GROUPS
INFO_EXPERT, INFO_RANK, INFO_GATE = 0, 2, 4


def _route_kernel(lg_ref, info_ref, cnt_ref, carry):
    @pl.when(pl.program_id(0) == 0)
    def _():
        carry[...] = jnp.zeros_like(carry)

    lg = lg_ref[...]
    tm = lg.shape[0]
    col_i = lax.broadcasted_iota(jnp.int32, lg.shape, 1)
    col = col_i.astype(F32)
    neg = jnp.float32(-jnp.inf)
    none = jnp.float32(ROUTER_COLS)

    def row_max(v):
        return jnp.max(v, axis=-1, keepdims=True)

    def first_at(v, m):
        return jnp.min(jnp.where(v == m, col, none), axis=-1, keepdims=True)

    gl = jnp.where(col_i < MOE_GROUPS, lg, neg)
    gmax = row_max(gl)
    p_g = 1.0 / jnp.sum(jnp.exp(gl - gmax), axis=-1, keepdims=True)
    lo = EXPERT_LANE0 + first_at(gl, gmax) * EXPERTS_PER_GROUP
    el = jnp.where((col >= lo) & (col < lo + EXPERTS_PER_GROUP), lg, neg)
    l1 = row_max(el)
    i1 = first_at(el, l1)
    el2 = jnp.where(col == i1, neg, el)
    l2 = row_max(el2)
    i2 = first_at(el2, l2)
    r = jnp.exp(l2 - l1)
    w1 = p_g / (1.0 + r)
    w2 = w1 * r

    hit1, hit2 = col == i1, col == i2
    onehot = jnp.where(hit1 | hit2, 1.0, 0.0)
    earlier = lax.broadcasted_iota(jnp.int32, (tm, tm), 0) > lax.broadcasted_iota(jnp.int32, (tm, tm), 1)
    before = jnp.dot(jnp.where(earlier, 1.0, 0.0).astype(BF16), onehot.astype(BF16),
                     preferred_element_type=F32) + carry[...]
    rank1 = jnp.sum(jnp.where(hit1, before, 0.0), axis=-1, keepdims=True)
    rank2 = jnp.sum(jnp.where(hit2, before, 0.0), axis=-1, keepdims=True)
    carry[...] += jnp.sum(onehot, axis=0, keepdims=True)
    cnt_ref[...] = carry[...]

    info = jnp.zeros(lg.shape, F32)
    for lane, v in ((INFO_EXPERT, i1 - EXPERT_LANE0), (INFO_EXPERT + 1, i2 - EXPERT_LANE0), (INFO_RANK, rank1),
                    (INFO_RANK + 1, rank2), (INFO_GATE, w1), (INFO_GATE + 1, w2)):
        info = jnp.where(col_i == lane, v, info)
    info_ref[...] = info


def _route(logits):
    return pl.pallas_call(
        _route_kernel,
        grid=(TOKENS // ROUTE_TM,),
        in_specs=[pl.BlockSpec((ROUTE_TM, ROUTER_COLS), lambda i: (i, 0))],
        out_specs=[pl.BlockSpec((ROUTE_TM, ROUTER_COLS), lambda i: (i, 0)),
                   pl.BlockSpec((1, ROUTER_COLS), lambda i: (0, 0))],
        out_shape=[jax.ShapeDtypeStruct((TOKENS, ROUTER_COLS), F32),
                   jax.ShapeDtypeStruct((1, ROUTER_COLS), F32)],
        scratch_shapes=[pltpu.VMEM((1, ROUTER_COLS), F32)],
        compiler_params=_cparams(("arbitrary",)),
        name="route",
    )(logits)


def _dispatch_plan(info, counts):
    expert = info[:, INFO_EXPERT:INFO_EXPERT + MOE_TOP_K].astype(jnp.int32)
    rank = info[:, INFO_RANK:INFO_RANK + MOE_TOP_K].astype(jnp.int32)
    cnt = counts[0, EXPERT_LANE0:EXPERT_LANE0 + N_EXPERTS].astype(jnp.int32)
    padded = ((cnt + MOE_ROWS - 1) // MOE_ROWS) * MOE_ROWS
    pends = jnp.cumsum(padded)
    pstarts = pends - padded
    ids = jnp.arange(N_EXPERTS, dtype=jnp.int32)
    dest = rank + jnp.sum(jnp.where(expert[..., None] == ids, pstarts, 0), axis=-1)
    n_used = pends[-1] // MOE_ROWS
    blocks = jnp.arange(MOE_BLOCKS, dtype=jnp.int32)
    block_e = jnp.sum((pends[None, :] <= (blocks * MOE_ROWS)[:, None]).astype(jnp.int32), axis=1)
    block_e = jnp.minimum(block_e, N_EXPERTS - 1)
    last_e = jnp.sum(jnp.where(blocks == n_used - 1, block_e, 0))
    used = (cnt > 0).astype(jnp.int32)
    ordinal = jnp.cumsum(used) - used
    n_experts_used = jnp.sum(used)
    by_ordinal = jnp.sum(jnp.where((ordinal[None, :] == ids[:, None]) & (used[None, :] > 0), ids[None, :], 0), axis=1)
    block_ord = jnp.sum(jnp.where(block_e[:, None] == ids[None, :], ordinal[None, :], 0), axis=1)
    block_first = (blocks * MOE_ROWS == jnp.sum(jnp.where(block_e[:, None] == ids[None, :], pstarts[None, :], 0), axis=1))
    block_first = (block_first & (blocks < n_used)).astype(jnp.int32)
    meta = jnp.concatenate([n_used.reshape(1), n_experts_used.reshape(1)]).astype(jnp.int32)
    return dest.reshape(TOKENS * MOE_TOP_K), block_ord.astype(jnp.int32), block_first, by_ordinal.astype(jnp.int32), meta, pends


MOE_SLOT_ROWS = MOE_ROWS * PACK_SUB
INVERT_UNROLL = 8


WEIGHT_SLOTS = 3


def _moe_kernel(dest_ref, ord_ref, first_ref, eo_ref, meta_ref, pend_ref, hp_ref, wg_hbm, wu_hbm, wd_hbm, y_ref,
                xbuf, wg_buf, wu_buf, wd_buf, tok_ref, sem, wsem):
    i = pl.program_id(0)
    n_used = meta_ref[0]
    n_experts_used = meta_ref[1]

    def slot_rows(slot):
        return xbuf.at[pl.ds(pl.multiple_of(slot * MOE_SLOT_ROWS, MOE_SLOT_ROWS), MOE_SLOT_ROWS), :]

    def weight_copies(ordinal):
        e = eo_ref[ordinal]
        ws = ordinal % WEIGHT_SLOTS
        return [pltpu.make_async_copy(hbm.at[e], buf.at[ws], wsem.at[ws])
                for hbm, buf in ((wg_hbm, wg_buf), (wu_hbm, wu_buf), (wd_hbm, wd_buf))]

    def invert_dispatch():
        def fill_expert(e, c):
            first = jnp.maximum(pend_ref[e] - MOE_ROWS, 0)

            def fill(r, c2):
                tok_ref[first + r] = (first + r) & (TOKENS - 1)
                return c2

            lax.fori_loop(0, MOE_ROWS, fill, 0, unroll=INVERT_UNROLL)
            return c

        lax.fori_loop(0, N_EXPERTS, fill_expert, 0)

        def place(t, c):
            for k in range(MOE_TOP_K):
                tok_ref[dest_ref[MOE_TOP_K * t + k]] = t
            return c

        lax.fori_loop(0, TOKENS, place, 0, unroll=INVERT_UNROLL)

    def gather(block, slot):
        base = block * MOE_ROWS
        for r in range(MOE_ROWS):
            src = hp_ref.at[pl.ds(pl.multiple_of(tok_ref[base + r] * PACK_SUB, PACK_SUB), PACK_SUB), :]
            dst = xbuf.at[pl.ds(pl.multiple_of(slot * MOE_SLOT_ROWS + r * PACK_SUB, PACK_SUB), PACK_SUB), :]
            pltpu.make_async_copy(src, dst, sem.at[slot]).start()

    @pl.when(i == 0)
    def _():
        for ahead in range(WEIGHT_SLOTS - 1):
            @pl.when(ahead < n_experts_used)
            def _():
                for cp in weight_copies(ahead):
                    cp.start()
        invert_dispatch()
        gather(0, 0)

    @pl.when(i + 1 < n_used)
    def _():
        gather(i + 1, (i + 1) % 2)

    @pl.when(i < n_used)
    def _():
        ordinal = ord_ref[i]

        @pl.when(first_ref[i] == 1)
        def _():
            for cp in weight_copies(ordinal):
                cp.wait()

            @pl.when(ordinal + WEIGHT_SLOTS - 1 < n_experts_used)
            def _():
                for cp in weight_copies(ordinal + WEIGHT_SLOTS - 1):
                    cp.start()

        slot = i % 2
        ws = ordinal % WEIGHT_SLOTS
        pltpu.make_async_copy(slot_rows(slot), slot_rows(slot), sem.at[slot]).wait()
        xb = _unpack_rows(xbuf, slot * MOE_SLOT_ROWS, MOE_ROWS).astype(BF16)
        a = jnp.dot(xb, wg_buf[ws].astype(BF16), preferred_element_type=F32)
        u = jnp.dot(xb, wu_buf[ws].astype(BF16), preferred_element_type=F32)
        act = (a * jax.nn.sigmoid(a) * u).astype(BF16)
        _pack_rows(jnp.dot(act, wd_buf[ws].astype(BF16), preferred_element_type=F32), y_ref)

    @pl.when(i >= n_used)
    def _():
        y_ref[...] = jnp.zeros_like(y_ref)


def _moe(dest, block_ord, block_first, by_ordinal, meta, pends, hp, w_gate, w_up, w_down):
    hbm = pl.BlockSpec(memory_space=pl.ANY)
    grid_spec = pltpu.PrefetchScalarGridSpec(
        num_scalar_prefetch=6,
        grid=(MOE_BLOCKS,),
        in_specs=[hbm, hbm, hbm, hbm],
        out_specs=pl.BlockSpec((MOE_SLOT_ROWS, LANES), lambda i, *_: (i, 0)),
        scratch_shapes=[pltpu.VMEM((2 * MOE_SLOT_ROWS, LANES), F32),
                        pltpu.VMEM((WEIGHT_SLOTS, D_MODEL, D_EXPERT), F32),
                        pltpu.VMEM((WEIGHT_SLOTS, D_MODEL, D_EXPERT), F32),
                        pltpu.VMEM((WEIGHT_SLOTS, D_EXPERT, D_MODEL), F32),
                        pltpu.SMEM((MOE_BLOCKS * MOE_ROWS,), jnp.int32),
                        pltpu.SemaphoreType.DMA((2,)), pltpu.SemaphoreType.DMA((WEIGHT_SLOTS,))],
    )
    return pl.pallas_call(
        _moe_kernel,
        grid_spec=grid_spec,
        out_shape=jax.ShapeDtypeStruct((MOE_BLOCKS * MOE_SLOT_ROWS, LANES), F32),
        compiler_params=_cparams(("arbitrary",)),
        name="moe",
    )(dest, block_ord, block_first, by_ordinal, meta, pends, hp, w_gate, w_up, w_down)


COMBINE_TM = 256


def _combine_kernel(dest_ref, x1_ref, info_ref, g_ref, y_ref, o_ref, ybuf, sem):
    i = pl.program_id(0)
    n = pl.num_programs(0)
    k_rows = COMBINE_TM * PACK_SUB
    slot_rows = MOE_TOP_K * k_rows

    def slot_ref(slot):
        return ybuf.at[pl.ds(pl.multiple_of(slot * slot_rows, slot_rows), slot_rows), :]

    def gather(tile, slot):
        base = tile * (COMBINE_TM * MOE_TOP_K)
        for r in range(COMBINE_TM):
            for k in range(MOE_TOP_K):
                row = dest_ref[base + r * MOE_TOP_K + k]
                src = y_ref.at[pl.ds(pl.multiple_of(row * PACK_SUB, PACK_SUB), PACK_SUB), :]
                at = slot * slot_rows + k * k_rows + r * PACK_SUB
                dst = ybuf.at[pl.ds(pl.multiple_of(at, PACK_SUB), PACK_SUB), :]
                pltpu.make_async_copy(src, dst, sem.at[slot]).start(priority=k)

    @pl.when(i == 0)
    def _():
        gather(0, 0)

    @pl.when(i + 1 < n)
    def _():
        gather(i + 1, (i + 1) % 2)

    slot = i % 2
    pltpu.make_async_copy(slot_ref(slot), slot_ref(slot), sem.at[slot]).wait()
    x2 = x1_ref[...]
    for k in range(MOE_TOP_K):
        yk = _unpack_rows(ybuf, slot * slot_rows + k * k_rows, COMBINE_TM)
        x2 = x2 + info_ref[:, INFO_GATE + k:INFO_GATE + k + 1] * yk
    inv = lax.rsqrt(jnp.mean(x2 * x2, axis=-1, keepdims=True) + RMS_EPS)
    o_ref[...] = x2 * inv * g_ref[...]


def _combine(dest, x1, info, g, y_pad):
    tm = COMBINE_TM
    grid_spec = pltpu.PrefetchScalarGridSpec(
        num_scalar_prefetch=1,
        grid=(TOKENS // tm,),
        in_specs=[
            pl.BlockSpec((tm, D_MODEL), lambda i, d: (i, 0)),
            pl.BlockSpec((tm, ROUTER_COLS), lambda i, d: (i, 0)),
            pl.BlockSpec((1, D_MODEL), lambda i, d: (0, 0)),
            pl.BlockSpec(memory_space=pl.ANY),
        ],
        out_specs=pl.BlockSpec((tm, D_MODEL), lambda i, d: (i, 0)),
        scratch_shapes=[pltpu.VMEM((2 * MOE_TOP_K * tm * PACK_SUB, LANES), F32), pltpu.SemaphoreType.DMA((2,))],
    )
    return pl.pallas_call(
        _combine_kernel,
        grid_spec=grid_spec,
        out_shape=jax.ShapeDtypeStruct((TOKENS, D_MODEL), F32),
        compiler_params=_cparams(("arbitrary",)),
        name="combine",
    )(dest, x1, info, g, y_pad)


def kernel(x, mix_norm_g, w_in, w_fourier_out, ssm_A_re, ssm_A_im, ssm_log_dt, ssm_B_re, ssm_B_im, ssm_C_re,
           ssm_C_im, ssm_D, ssm_w_glu, w_ssm_out, w_out, ffn_norm_g, router_group_w, router_group_b,
           router_expert_w, router_expert_b, expert_w_gate, expert_w_up, expert_w_down, final_norm_g):
    assert x.shape == (BATCH, SEQ, D_MODEL) and w_in.shape[0] == 1
    tw, f2, cdft = _dft_constants()
    rq, rc = _replication_matrices()

    vf, us, xr, gates = _inproj(x, mix_norm_g[0][None], w_in[0].astype(BF16), cdft)
    fmix = _dft(vf, tw, f2)

    gen, dtoe, aq = _ssm_factors(ssm_A_re[0], ssm_A_im[0], ssm_log_dt[0], ssm_B_re[0], ssm_B_im[0],
                                 ssm_C_re[0], ssm_C_im[0])
    sre, sim = _ssm_states(xr, gen, rq, rc)
    hre, him = _ssm_scan(sre, sim, aq)
    yconv = _ssm_out(xr, gen, dtoe, rq, rc, hre, him)

    w_router = jnp.concatenate([router_group_w[0], router_expert_w[0]], axis=1)
    w_router = jnp.pad(w_router, ((0, 0), (0, ROUTER_COLS - w_router.shape[1])))
    b_router = jnp.concatenate([router_group_b[0], router_expert_b[0]])
    b_router = jnp.pad(b_router, (0, ROUTER_COLS - b_router.shape[0]))[None]
    wr_hi = w_router.astype(BF16)
    wr_lo = (w_router - wr_hi.astype(F32)).astype(BF16)
    x1, hp, logits = _merge(x.reshape(TOKENS, D_MODEL), fmix, yconv, us, gates, ssm_D[0][None],
                            w_fourier_out[0].astype(BF16), ssm_w_glu[0].astype(BF16), w_ssm_out[0].astype(BF16),
                            w_out[0].astype(BF16), ffn_norm_g[0][None], wr_hi, wr_lo, b_router)

    info, counts = _route(logits)
    dest, block_ord, block_first, by_ordinal, meta, pends = _dispatch_plan(info, counts)
    y_pad = _moe(dest, block_ord, block_first, by_ordinal, meta, pends, hp, expert_w_gate[0], expert_w_up[0],
                 expert_w_down[0])
    out = _combine(dest, x1, info, final_norm_g[None], y_pad)
    return out.reshape(BATCH, SEQ, D_MODEL)
```

```python
import math

import numpy as np
import jax
import jax.numpy as jnp
from jax import lax
from jax.experimental import pallas as pl
from jax.experimental.pallas import tpu as pltpu

F32 = jnp.float32
BF16 = jnp.bfloat16

D_MODEL = 1024
BATCH = 4
SEQ = 4096
TOKENS = BATCH * SEQ
FOURIER_WIDTH = 512
FOURIER_GROUP_CH = 128
FOURIER_GROUPS = 4
SSM_WIDTH = 512
SSM_GROUP_CH = 16
SSM_GROUPS = 32
SSM_STATE = 64
MOE_GROUPS = 8
EXPERTS_PER_GROUP = 8
N_EXPERTS = 64
MOE_TOP_K = 2
D_EXPERT = 512
RMS_EPS = 1e-6

LANES = 128
SSM_CHUNK = 16
SSM_LANE_BLOCKS = SSM_WIDTH // LANES
GROUPS_PER_BLOCK = LANES // SSM_GROUP_CH
CHUNK_COLS = SSM_CHUNK * LANES
N_CHUNKS = SEQ // SSM_CHUNK
STATE_COLS = GROUPS_PER_BLOCK * SSM_STATE
MOE_ROWS = 256
MOE_BLOCKS = TOKENS * MOE_TOP_K // MOE_ROWS + N_EXPERTS
ROUTER_COLS = 128
VMEM_LIMIT = 48 * 1024 * 1024
SSM_OUT_VMEM_LIMIT = 58 * 1024 * 1024


def _cparams(sem, vmem=VMEM_LIMIT):
    return pltpu.CompilerParams(dimension_semantics=sem, vmem_limit_bytes=vmem)


IN_TM = 512


def _inproj_kernel(x_ref, g_ref, w_ref, cdft_ref, vf_ref, us_ref, xr_ref, gates_ref, zs_scr):
    x = x_ref[0]
    inv = lax.rsqrt(jnp.mean(x * x, axis=-1, keepdims=True) + RMS_EPS)
    h = (x * inv * g_ref[...]).astype(BF16)
    zf = jnp.dot(h, w_ref[:, 0:FOURIER_WIDTH], preferred_element_type=F32).astype(BF16)
    cdft = cdft_ref[...].astype(BF16)
    for g in range(FOURIER_GROUPS):
        sl = slice(g * LANES, (g + 1) * LANES)
        v = jnp.dot(zf[:, sl], cdft, preferred_element_type=F32)
        vf_ref[0, 0, :, sl] = v[:, :LANES].astype(BF16)
        vf_ref[0, 1, :, sl] = v[:, LANES:].astype(BF16)
    zs = jnp.dot(h, w_ref[:, FOURIER_WIDTH:FOURIER_WIDTH + SSM_WIDTH], preferred_element_type=F32)
    for j in range(SSM_LANE_BLOCKS):
        us_ref[j] = zs[:, j * LANES:(j + 1) * LANES].astype(BF16)
        zs_scr[j] = zs[:, j * LANES:(j + 1) * LANES]
    for j in range(SSM_LANE_BLOCKS):
        for q in range(SSM_CHUNK):
            piece = zs_scr[j, pl.ds(q, IN_TM // SSM_CHUNK, stride=SSM_CHUNK), :]
            xr_ref[j, :, q * LANES:(q + 1) * LANES] = piece.astype(BF16)
    base = FOURIER_WIDTH + SSM_WIDTH
    for n in range(4):
        zg = jnp.dot(h, w_ref[:, base + n * 512: base + (n + 1) * 512], preferred_element_type=F32)
        gates_ref[:, n * 512:(n + 1) * 512] = jax.nn.sigmoid(zg).astype(BF16)


def _inproj(x, g, w_in, cdft):
    nt = SEQ // IN_TM
    return pl.pallas_call(
        _inproj_kernel,
        grid=(BATCH, nt),
        in_specs=[
            pl.BlockSpec((1, IN_TM, D_MODEL), lambda b, i: (b, i, 0)),
            pl.BlockSpec((1, D_MODEL), lambda b, i: (0, 0)),
            pl.BlockSpec(w_in.shape, lambda b, i: (0, 0)),
            pl.BlockSpec(cdft.shape, lambda b, i: (0, 0)),
        ],
        out_specs=[
            pl.BlockSpec((1, 2, IN_TM, FOURIER_WIDTH), lambda b, i: (b, 0, i, 0)),
            pl.BlockSpec((SSM_LANE_BLOCKS, IN_TM, LANES), lambda b, i: (0, b * nt + i, 0)),
            pl.BlockSpec((SSM_LANE_BLOCKS, IN_TM // SSM_CHUNK, CHUNK_COLS), lambda b, i: (0, b * nt + i, 0)),
            pl.BlockSpec((IN_TM, 2 * D_MODEL), lambda b, i: (b * nt + i, 0)),
        ],
        out_shape=[
            jax.ShapeDtypeStruct((BATCH, 2, SEQ, FOURIER_WIDTH), BF16),
            jax.ShapeDtypeStruct((SSM_LANE_BLOCKS, TOKENS, LANES), BF16),
            jax.ShapeDtypeStruct((SSM_LANE_BLOCKS, TOKENS // SSM_CHUNK, CHUNK_COLS), BF16),
            jax.ShapeDtypeStruct((TOKENS, 2 * D_MODEL), BF16),
        ],
        scratch_shapes=[pltpu.VMEM((SSM_LANE_BLOCKS, IN_TM, LANES), F32)],
        compiler_params=_cparams(("parallel", "parallel")),
        name="inproj",
    )(x, g, w_in, cdft)


DFT_R1 = 8
DFT_R2 = SEQ // DFT_R1
DFT_LANES = 2 * LANES
DFT_ROWS = 16


def _cmul_const(z, w):
    re, im = z
    if abs(w.imag) < 1e-12:
        return (re, im) if abs(w.real - 1.0) < 1e-12 else (re * w.real, im * w.real)
    if abs(w.real) < 1e-12:
        return (im, -re) if abs(w.imag + 1.0) < 1e-12 else (-im * w.imag, re * w.imag)
    return re * w.real - im * w.imag, re * w.imag + im * w.real


def _fft_blocks(xs):
    n = len(xs)
    if n == 1:
        return xs
    even, odd = _fft_blocks(xs[0::2]), _fft_blocks(xs[1::2])
    out = [None] * n
    for k in range(n // 2):
        tr, ti = _cmul_const(odd[k], np.exp(-2j * np.pi * k / n))
        out[k] = (even[k][0] + tr, even[k][1] + ti)
        out[k + n // 2] = (even[k][0] - tr, even[k][1] - ti)
    return out


def _dft_kernel(v_ref, tw_ref, f_ref, o_ref, a_scr, o_scr):
    def tile(i, c):
        r0 = pl.multiple_of(i * DFT_ROWS, DFT_ROWS)
        for slab in range(DFT_LANES // LANES):
            lanes = slice(slab * LANES, (slab + 1) * LANES)
            xs = [(v_ref[0, 0, pl.ds(s1 * DFT_R2 + r0, DFT_ROWS), lanes].astype(F32),
                   v_ref[0, 1, pl.ds(s1 * DFT_R2 + r0, DFT_ROWS), lanes].astype(F32)) for s1 in range(DFT_R1)]
            for t1, (ar, ai) in enumerate(_fft_blocks(xs)):
                tr, ti = tw_ref[0, t1, pl.ds(r0, DFT_ROWS), :], tw_ref[1, t1, pl.ds(r0, DFT_ROWS), :]
                a_scr[t1, pl.ds(r0, DFT_ROWS), lanes] = (ar * tr - ai * ti).astype(BF16)
                a_scr[t1, pl.ds(DFT_R2 + r0, DFT_ROWS), lanes] = (ar * ti + ai * tr).astype(BF16)
        return c

    lax.fori_loop(0, DFT_R2 // DFT_ROWS, tile, 0)

    f2 = f_ref[...].astype(BF16)
    for t1 in range(DFT_R1):
        r = jnp.dot(f2, a_scr[t1], preferred_element_type=F32)
        for slab in range(DFT_LANES // LANES):
            o_scr[slab, pl.ds(t1, DFT_R2, stride=DFT_R1), :] = r[:, slab * LANES:(slab + 1) * LANES]
    for slab in range(DFT_LANES // LANES):
        o_ref[:, slab * LANES:(slab + 1) * LANES] = o_scr[slab].astype(BF16)


def _dft(v, tw, f2):
    nh = FOURIER_WIDTH // DFT_LANES
    return pl.pallas_call(
        _dft_kernel,
        grid=(BATCH, nh),
        in_specs=[
            pl.BlockSpec((1, 2, SEQ, DFT_LANES), lambda b, h: (b, 0, 0, h)),
            pl.BlockSpec(tw.shape, lambda b, h: (0, 0, 0, 0)),
            pl.BlockSpec(f2.shape, lambda b, h: (0, 0)),
        ],
        out_specs=pl.BlockSpec((SEQ, DFT_LANES), lambda b, h: (b, h)),
        out_shape=jax.ShapeDtypeStruct((TOKENS, FOURIER_WIDTH), BF16),
        scratch_shapes=[pltpu.VMEM((DFT_R1, 2 * DFT_R2, DFT_LANES), BF16),
                        pltpu.VMEM((DFT_LANES // LANES, SEQ, LANES), F32)],
        compiler_params=_cparams(("parallel", "parallel")),
        name="dft",
    )(v, tw, f2)


def _dft_constants():
    t1 = np.arange(DFT_R1)
    s2 = np.arange(DFT_R2)
    ang = 2.0 * np.pi * np.outer(t1, s2) / SEQ
    scale = 1.0 / math.sqrt(SEQ)
    tw = np.stack([np.cos(ang) * scale, -np.sin(ang) * scale])
    tw = np.repeat(tw[..., None], LANES, axis=-1)
    ang2 = 2.0 * np.pi * np.outer(s2, s2) / DFT_R2
    f2 = np.concatenate([np.cos(ang2), np.sin(ang2)], axis=1)
    kc = np.arange(FOURIER_GROUP_CH)
    angc = 2.0 * np.pi * np.outer(kc, kc) / FOURIER_GROUP_CH
    cs = 1.0 / math.sqrt(FOURIER_GROUP_CH)
    cdft = np.concatenate([np.cos(angc) * cs, -np.sin(angc) * cs], axis=1)
    return tuple(jnp.asarray(v, F32) for v in (tw, f2, cdft))


GEN_KINDS = 8
TOEPLITZ_LAGS = 2 * SSM_CHUNK - 1


def _ssm_factors(a_re, a_im, log_dt, b_re, b_im, c_re, c_im):
    q_len = SSM_CHUNK
    jb, gb, ch, ns = SSM_LANE_BLOCKS, GROUPS_PER_BLOCK, SSM_GROUP_CH, SSM_STATE
    dt = jnp.exp(log_dt)[..., None]
    lr, li = a_re * dt, a_im * dt
    steps = jnp.arange(q_len + 1, dtype=F32)
    mag = jnp.exp(lr[..., None] * steps)
    ang = li[..., None] * steps
    pr, pi = mag * jnp.cos(ang), mag * jnp.sin(ang)
    ar, ai = pr[..., 1], pi[..., 1]
    den = a_re * a_re + a_im * a_im
    cr = ((ar - 1.0) * a_re + ai * a_im) / den
    ci = (ai * a_re - (ar - 1.0) * a_im) / den
    bbr = cr[..., None] * b_re - ci[..., None] * b_im
    bbi = cr[..., None] * b_im + ci[..., None] * b_re

    def per_dir(fwd, bwd):
        return jnp.stack([fwd, bwd])

    pws = [per_dir(p[0][..., q_len - 1::-1][..., :q_len], p[1][..., :q_len]) for p in (pr, pi)]
    pwo = [per_dir(p[0][..., 1:], p[1][..., q_len:0:-1]) for p in (pr, pi)]
    ct = [jnp.swapaxes(c, -1, -2) for c in (c_re, c_im)]
    gen = jnp.stack([pws[0], pws[1], bbr, bbi, pwo[0], pwo[1], ct[0], ct[1]])
    gen = gen.reshape(GEN_KINDS, 2, jb, gb * ns, q_len)
    gen = jnp.transpose(gen, (2, 1, 0, 3, 4))

    prq, piq = pr[..., :q_len], pi[..., :q_len]
    cpr = jnp.einsum('dgcn,dgnt->dgtcn', c_re, prq) - jnp.einsum('dgcn,dgnt->dgtcn', c_im, piq)
    cpi = jnp.einsum('dgcn,dgnt->dgtcn', c_re, piq) + jnp.einsum('dgcn,dgnt->dgtcn', c_im, prq)
    kern = jnp.einsum('dgtcn,dgne->dgtce', cpr, bbr) - jnp.einsum('dgtcn,dgne->dgtce', cpi, bbi)
    lags = jnp.concatenate([kern[1][:, :0:-1], kern[0][:, :1] + kern[1][:, :1], kern[0][:, 1:]], axis=1)
    lags = lags.reshape(jb, gb, TOEPLITZ_LAGS, ch, ch)
    dtoe = jnp.transpose(lags, (0, 2, 1, 4, 3)).reshape(jb, TOEPLITZ_LAGS * LANES, ch)

    aq = jnp.stack([pr[0][..., q_len], pi[0][..., q_len], pr[1][..., q_len], pi[1][..., q_len]])
    aq = aq.reshape(4, jb, 1, STATE_COLS)
    aq = jnp.broadcast_to(aq, (4, jb, BATCH, STATE_COLS))
    aq = jnp.transpose(aq, (1, 0, 2, 3)).reshape(jb, 4, BATCH * STATE_COLS)
    return gen, dtoe, aq


GEN_ROWS = 128
GROUP_SHIFT_CH = 4
GROUP_SHIFT_STATE = 6


def _replication_matrices():
    rq = np.zeros((SSM_CHUNK, SSM_CHUNK, GROUPS_PER_BLOCK, SSM_GROUP_CH), np.float32)
    rc = np.zeros((SSM_GROUP_CH, SSM_CHUNK, GROUPS_PER_BLOCK, SSM_GROUP_CH), np.float32)
    for i in range(SSM_CHUNK):
        rq[i, i] = 1.0
        rc[i, :, :, i] = 1.0
    return jnp.asarray(rq.reshape(SSM_CHUNK, CHUNK_COLS), BF16), jnp.asarray(rc.reshape(SSM_GROUP_CH, CHUNK_COLS), BF16)


def _state_operator(gen_ref, kind0, rq_ref, rc_ref, out_ref, row0, negate_im):
    col_g = (lax.broadcasted_iota(jnp.int32, (GEN_ROWS, CHUNK_COLS), 1) >> GROUP_SHIFT_CH) & (GROUPS_PER_BLOCK - 1)
    for r in range(STATE_COLS // GEN_ROWS):
        rows = slice(r * GEN_ROWS, (r + 1) * GEN_ROWS)
        pr = jnp.dot(gen_ref[kind0, rows, :].astype(BF16), rq_ref[...], preferred_element_type=F32)
        pi = jnp.dot(gen_ref[kind0 + 1, rows, :].astype(BF16), rq_ref[...], preferred_element_type=F32)
        fr = jnp.dot(gen_ref[kind0 + 2, rows, :].astype(BF16), rc_ref[...], preferred_element_type=F32)
        fi = jnp.dot(gen_ref[kind0 + 3, rows, :].astype(BF16), rc_ref[...], preferred_element_type=F32)
        row_g = ((lax.broadcasted_iota(jnp.int32, (GEN_ROWS, CHUNK_COLS), 0) + r * GEN_ROWS) >> GROUP_SHIFT_STATE)
        keep = (row_g & (GROUPS_PER_BLOCK - 1)) == col_g
        w_re = jnp.where(keep, pr * fr - pi * fi, 0.0)
        w_im = jnp.where(keep, pr * fi + pi * fr, 0.0)
        out_ref[row0 + r * GEN_ROWS:row0 + (r + 1) * GEN_ROWS, :] = w_re.astype(BF16)
        im_rows = slice(row0 + STATE_COLS + r * GEN_ROWS, row0 + STATE_COLS + (r + 1) * GEN_ROWS)
        out_ref[im_rows, :] = (-w_im if negate_im else w_im).astype(BF16)


def _ssm_state_kernel(x_ref, gen_ref, rq_ref, rc_ref, sre_ref, sim_ref, wst_scr):
    _state_operator(gen_ref.at[0, 0], 0, rq_ref, rc_ref, wst_scr, 0, False)
    s = lax.dot_general(x_ref[0], wst_scr[...], (((1,), (1,)), ((), ())), preferred_element_type=F32)
    for b in range(BATCH):
        rows = slice(b * N_CHUNKS, (b + 1) * N_CHUNKS)
        sre_ref[0, :, b * STATE_COLS:(b + 1) * STATE_COLS] = s[rows, :STATE_COLS]
        sim_ref[0, :, b * STATE_COLS:(b + 1) * STATE_COLS] = s[rows, STATE_COLS:]


def _ssm_states(xr, gen, rq, rc):
    out_spec = pl.BlockSpec((1, N_CHUNKS, BATCH * STATE_COLS), lambda j, d: (j, 0, d))
    shape = jax.ShapeDtypeStruct((SSM_LANE_BLOCKS, N_CHUNKS, 2 * BATCH * STATE_COLS), F32)
    return pl.pallas_call(
        _ssm_state_kernel,
        grid=(SSM_LANE_BLOCKS, 2),
        in_specs=[
            pl.BlockSpec((1, BATCH * N_CHUNKS, CHUNK_COLS), lambda j, d: (j, 0, 0)),
            pl.BlockSpec((1, 1, GEN_KINDS, STATE_COLS, SSM_CHUNK), lambda j, d: (j, d, 0, 0, 0)),
            pl.BlockSpec(rq.shape, lambda j, d: (0, 0)),
            pl.BlockSpec(rc.shape, lambda j, d: (0, 0)),
        ],
        out_specs=[out_spec, out_spec],
        out_shape=[shape, shape],
        scratch_shapes=[pltpu.VMEM((2 * STATE_COLS, CHUNK_COLS), BF16)],
        compiler_params=_cparams(("parallel", "parallel")),
        name="ssm_states",
    )(xr, gen, rq, rc)


def _ssm_scan_kernel(sre_ref, sim_ref, aq_ref, hre_ref, him_ref):
    half = BATCH * STATE_COLS
    fwd, bwd = slice(0, half), slice(half, 2 * half)
    a0r, a0i, a1r, a1i = aq_ref[0, 0:1], aq_ref[0, 1:2], aq_ref[0, 2:3], aq_ref[0, 3:4]

    def body(k, carry):
        fr, fi, br, bi = carry
        kb = N_CHUNKS - 1 - k
        hre_ref[0, pl.ds(k, 1), fwd] = fr
        him_ref[0, pl.ds(k, 1), fwd] = fi
        hre_ref[0, pl.ds(kb, 1), bwd] = br
        him_ref[0, pl.ds(kb, 1), bwd] = bi
        sfr, sfi = sre_ref[0, pl.ds(k, 1), fwd], sim_ref[0, pl.ds(k, 1), fwd]
        sbr, sbi = sre_ref[0, pl.ds(kb, 1), bwd], sim_ref[0, pl.ds(kb, 1), bwd]
        return (a0r * fr - a0i * fi + sfr, a0r * fi + a0i * fr + sfi,
                a1r * br - a1i * bi + sbr, a1r * bi + a1i * br + sbi)

    z = jnp.zeros((1, half), F32)
    lax.fori_loop(0, N_CHUNKS, body, (z, z, z, z))


def _ssm_scan(sre, sim, aq):
    spec = pl.BlockSpec((1, N_CHUNKS, 2 * BATCH * STATE_COLS), lambda j: (j, 0, 0))
    shape = jax.ShapeDtypeStruct(sre.shape, F32)
    return pl.pallas_call(
        _ssm_scan_kernel,
        grid=(SSM_LANE_BLOCKS,),
        in_specs=[spec, spec, pl.BlockSpec((1, 4, BATCH * STATE_COLS), lambda j: (j, 0, 0))],
        out_specs=[spec, spec],
        out_shape=[shape, shape],
        compiler_params=_cparams(("parallel",)),
        name="ssm_scan",
    )(sre, sim, aq)


def _ssm_out_kernel(x_ref, gen_ref, dtoe_ref, rq_ref, rc_ref, h0r_ref, h0i_ref, h1r_ref, h1i_ref, y_ref,
                    m_scr, wo_scr, lag_scr, y_scr):
    @pl.when(pl.program_id(1) == 0)
    def _():
        for d in range(2):
            _state_operator(gen_ref.at[0, d], 4, rq_ref, rc_ref, wo_scr, d * 2 * STATE_COLS, True)
        shape = (TOEPLITZ_LAGS * LANES, LANES)
        t = jnp.dot(dtoe_ref[0].astype(BF16), rc_ref[:, :LANES], preferred_element_type=F32)
        row_g = (lax.broadcasted_iota(jnp.int32, shape, 0) >> GROUP_SHIFT_CH) & (GROUPS_PER_BLOCK - 1)
        col_g = (lax.broadcasted_iota(jnp.int32, shape, 1) >> GROUP_SHIFT_CH) & (GROUPS_PER_BLOCK - 1)
        lag_scr[...] = jnp.where(row_g == col_g, t, 0.0).astype(BF16)
        for qi in range(SSM_CHUNK):
            for qo in range(SSM_CHUNK):
                lag = qo - qi + SSM_CHUNK - 1
                m_scr[qi * LANES:(qi + 1) * LANES, qo * LANES:(qo + 1) * LANES] = lag_scr[lag * LANES:(lag + 1) * LANES, :]

    acc = jnp.dot(x_ref[0], m_scr[...], preferred_element_type=F32)
    for i, h_ref in enumerate((h0r_ref, h0i_ref, h1r_ref, h1i_ref)):
        h = jnp.concatenate([h_ref[0, :, b * STATE_COLS:(b + 1) * STATE_COLS] for b in range(SSM_OUT_BATCH)], axis=0)
        acc += jnp.dot(h.astype(BF16), wo_scr[i * STATE_COLS:(i + 1) * STATE_COLS, :], preferred_element_type=F32)
    for b in range(SSM_OUT_BATCH):
        for q in range(SSM_CHUNK):
            y_scr[pl.ds(q, N_CHUNKS, stride=SSM_CHUNK), :] = acc[b * N_CHUNKS:(b + 1) * N_CHUNKS, q * LANES:(q + 1) * LANES]
        y_ref[0, b * SEQ:(b + 1) * SEQ, :] = y_scr[...].astype(BF16)


SSM_OUT_BATCH = 2


def _ssm_out(xr, gen, dtoe, rq, rc, hre, him):
    nb = BATCH // SSM_OUT_BATCH

    def hspec(d):
        return pl.BlockSpec((1, N_CHUNKS, SSM_OUT_BATCH * STATE_COLS), lambda j, b: (j, 0, d * nb + b))
    return pl.pallas_call(
        _ssm_out_kernel,
        grid=(SSM_LANE_BLOCKS, nb),
        in_specs=[
            pl.BlockSpec((1, SSM_OUT_BATCH * N_CHUNKS, CHUNK_COLS), lambda j, b: (j, b, 0)),
            pl.BlockSpec((1, 2, GEN_KINDS, STATE_COLS, SSM_CHUNK), lambda j, b: (j, 0, 0, 0, 0)),
            pl.BlockSpec((1, TOEPLITZ_LAGS * LANES, SSM_GROUP_CH), lambda j, b: (j, 0, 0)),
            pl.BlockSpec(rq.shape, lambda j, b: (0, 0)),
            pl.BlockSpec(rc.shape, lambda j, b: (0, 0)),
            hspec(0), hspec(0), hspec(1), hspec(1),
        ],
        out_specs=pl.BlockSpec((1, SSM_OUT_BATCH * SEQ, LANES), lambda j, b: (j, b, 0)),
        out_shape=jax.ShapeDtypeStruct((SSM_LANE_BLOCKS, TOKENS, LANES), BF16),
        scratch_shapes=[pltpu.VMEM((CHUNK_COLS, CHUNK_COLS), BF16), pltpu.VMEM((4 * STATE_COLS, CHUNK_COLS), BF16),
                        pltpu.VMEM((TOEPLITZ_LAGS * LANES, LANES), BF16), pltpu.VMEM((SEQ, LANES), F32)],
        compiler_params=_cparams(("parallel", "arbitrary"), vmem=SSM_OUT_VMEM_LIMIT),
        name="ssm_out",
    )(xr, gen, dtoe, rq, rc, hre, him, hre, him)


MERGE_TM = 512
GELU_C = math.sqrt(2.0 / math.pi)
PACK_SUB = D_MODEL // LANES


def _split_bf16(v):
    hi = v.astype(BF16)
    lo = (v - hi.astype(F32)).astype(BF16)
    return hi, lo


def _pack_rows(v, out_ref):
    for s in range(PACK_SUB):
        out_ref[pl.ds(s, v.shape[0], stride=PACK_SUB), :] = v[:, s * LANES:(s + 1) * LANES]


def _unpack_rows(buf_ref, start, rows):
    return jnp.concatenate([buf_ref[pl.ds(start + s, rows, stride=PACK_SUB), :] for s in range(PACK_SUB)], axis=1)


def _merge_kernel(x_ref, fm_ref, yc_ref, us_ref, gates_ref, dskip_ref, wf_ref, wglu_ref, ws_ref, wo_ref,
                  ng_ref, wrh_ref, wrl_ref, rb_ref, x1_ref, hp_ref, logit_ref):
    conv = jnp.concatenate([yc_ref[j].astype(F32) for j in range(SSM_LANE_BLOCKS)], axis=-1)
    u = jnp.concatenate([us_ref[j].astype(F32) for j in range(SSM_LANE_BLOCKS)], axis=-1)
    y = conv + dskip_ref[...] * u
    y = 0.5 * y * (1.0 + jnp.tanh(GELU_C * (y + 0.044715 * (y * y * y))))
    glu = jax.nn.sigmoid(jnp.dot(y.astype(BF16), wglu_ref[...], preferred_element_type=F32))
    y_s = jnp.dot((y * glu).astype(BF16), ws_ref[...], preferred_element_type=F32)
    y_f = jnp.dot(fm_ref[...], wf_ref[...], preferred_element_type=F32)
    merged = (gates_ref[:, :D_MODEL].astype(F32) * y_f + gates_ref[:, D_MODEL:].astype(F32) * y_s)
    x1 = x_ref[...] + jnp.dot(merged.astype(BF16), wo_ref[...], preferred_element_type=F32)
    x1_ref[...] = x1
    inv = lax.rsqrt(jnp.mean(x1 * x1, axis=-1, keepdims=True) + RMS_EPS)
    hn = x1 * inv * ng_ref[...]
    _pack_rows(hn, hp_ref)
    hi, lo = _split_bf16(hn)
    logits = (jnp.dot(hi, wrh_ref[...], preferred_element_type=F32)
              + jnp.dot(lo, wrh_ref[...], preferred_element_type=F32)
              + jnp.dot(hi, wrl_ref[...], preferred_element_type=F32))
    logit_ref[...] = logits + rb_ref[...]


def _merge(x, fmix, yconv, us, gates, dskip, wf, wglu, ws, wo, ng, wrh, wrl, rb):
    tm = MERGE_TM
    full = lambda a: pl.BlockSpec(a.shape, lambda i: (0,) * a.ndim)
    return pl.pallas_call(
        _merge_kernel,
        grid=(TOKENS // tm,),
        in_specs=[
            pl.BlockSpec((tm, D_MODEL), lambda i: (i, 0)),
            pl.BlockSpec((tm, FOURIER_WIDTH), lambda i: (i, 0)),
            pl.BlockSpec((SSM_LANE_BLOCKS, tm, LANES), lambda i: (0, i, 0)),
            pl.BlockSpec((SSM_LANE_BLOCKS, tm, LANES), lambda i: (0, i, 0)),
            pl.BlockSpec((tm, 2 * D_MODEL), lambda i: (i, 0)),
            full(dskip), full(wf), full(wglu), full(ws), full(wo), full(ng), full(wrh), full(wrl), full(rb),
        ],
        out_specs=[
            pl.BlockSpec((tm, D_MODEL), lambda i: (i, 0)),
            pl.BlockSpec((tm * PACK_SUB, LANES), lambda i: (i, 0)),
            pl.BlockSpec((tm, ROUTER_COLS), lambda i: (i, 0)),
        ],
        out_shape=[
            jax.ShapeDtypeStruct((TOKENS, D_MODEL), F32),
            jax.ShapeDtypeStruct((TOKENS * PACK_SUB, LANES), F32),
            jax.ShapeDtypeStruct((TOKENS, ROUTER_COLS), F32),
        ],
        compiler_params=_cparams(("parallel",)),
        name="merge",
    )(x, fmix, yconv, us, gates, dskip, wf, wglu, ws, wo, ng, wrh, wrl, rb)


ROUTE_TM = 512
EXPERT_LANE0 = MOE_GROUPS
INFO_EXPERT, INFO_RANK, INFO_GATE = 0, 2, 4


def _route_kernel(lg_ref, info_ref, cnt_ref, carry):
    @pl.when(pl.program_id(0) == 0)
    def _():
        carry[...] = jnp.zeros_like(carry)

    lg = lg_ref[...]
    tm = lg.shape[0]
    col_i = lax.broadcasted_iota(jnp.int32, lg.shape, 1)
    col = col_i.astype(F32)
    neg = jnp.float32(-jnp.inf)
    none = jnp.float32(ROUTER_COLS)

    def row_max(v):
        return jnp.max(v, axis=-1, keepdims=True)

    def first_at(v, m):
        return jnp.min(jnp.where(v == m, col, none), axis=-1, keepdims=True)

    gl = jnp.where(col_i < MOE_GROUPS, lg, neg)
    gmax = row_max(gl)
    p_g = 1.0 / jnp.sum(jnp.exp(gl - gmax), axis=-1, keepdims=True)
    lo = EXPERT_LANE0 + first_at(gl, gmax) * EXPERTS_PER_GROUP
    el = jnp.where((col >= lo) & (col < lo + EXPERTS_PER_GROUP), lg, neg)
    l1 = row_max(el)
    i1 = first_at(el, l1)
    el2 = jnp.where(col == i1, neg, el)
    l2 = row_max(el2)
    i2 = first_at(el2, l2)
    r = jnp.exp(l2 - l1)
    w1 = p_g / (1.0 + r)
    w2 = w1 * r

    hit1, hit2 = col == i1, col == i2
    onehot = jnp.where(hit1 | hit2, 1.0, 0.0)
    earlier = lax.broadcasted_iota(jnp.int32, (tm, tm), 0) > lax.broadcasted_iota(jnp.int32, (tm, tm), 1)
    before = jnp.dot(jnp.where(earlier, 1.0, 0.0).astype(BF16), onehot.astype(BF16),
                     preferred_element_type=F32) + carry[...]
    rank1 = jnp.sum(jnp.where(hit1, before, 0.0), axis=-1, keepdims=True)
    rank2 = jnp.sum(jnp.where(hit2, before, 0.0), axis=-1, keepdims=True)
    carry[...] += jnp.sum(onehot, axis=0, keepdims=True)
    cnt_ref[...] = carry[...]

    info = jnp.zeros(lg.shape, F32)
    for lane, v in ((INFO_EXPERT, i1 - EXPERT_LANE0), (INFO_EXPERT + 1, i2 - EXPERT_LANE0), (INFO_RANK, rank1),
                    (INFO_RANK + 1, rank2), (INFO_GATE, w1), (INFO_GATE + 1, w2)):
        info = jnp.where(col_i == lane, v, info)
    info_ref[...] = info


def _route(logits):
    return pl.pallas_call(
        _route_kernel,
        grid=(TOKENS // ROUTE_TM,),
        in_specs=[pl.BlockSpec((ROUTE_TM, ROUTER_COLS), lambda i: (i, 0))],
        out_specs=[pl.BlockSpec((ROUTE_TM, ROUTER_COLS), lambda i: (i, 0)),
                   pl.BlockSpec((1, ROUTER_COLS), lambda i: (0, 0))],
        out_shape=[jax.ShapeDtypeStruct((TOKENS, ROUTER_COLS), F32),
                   jax.ShapeDtypeStruct((1, ROUTER_COLS), F32)],
        scratch_shapes=[pltpu.VMEM((1, ROUTER_COLS), F32)],
        compiler_params=_cparams(("arbitrary",)),
        name="route",
    )(logits)


def _dispatch_plan(info, counts):
    expert = info[:, INFO_EXPERT:INFO_EXPERT + MOE_TOP_K].astype(jnp.int32)
    rank = info[:, INFO_RANK:INFO_RANK + MOE_TOP_K].astype(jnp.int32)
    cnt = counts[0, EXPERT_LANE0:EXPERT_LANE0 + N_EXPERTS].astype(jnp.int32)
    padded = ((cnt + MOE_ROWS - 1) // MOE_ROWS) * MOE_ROWS
    pends = jnp.cumsum(padded)
    pstarts = pends - padded
    ids = jnp.arange(N_EXPERTS, dtype=jnp.int32)
    dest = rank + jnp.sum(jnp.where(expert[..., None] == ids, pstarts, 0), axis=-1)
    n_used = pends[-1] // MOE_ROWS
    blocks = jnp.arange(MOE_BLOCKS, dtype=jnp.int32)
    block_e = jnp.sum((pends[None, :] <= (blocks * MOE_ROWS)[:, None]).astype(jnp.int32), axis=1)
    block_e = jnp.minimum(block_e, N_EXPERTS - 1)
    last_e = jnp.sum(jnp.where(blocks == n_used - 1, block_e, 0))
    used = (cnt > 0).astype(jnp.int32)
    ordinal = jnp.cumsum(used) - used
    n_experts_used = jnp.sum(used)
    by_ordinal = jnp.sum(jnp.where((ordinal[None, :] == ids[:, None]) & (used[None, :] > 0), ids[None, :], 0), axis=1)
    block_ord = jnp.sum(jnp.where(block_e[:, None] == ids[None, :], ordinal[None, :], 0), axis=1)
    block_first = (blocks * MOE_ROWS == jnp.sum(jnp.where(block_e[:, None] == ids[None, :], pstarts[None, :], 0), axis=1))
    block_first = (block_first & (blocks < n_used)).astype(jnp.int32)
    meta = jnp.concatenate([n_used.reshape(1), n_experts_used.reshape(1)]).astype(jnp.int32)
    return dest.reshape(TOKENS * MOE_TOP_K), block_ord.astype(jnp.int32), block_first, by_ordinal.astype(jnp.int32), meta, pends


MOE_SLOT_ROWS = MOE_ROWS * PACK_SUB
INVERT_UNROLL = 8


WEIGHT_SLOTS = 3
GATHER_SLOTS = 3


def _moe_kernel(dest_ref, ord_ref, first_ref, eo_ref, meta_ref, pend_ref, hp_ref, wg_hbm, wu_hbm, wd_hbm, y_ref,
                xbuf, wg_buf, wu_buf, wd_buf, tok_ref, sem, wsem):
    i = pl.program_id(0)
    n_used = meta_ref[0]
    n_experts_used = meta_ref[1]

    def slot_rows(slot):
        return xbuf.at[pl.ds(pl.multiple_of(slot * MOE_SLOT_ROWS, MOE_SLOT_ROWS), MOE_SLOT_ROWS), :]

    def weight_copies(ordinal):
        e = eo_ref[ordinal]
        ws = ordinal % WEIGHT_SLOTS
        return [pltpu.make_async_copy(hbm.at[e], buf.at[ws], wsem.at[ws])
                for hbm, buf in ((wg_hbm, wg_buf), (wu_hbm, wu_buf), (wd_hbm, wd_buf))]

    def invert_dispatch():
        def fill_expert(e, c):
            first = jnp.maximum(pend_ref[e] - MOE_ROWS, 0)

            def fill(r, c2):
                tok_ref[first + r] = (first + r) & (TOKENS - 1)
                return c2

            lax.fori_loop(0, MOE_ROWS, fill, 0, unroll=INVERT_UNROLL)
            return c

        lax.fori_loop(0, N_EXPERTS, fill_expert, 0)

        def place(t, c):
            for k in range(MOE_TOP_K):
                tok_ref[dest_ref[MOE_TOP_K * t + k]] = t
            return c

        lax.fori_loop(0, TOKENS, place, 0, unroll=INVERT_UNROLL)

    def gather(block, slot):
        base = block * MOE_ROWS
        for r in range(MOE_ROWS):
            src = hp_ref.at[pl.ds(pl.multiple_of(tok_ref[base + r] * PACK_SUB, PACK_SUB), PACK_SUB), :]
            dst = xbuf.at[pl.ds(pl.multiple_of(slot * MOE_SLOT_ROWS + r * PACK_SUB, PACK_SUB), PACK_SUB), :]
            pltpu.make_async_copy(src, dst, sem.at[slot]).start()

    @pl.when(i == 0)
    def _():
        for ahead in range(WEIGHT_SLOTS - 1):
            @pl.when(ahead < n_experts_used)
            def _():
                for cp in weight_copies(ahead):
                    cp.start()
        invert_dispatch()
        for ahead in range(GATHER_SLOTS - 1):
            gather(jnp.minimum(ahead, n_used - 1), ahead)

    @pl.when(i < n_used)
    def _():
        ordinal = ord_ref[i]

        @pl.when(first_ref[i] == 1)
        def _():
            for cp in weight_copies(ordinal):
                cp.wait()

            @pl.when(ordinal + WEIGHT_SLOTS - 1 < n_experts_used)
            def _():
                for cp in weight_copies(ordinal + WEIGHT_SLOTS - 1):
                    cp.start()

        slot = i % GATHER_SLOTS
        ws = ordinal % WEIGHT_SLOTS
        pltpu.make_async_copy(slot_rows(slot), slot_rows(slot), sem.at[slot]).wait()
        ahead = i + GATHER_SLOTS - 1
        xb = _unpack_rows(xbuf, slot * MOE_SLOT_ROWS, MOE_ROWS).astype(BF16)
        gather(jnp.minimum(ahead, n_used - 1), ahead % GATHER_SLOTS)
        a = jnp.dot(xb, wg_buf[ws].astype(BF16), preferred_element_type=F32)
        u = jnp.dot(xb, wu_buf[ws].astype(BF16), preferred_element_type=F32)
        act = (a * jax.nn.sigmoid(a) * u).astype(BF16)
        y = jnp.dot(act, wd_buf[ws].astype(BF16), preferred_element_type=F32)
        _pack_rows(y, y_ref)

        @pl.when(i == n_used - 1)
        def _():
            for extra in range(1, GATHER_SLOTS):
                other = (i + extra) % GATHER_SLOTS
                pltpu.make_async_copy(slot_rows(other), slot_rows(other), sem.at[other]).wait()

    @pl.when(i >= n_used)
    def _():
        y_ref[...] = jnp.zeros_like(y_ref)


def _moe(dest, block_ord, block_first, by_ordinal, meta, pends, hp, w_gate, w_up, w_down):
    hbm = pl.BlockSpec(memory_space=pl.ANY)
    grid_spec = pltpu.PrefetchScalarGridSpec(
        num_scalar_prefetch=6,
        grid=(MOE_BLOCKS,),
        in_specs=[hbm, hbm, hbm, hbm],
        out_specs=pl.BlockSpec((MOE_SLOT_ROWS, LANES), lambda i, *_: (i, 0)),
        scratch_shapes=[pltpu.VMEM((GATHER_SLOTS * MOE_SLOT_ROWS, LANES), F32),
                        pltpu.VMEM((WEIGHT_SLOTS, D_MODEL, D_EXPERT), F32),
                        pltpu.VMEM((WEIGHT_SLOTS, D_MODEL, D_EXPERT), F32),
                        pltpu.VMEM((WEIGHT_SLOTS, D_EXPERT, D_MODEL), F32),
                        pltpu.SMEM((MOE_BLOCKS * MOE_ROWS,), jnp.int32),
                        pltpu.SemaphoreType.DMA((GATHER_SLOTS,)), pltpu.SemaphoreType.DMA((WEIGHT_SLOTS,))],
    )
    return pl.pallas_call(
        _moe_kernel,
        grid_spec=grid_spec,
        out_shape=jax.ShapeDtypeStruct((MOE_BLOCKS * MOE_SLOT_ROWS, LANES), F32),
        compiler_params=_cparams(("arbitrary",)),
        name="moe",
    )(dest, block_ord, block_first, by_ordinal, meta, pends, hp, w_gate, w_up, w_down)


COMBINE_TM = 256


def _combine_kernel(dest_ref, x1_ref, info_ref, g_ref, y_ref, o_ref, ybuf, sem):
    i = pl.program_id(0)
    n = pl.num_programs(0)
    k_rows = COMBINE_TM * PACK_SUB
    slot_rows = MOE_TOP_K * k_rows

    def slot_ref(slot):
        return ybuf.at[pl.ds(pl.multiple_of(slot * slot_rows, slot_rows), slot_rows), :]

    def gather(tile, slot):
        base = tile * (COMBINE_TM * MOE_TOP_K)
        for r in range(COMBINE_TM):
            for k in range(MOE_TOP_K):
                row = dest_ref[base + r * MOE_TOP_K + k]
                src = y_ref.at[pl.ds(pl.multiple_of(row * PACK_SUB, PACK_SUB), PACK_SUB), :]
                at = slot * slot_rows + k * k_rows + r * PACK_SUB
                dst = ybuf.at[pl.ds(pl.multiple_of(at, PACK_SUB), PACK_SUB), :]
                pltpu.make_async_copy(src, dst, sem.at[slot]).start(priority=k)

    @pl.when(i == 0)
    def _():
        gather(0, 0)

    @pl.when(i + 1 < n)
    def _():
        gather(i + 1, (i + 1) % 2)

    slot = i % 2
    pltpu.make_async_copy(slot_ref(slot), slot_ref(slot), sem.at[slot]).wait()
    x2 = x1_ref[...]
    for k in range(MOE_TOP_K):
        yk = _unpack_rows(ybuf, slot * slot_rows + k * k_rows, COMBINE_TM)
        x2 = x2 + info_ref[:, INFO_GATE + k:INFO_GATE + k + 1] * yk
    inv = lax.rsqrt(jnp.mean(x2 * x2, axis=-1, keepdims=True) + RMS_EPS)
    o_ref[...] = x2 * inv * g_ref[...]


def _combine(dest, x1, info, g, y_pad):
    tm = COMBINE_TM
    grid_spec = pltpu.PrefetchScalarGridSpec(
        num_scalar_prefetch=1,
        grid=(TOKENS // tm,),
        in_specs=[
            pl.BlockSpec((tm, D_MODEL), lambda i, d: (i, 0)),
            pl.BlockSpec((tm, ROUTER_COLS), lambda i, d: (i, 0)),
            pl.BlockSpec((1, D_MODEL), lambda i, d: (0, 0)),
            pl.BlockSpec(memory_space=pl.ANY),
        ],
        out_specs=pl.BlockSpec((tm, D_MODEL), lambda i, d: (i, 0)),
        scratch_shapes=[pltpu.VMEM((2 * MOE_TOP_K * tm * PACK_SUB, LANES), F32), pltpu.SemaphoreType.DMA((2,))],
    )
    return pl.pallas_call(
        _combine_kernel,
        grid_spec=grid_spec,
        out_shape=jax.ShapeDtypeStruct((TOKENS, D_MODEL), F32),
        compiler_params=_cparams(("arbitrary",)),
        name="combine",
    )(dest, x1, info, g, y_pad)


def kernel(x, mix_norm_g, w_in, w_fourier_out, ssm_A_re, ssm_A_im, ssm_log_dt, ssm_B_re, ssm_B_im, ssm_C_re,
           ssm_C_im, ssm_D, ssm_w_glu, w_ssm_out, w_out, ffn_norm_g, router_group_w, router_group_b,
           router_expert_w, router_expert_b, expert_w_gate, expert_w_up, expert_w_down, final_norm_g):
    assert x.shape == (BATCH, SEQ, D_MODEL) and w_in.shape[0] == 1
    tw, f2, cdft = _dft_constants()
    rq, rc = _replication_matrices()

    vf, us, xr, gates = _inproj(x, mix_norm_g[0][None], w_in[0].astype(BF16), cdft)
    fmix = _dft(vf, tw, f2)

    gen, dtoe, aq = _ssm_factors(ssm_A_re[0], ssm_A_im[0], ssm_log_dt[0], ssm_B_re[0], ssm_B_im[0],
                                 ssm_C_re[0], ssm_C_im[0])
    sre, sim = _ssm_states(xr, gen, rq, rc)
    hre, him = _ssm_scan(sre, sim, aq)
    yconv = _ssm_out(xr, gen, dtoe, rq, rc, hre, him)

    w_router = jnp.concatenate([router_group_w[0], router_expert_w[0]], axis=1)
    w_router = jnp.pad(w_router, ((0, 0), (0, ROUTER_COLS - w_router.shape[1])))
    b_router = jnp.concatenate([router_group_b[0], router_expert_b[0]])
    b_router = jnp.pad(b_router, (0, ROUTER_COLS - b_router.shape[0]))[None]
    wr_hi = w_router.astype(BF16)
    wr_lo = (w_router - wr_hi.astype(F32)).astype(BF16)
    x1, hp, logits = _merge(x.reshape(TOKENS, D_MODEL), fmix, yconv, us, gates, ssm_D[0][None],
                            w_fourier_out[0].astype(BF16), ssm_w_glu[0].astype(BF16), w_ssm_out[0].astype(BF16),
                            w_out[0].astype(BF16), ffn_norm_g[0][None], wr_hi, wr_lo, b_router)

    info, counts = _route(logits)
    dest, block_ord, block_first, by_ordinal, meta, pends = _dispatch_plan(info, counts)
    y_pad = _moe(dest, block_ord, block_first, by_ordinal, meta, pends, hp, expert_w_gate[0], expert_w_up[0],
                 expert_w_down[0])
    out = _combine(dest, x1, info, final_norm_g[None], y_pad)
    return out.reshape(BATCH, SEQ, D_MODEL)
```

```python
import math

import numpy as np
import jax
import jax.numpy as jnp
from jax import lax
from jax.experimental import pallas as pl
from jax.experimental.pallas import tpu as pltpu

F32 = jnp.float32
BF16 = jnp.bfloat16

D_MODEL = 1024
BATCH = 4
SEQ = 4096
TOKENS = BATCH * SEQ
FOURIER_WIDTH = 512
FOURIER_GROUP_CH = 128
FOURIER_GROUPS = 4
SSM_WIDTH = 512
SSM_GROUP_CH = 16
SSM_GROUPS = 32
SSM_STATE = 64
MOE_GROUPS = 8
EXPERTS_PER_GROUP = 8
N_EXPERTS = 64
MOE_TOP_K = 2
D_EXPERT = 512
RMS_EPS = 1e-6

LANES = 128
SSM_CHUNK = 16
SSM_LANE_BLOCKS = SSM_WIDTH // LANES
GROUPS_PER_BLOCK = LANES // SSM_GROUP_CH
CHUNK_COLS = SSM_CHUNK * LANES
N_CHUNKS = SEQ // SSM_CHUNK
STATE_COLS = GROUPS_PER_BLOCK * SSM_STATE
MOE_ROWS = 256
MOE_BLOCKS = TOKENS * MOE_TOP_K // MOE_ROWS + N_EXPERTS
ROUTER_COLS = 128
VMEM_LIMIT = 48 * 1024 * 1024


def _cparams(sem, vmem=VMEM_LIMIT):
    return pltpu.CompilerParams(dimension_semantics=sem, vmem_limit_bytes=vmem)


IN_TM = 512


def _inproj_kernel(x_ref, g_ref, w_ref, cdft_ref, vf_ref, us_ref, xr_ref, gates_ref, zs_scr):
    x = x_ref[0]
    inv = lax.rsqrt(jnp.mean(x * x, axis=-1, keepdims=True) + RMS_EPS)
    h = (x * inv * g_ref[...]).astype(BF16)
    zf = jnp.dot(h, w_ref[:, 0:FOURIER_WIDTH], preferred_element_type=F32).astype(BF16)
    cdft = cdft_ref[...].astype(BF16)
    for g in range(FOURIER_GROUPS):
        sl = slice(g * LANES, (g + 1) * LANES)
        v = jnp.dot(zf[:, sl], cdft, preferred_element_type=F32)
        vf_ref[0, 0, :, sl] = v[:, :LANES].astype(BF16)
        vf_ref[0, 1, :, sl] = v[:, LANES:].astype(BF16)
    zs = jnp.dot(h, w_ref[:, FOURIER_WIDTH:FOURIER_WIDTH + SSM_WIDTH], preferred_element_type=F32)
    for j in range(SSM_LANE_BLOCKS):
        us_ref[j] = zs[:, j * LANES:(j + 1) * LANES].astype(BF16)
        zs_scr[j] = zs[:, j * LANES:(j + 1) * LANES]
    for j in range(SSM_LANE_BLOCKS):
        for q in range(SSM_CHUNK):
            piece = zs_scr[j, pl.ds(q, IN_TM // SSM_CHUNK, stride=SSM_CHUNK), :]
            xr_ref[j, :, q * LANES:(q + 1) * LANES] = piece.astype(BF16)
    base = FOURIER_WIDTH + SSM_WIDTH
    for n in range(4):
        zg = jnp.dot(h, w_ref[:, base + n * 512: base + (n + 1) * 512], preferred_element_type=F32)
        gates_ref[:, n * 512:(n + 1) * 512] = jax.nn.sigmoid(zg).astype(BF16)


def _inproj(x, g, w_in, cdft):
    nt = SEQ // IN_TM
    return pl.pallas_call(
        _inproj_kernel,
        grid=(BATCH, nt),
        in_specs=[
            pl.BlockSpec((1, IN_TM, D_MODEL), lambda b, i: (b, i, 0)),
            pl.BlockSpec((1, D_MODEL), lambda b, i: (0, 0)),
            pl.BlockSpec(w_in.shape, lambda b, i: (0, 0)),
            pl.BlockSpec(cdft.shape, lambda b, i: (0, 0)),
        ],
        out_specs=[
            pl.BlockSpec((1, 2, IN_TM, FOURIER_WIDTH), lambda b, i: (b, 0, i, 0)),
            pl.BlockSpec((SSM_LANE_BLOCKS, IN_TM, LANES), lambda b, i: (0, b * nt + i, 0)),
            pl.BlockSpec((SSM_LANE_BLOCKS, IN_TM // SSM_CHUNK, CHUNK_COLS), lambda b, i: (0, b * nt + i, 0)),
            pl.BlockSpec((IN_TM, 2 * D_MODEL), lambda b, i: (b * nt + i, 0)),
        ],
        out_shape=[
            jax.ShapeDtypeStruct((BATCH, 2, SEQ, FOURIER_WIDTH), BF16),
            jax.ShapeDtypeStruct((SSM_LANE_BLOCKS, TOKENS, LANES), BF16),
            jax.ShapeDtypeStruct((SSM_LANE_BLOCKS, TOKENS // SSM_CHUNK, CHUNK_COLS), BF16),
            jax.ShapeDtypeStruct((TOKENS, 2 * D_MODEL), BF16),
        ],
        scratch_shapes=[pltpu.VMEM((SSM_LANE_BLOCKS, IN_TM, LANES), F32)],
        compiler_params=_cparams(("parallel", "parallel")),
        name="inproj",
    )(x, g, w_in, cdft)


DFT_R1 = 8
DFT_R2 = SEQ // DFT_R1
DFT_LANES = 2 * LANES
DFT_ROWS = 16


def _cmul_const(z, w):
    re, im = z
    if abs(w.imag) < 1e-12:
        return (re, im) if abs(w.real - 1.0) < 1e-12 else (re * w.real, im * w.real)
    if abs(w.real) < 1e-12:
        return (im, -re) if abs(w.imag + 1.0) < 1e-12 else (-im * w.imag, re * w.imag)
    return re * w.real - im * w.imag, re * w.imag + im * w.real


def _fft_blocks(xs):
    n = len(xs)
    if n == 1:
        return xs
    even, odd = _fft_blocks(xs[0::2]), _fft_blocks(xs[1::2])
    out = [None] * n
    for k in range(n // 2):
        tr, ti = _cmul_const(odd[k], np.exp(-2j * np.pi * k / n))
        out[k] = (even[k][0] + tr, even[k][1] + ti)
        out[k + n // 2] = (even[k][0] - tr, even[k][1] - ti)
    return out


def _dft_kernel(v_ref, tw_ref, f_ref, o_ref, a_scr, o_scr):
    def tile(i, c):
        r0 = pl.multiple_of(i * DFT_ROWS, DFT_ROWS)
        for slab in range(DFT_LANES // LANES):
            lanes = slice(slab * LANES, (slab + 1) * LANES)
            xs = [(v_ref[0, 0, pl.ds(s1 * DFT_R2 + r0, DFT_ROWS), lanes].astype(F32),
                   v_ref[0, 1, pl.ds(s1 * DFT_R2 + r0, DFT_ROWS), lanes].astype(F32)) for s1 in range(DFT_R1)]
            for t1, (ar, ai) in enumerate(_fft_blocks(xs)):
                tr, ti = tw_ref[0, t1, pl.ds(r0, DFT_ROWS), :], tw_ref[1, t1, pl.ds(r0, DFT_ROWS), :]
                a_scr[t1, pl.ds(r0, DFT_ROWS), lanes] = (ar * tr - ai * ti).astype(BF16)
                a_scr[t1, pl.ds(DFT_R2 + r0, DFT_ROWS), lanes] = (ar * ti + ai * tr).astype(BF16)
        return c

    lax.fori_loop(0, DFT_R2 // DFT_ROWS, tile, 0)

    f2 = f_ref[...].astype(BF16)
    for t1 in range(DFT_R1):
        r = jnp.dot(f2, a_scr[t1], preferred_element_type=F32)
        for slab in range(DFT_LANES // LANES):
            o_scr[slab, pl.ds(t1, DFT_R2, stride=DFT_R1), :] = r[:, slab * LANES:(slab + 1) * LANES]
    for slab in range(DFT_LANES // LANES):
        o_ref[:, slab * LANES:(slab + 1) * LANES] = o_scr[slab].astype(BF16)


def _dft(v, tw, f2):
    nh = FOURIER_WIDTH // DFT_LANES
    return pl.pallas_call(
        _dft_kernel,
        grid=(BATCH, nh),
        in_specs=[
            pl.BlockSpec((1, 2, SEQ, DFT_LANES), lambda b, h: (b, 0, 0, h)),
            pl.BlockSpec(tw.shape, lambda b, h: (0, 0, 0, 0)),
            pl.BlockSpec(f2.shape, lambda b, h: (0, 0)),
        ],
        out_specs=pl.BlockSpec((SEQ, DFT_LANES), lambda b, h: (b, h)),
        out_shape=jax.ShapeDtypeStruct((TOKENS, FOURIER_WIDTH), BF16),
        scratch_shapes=[pltpu.VMEM((DFT_R1, 2 * DFT_R2, DFT_LANES), BF16),
                        pltpu.VMEM((DFT_LANES // LANES, SEQ, LANES), F32)],
        compiler_params=_cparams(("parallel", "parallel")),
        name="dft",
    )(v, tw, f2)


def _dft_constants():
    t1 = np.arange(DFT_R1)
    s2 = np.arange(DFT_R2)
    ang = 2.0 * np.pi * np.outer(t1, s2) / SEQ
    scale = 1.0 / math.sqrt(SEQ)
    tw = np.stack([np.cos(ang) * scale, -np.sin(ang) * scale])
    tw = np.repeat(tw[..., None], LANES, axis=-1)
    ang2 = 2.0 * np.pi * np.outer(s2, s2) / DFT_R2
    f2 = np.concatenate([np.cos(ang2), np.sin(ang2)], axis=1)
    kc = np.arange(FOURIER_GROUP_CH)
    angc = 2.0 * np.pi * np.outer(kc, kc) / FOURIER_GROUP_CH
    cs = 1.0 / math.sqrt(FOURIER_GROUP_CH)
    cdft = np.concatenate([np.cos(angc) * cs, -np.sin(angc) * cs], axis=1)
    return tuple(jnp.asarray(v, F32) for v in (tw, f2, cdft))


GEN_KINDS = 8
TOEPLITZ_LAGS = 2 * SSM_CHUNK - 1


def _ssm_factors(a_re, a_im, log_dt, b_re, b_im, c_re, c_im):
    q_len = SSM_CHUNK
    jb, gb, ch, ns = SSM_LANE_BLOCKS, GROUPS_PER_BLOCK, SSM_GROUP_CH, SSM_STATE
    dt = jnp.exp(log_dt)[..., None]
    lr, li = a_re * dt, a_im * dt
    steps = jnp.arange(q_len + 1, dtype=F32)
    mag = jnp.exp(lr[..., None] * steps)
    ang = li[..., None] * steps
    pr, pi = mag * jnp.cos(ang), mag * jnp.sin(ang)
    ar, ai = pr[..., 1], pi[..., 1]
    den = a_re * a_re + a_im * a_im
    cr = ((ar - 1.0) * a_re + ai * a_im) / den
    ci = (ai * a_re - (ar - 1.0) * a_im) / den
    bbr = cr[..., None] * b_re - ci[..., None] * b_im
    bbi = cr[..., None] * b_im + ci[..., None] * b_re

    def per_dir(fwd, bwd):
        return jnp.stack([fwd, bwd])

    pws = [per_dir(p[0][..., q_len - 1::-1][..., :q_len], p[1][..., :q_len]) for p in (pr, pi)]
    pwo = [per_dir(p[0][..., 1:], p[1][..., q_len:0:-1]) for p in (pr, pi)]
    ct = [jnp.swapaxes(c, -1, -2) for c in (c_re, c_im)]
    gen = jnp.stack([pws[0], pws[1], bbr, bbi, pwo[0], pwo[1], ct[0], ct[1]])
    gen = gen.reshape(GEN_KINDS, 2, jb, gb * ns, q_len)

    prq, piq = pr[..., :q_len], pi[..., :q_len]
    cpr = jnp.einsum('dgcn,dgnt->dgtcn', c_re, prq) - jnp.einsum('dgcn,dgnt->dgtcn', c_im, piq)
    cpi = jnp.einsum('dgcn,dgnt->dgtcn', c_re, piq) + jnp.einsum('dgcn,dgnt->dgtcn', c_im, prq)
    kern = jnp.einsum('dgtcn,dgne->dgtec', cpr, bbr) - jnp.einsum('dgtcn,dgne->dgtec', cpi, bbi)
    lags = jnp.concatenate([kern[1][:, :0:-1], kern[0][:, :1] + kern[1][:, :1], kern[0][:, 1:]], axis=1)
    lags = lags.reshape(jb, gb, TOEPLITZ_LAGS, ch, ch)
    dtoe = jnp.transpose(lags, (0, 2, 1, 3, 4)).reshape(jb, TOEPLITZ_LAGS * LANES, ch)

    aq = jnp.stack([pr[0][..., q_len], pi[0][..., q_len], pr[1][..., q_len], pi[1][..., q_len]])
    aq = aq.reshape(4, jb, 1, STATE_COLS)
    aq = jnp.broadcast_to(aq, (4, jb, BATCH, STATE_COLS))
    aq = jnp.transpose(aq, (1, 0, 2, 3)).reshape(jb, 4, BATCH * STATE_COLS)
    return gen, dtoe, aq


GEN_ROWS = 128
GROUP_SHIFT_CH = 4
GROUP_SHIFT_STATE = 6


def _replication_matrices():
    rq = np.zeros((SSM_CHUNK, SSM_CHUNK, GROUPS_PER_BLOCK, SSM_GROUP_CH), np.float32)
    rc = np.zeros((SSM_GROUP_CH, SSM_CHUNK, GROUPS_PER_BLOCK, SSM_GROUP_CH), np.float32)
    for i in range(SSM_CHUNK):
        rq[i, i] = 1.0
        rc[i, :, :, i] = 1.0
    return jnp.asarray(rq.reshape(SSM_CHUNK, CHUNK_COLS), BF16), jnp.asarray(rc.reshape(SSM_GROUP_CH, CHUNK_COLS), BF16)


def _state_operator(factor, kind0, rq_ref, rc_ref, out_ref, row0, negate_im):
    col_g = (lax.broadcasted_iota(jnp.int32, (GEN_ROWS, CHUNK_COLS), 1) >> GROUP_SHIFT_CH) & (GROUPS_PER_BLOCK - 1)
    for r in range(STATE_COLS // GEN_ROWS):
        rows = slice(r * GEN_ROWS, (r + 1) * GEN_ROWS)
        pr = jnp.dot(factor(kind0, rows).astype(BF16), rq_ref[...], preferred_element_type=F32)
        pi = jnp.dot(factor(kind0 + 1, rows).astype(BF16), rq_ref[...], preferred_element_type=F32)
        fr = jnp.dot(factor(kind0 + 2, rows).astype(BF16), rc_ref[...], preferred_element_type=F32)
        fi = jnp.dot(factor(kind0 + 3, rows).astype(BF16), rc_ref[...], preferred_element_type=F32)
        row_g = ((lax.broadcasted_iota(jnp.int32, (GEN_ROWS, CHUNK_COLS), 0) + r * GEN_ROWS) >> GROUP_SHIFT_STATE)
        keep = (row_g & (GROUPS_PER_BLOCK - 1)) == col_g
        w_re = jnp.where(keep, pr * fr - pi * fi, 0.0)
        w_im = jnp.where(keep, pr * fi + pi * fr, 0.0)
        out_ref[row0 + r * GEN_ROWS:row0 + (r + 1) * GEN_ROWS, :] = w_re.astype(BF16)
        im_rows = slice(row0 + STATE_COLS + r * GEN_ROWS, row0 + STATE_COLS + (r + 1) * GEN_ROWS)
        out_ref[im_rows, :] = (-w_im if negate_im else w_im).astype(BF16)


def _ssm_state_kernel(x_ref, gen_ref, rq_ref, rc_ref, sre_ref, sim_ref, wst_scr):
    _state_operator(lambda kind, rows: gen_ref[kind, 0, 0, rows, :], 0, rq_ref, rc_ref, wst_scr, 0, False)
    s = lax.dot_general(x_ref[0], wst_scr[...], (((1,), (1,)), ((), ())), preferred_element_type=F32)
    for b in range(BATCH):
        rows = slice(b * N_CHUNKS, (b + 1) * N_CHUNKS)
        sre_ref[0, :, b * STATE_COLS:(b + 1) * STATE_COLS] = s[rows, :STATE_COLS]
        sim_ref[0, :, b * STATE_COLS:(b + 1) * STATE_COLS] = s[rows, STATE_COLS:]


def _ssm_states(xr, gen, rq, rc):
    out_spec = pl.BlockSpec((1, N_CHUNKS, BATCH * STATE_COLS), lambda j, d: (j, 0, d))
    shape = jax.ShapeDtypeStruct((SSM_LANE_BLOCKS, N_CHUNKS, 2 * BATCH * STATE_COLS), F32)
    return pl.pallas_call(
        _ssm_state_kernel,
        grid=(SSM_LANE_BLOCKS, 2),
        in_specs=[
            pl.BlockSpec((1, BATCH * N_CHUNKS, CHUNK_COLS), lambda j, d: (j, 0, 0)),
            pl.BlockSpec((GEN_KINDS, 1, 1, STATE_COLS, SSM_CHUNK), lambda j, d: (0, d, j, 0, 0)),
            pl.BlockSpec(rq.shape, lambda j, d: (0, 0)),
            pl.BlockSpec(rc.shape, lambda j, d: (0, 0)),
        ],
        out_specs=[out_spec, out_spec],
        out_shape=[shape, shape],
        scratch_shapes=[pltpu.VMEM((2 * STATE_COLS, CHUNK_COLS), BF16)],
        compiler_params=_cparams(("parallel", "parallel")),
        name="ssm_states",
    )(xr, gen, rq, rc)


def _ssm_scan_kernel(sre_ref, sim_ref, aq_ref, hre_ref, him_ref):
    half = BATCH * STATE_COLS
    fwd, bwd = slice(0, half), slice(half, 2 * half)
    a0r, a0i, a1r, a1i = aq_ref[0, 0:1], aq_ref[0, 1:2], aq_ref[0, 2:3], aq_ref[0, 3:4]

    def body(k, carry):
        fr, fi, br, bi = carry
        kb = N_CHUNKS - 1 - k
        hre_ref[0, pl.ds(k, 1), fwd] = fr
        him_ref[0, pl.ds(k, 1), fwd] = fi
        hre_ref[0, pl.ds(kb, 1), bwd] = br
        him_ref[0, pl.ds(kb, 1), bwd] = bi
        sfr, sfi = sre_ref[0, pl.ds(k, 1), fwd], sim_ref[0, pl.ds(k, 1), fwd]
        sbr, sbi = sre_ref[0, pl.ds(kb, 1), bwd], sim_ref[0, pl.ds(kb, 1), bwd]
        return (a0r * fr - a0i * fi + sfr, a0r * fi + a0i * fr + sfi,
                a1r * br - a1i * bi + sbr, a1r * bi + a1i * br + sbi)

    z = jnp.zeros((1, half), F32)
    lax.fori_loop(0, N_CHUNKS, body, (z, z, z, z))


def _ssm_scan(sre, sim, aq):
    spec = pl.BlockSpec((1, N_CHUNKS, 2 * BATCH * STATE_COLS), lambda j: (j, 0, 0))
    shape = jax.ShapeDtypeStruct(sre.shape, F32)
    return pl.pallas_call(
        _ssm_scan_kernel,
        grid=(SSM_LANE_BLOCKS,),
        in_specs=[spec, spec, pl.BlockSpec((1, 4, BATCH * STATE_COLS), lambda j: (j, 0, 0))],
        out_specs=[spec, spec],
        out_shape=[shape, shape],
        compiler_params=_cparams(("parallel",)),
        name="ssm_scan",
    )(sre, sim, aq)


def _ssm_out_kernel(x_ref, gen_ref, dtoe_ref, rq_ref, rc_ref, h0r_ref, h0i_ref, h1r_ref, h1i_ref, y_ref,
                    m_scr, wo_scr, lag_scr, y_scr):
    @pl.when(pl.program_id(1) == 0)
    def _():
        for d in range(2):
            _state_operator(lambda kind, rows, d=d: gen_ref[kind, d, 0, rows, :], 4, rq_ref, rc_ref, wo_scr,
                            d * 2 * STATE_COLS, True)
        shape = (TOEPLITZ_LAGS * LANES, LANES)
        t = jnp.dot(dtoe_ref[0].astype(BF16), rc_ref[:, :LANES], preferred_element_type=F32)
        row_g = (lax.broadcasted_iota(jnp.int32, shape, 0) >> GROUP_SHIFT_CH) & (GROUPS_PER_BLOCK - 1)
        col_g = (lax.broadcasted_iota(jnp.int32, shape, 1) >> GROUP_SHIFT_CH) & (GROUPS_PER_BLOCK - 1)
        lag_scr[...] = jnp.where(row_g == col_g, t, 0.0).astype(BF16)
        for qi in range(SSM_CHUNK):
            for qo in range(SSM_CHUNK):
                lag = qo - qi + SSM_CHUNK - 1
                m_scr[qi * LANES:(qi + 1) * LANES, qo * LANES:(qo + 1) * LANES] = lag_scr[lag * LANES:(lag + 1) * LANES, :]

    acc = jnp.dot(x_ref[0], m_scr[...], preferred_element_type=F32)
    for i, h_ref in enumerate((h0r_ref, h0i_ref, h1r_ref, h1i_ref)):
        h = jnp.concatenate([h_ref[0, :, b * STATE_COLS:(b + 1) * STATE_COLS] for b in range(SSM_OUT_BATCH)], axis=0)
        acc += jnp.dot(h.astype(BF16), wo_scr[i * STATE_COLS:(i + 1) * STATE_COLS, :], preferred_element_type=F32)
    for b in range(SSM_OUT_BATCH):
        for q in range(SSM_CHUNK):
            y_scr[pl.ds(q, N_CHUNKS, stride=SSM_CHUNK), :] = acc[b * N_CHUNKS:(b + 1) * N_CHUNKS, q * LANES:(q + 1) * LANES]
        y_ref[0, b * SEQ:(b + 1) * SEQ, :] = y_scr[...].astype(BF16)


SSM_OUT_BATCH = 1


def _ssm_out(xr, gen, dtoe, rq, rc, hre, him):
    nb = BATCH // SSM_OUT_BATCH

    def hspec(d):
        return pl.BlockSpec((1, N_CHUNKS, SSM_OUT_BATCH * STATE_COLS), lambda j, b: (j, 0, d * nb + b))
    return pl.pallas_call(
        _ssm_out_kernel,
        grid=(SSM_LANE_BLOCKS, nb),
        in_specs=[
            pl.BlockSpec((1, SSM_OUT_BATCH * N_CHUNKS, CHUNK_COLS), lambda j, b: (j, b, 0)),
            pl.BlockSpec((GEN_KINDS, 2, 1, STATE_COLS, SSM_CHUNK), lambda j, b: (0, 0, j, 0, 0)),
            pl.BlockSpec((1, TOEPLITZ_LAGS * LANES, SSM_GROUP_CH), lambda j, b: (j, 0, 0)),
            pl.BlockSpec(rq.shape, lambda j, b: (0, 0)),
            pl.BlockSpec(rc.shape, lambda j, b: (0, 0)),
            hspec(0), hspec(0), hspec(1), hspec(1),
        ],
        out_specs=pl.BlockSpec((1, SSM_OUT_BATCH * SEQ, LANES), lambda j, b: (j, b, 0)),
        out_shape=jax.ShapeDtypeStruct((SSM_LANE_BLOCKS, TOKENS, LANES), BF16),
        scratch_shapes=[pltpu.VMEM((CHUNK_COLS, CHUNK_COLS), BF16), pltpu.VMEM((4 * STATE_COLS, CHUNK_COLS), BF16),
                        pltpu.VMEM((TOEPLITZ_LAGS * LANES, LANES), BF16), pltpu.VMEM((SEQ, LANES), F32)],
        compiler_params=_cparams(("parallel", "arbitrary")),
        name="ssm_out",
    )(xr, gen, dtoe, rq, rc, hre, him, hre, him)


MERGE_TM = 512
GELU_C = math.sqrt(2.0 / math.pi)
PACK_SUB = D_MODEL // LANES


def _split_bf16(v):
    hi = v.astype(BF16)
    lo = (v - hi.astype(F32)).astype(BF16)
    return hi, lo


def _pack_rows(v, out_ref):
    for s in range(PACK_SUB):
        out_ref[pl.ds(s, v.shape[0], stride=PACK_SUB), :] = v[:, s * LANES:(s + 1) * LANES]


def _unpack_rows(buf_ref, start, rows):
    return jnp.concatenate([buf_ref[pl.ds(start + s, rows, stride=PACK_SUB), :] for s in range(PACK_SUB)], axis=1)


def _merge_kernel(x_ref, fm_ref, yc_ref, us_ref, gates_ref, dskip_ref, wf_ref, wglu_ref, ws_ref, wo_ref,
                  ng_ref, wrh_ref, wrl_ref, rb_ref, x1_ref, hp_ref, logit_ref):
    conv = jnp.concatenate([yc_ref[j].astype(F32) for j in range(SSM_LANE_BLOCKS)], axis=-1)
    u = jnp.concatenate([us_ref[j].astype(F32) for j in range(SSM_LANE_BLOCKS)], axis=-1)
    y = conv + dskip_ref[...] * u
    y = 0.5 * y * (1.0 + jnp.tanh(GELU_C * (y + 0.044715 * (y * y * y))))
    glu = jax.nn.sigmoid(jnp.dot(y.astype(BF16), wglu_ref[...], preferred_element_type=F32))
    y_s = jnp.dot((y * glu).astype(BF16), ws_ref[...], preferred_element_type=F32)
    y_f = jnp.dot(fm_ref[...], wf_ref[...], preferred_element_type=F32)
    merged = (gates_ref[:, :D_MODEL].astype(F32) * y_f + gates_ref[:, D_MODEL:].astype(F32) * y_s)
    x1 = x_ref[...] + jnp.dot(merged.astype(BF16), wo_ref[...], preferred_element_type=F32)
    x1_ref[...] = x1
    inv = lax.rsqrt(jnp.mean(x1 * x1, axis=-1, keepdims=True) + RMS_EPS)
    hn = x1 * inv * ng_ref[...]
    _pack_rows(hn, hp_ref)
    hi, lo = _split_bf16(hn)
    logits = (jnp.dot(hi, wrh_ref[...], preferred_element_type=F32)
              + jnp.dot(lo, wrh_ref[...], preferred_element_type=F32)
              + jnp.dot(hi, wrl_ref[...], preferred_element_type=F32))
    logit_ref[...] = logits + rb_ref[...]


def _merge(x, fmix, yconv, us, gates, dskip, wf, wglu, ws, wo, ng, wrh, wrl, rb):
    tm = MERGE_TM
    full = lambda a: pl.BlockSpec(a.shape, lambda i: (0,) * a.ndim)
    return pl.pallas_call(
        _merge_kernel,
        grid=(TOKENS // tm,),
        in_specs=[
            pl.BlockSpec((tm, D_MODEL), lambda i: (i, 0)),
            pl.BlockSpec((tm, FOURIER_WIDTH), lambda i: (i, 0)),
            pl.BlockSpec((SSM_LANE_BLOCKS, tm, LANES), lambda i: (0, i, 0)),
            pl.BlockSpec((SSM_LANE_BLOCKS, tm, LANES), lambda i: (0, i, 0)),
            pl.BlockSpec((tm, 2 * D_MODEL), lambda i: (i, 0)),
            full(dskip), full(wf), full(wglu), full(ws), full(wo), full(ng), full(wrh), full(wrl), full(rb),
        ],
        out_specs=[
            pl.BlockSpec((tm, D_MODEL), lambda i: (i, 0)),
            pl.BlockSpec((tm * PACK_SUB, LANES), lambda i: (i, 0)),
            pl.BlockSpec((tm, ROUTER_COLS), lambda i: (i, 0)),
        ],
        out_shape=[
            jax.ShapeDtypeStruct((TOKENS, D_MODEL), F32),
            jax.ShapeDtypeStruct((TOKENS * PACK_SUB, LANES), F32),
            jax.ShapeDtypeStruct((TOKENS, ROUTER_COLS), F32),
        ],
        compiler_params=_cparams(("parallel",)),
        name="merge",
    )(x, fmix, yconv, us, gates, dskip, wf, wglu, ws, wo, ng, wrh, wrl, rb)


ROUTE_TM = 512
EXPERT_LANE0 = MOE_GROUPS
INFO_EXPERT, INFO_RANK, INFO_GATE = 0, 2, 4
INFO_FIELDS = 8


def _route_kernel(lg_ref, info_ref, fields_ref, cnt_ref, carry):
    @pl.when(pl.program_id(0) == 0)
    def _():
        carry[...] = jnp.zeros_like(carry)

    lg = lg_ref[...]
    tm = lg.shape[0]
    col_i = lax.broadcasted_iota(jnp.int32, lg.shape, 1)
    col = col_i.astype(F32)
    neg = jnp.float32(-jnp.inf)
    none = jnp.float32(ROUTER_COLS)

    def row_max(v):
        return jnp.max(v, axis=-1, keepdims=True)

    def first_at(v, m):
        return jnp.min(jnp.where(v == m, col, none), axis=-1, keepdims=True)

    gl = jnp.where(col_i < MOE_GROUPS, lg, neg)
    gmax = row_max(gl)
    p_g = 1.0 / jnp.sum(jnp.exp(gl - gmax), axis=-1, keepdims=True)
    lo = EXPERT_LANE0 + first_at(gl, gmax) * EXPERTS_PER_GROUP
    el = jnp.where((col >= lo) & (col < lo + EXPERTS_PER_GROUP), lg, neg)
    l1 = row_max(el)
    i1 = first_at(el, l1)
    el2 = jnp.where(col == i1, neg, el)
    l2 = row_max(el2)
    i2 = first_at(el2, l2)
    r = jnp.exp(l2 - l1)
    w1 = p_g / (1.0 + r)
    w2 = w1 * r

    hit1, hit2 = col == i1, col == i2
    onehot = jnp.where(hit1 | hit2, 1.0, 0.0)
    earlier = lax.broadcasted_iota(jnp.int32, (tm, tm), 0) > lax.broadcasted_iota(jnp.int32, (tm, tm), 1)
    before = jnp.dot(jnp.where(earlier, 1.0, 0.0).astype(BF16), onehot.astype(BF16),
                     preferred_element_type=F32) + carry[...]
    rank1 = jnp.sum(jnp.where(hit1, before, 0.0), axis=-1, keepdims=True)
    rank2 = jnp.sum(jnp.where(hit2, before, 0.0), axis=-1, keepdims=True)
    carry[...] += jnp.sum(onehot, axis=0, keepdims=True)
    cnt_ref[...] = carry[...]

    info = jnp.zeros(lg.shape, F32)
    for lane, v in ((INFO_EXPERT, i1 - EXPERT_LANE0), (INFO_EXPERT + 1, i2 - EXPERT_LANE0), (INFO_RANK, rank1),
                    (INFO_RANK + 1, rank2), (INFO_GATE, w1), (INFO_GATE + 1, w2)):
        info = jnp.where(col_i == lane, v, info)
    info_ref[...] = info
    fields_ref[...] = info.T[:INFO_FIELDS]


def _route(logits):
    return pl.pallas_call(
        _route_kernel,
        grid=(TOKENS // ROUTE_TM,),
        in_specs=[pl.BlockSpec((ROUTE_TM, ROUTER_COLS), lambda i: (i, 0))],
        out_specs=[pl.BlockSpec((ROUTE_TM, ROUTER_COLS), lambda i: (i, 0)),
                   pl.BlockSpec((INFO_FIELDS, ROUTE_TM), lambda i: (0, i)),
                   pl.BlockSpec((1, ROUTER_COLS), lambda i: (0, 0))],
        out_shape=[jax.ShapeDtypeStruct((TOKENS, ROUTER_COLS), F32),
                   jax.ShapeDtypeStruct((INFO_FIELDS, TOKENS), F32),
                   jax.ShapeDtypeStruct((1, ROUTER_COLS), F32)],
        scratch_shapes=[pltpu.VMEM((1, ROUTER_COLS), F32)],
        compiler_params=_cparams(("arbitrary",)),
        name="route",
    )(logits)


def _dispatch_plan(fields, counts):
    expert = fields[INFO_EXPERT:INFO_EXPERT + MOE_TOP_K].astype(jnp.int32)
    rank = fields[INFO_RANK:INFO_RANK + MOE_TOP_K].astype(jnp.int32)
    cnt = counts[0, EXPERT_LANE0:EXPERT_LANE0 + N_EXPERTS].astype(jnp.int32)
    padded = ((cnt + MOE_ROWS - 1) // MOE_ROWS) * MOE_ROWS
    pends = jnp.cumsum(padded)
    pstarts = pends - padded
    ids = jnp.arange(N_EXPERTS, dtype=jnp.int32)
    dest = rank + jnp.sum(jnp.where(expert[..., None] == ids, pstarts, 0), axis=-1)
    n_used = pends[-1] // MOE_ROWS
    blocks = jnp.arange(MOE_BLOCKS, dtype=jnp.int32)
    block_e = jnp.sum((pends[None, :] <= (blocks * MOE_ROWS)[:, None]).astype(jnp.int32), axis=1)
    block_e = jnp.minimum(block_e, N_EXPERTS - 1)
    last_e = jnp.sum(jnp.where(blocks == n_used - 1, block_e, 0))
    used = (cnt > 0).astype(jnp.int32)
    ordinal = jnp.cumsum(used) - used
    n_experts_used = jnp.sum(used)
    by_ordinal = jnp.sum(jnp.where((ordinal[None, :] == ids[:, None]) & (used[None, :] > 0), ids[None, :], 0), axis=1)
    block_ord = jnp.sum(jnp.where(block_e[:, None] == ids[None, :], ordinal[None, :], 0), axis=1)
    block_first = (blocks * MOE_ROWS == jnp.sum(jnp.where(block_e[:, None] == ids[None, :], pstarts[None, :], 0), axis=1))
    block_first = (block_first & (blocks < n_used)).astype(jnp.int32)
    meta = jnp.concatenate([n_used.reshape(1), n_experts_used.reshape(1)]).astype(jnp.int32)
    return dest.reshape(MOE_TOP_K * TOKENS), block_ord.astype(jnp.int32), block_first, by_ordinal.astype(jnp.int32), meta, pends


MOE_SLOT_ROWS = MOE_ROWS * PACK_SUB
INVERT_UNROLL = 8


WEIGHT_SLOTS = 3
GATHER_SLOTS = 3


def _moe_kernel(dest_ref, ord_ref, first_ref, eo_ref, meta_ref, pend_ref, hp_ref, wg_hbm, wu_hbm, wd_hbm, y_ref,
                xbuf, wg_buf, wu_buf, wd_buf, tok_ref, sem, wsem):
    i = pl.program_id(0)
    n_used = meta_ref[0]
    n_experts_used = meta_ref[1]

    def slot_rows(slot):
        return xbuf.at[pl.ds(pl.multiple_of(slot * MOE_SLOT_ROWS, MOE_SLOT_ROWS), MOE_SLOT_ROWS), :]

    def weight_copies(ordinal):
        e = eo_ref[ordinal]
        ws = ordinal % WEIGHT_SLOTS
        return [pltpu.make_async_copy(hbm.at[e], buf.at[ws], wsem.at[ws])
                for hbm, buf in ((wg_hbm, wg_buf), (wu_hbm, wu_buf), (wd_hbm, wd_buf))]

    def invert_dispatch():
        def fill_expert(e, c):
            first = jnp.maximum(pend_ref[e] - MOE_ROWS, 0)

            def fill(r, c2):
                tok_ref[first + r] = (first + r) & (TOKENS - 1)
                return c2

            lax.fori_loop(0, MOE_ROWS, fill, 0, unroll=INVERT_UNROLL)
            return c

        lax.fori_loop(0, N_EXPERTS, fill_expert, 0)

        def place(t, c):
            for k in range(MOE_TOP_K):
                tok_ref[dest_ref[k * TOKENS + t]] = t
            return c

        lax.fori_loop(0, TOKENS, place, 0, unroll=INVERT_UNROLL)

    def gather(block, slot):
        base = block * MOE_ROWS
        for r in range(MOE_ROWS):
            src = hp_ref.at[pl.ds(pl.multiple_of(tok_ref[base + r] * PACK_SUB, PACK_SUB), PACK_SUB), :]
            dst = xbuf.at[pl.ds(pl.multiple_of(slot * MOE_SLOT_ROWS + r * PACK_SUB, PACK_SUB), PACK_SUB), :]
            pltpu.make_async_copy(src, dst, sem.at[slot]).start()

    @pl.when(i == 0)
    def _():
        for ahead in range(WEIGHT_SLOTS - 1):
            @pl.when(ahead < n_experts_used)
            def _():
                for cp in weight_copies(ahead):
                    cp.start()
        invert_dispatch()
        for ahead in range(GATHER_SLOTS - 1):
            gather(jnp.minimum(ahead, n_used - 1), ahead)

    @pl.when(i < n_used)
    def _():
        ordinal = ord_ref[i]

        @pl.when(first_ref[i] == 1)
        def _():
            for cp in weight_copies(ordinal):
                cp.wait()

            @pl.when(ordinal + WEIGHT_SLOTS - 1 < n_experts_used)
            def _():
                for cp in weight_copies(ordinal + WEIGHT_SLOTS - 1):
                    cp.start()

        slot = i % GATHER_SLOTS
        ws = ordinal % WEIGHT_SLOTS
        pltpu.make_async_copy(slot_rows(slot), slot_rows(slot), sem.at[slot]).wait()
        ahead = i + GATHER_SLOTS - 1
        xb = _unpack_rows(xbuf, slot * MOE_SLOT_ROWS, MOE_ROWS).astype(BF16)
        gather(jnp.minimum(ahead, n_used - 1), ahead % GATHER_SLOTS)
        a = jnp.dot(xb, wg_buf[ws].astype(BF16), preferred_element_type=F32)
        u = jnp.dot(xb, wu_buf[ws].astype(BF16), preferred_element_type=F32)
        act = (a * jax.nn.sigmoid(a) * u).astype(BF16)
        y = jnp.dot(act, wd_buf[ws].astype(BF16), preferred_element_type=F32)
        _pack_rows(y, y_ref)

        @pl.when(i == n_used - 1)
        def _():
            for extra in range(1, GATHER_SLOTS):
                other = (i + extra) % GATHER_SLOTS
                pltpu.make_async_copy(slot_rows(other), slot_rows(other), sem.at[other]).wait()

    @pl.when(i >= n_used)
    def _():
        y_ref[...] = jnp.zeros_like(y_ref)


def _moe(dest, block_ord, block_first, by_ordinal, meta, pends, hp, w_gate, w_up, w_down):
    hbm = pl.BlockSpec(memory_space=pl.ANY)
    grid_spec = pltpu.PrefetchScalarGridSpec(
        num_scalar_prefetch=6,
        grid=(MOE_BLOCKS,),
        in_specs=[hbm, hbm, hbm, hbm],
        out_specs=pl.BlockSpec((MOE_SLOT_ROWS, LANES), lambda i, *_: (i, 0)),
        scratch_shapes=[pltpu.VMEM((GATHER_SLOTS * MOE_SLOT_ROWS, LANES), F32),
                        pltpu.VMEM((WEIGHT_SLOTS, D_MODEL, D_EXPERT), F32),
                        pltpu.VMEM((WEIGHT_SLOTS, D_MODEL, D_EXPERT), F32),
                        pltpu.VMEM((WEIGHT_SLOTS, D_EXPERT, D_MODEL), F32),
                        pltpu.SMEM((MOE_BLOCKS * MOE_ROWS,), jnp.int32),
                        pltpu.SemaphoreType.DMA((GATHER_SLOTS,)), pltpu.SemaphoreType.DMA((WEIGHT_SLOTS,))],
    )
    return pl.pallas_call(
        _moe_kernel,
        grid_spec=grid_spec,
        out_shape=jax.ShapeDtypeStruct((MOE_BLOCKS * MOE_SLOT_ROWS, LANES), F32),
        compiler_params=_cparams(("arbitrary",)),
        name="moe",
    )(dest, block_ord, block_first, by_ordinal, meta, pends, hp, w_gate, w_up, w_down)


COMBINE_TM = 256


def _combine_kernel(dest_ref, x1_ref, info_ref, g_ref, y_ref, o_ref, ybuf, sem):
    i = pl.program_id(0)
    last = pl.num_programs(0) - 1
    k_rows = COMBINE_TM * PACK_SUB
    slot_rows = MOE_TOP_K * k_rows

    def slot_ref(slot):
        return ybuf.at[pl.ds(pl.multiple_of(slot * slot_rows, slot_rows), slot_rows), :]

    def gather(tile, slot):
        for r in range(COMBINE_TM):
            for k in range(MOE_TOP_K):
                row = dest_ref[k * TOKENS + tile * COMBINE_TM + r]
                src = y_ref.at[pl.ds(pl.multiple_of(row * PACK_SUB, PACK_SUB), PACK_SUB), :]
                at = slot * slot_rows + k * k_rows + r * PACK_SUB
                dst = ybuf.at[pl.ds(pl.multiple_of(at, PACK_SUB), PACK_SUB), :]
                pltpu.make_async_copy(src, dst, sem.at[slot]).start(priority=k)

    @pl.when(i == 0)
    def _():
        for ahead in range(GATHER_SLOTS - 1):
            gather(ahead, ahead)

    slot = i % GATHER_SLOTS
    pltpu.make_async_copy(slot_ref(slot), slot_ref(slot), sem.at[slot]).wait()
    ahead = i + GATHER_SLOTS - 1
    gather(jnp.minimum(ahead, last), ahead % GATHER_SLOTS)
    x2 = x1_ref[...]
    for k in range(MOE_TOP_K):
        yk = _unpack_rows(ybuf, slot * slot_rows + k * k_rows, COMBINE_TM)
        x2 = x2 + info_ref[:, INFO_GATE + k:INFO_GATE + k + 1] * yk
    inv = lax.rsqrt(jnp.mean(x2 * x2, axis=-1, keepdims=True) + RMS_EPS)
    o_ref[...] = x2 * inv * g_ref[...]

    @pl.when(i == last)
    def _():
        for extra in range(1, GATHER_SLOTS):
            other = (i + extra) % GATHER_SLOTS
            pltpu.make_async_copy(slot_ref(other), slot_ref(other), sem.at[other]).wait()


def _combine(dest, x1, info, g, y_pad):
    tm = COMBINE_TM
    grid_spec = pltpu.PrefetchScalarGridSpec(
        num_scalar_prefetch=1,
        grid=(TOKENS // tm,),
        in_specs=[
            pl.BlockSpec((tm, D_MODEL), lambda i, d: (i, 0)),
            pl.BlockSpec((tm, ROUTER_COLS), lambda i, d: (i, 0)),
            pl.BlockSpec((1, D_MODEL), lambda i, d: (0, 0)),
            pl.BlockSpec(memory_space=pl.ANY),
        ],
        out_specs=pl.BlockSpec((tm, D_MODEL), lambda i, d: (i, 0)),
        scratch_shapes=[pltpu.VMEM((GATHER_SLOTS * MOE_TOP_K * tm * PACK_SUB, LANES), F32),
                        pltpu.SemaphoreType.DMA((GATHER_SLOTS,))],
    )
    return pl.pallas_call(
        _combine_kernel,
        grid_spec=grid_spec,
        out_shape=jax.ShapeDtypeStruct((TOKENS, D_MODEL), F32),
        compiler_params=_cparams(("arbitrary",)),
        name="combine",
    )(dest, x1, info, g, y_pad)


def kernel(x, mix_norm_g, w_in, w_fourier_out, ssm_A_re, ssm_A_im, ssm_log_dt, ssm_B_re, ssm_B_im, ssm_C_re,
           ssm_C_im, ssm_D, ssm_w_glu, w_ssm_out, w_out, ffn_norm_g, router_group_w, router_group_b,
           router_expert_w, router_expert_b, expert_w_gate, expert_w_up, expert_w_down, final_norm_g):
    assert x.shape == (BATCH, SEQ, D_MODEL) and w_in.shape[0] == 1
    tw, f2, cdft = _dft_constants()
    rq, rc = _replication_matrices()

    vf, us, xr, gates = _inproj(x, mix_norm_g[0][None], w_in[0].astype(BF16), cdft)
    fmix = _dft(vf, tw, f2)

    gen, dtoe, aq = _ssm_factors(ssm_A_re[0], ssm_A_im[0], ssm_log_dt[0], ssm_B_re[0], ssm_B_im[0],
                                 ssm_C_re[0], ssm_C_im[0])
    sre, sim = _ssm_states(xr, gen, rq, rc)
    hre, him = _ssm_scan(sre, sim, aq)
    yconv = _ssm_out(xr, gen, dtoe, rq, rc, hre, him)

    w_router = jnp.concatenate([router_group_w[0], router_expert_w[0]], axis=1)
    w_router = jnp.pad(w_router, ((0, 0), (0, ROUTER_COLS - w_router.shape[1])))
    b_router = jnp.concatenate([router_group_b[0], router_expert_b[0]])
    b_router = jnp.pad(b_router, (0, ROUTER_COLS - b_router.shape[0]))[None]
    wr_hi = w_router.astype(BF16)
    wr_lo = (w_router - wr_hi.astype(F32)).astype(BF16)
    x1, hp, logits = _merge(x.reshape(TOKENS, D_MODEL), fmix, yconv, us, gates, ssm_D[0][None],
                            w_fourier_out[0].astype(BF16), ssm_w_glu[0].astype(BF16), w_ssm_out[0].astype(BF16),
                            w_out[0].astype(BF16), ffn_norm_g[0][None], wr_hi, wr_lo, b_router)

    info, fields, counts = _route(logits)
    dest, block_ord, block_first, by_ordinal, meta, pends = _dispatch_plan(fields, counts)
    y_pad = _moe(dest, block_ord, block_first, by_ordinal, meta, pends, hp, expert_w_gate[0], expert_w_up[0],
                 expert_w_down[0])
    out = _combine(dest, x1, info, final_norm_g[None], y_pad)
    return out.reshape(BATCH, SEQ, D_MODEL)
```

```python
import math

import numpy as np
import jax
import jax.numpy as jnp
from jax import lax
from jax.experimental import pallas as pl
from jax.experimental.pallas import tpu as pltpu

F32 = jnp.float32
BF16 = jnp.bfloat16

D_MODEL = 1024
BATCH = 4
SEQ = 4096
TOKENS = BATCH * SEQ
FOURIER_WIDTH = 512
FOURIER_GROUP_CH = 128
FOURIER_GROUPS = 4
SSM_WIDTH = 512
SSM_GROUP_CH = 16
SSM_GROUPS = 32
SSM_STATE = 64
MOE_GROUPS = 8
EXPERTS_PER_GROUP = 8
N_EXPERTS = 64
MOE_TOP_K = 2
D_EXPERT = 512
RMS_EPS = 1e-6

LANES = 128
SSM_CHUNK = 16
SSM_LANE_BLOCKS = SSM_WIDTH // LANES
GROUPS_PER_BLOCK = LANES // SSM_GROUP_CH
CHUNK_COLS = SSM_CHUNK * LANES
N_CHUNKS = SEQ // SSM_CHUNK
STATE_COLS = GROUPS_PER_BLOCK * SSM_STATE
MOE_ROWS = 256
MOE_BLOCKS = TOKENS * MOE_TOP_K // MOE_ROWS + N_EXPERTS
ROUTER_COLS = 128
VMEM_LIMIT = 48 * 1024 * 1024


def _cparams(sem, vmem=VMEM_LIMIT):
    return pltpu.CompilerParams(dimension_semantics=sem, vmem_limit_bytes=vmem)


IN_TM = 512


def _inproj_kernel(x_ref, g_ref, w_ref, cdft_ref, vf_ref, us_ref, xr_ref, gates_ref, zs_scr):
    x = x_ref[0]
    inv = lax.rsqrt(jnp.mean(x * x, axis=-1, keepdims=True) + RMS_EPS)
    h = (x * inv * g_ref[...]).astype(BF16)
    zf = jnp.dot(h, w_ref[:, 0:FOURIER_WIDTH], preferred_element_type=F32).astype(BF16)
    cdft = cdft_ref[...].astype(BF16)
    for g in range(FOURIER_GROUPS):
        sl = slice(g * LANES, (g + 1) * LANES)
        v = jnp.dot(zf[:, sl], cdft, preferred_element_type=F32)
        vf_ref[0, 0, :, sl] = v[:, :LANES].astype(BF16)
        vf_ref[0, 1, :, sl] = v[:, LANES:].astype(BF16)
    zs = jnp.dot(h, w_ref[:, FOURIER_WIDTH:FOURIER_WIDTH + SSM_WIDTH], preferred_element_type=F32)
    for j in range(SSM_LANE_BLOCKS):
        us_ref[j] = zs[:, j * LANES:(j + 1) * LANES].astype(BF16)
        zs_scr[j] = zs[:, j * LANES:(j + 1) * LANES]
    for j in range(SSM_LANE_BLOCKS):
        for q in range(SSM_CHUNK):
            piece = zs_scr[j, pl.ds(q, IN_TM // SSM_CHUNK, stride=SSM_CHUNK), :]
            xr_ref[j, :, q * LANES:(q + 1) * LANES] = piece.astype(BF16)
    base = FOURIER_WIDTH + SSM_WIDTH
    for n in range(4):
        zg = jnp.dot(h, w_ref[:, base + n * 512: base + (n + 1) * 512], preferred_element_type=F32)
        gates_ref[:, n * 512:(n + 1) * 512] = jax.nn.sigmoid(zg).astype(BF16)


def _inproj(x, g, w_in, cdft):
    nt = SEQ // IN_TM
    return pl.pallas_call(
        _inproj_kernel,
        grid=(BATCH, nt),
        in_specs=[
            pl.BlockSpec((1, IN_TM, D_MODEL), lambda b, i: (b, i, 0)),
            pl.BlockSpec((1, D_MODEL), lambda b, i: (0, 0)),
            pl.BlockSpec(w_in.shape, lambda b, i: (0, 0)),
            pl.BlockSpec(cdft.shape, lambda b, i: (0, 0)),
        ],
        out_specs=[
            pl.BlockSpec((1, 2, IN_TM, FOURIER_WIDTH), lambda b, i: (b, 0, i, 0)),
            pl.BlockSpec((SSM_LANE_BLOCKS, IN_TM, LANES), lambda b, i: (0, b * nt + i, 0)),
            pl.BlockSpec((SSM_LANE_BLOCKS, IN_TM // SSM_CHUNK, CHUNK_COLS), lambda b, i: (0, b * nt + i, 0)),
            pl.BlockSpec((IN_TM, 2 * D_MODEL), lambda b, i: (b * nt + i, 0)),
        ],
        out_shape=[
            jax.ShapeDtypeStruct((BATCH, 2, SEQ, FOURIER_WIDTH), BF16),
            jax.ShapeDtypeStruct((SSM_LANE_BLOCKS, TOKENS, LANES), BF16),
            jax.ShapeDtypeStruct((SSM_LANE_BLOCKS, TOKENS // SSM_CHUNK, CHUNK_COLS), BF16),
            jax.ShapeDtypeStruct((TOKENS, 2 * D_MODEL), BF16),
        ],
        scratch_shapes=[pltpu.VMEM((SSM_LANE_BLOCKS, IN_TM, LANES), F32)],
        compiler_params=_cparams(("parallel", "parallel")),
        name="inproj",
    )(x, g, w_in, cdft)


DFT_R1 = 8
DFT_R2 = SEQ // DFT_R1
DFT_LANES = 2 * LANES
DFT_ROWS = 16


def _cmul_const(z, w):
    re, im = z
    if abs(w.imag) < 1e-12:
        return (re, im) if abs(w.real - 1.0) < 1e-12 else (re * w.real, im * w.real)
    if abs(w.real) < 1e-12:
        return (im, -re) if abs(w.imag + 1.0) < 1e-12 else (-im * w.imag, re * w.imag)
    return re * w.real - im * w.imag, re * w.imag + im * w.real


def _fft_blocks(xs):
    n = len(xs)
    if n == 1:
        return xs
    even, odd = _fft_blocks(xs[0::2]), _fft_blocks(xs[1::2])
    out = [None] * n
    for k in range(n // 2):
        tr, ti = _cmul_const(odd[k], np.exp(-2j * np.pi * k / n))
        out[k] = (even[k][0] + tr, even[k][1] + ti)
        out[k + n // 2] = (even[k][0] - tr, even[k][1] - ti)
    return out


def _dft_kernel(v_ref, tw_ref, f_ref, o_ref, a_scr, o_scr):
    def tile(i, c):
        r0 = pl.multiple_of(i * DFT_ROWS, DFT_ROWS)
        for slab in range(DFT_LANES // LANES):
            lanes = slice(slab * LANES, (slab + 1) * LANES)
            xs = [(v_ref[0, 0, pl.ds(s1 * DFT_R2 + r0, DFT_ROWS), lanes].astype(F32),
                   v_ref[0, 1, pl.ds(s1 * DFT_R2 + r0, DFT_ROWS), lanes].astype(F32)) for s1 in range(DFT_R1)]
            for t1, (ar, ai) in enumerate(_fft_blocks(xs)):
                tr, ti = tw_ref[0, t1, pl.ds(r0, DFT_ROWS), :], tw_ref[1, t1, pl.ds(r0, DFT_ROWS), :]
                a_scr[t1, pl.ds(r0, DFT_ROWS), lanes] = (ar * tr - ai * ti).astype(BF16)
                a_scr[t1, pl.ds(DFT_R2 + r0, DFT_ROWS), lanes] = (ar * ti + ai * tr).astype(BF16)
        return c

    lax.fori_loop(0, DFT_R2 // DFT_ROWS, tile, 0)

    f2 = f_ref[...].astype(BF16)
    for t1 in range(DFT_R1):
        r = jnp.dot(f2, a_scr[t1], preferred_element_type=F32)
        for slab in range(DFT_LANES // LANES):
            o_scr[slab, pl.ds(t1, DFT_R2, stride=DFT_R1), :] = r[:, slab * LANES:(slab + 1) * LANES]
    for slab in range(DFT_LANES // LANES):
        o_ref[:, slab * LANES:(slab + 1) * LANES] = o_scr[slab].astype(BF16)


def _dft(v, tw, f2):
    nh = FOURIER_WIDTH // DFT_LANES
    return pl.pallas_call(
        _dft_kernel,
        grid=(BATCH, nh),
        in_specs=[
            pl.BlockSpec((1, 2, SEQ, DFT_LANES), lambda b, h: (b, 0, 0, h)),
            pl.BlockSpec(tw.shape, lambda b, h: (0, 0, 0, 0)),
            pl.BlockSpec(f2.shape, lambda b, h: (0, 0)),
        ],
        out_specs=pl.BlockSpec((SEQ, DFT_LANES), lambda b, h: (b, h)),
        out_shape=jax.ShapeDtypeStruct((TOKENS, FOURIER_WIDTH), BF16),
        scratch_shapes=[pltpu.VMEM((DFT_R1, 2 * DFT_R2, DFT_LANES), BF16),
                        pltpu.VMEM((DFT_LANES // LANES, SEQ, LANES), F32)],
        compiler_params=_cparams(("parallel", "parallel")),
        name="dft",
    )(v, tw, f2)


def _dft_constants():
    t1 = np.arange(DFT_R1)
    s2 = np.arange(DFT_R2)
    ang = 2.0 * np.pi * np.outer(t1, s2) / SEQ
    scale = 1.0 / math.sqrt(SEQ)
    tw = np.stack([np.cos(ang) * scale, -np.sin(ang) * scale])
    tw = np.repeat(tw[..., None], LANES, axis=-1)
    ang2 = 2.0 * np.pi * np.outer(s2, s2) / DFT_R2
    f2 = np.concatenate([np.cos(ang2), np.sin(ang2)], axis=1)
    kc = np.arange(FOURIER_GROUP_CH)
    angc = 2.0 * np.pi * np.outer(kc, kc) / FOURIER_GROUP_CH
    cs = 1.0 / math.sqrt(FOURIER_GROUP_CH)
    cdft = np.concatenate([np.cos(angc) * cs, -np.sin(angc) * cs], axis=1)
    return tuple(jnp.asarray(v, F32) for v in (tw, f2, cdft))


GEN_KINDS = 8
TOEPLITZ_LAGS = 2 * SSM_CHUNK - 1


def _ssm_factors(a_re, a_im, log_dt, b_re, b_im, c_re, c_im):
    q_len = SSM_CHUNK
    jb, gb, ch, ns = SSM_LANE_BLOCKS, GROUPS_PER_BLOCK, SSM_GROUP_CH, SSM_STATE
    dt = jnp.exp(log_dt)[..., None]
    lr, li = a_re * dt, a_im * dt
    steps = jnp.arange(q_len + 1, dtype=F32)
    mag = jnp.exp(lr[..., None] * steps)
    ang = li[..., None] * steps
    pr, pi = mag * jnp.cos(ang), mag * jnp.sin(ang)
    ar, ai = pr[..., 1], pi[..., 1]
    den = a_re * a_re + a_im * a_im
    cr = ((ar - 1.0) * a_re + ai * a_im) / den
    ci = (ai * a_re - (ar - 1.0) * a_im) / den
    bbr = cr[..., None] * b_re - ci[..., None] * b_im
    bbi = cr[..., None] * b_im + ci[..., None] * b_re

    def per_dir(fwd, bwd):
        return jnp.stack([fwd, bwd])

    pws = [per_dir(p[0][..., q_len - 1::-1][..., :q_len], p[1][..., :q_len]) for p in (pr, pi)]
    pwo = [per_dir(p[0][..., 1:], p[1][..., q_len:0:-1]) for p in (pr, pi)]
    ct = [jnp.swapaxes(c, -1, -2) for c in (c_re, c_im)]
    gen = jnp.stack([pws[0], pws[1], bbr, bbi, pwo[0], pwo[1], ct[0], ct[1]])
    gen = gen.reshape(GEN_KINDS, 2, jb, gb * ns, q_len)

    prq, piq = pr[..., :q_len], pi[..., :q_len]
    cpr = jnp.einsum('dgcn,dgnt->dgtcn', c_re, prq) - jnp.einsum('dgcn,dgnt->dgtcn', c_im, piq)
    cpi = jnp.einsum('dgcn,dgnt->dgtcn', c_re, piq) + jnp.einsum('dgcn,dgnt->dgtcn', c_im, prq)
    kern = jnp.einsum('dgtcn,dgne->dgtec', cpr, bbr) - jnp.einsum('dgtcn,dgne->dgtec', cpi, bbi)
    lags = jnp.concatenate([kern[1][:, :0:-1], kern[0][:, :1] + kern[1][:, :1], kern[0][:, 1:]], axis=1)
    lags = lags.reshape(jb, gb, TOEPLITZ_LAGS, ch, ch)
    dtoe = jnp.transpose(lags, (0, 2, 1, 3, 4)).reshape(jb, TOEPLITZ_LAGS * LANES, ch)

    aq = jnp.stack([pr[0][..., q_len], pi[0][..., q_len], pr[1][..., q_len], pi[1][..., q_len]])
    aq = aq.reshape(4, jb, 1, STATE_COLS)
    aq = jnp.broadcast_to(aq, (4, jb, BATCH, STATE_COLS))
    aq = jnp.transpose(aq, (1, 0, 2, 3)).reshape(jb, 4, BATCH * STATE_COLS)
    return gen, dtoe, aq


GEN_ROWS = 128
GROUP_SHIFT_CH = 4
GROUP_SHIFT_STATE = 6


def _replication_matrices():
    rq = np.zeros((SSM_CHUNK, SSM_CHUNK, GROUPS_PER_BLOCK, SSM_GROUP_CH), np.float32)
    rc = np.zeros((SSM_GROUP_CH, SSM_CHUNK, GROUPS_PER_BLOCK, SSM_GROUP_CH), np.float32)
    for i in range(SSM_CHUNK):
        rq[i, i] = 1.0
        rc[i, :, :, i] = 1.0
    return jnp.asarray(rq.reshape(SSM_CHUNK, CHUNK_COLS), BF16), jnp.asarray(rc.reshape(SSM_GROUP_CH, CHUNK_COLS), BF16)


def _state_operator(factor, kind0, rq_ref, rc_ref, out_ref, row0, negate_im):
    col_g = (lax.broadcasted_iota(jnp.int32, (GEN_ROWS, CHUNK_COLS), 1) >> GROUP_SHIFT_CH) & (GROUPS_PER_BLOCK - 1)
    for r in range(STATE_COLS // GEN_ROWS):
        rows = slice(r * GEN_ROWS, (r + 1) * GEN_ROWS)
        pr = jnp.dot(factor(kind0, rows).astype(BF16), rq_ref[...], preferred_element_type=F32)
        pi = jnp.dot(factor(kind0 + 1, rows).astype(BF16), rq_ref[...], preferred_element_type=F32)
        fr = jnp.dot(factor(kind0 + 2, rows).astype(BF16), rc_ref[...], preferred_element_type=F32)
        fi = jnp.dot(factor(kind0 + 3, rows).astype(BF16), rc_ref[...], preferred_element_type=F32)
        row_g = ((lax.broadcasted_iota(jnp.int32, (GEN_ROWS, CHUNK_COLS), 0) + r * GEN_ROWS) >> GROUP_SHIFT_STATE)
        keep = (row_g & (GROUPS_PER_BLOCK - 1)) == col_g
        w_re = jnp.where(keep, pr * fr - pi * fi, 0.0)
        w_im = jnp.where(keep, pr * fi + pi * fr, 0.0)
        out_ref[row0 + r * GEN_ROWS:row0 + (r + 1) * GEN_ROWS, :] = w_re.astype(BF16)
        im_rows = slice(row0 + STATE_COLS + r * GEN_ROWS, row0 + STATE_COLS + (r + 1) * GEN_ROWS)
        out_ref[im_rows, :] = (-w_im if negate_im else w_im).astype(BF16)


def _ssm_state_kernel(x_ref, gen_ref, rq_ref, rc_ref, sre_ref, sim_ref, wst_scr):
    _state_operator(lambda kind, rows: gen_ref[kind, 0, 0, rows, :], 0, rq_ref, rc_ref, wst_scr, 0, False)
    s = lax.dot_general(x_ref[0], wst_scr[...], (((1,), (1,)), ((), ())), preferred_element_type=F32)
    for b in range(BATCH):
        rows = slice(b * N_CHUNKS, (b + 1) * N_CHUNKS)
        sre_ref[0, :, b * STATE_COLS:(b + 1) * STATE_COLS] = s[rows, :STATE_COLS]
        sim_ref[0, :, b * STATE_COLS:(b + 1) * STATE_COLS] = s[rows, STATE_COLS:]


def _ssm_states(xr, gen, rq, rc):
    out_spec = pl.BlockSpec((1, N_CHUNKS, BATCH * STATE_COLS), lambda j, d: (j, 0, d))
    shape = jax.ShapeDtypeStruct((SSM_LANE_BLOCKS, N_CHUNKS, 2 * BATCH * STATE_COLS), F32)
    return pl.pallas_call(
        _ssm_state_kernel,
        grid=(SSM_LANE_BLOCKS, 2),
        in_specs=[
            pl.BlockSpec((1, BATCH * N_CHUNKS, CHUNK_COLS), lambda j, d: (j, 0, 0)),
            pl.BlockSpec((GEN_KINDS, 1, 1, STATE_COLS, SSM_CHUNK), lambda j, d: (0, d, j, 0, 0)),
            pl.BlockSpec(rq.shape, lambda j, d: (0, 0)),
            pl.BlockSpec(rc.shape, lambda j, d: (0, 0)),
        ],
        out_specs=[out_spec, out_spec],
        out_shape=[shape, shape],
        scratch_shapes=[pltpu.VMEM((2 * STATE_COLS, CHUNK_COLS), BF16)],
        compiler_params=_cparams(("parallel", "parallel")),
        name="ssm_states",
    )(xr, gen, rq, rc)


def _ssm_scan_kernel(sre_ref, sim_ref, aq_ref, hre_ref, him_ref):
    half = BATCH * STATE_COLS
    fwd, bwd = slice(0, half), slice(half, 2 * half)
    a0r, a0i, a1r, a1i = aq_ref[0, 0:1], aq_ref[0, 1:2], aq_ref[0, 2:3], aq_ref[0, 3:4]

    def body(k, carry):
        fr, fi, br, bi = carry
        kb = N_CHUNKS - 1 - k
        hre_ref[0, pl.ds(k, 1), fwd] = fr
        him_ref[0, pl.ds(k, 1), fwd] = fi
        hre_ref[0, pl.ds(kb, 1), bwd] = br
        him_ref[0, pl.ds(kb, 1), bwd] = bi
        sfr, sfi = sre_ref[0, pl.ds(k, 1), fwd], sim_ref[0, pl.ds(k, 1), fwd]
        sbr, sbi = sre_ref[0, pl.ds(kb, 1), bwd], sim_ref[0, pl.ds(kb, 1), bwd]
        return (a0r * fr - a0i * fi + sfr, a0r * fi + a0i * fr + sfi,
                a1r * br - a1i * bi + sbr, a1r * bi + a1i * br + sbi)

    z = jnp.zeros((1, half), F32)
    lax.fori_loop(0, N_CHUNKS, body, (z, z, z, z))


def _ssm_scan(sre, sim, aq):
    spec = pl.BlockSpec((1, N_CHUNKS, 2 * BATCH * STATE_COLS), lambda j: (j, 0, 0))
    shape = jax.ShapeDtypeStruct(sre.shape, F32)
    return pl.pallas_call(
        _ssm_scan_kernel,
        grid=(SSM_LANE_BLOCKS,),
        in_specs=[spec, spec, pl.BlockSpec((1, 4, BATCH * STATE_COLS), lambda j: (j, 0, 0))],
        out_specs=[spec, spec],
        out_shape=[shape, shape],
        compiler_params=_cparams(("parallel",)),
        name="ssm_scan",
    )(sre, sim, aq)


def _ssm_out_kernel(x_ref, gen_ref, dtoe_ref, rq_ref, rc_ref, h0r_ref, h0i_ref, h1r_ref, h1i_ref, y_ref,
                    m_scr, wo_scr, lag_scr, y_scr):
    @pl.when(pl.program_id(1) == 0)
    def _():
        for d in range(2):
            _state_operator(lambda kind, rows, d=d: gen_ref[kind, d, 0, rows, :], 4, rq_ref, rc_ref, wo_scr,
                            d * 2 * STATE_COLS, True)
        shape = (TOEPLITZ_LAGS * LANES, LANES)
        t = jnp.dot(dtoe_ref[0].astype(BF16), rc_ref[:, :LANES], preferred_element_type=F32)
        row_g = (lax.broadcasted_iota(jnp.int32, shape, 0) >> GROUP_SHIFT_CH) & (GROUPS_PER_BLOCK - 1)
        col_g = (lax.broadcasted_iota(jnp.int32, shape, 1) >> GROUP_SHIFT_CH) & (GROUPS_PER_BLOCK - 1)
        lag_scr[...] = jnp.where(row_g == col_g, t, 0.0).astype(BF16)
        for qi in range(SSM_CHUNK):
            for qo in range(SSM_CHUNK):
                lag = qo - qi + SSM_CHUNK - 1
                m_scr[qi * LANES:(qi + 1) * LANES, qo * LANES:(qo + 1) * LANES] = lag_scr[lag * LANES:(lag + 1) * LANES, :]

    acc = jnp.dot(x_ref[0], m_scr[...], preferred_element_type=F32)
    for i, h_ref in enumerate((h0r_ref, h0i_ref, h1r_ref, h1i_ref)):
        h = jnp.concatenate([h_ref[0, :, b * STATE_COLS:(b + 1) * STATE_COLS] for b in range(SSM_OUT_BATCH)], axis=0)
        acc += jnp.dot(h.astype(BF16), wo_scr[i * STATE_COLS:(i + 1) * STATE_COLS, :], preferred_element_type=F32)
    for b in range(SSM_OUT_BATCH):
        for q in range(SSM_CHUNK):
            y_scr[pl.ds(q, N_CHUNKS, stride=SSM_CHUNK), :] = acc[b * N_CHUNKS:(b + 1) * N_CHUNKS, q * LANES:(q + 1) * LANES]
        y_ref[0, b * SEQ:(b + 1) * SEQ, :] = y_scr[...].astype(BF16)


SSM_OUT_BATCH = 1


def _ssm_out(xr, gen, dtoe, rq, rc, hre, him):
    nb = BATCH // SSM_OUT_BATCH

    def hspec(d):
        return pl.BlockSpec((1, N_CHUNKS, SSM_OUT_BATCH * STATE_COLS), lambda j, b: (j, 0, d * nb + b))
    return pl.pallas_call(
        _ssm_out_kernel,
        grid=(SSM_LANE_BLOCKS, nb),
        in_specs=[
            pl.BlockSpec((1, SSM_OUT_BATCH * N_CHUNKS, CHUNK_COLS), lambda j, b: (j, b, 0)),
            pl.BlockSpec((GEN_KINDS, 2, 1, STATE_COLS, SSM_CHUNK), lambda j, b: (0, 0, j, 0, 0)),
            pl.BlockSpec((1, TOEPLITZ_LAGS * LANES, SSM_GROUP_CH), lambda j, b: (j, 0, 0)),
            pl.BlockSpec(rq.shape, lambda j, b: (0, 0)),
            pl.BlockSpec(rc.shape, lambda j, b: (0, 0)),
            hspec(0), hspec(0), hspec(1), hspec(1),
        ],
        out_specs=pl.BlockSpec((1, SSM_OUT_BATCH * SEQ, LANES), lambda j, b: (j, b, 0)),
        out_shape=jax.ShapeDtypeStruct((SSM_LANE_BLOCKS, TOKENS, LANES), BF16),
        scratch_shapes=[pltpu.VMEM((CHUNK_COLS, CHUNK_COLS), BF16), pltpu.VMEM((4 * STATE_COLS, CHUNK_COLS), BF16),
                        pltpu.VMEM((TOEPLITZ_LAGS * LANES, LANES), BF16), pltpu.VMEM((SEQ, LANES), F32)],
        compiler_params=_cparams(("parallel", "arbitrary")),
        name="ssm_out",
    )(xr, gen, dtoe, rq, rc, hre, him, hre, him)


MERGE_TM = 512
GELU_C = math.sqrt(2.0 / math.pi)
PACK_SUB = D_MODEL // LANES


def _split_bf16(v):
    hi = v.astype(BF16)
    lo = (v - hi.astype(F32)).astype(BF16)
    return hi, lo


def _pack_rows(v, out_ref):
    for s in range(PACK_SUB):
        out_ref[pl.ds(s, v.shape[0], stride=PACK_SUB), :] = v[:, s * LANES:(s + 1) * LANES]


def _unpack_rows(buf_ref, start, rows):
    return jnp.concatenate([buf_ref[pl.ds(start + s, rows, stride=PACK_SUB), :] for s in range(PACK_SUB)], axis=1)


def _merge_kernel(x_ref, fm_ref, yc_ref, us_ref, gates_ref, dskip_ref, wf_ref, wglu_ref, ws_ref, wo_ref,
                  ng_ref, wrh_ref, wrl_ref, rb_ref, x1_ref, hp_ref, logit_ref):
    conv = jnp.concatenate([yc_ref[j].astype(F32) for j in range(SSM_LANE_BLOCKS)], axis=-1)
    u = jnp.concatenate([us_ref[j].astype(F32) for j in range(SSM_LANE_BLOCKS)], axis=-1)
    y = conv + dskip_ref[...] * u
    y = 0.5 * y * (1.0 + jnp.tanh(GELU_C * (y + 0.044715 * (y * y * y))))
    glu = jax.nn.sigmoid(jnp.dot(y.astype(BF16), wglu_ref[...], preferred_element_type=F32))
    y_s = jnp.dot((y * glu).astype(BF16), ws_ref[...], preferred_element_type=F32)
    y_f = jnp.dot(fm_ref[...], wf_ref[...], preferred_element_type=F32)
    merged = (gates_ref[:, :D_MODEL].astype(F32) * y_f + gates_ref[:, D_MODEL:].astype(F32) * y_s)
    x1 = x_ref[...] + jnp.dot(merged.astype(BF16), wo_ref[...], preferred_element_type=F32)
    x1_ref[...] = x1
    inv = lax.rsqrt(jnp.mean(x1 * x1, axis=-1, keepdims=True) + RMS_EPS)
    hn = x1 * inv * ng_ref[...]
    _pack_rows(hn, hp_ref)
    hi, lo = _split_bf16(hn)
    logits = (jnp.dot(hi, wrh_ref[...], preferred_element_type=F32)
              + jnp.dot(lo, wrh_ref[...], preferred_element_type=F32)
              + jnp.dot(hi, wrl_ref[...], preferred_element_type=F32))
    logit_ref[...] = logits + rb_ref[...]


def _merge(x, fmix, yconv, us, gates, dskip, wf, wglu, ws, wo, ng, wrh, wrl, rb):
    tm = MERGE_TM
    full = lambda a: pl.BlockSpec(a.shape, lambda i: (0,) * a.ndim)
    return pl.pallas_call(
        _merge_kernel,
        grid=(TOKENS // tm,),
        in_specs=[
            pl.BlockSpec((tm, D_MODEL), lambda i: (i, 0)),
            pl.BlockSpec((tm, FOURIER_WIDTH), lambda i: (i, 0)),
            pl.BlockSpec((SSM_LANE_BLOCKS, tm, LANES), lambda i: (0, i, 0)),
            pl.BlockSpec((SSM_LANE_BLOCKS, tm, LANES), lambda i: (0, i, 0)),
            pl.BlockSpec((tm, 2 * D_MODEL), lambda i: (i, 0)),
            full(dskip), full(wf), full(wglu), full(ws), full(wo), full(ng), full(wrh), full(wrl), full(rb),
        ],
        out_specs=[
            pl.BlockSpec((tm, D_MODEL), lambda i: (i, 0)),
            pl.BlockSpec((tm * PACK_SUB, LANES), lambda i: (i, 0)),
            pl.BlockSpec((tm, ROUTER_COLS), lambda i: (i, 0)),
        ],
        out_shape=[
            jax.ShapeDtypeStruct((TOKENS, D_MODEL), F32),
            jax.ShapeDtypeStruct((TOKENS * PACK_SUB, LANES), F32),
            jax.ShapeDtypeStruct((TOKENS, ROUTER_COLS), F32),
        ],
        compiler_params=_cparams(("parallel",)),
        name="merge",
    )(x, fmix, yconv, us, gates, dskip, wf, wglu, ws, wo, ng, wrh, wrl, rb)


ROUTE_TM = 512
EXPERT_LANE0 = MOE_GROUPS
INFO_EXPERT, INFO_RANK, INFO_GATE = 0, 2, 4
INFO_FIELDS = 8


def _route_kernel(lg_ref, info_ref, fields_ref, cnt_ref, carry):
    @pl.when(pl.program_id(0) == 0)
    def _():
        carry[...] = jnp.zeros_like(carry)

    lg = lg_ref[...]
    tm = lg.shape[0]
    col_i = lax.broadcasted_iota(jnp.int32, lg.shape, 1)
    col = col_i.astype(F32)
    neg = jnp.float32(-jnp.inf)
    none = jnp.float32(ROUTER_COLS)

    def row_max(v):
        return jnp.max(v, axis=-1, keepdims=True)

    def first_at(v, m):
        return jnp.min(jnp.where(v == m, col, none), axis=-1, keepdims=True)

    gl = jnp.where(col_i < MOE_GROUPS, lg, neg)
    gmax = row_max(gl)
    p_g = 1.0 / jnp.sum(jnp.exp(gl - gmax), axis=-1, keepdims=True)
    lo = EXPERT_LANE0 + first_at(gl, gmax) * EXPERTS_PER_GROUP
    el = jnp.where((col >= lo) & (col < lo + EXPERTS_PER_GROUP), lg, neg)
    l1 = row_max(el)
    i1 = first_at(el, l1)
    el2 = jnp.where(col == i1, neg, el)
    l2 = row_max(el2)
    i2 = first_at(el2, l2)
    r = jnp.exp(l2 - l1)
    w1 = p_g / (1.0 + r)
    w2 = w1 * r

    hit1, hit2 = col == i1, col == i2
    onehot = jnp.where(hit1 | hit2, 1.0, 0.0)
    earlier = lax.broadcasted_iota(jnp.int32, (tm, tm), 0) > lax.broadcasted_iota(jnp.int32, (tm, tm), 1)
    before = jnp.dot(jnp.where(earlier, 1.0, 0.0).astype(BF16), onehot.astype(BF16),
                     preferred_element_type=F32) + carry[...]
    rank1 = jnp.sum(jnp.where(hit1, before, 0.0), axis=-1, keepdims=True)
    rank2 = jnp.sum(jnp.where(hit2, before, 0.0), axis=-1, keepdims=True)
    carry[...] += jnp.sum(onehot, axis=0, keepdims=True)
    cnt_ref[...] = carry[...]

    info = jnp.zeros(lg.shape, F32)
    for lane, v in ((INFO_EXPERT, i1 - EXPERT_LANE0), (INFO_EXPERT + 1, i2 - EXPERT_LANE0), (INFO_RANK, rank1),
                    (INFO_RANK + 1, rank2), (INFO_GATE, w1), (INFO_GATE + 1, w2)):
        info = jnp.where(col_i == lane, v, info)
    info_ref[...] = info
    fields_ref[...] = info.T[:INFO_FIELDS]


def _route(logits):
    return pl.pallas_call(
        _route_kernel,
        grid=(TOKENS // ROUTE_TM,),
        in_specs=[pl.BlockSpec((ROUTE_TM, ROUTER_COLS), lambda i: (i, 0))],
        out_specs=[pl.BlockSpec((ROUTE_TM, ROUTER_COLS), lambda i: (i, 0)),
                   pl.BlockSpec((INFO_FIELDS, ROUTE_TM), lambda i: (0, i)),
                   pl.BlockSpec((1, ROUTER_COLS), lambda i: (0, 0))],
        out_shape=[jax.ShapeDtypeStruct((TOKENS, ROUTER_COLS), F32),
                   jax.ShapeDtypeStruct((INFO_FIELDS, TOKENS), F32),
                   jax.ShapeDtypeStruct((1, ROUTER_COLS), F32)],
        scratch_shapes=[pltpu.VMEM((1, ROUTER_COLS), F32)],
        compiler_params=_cparams(("arbitrary",)),
        name="route",
    )(logits)


def _dispatch_plan(fields, counts):
    expert = fields[INFO_EXPERT:INFO_EXPERT + MOE_TOP_K].astype(jnp.int32)
    rank = fields[INFO_RANK:INFO_RANK + MOE_TOP_K].astype(jnp.int32)
    cnt = counts[0, EXPERT_LANE0:EXPERT_LANE0 + N_EXPERTS].astype(jnp.int32)
    padded = ((cnt + MOE_ROWS - 1) // MOE_ROWS) * MOE_ROWS
    pends = jnp.cumsum(padded)
    pstarts = pends - padded
    ids = jnp.arange(N_EXPERTS, dtype=jnp.int32)
    dest = rank + jnp.sum(jnp.where(expert[..., None] == ids, pstarts, 0), axis=-1)
    n_used = pends[-1] // MOE_ROWS
    blocks = jnp.arange(MOE_BLOCKS, dtype=jnp.int32)
    block_e = jnp.sum((pends[None, :] <= (blocks * MOE_ROWS)[:, None]).astype(jnp.int32), axis=1)
    block_e = jnp.minimum(block_e, N_EXPERTS - 1)
    last_e = jnp.sum(jnp.where(blocks == n_used - 1, block_e, 0))
    used = (cnt > 0).astype(jnp.int32)
    ordinal = jnp.cumsum(used) - used
    n_experts_used = jnp.sum(used)
    by_ordinal = jnp.sum(jnp.where((ordinal[None, :] == ids[:, None]) & (used[None, :] > 0), ids[None, :], 0), axis=1)
    block_ord = jnp.sum(jnp.where(block_e[:, None] == ids[None, :], ordinal[None, :], 0), axis=1)
    block_first = (blocks * MOE_ROWS == jnp.sum(jnp.where(block_e[:, None] == ids[None, :], pstarts[None, :], 0), axis=1))
    block_first = (block_first & (blocks < n_used)).astype(jnp.int32)
    meta = jnp.concatenate([n_used.reshape(1), n_experts_used.reshape(1)]).astype(jnp.int32)
    return dest.reshape(MOE_TOP_K * TOKENS), block_ord.astype(jnp.int32), block_first, by_ordinal.astype(jnp.int32), meta, pends


MOE_SLOT_ROWS = MOE_ROWS * PACK_SUB
INVERT_UNROLL = 8


WEIGHT_SLOTS = 3
GATHER_SLOTS = 3
WEIGHT_DMA_PRIORITY = 1


def _moe_kernel(dest_ref, ord_ref, first_ref, eo_ref, meta_ref, pend_ref, hp_ref, wg_hbm, wu_hbm, wd_hbm, y_ref,
                xbuf, wg_buf, wu_buf, wd_buf, tok_ref, sem, wsem):
    i = pl.program_id(0)
    n_used = meta_ref[0]
    n_experts_used = meta_ref[1]

    def slot_rows(slot):
        return xbuf.at[pl.ds(pl.multiple_of(slot * MOE_SLOT_ROWS, MOE_SLOT_ROWS), MOE_SLOT_ROWS), :]

    def weight_copies(ordinal):
        e = eo_ref[ordinal]
        ws = ordinal % WEIGHT_SLOTS
        return [pltpu.make_async_copy(hbm.at[e], buf.at[ws], wsem.at[ws])
                for hbm, buf in ((wg_hbm, wg_buf), (wu_hbm, wu_buf), (wd_hbm, wd_buf))]

    def invert_dispatch():
        def fill_expert(e, c):
            first = jnp.maximum(pend_ref[e] - MOE_ROWS, 0)

            def fill(r, c2):
                tok_ref[first + r] = (first + r) & (TOKENS - 1)
                return c2

            lax.fori_loop(0, MOE_ROWS, fill, 0, unroll=INVERT_UNROLL)
            return c

        lax.fori_loop(0, N_EXPERTS, fill_expert, 0)

        def place(t, c):
            for k in range(MOE_TOP_K):
                tok_ref[dest_ref[k * TOKENS + t]] = t
            return c

        lax.fori_loop(0, TOKENS, place, 0, unroll=INVERT_UNROLL)

    def gather(block, slot):
        base = block * MOE_ROWS
        for r in range(MOE_ROWS):
            src = hp_ref.at[pl.ds(pl.multiple_of(tok_ref[base + r] * PACK_SUB, PACK_SUB), PACK_SUB), :]
            dst = xbuf.at[pl.ds(pl.multiple_of(slot * MOE_SLOT_ROWS + r * PACK_SUB, PACK_SUB), PACK_SUB), :]
            pltpu.make_async_copy(src, dst, sem.at[slot]).start()

    @pl.when(i == 0)
    def _():
        for ahead in range(WEIGHT_SLOTS - 1):
            @pl.when(ahead < n_experts_used)
            def _():
                for cp in weight_copies(ahead):
                    cp.start(priority=WEIGHT_DMA_PRIORITY)
        invert_dispatch()
        for ahead in range(GATHER_SLOTS - 1):
            gather(jnp.minimum(ahead, n_used - 1), ahead)

    @pl.when(i < n_used)
    def _():
        ordinal = ord_ref[i]

        @pl.when(first_ref[i] == 1)
        def _():
            for cp in weight_copies(ordinal):
                cp.wait()

            @pl.when(ordinal + WEIGHT_SLOTS - 1 < n_experts_used)
            def _():
                for cp in weight_copies(ordinal + WEIGHT_SLOTS - 1):
                    cp.start(priority=WEIGHT_DMA_PRIORITY)

        slot = i % GATHER_SLOTS
        ws = ordinal % WEIGHT_SLOTS
        pltpu.make_async_copy(slot_rows(slot), slot_rows(slot), sem.at[slot]).wait()
        ahead = i + GATHER_SLOTS - 1
        xb = _unpack_rows(xbuf, slot * MOE_SLOT_ROWS, MOE_ROWS).astype(BF16)
        gather(jnp.minimum(ahead, n_used - 1), ahead % GATHER_SLOTS)
        a = jnp.dot(xb, wg_buf[ws].astype(BF16), preferred_element_type=F32)
        u = jnp.dot(xb, wu_buf[ws].astype(BF16), preferred_element_type=F32)
        act = (a * jax.nn.sigmoid(a) * u).astype(BF16)
        y = jnp.dot(act, wd_buf[ws].astype(BF16), preferred_element_type=F32)
        _pack_rows(y, y_ref)

        @pl.when(i == n_used - 1)
        def _():
            for extra in range(1, GATHER_SLOTS):
                other = (i + extra) % GATHER_SLOTS
                pltpu.make_async_copy(slot_rows(other), slot_rows(other), sem.at[other]).wait()

    @pl.when(i >= n_used)
    def _():
        y_ref[...] = jnp.zeros_like(y_ref)


def _moe(dest, block_ord, block_first, by_ordinal, meta, pends, hp, w_gate, w_up, w_down):
    hbm = pl.BlockSpec(memory_space=pl.ANY)
    grid_spec = pltpu.PrefetchScalarGridSpec(
        num_scalar_prefetch=6,
        grid=(MOE_BLOCKS,),
        in_specs=[hbm, hbm, hbm, hbm],
        out_specs=pl.BlockSpec((MOE_SLOT_ROWS, LANES), lambda i, *_: (i, 0)),
        scratch_shapes=[pltpu.VMEM((GATHER_SLOTS * MOE_SLOT_ROWS, LANES), F32),
                        pltpu.VMEM((WEIGHT_SLOTS, D_MODEL, D_EXPERT), F32),
                        pltpu.VMEM((WEIGHT_SLOTS, D_MODEL, D_EXPERT), F32),
                        pltpu.VMEM((WEIGHT_SLOTS, D_EXPERT, D_MODEL), F32),
                        pltpu.SMEM((MOE_BLOCKS * MOE_ROWS,), jnp.int32),
                        pltpu.SemaphoreType.DMA((GATHER_SLOTS,)), pltpu.SemaphoreType.DMA((WEIGHT_SLOTS,))],
    )
    return pl.pallas_call(
        _moe_kernel,
        grid_spec=grid_spec,
        out_shape=jax.ShapeDtypeStruct((MOE_BLOCKS * MOE_SLOT_ROWS, LANES), F32),
        compiler_params=_cparams(("arbitrary",)),
        name="moe",
    )(dest, block_ord, block_first, by_ordinal, meta, pends, hp, w_gate, w_up, w_down)


COMBINE_TM = 256


def _combine_kernel(dest_ref, x1_ref, info_ref, g_ref, y_ref, o_ref, ybuf, sem):
    i = pl.program_id(0)
    last = pl.num_programs(0) - 1
    k_rows = COMBINE_TM * PACK_SUB
    slot_rows = MOE_TOP_K * k_rows

    def slot_ref(slot):
        return ybuf.at[pl.ds(pl.multiple_of(slot * slot_rows, slot_rows), slot_rows), :]

    def gather(tile, slot):
        for r in range(COMBINE_TM):
            for k in range(MOE_TOP_K):
                row = dest_ref[k * TOKENS + tile * COMBINE_TM + r]
                src = y_ref.at[pl.ds(pl.multiple_of(row * PACK_SUB, PACK_SUB), PACK_SUB), :]
                at = slot * slot_rows + k * k_rows + r * PACK_SUB
                dst = ybuf.at[pl.ds(pl.multiple_of(at, PACK_SUB), PACK_SUB), :]
                pltpu.make_async_copy(src, dst, sem.at[slot]).start(priority=k)

    @pl.when(i == 0)
    def _():
        for ahead in range(GATHER_SLOTS - 1):
            gather(ahead, ahead)

    slot = i % GATHER_SLOTS
    pltpu.make_async_copy(slot_ref(slot), slot_ref(slot), sem.at[slot]).wait()
    ahead = i + GATHER_SLOTS - 1
    gather(jnp.minimum(ahead, last), ahead % GATHER_SLOTS)
    x2 = x1_ref[...]
    for k in range(MOE_TOP_K):
        yk = _unpack_rows(ybuf, slot * slot_rows + k * k_rows, COMBINE_TM)
        x2 = x2 + info_ref[:, INFO_GATE + k:INFO_GATE + k + 1] * yk
    inv = lax.rsqrt(jnp.mean(x2 * x2, axis=-1, keepdims=True) + RMS_EPS)
    o_ref[...] = x2 * inv * g_ref[...]

    @pl.when(i == last)
    def _():
        for extra in range(1, GATHER_SLOTS):
            other = (i + extra) % GATHER_SLOTS
            pltpu.make_async_copy(slot_ref(other), slot_ref(other), sem.at[other]).wait()


def _combine(dest, x1, info, g, y_pad):
    tm = COMBINE_TM
    grid_spec = pltpu.PrefetchScalarGridSpec(
        num_scalar_prefetch=1,
        grid=(TOKENS // tm,),
        in_specs=[
            pl.BlockSpec((tm, D_MODEL), lambda i, d: (i, 0)),
            pl.BlockSpec((tm, ROUTER_COLS), lambda i, d: (i, 0)),
            pl.BlockSpec((1, D_MODEL), lambda i, d: (0, 0)),
            pl.BlockSpec(memory_space=pl.ANY),
        ],
        out_specs=pl.BlockSpec((tm, D_MODEL), lambda i, d: (i, 0)),
        scratch_shapes=[pltpu.VMEM((GATHER_SLOTS * MOE_TOP_K * tm * PACK_SUB, LANES), F32),
                        pltpu.SemaphoreType.DMA((GATHER_SLOTS,))],
    )
    return pl.pallas_call(
        _combine_kernel,
        grid_spec=grid_spec,
        out_shape=jax.ShapeDtypeStruct((TOKENS, D_MODEL), F32),
        compiler_params=_cparams(("arbitrary",)),
        name="combine",
    )(dest, x1, info, g, y_pad)


def kernel(x, mix_norm_g, w_in, w_fourier_out, ssm_A_re, ssm_A_im, ssm_log_dt, ssm_B_re, ssm_B_im, ssm_C_re,
           ssm_C_im, ssm_D, ssm_w_glu, w_ssm_out, w_out, ffn_norm_g, router_group_w, router_group_b,
           router_expert_w, router_expert_b, expert_w_gate, expert_w_up, expert_w_down, final_norm_g):
    assert x.shape == (BATCH, SEQ, D_MODEL) and w_in.shape[0] == 1
    tw, f2, cdft = _dft_constants()
    rq, rc = _replication_matrices()

    vf, us, xr, gates = _inproj(x, mix_norm_g[0][None], w_in[0].astype(BF16), cdft)
    fmix = _dft(vf, tw, f2)

    gen, dtoe, aq = _ssm_factors(ssm_A_re[0], ssm_A_im[0], ssm_log_dt[0], ssm_B_re[0], ssm_B_im[0],
                                 ssm_C_re[0], ssm_C_im[0])
    sre, sim = _ssm_states(xr, gen, rq, rc)
    hre, him = _ssm_scan(sre, sim, aq)
    yconv = _ssm_out(xr, gen, dtoe, rq, rc, hre, him)

    w_router = jnp.concatenate([router_group_w[0], router_expert_w[0]], axis=1)
    w_router = jnp.pad(w_router, ((0, 0), (0, ROUTER_COLS - w_router.shape[1])))
    b_router = jnp.concatenate([router_group_b[0], router_expert_b[0]])
    b_router = jnp.pad(b_router, (0, ROUTER_COLS - b_router.shape[0]))[None]
    wr_hi = w_router.astype(BF16)
    wr_lo = (w_router - wr_hi.astype(F32)).astype(BF16)
    x1, hp, logits = _merge(x.reshape(TOKENS, D_MODEL), fmix, yconv, us, gates, ssm_D[0][None],
                            w_fourier_out[0].astype(BF16), ssm_w_glu[0].astype(BF16), w_ssm_out[0].astype(BF16),
                            w_out[0].astype(BF16), ffn_norm_g[0][None], wr_hi, wr_lo, b_router)

    info, fields, counts = _route(logits)
    dest, block_ord, block_first, by_ordinal, meta, pends = _dispatch_plan(fields, counts)
    y_pad = _moe(dest, block_ord, block_first, by_ordinal, meta, pends, hp, expert_w_gate[0], expert_w_up[0],
                 expert_w_down[0])
    out = _combine(dest, x1, info, final_norm_g[None], y_pad)
    return out.reshape(BATCH, SEQ, D_MODEL)
```

```python
import math

import numpy as np
import jax
import jax.numpy as jnp
from jax import lax
from jax.experimental import pallas as pl
from jax.experimental.pallas import tpu as pltpu

F32 = jnp.float32
BF16 = jnp.bfloat16

D_MODEL = 1024
BATCH = 4
SEQ = 4096
TOKENS = BATCH * SEQ
FOURIER_WIDTH = 512
FOURIER_GROUP_CH = 128
FOURIER_GROUPS = 4
SSM_WIDTH = 512
SSM_GROUP_CH = 16
SSM_GROUPS = 32
SSM_STATE = 64
MOE_GROUPS = 8
EXPERTS_PER_GROUP = 8
N_EXPERTS = 64
MOE_TOP_K = 2
D_EXPERT = 512
RMS_EPS = 1e-6

LANES = 128
SSM_CHUNK = 16
SSM_LANE_BLOCKS = SSM_WIDTH // LANES
GROUPS_PER_BLOCK = LANES // SSM_GROUP_CH
N_CHUNKS = SEQ // SSM_CHUNK
GROUP_COLS = SSM_CHUNK * SSM_GROUP_CH
GROUP_SHIFT_CH = 4
MOE_ROWS = 256
MOE_BLOCKS = TOKENS * MOE_TOP_K // MOE_ROWS + N_EXPERTS
ROUTER_COLS = 128
VMEM_LIMIT = 48 * 1024 * 1024


def _cparams(sem, vmem=VMEM_LIMIT):
    return pltpu.CompilerParams(dimension_semantics=sem, vmem_limit_bytes=vmem)


IN_TM = 512


def _inproj_kernel(x_ref, g_ref, w_ref, cdft_ref, vf_ref, us_ref, xg_ref, gates_ref, zs_scr):
    x = x_ref[0]
    inv = lax.rsqrt(jnp.mean(x * x, axis=-1, keepdims=True) + RMS_EPS)
    h = (x * inv * g_ref[...]).astype(BF16)
    zf = jnp.dot(h, w_ref[:, 0:FOURIER_WIDTH], preferred_element_type=F32).astype(BF16)
    cdft = cdft_ref[...].astype(BF16)
    for g in range(FOURIER_GROUPS):
        sl = slice(g * LANES, (g + 1) * LANES)
        v = jnp.dot(zf[:, sl], cdft, preferred_element_type=F32)
        vf_ref[0, 0, :, sl] = v[:, :LANES].astype(BF16)
        vf_ref[0, 1, :, sl] = v[:, LANES:].astype(BF16)
    zs = jnp.dot(h, w_ref[:, FOURIER_WIDTH:FOURIER_WIDTH + SSM_WIDTH], preferred_element_type=F32)
    for j in range(SSM_LANE_BLOCKS):
        us_ref[j] = zs[:, j * LANES:(j + 1) * LANES].astype(BF16)
        zs_scr[j] = zs[:, j * LANES:(j + 1) * LANES]
    seg = lax.broadcasted_iota(jnp.int32, (IN_TM // SSM_CHUNK, LANES), 1) >> GROUP_SHIFT_CH
    for j in range(SSM_LANE_BLOCKS):
        pieces = [zs_scr[j, pl.ds(q, IN_TM // SSM_CHUNK, stride=SSM_CHUNK), :] for q in range(SSM_CHUNK)]
        for g in range(GROUPS_PER_BLOCK):
            for half in range(SSM_CHUNK // GROUPS_PER_BLOCK):
                acc = jnp.zeros((IN_TM // SSM_CHUNK, LANES), F32)
                for ql in range(GROUPS_PER_BLOCK):
                    shift = ((ql - g) % GROUPS_PER_BLOCK) * SSM_GROUP_CH
                    piece = pieces[half * GROUPS_PER_BLOCK + ql]
                    moved = piece if shift == 0 else pltpu.roll(piece, shift, axis=1)
                    acc = jnp.where(seg == ql, moved, acc)
                xg_ref[j * GROUPS_PER_BLOCK + g, :, half * LANES:(half + 1) * LANES] = acc.astype(BF16)
    base = FOURIER_WIDTH + SSM_WIDTH
    for n in range(4):
        zg = jnp.dot(h, w_ref[:, base + n * 512: base + (n + 1) * 512], preferred_element_type=F32)
        gates_ref[:, n * 512:(n + 1) * 512] = jax.nn.sigmoid(zg).astype(BF16)


def _inproj(x, g, w_in, cdft):
    nt = SEQ // IN_TM
    return pl.pallas_call(
        _inproj_kernel,
        grid=(BATCH, nt),
        in_specs=[
            pl.BlockSpec((1, IN_TM, D_MODEL), lambda b, i: (b, i, 0)),
            pl.BlockSpec((1, D_MODEL), lambda b, i: (0, 0)),
            pl.BlockSpec(w_in.shape, lambda b, i: (0, 0)),
            pl.BlockSpec(cdft.shape, lambda b, i: (0, 0)),
        ],
        out_specs=[
            pl.BlockSpec((1, 2, IN_TM, FOURIER_WIDTH), lambda b, i: (b, 0, i, 0)),
            pl.BlockSpec((SSM_LANE_BLOCKS, IN_TM, LANES), lambda b, i: (0, b * nt + i, 0)),
            pl.BlockSpec((SSM_GROUPS, IN_TM // SSM_CHUNK, GROUP_COLS), lambda b, i: (0, b * nt + i, 0)),
            pl.BlockSpec((IN_TM, 2 * D_MODEL), lambda b, i: (b * nt + i, 0)),
        ],
        out_shape=[
            jax.ShapeDtypeStruct((BATCH, 2, SEQ, FOURIER_WIDTH), BF16),
            jax.ShapeDtypeStruct((SSM_LANE_BLOCKS, TOKENS, LANES), BF16),
            jax.ShapeDtypeStruct((SSM_GROUPS, TOKENS // SSM_CHUNK, GROUP_COLS), BF16),
            jax.ShapeDtypeStruct((TOKENS, 2 * D_MODEL), BF16),
        ],
        scratch_shapes=[pltpu.VMEM((SSM_LANE_BLOCKS, IN_TM, LANES), F32)],
        compiler_params=_cparams(("parallel", "parallel")),
        name="inproj",
    )(x, g, w_in, cdft)


DFT_R1 = 8
DFT_R2 = SEQ // DFT_R1
DFT_LANES = 2 * LANES
DFT_ROWS = 16


def _cmul_const(z, w):
    re, im = z
    if abs(w.imag) < 1e-12:
        return (re, im) if abs(w.real - 1.0) < 1e-12 else (re * w.real, im * w.real)
    if abs(w.real) < 1e-12:
        return (im, -re) if abs(w.imag + 1.0) < 1e-12 else (-im * w.imag, re * w.imag)
    return re * w.real - im * w.imag, re * w.imag + im * w.real


def _fft_blocks(xs):
    n = len(xs)
    if n == 1:
        return xs
    even, odd = _fft_blocks(xs[0::2]), _fft_blocks(xs[1::2])
    out = [None] * n
    for k in range(n // 2):
        tr, ti = _cmul_const(odd[k], np.exp(-2j * np.pi * k / n))
        out[k] = (even[k][0] + tr, even[k][1] + ti)
        out[k + n // 2] = (even[k][0] - tr, even[k][1] - ti)
    return out


def _dft_kernel(v_ref, tw_ref, f_ref, o_ref, a_scr, o_scr):
    def tile(i, c):
        r0 = pl.multiple_of(i * DFT_ROWS, DFT_ROWS)
        for slab in range(DFT_LANES // LANES):
            lanes = slice(slab * LANES, (slab + 1) * LANES)
            xs = [(v_ref[0, 0, pl.ds(s1 * DFT_R2 + r0, DFT_ROWS), lanes].astype(F32),
                   v_ref[0, 1, pl.ds(s1 * DFT_R2 + r0, DFT_ROWS), lanes].astype(F32)) for s1 in range(DFT_R1)]
            for t1, (ar, ai) in enumerate(_fft_blocks(xs)):
                tr, ti = tw_ref[0, t1, pl.ds(r0, DFT_ROWS), :], tw_ref[1, t1, pl.ds(r0, DFT_ROWS), :]
                a_scr[t1, pl.ds(r0, DFT_ROWS), lanes] = (ar * tr - ai * ti).astype(BF16)
                a_scr[t1, pl.ds(DFT_R2 + r0, DFT_ROWS), lanes] = (ar * ti + ai * tr).astype(BF16)
        return c

    lax.fori_loop(0, DFT_R2 // DFT_ROWS, tile, 0)

    f2 = f_ref[...].astype(BF16)
    for t1 in range(DFT_R1):
        r = jnp.dot(f2, a_scr[t1], preferred_element_type=F32)
        for slab in range(DFT_LANES // LANES):
            o_scr[slab, pl.ds(t1, DFT_R2, stride=DFT_R1), :] = r[:, slab * LANES:(slab + 1) * LANES]
    for slab in range(DFT_LANES // LANES):
        o_ref[:, slab * LANES:(slab + 1) * LANES] = o_scr[slab].astype(BF16)


def _dft(v, tw, f2):
    nh = FOURIER_WIDTH // DFT_LANES
    return pl.pallas_call(
        _dft_kernel,
        grid=(BATCH, nh),
        in_specs=[
            pl.BlockSpec((1, 2, SEQ, DFT_LANES), lambda b, h: (b, 0, 0, h)),
            pl.BlockSpec(tw.shape, lambda b, h: (0, 0, 0, 0)),
            pl.BlockSpec(f2.shape, lambda b, h: (0, 0)),
        ],
        out_specs=pl.BlockSpec((SEQ, DFT_LANES), lambda b, h: (b, h)),
        out_shape=jax.ShapeDtypeStruct((TOKENS, FOURIER_WIDTH), BF16),
        scratch_shapes=[pltpu.VMEM((DFT_R1, 2 * DFT_R2, DFT_LANES), BF16),
                        pltpu.VMEM((DFT_LANES // LANES, SEQ, LANES), F32)],
        compiler_params=_cparams(("parallel", "parallel")),
        name="dft",
    )(v, tw, f2)


def _dft_constants():
    t1 = np.arange(DFT_R1)
    s2 = np.arange(DFT_R2)
    ang = 2.0 * np.pi * np.outer(t1, s2) / SEQ
    scale = 1.0 / math.sqrt(SEQ)
    tw = np.stack([np.cos(ang) * scale, -np.sin(ang) * scale])
    tw = np.repeat(tw[..., None], LANES, axis=-1)
    ang2 = 2.0 * np.pi * np.outer(s2, s2) / DFT_R2
    f2 = np.concatenate([np.cos(ang2), np.sin(ang2)], axis=1)
    kc = np.arange(FOURIER_GROUP_CH)
    angc = 2.0 * np.pi * np.outer(kc, kc) / FOURIER_GROUP_CH
    cs = 1.0 / math.sqrt(FOURIER_GROUP_CH)
    cdft = np.concatenate([np.cos(angc) * cs, -np.sin(angc) * cs], axis=1)
    return tuple(jnp.asarray(v, F32) for v in (tw, f2, cdft))


GROUP_BLOCKS = SSM_GROUPS // GROUPS_PER_BLOCK
DIR_STATE = 2 * SSM_STATE
STATE_LANES = SSM_GROUPS * DIR_STATE


def _ssm_operators(a_re, a_im, log_dt, b_re, b_im, c_re, c_im):
    q_len = SSM_CHUNK
    dt = jnp.exp(log_dt)[..., None]
    lr, li = a_re * dt, a_im * dt
    steps = jnp.arange(q_len + 1, dtype=F32)
    mag = jnp.exp(lr[..., None] * steps)
    ang = li[..., None] * steps
    pr, pi = mag * jnp.cos(ang), mag * jnp.sin(ang)
    ar, ai = pr[..., 1], pi[..., 1]
    den = a_re * a_re + a_im * a_im
    cr = ((ar - 1.0) * a_re + ai * a_im) / den
    ci = (ai * a_re - (ar - 1.0) * a_im) / den
    bbr = cr[..., None] * b_re - ci[..., None] * b_im
    bbi = cr[..., None] * b_im + ci[..., None] * b_re

    prq, piq = pr[..., :q_len], pi[..., :q_len]
    cpr = jnp.einsum('dgcn,dgnt->dgtcn', c_re, prq) - jnp.einsum('dgcn,dgnt->dgtcn', c_im, piq)
    cpi = jnp.einsum('dgcn,dgnt->dgtcn', c_re, piq) + jnp.einsum('dgcn,dgnt->dgtcn', c_im, prq)
    kern = jnp.einsum('dgtcn,dgne->dgtec', cpr, bbr) - jnp.einsum('dgtcn,dgne->dgtec', cpi, bbi)
    lags = jnp.concatenate([kern[1][:, :0:-1], kern[0][:, :1] + kern[1][:, :1], kern[0][:, 1:]], axis=1)
    mg = jnp.stack([lags[:, q_len - 1 - qi:2 * q_len - 1 - qi] for qi in range(q_len)], axis=1)
    mg = jnp.transpose(mg, (0, 1, 3, 2, 4)).reshape(SSM_GROUPS, GROUP_COLS, GROUP_COLS)

    def cmul(xr, xi, yr, yi):
        return xr * yr - xi * yi, xr * yi + xi * yr

    pin_r = jnp.stack([pr[0][..., q_len - 1::-1], pr[1][..., :q_len]])
    pin_i = jnp.stack([pi[0][..., q_len - 1::-1], pi[1][..., :q_len]])
    wr, wi = cmul(pin_r[..., :, None], pin_i[..., :, None], bbr[..., None, :], bbi[..., None, :])
    wsg = jnp.stack([wr, wi])
    wsg = jnp.transpose(wsg, (2, 4, 5, 0, 1, 3)).reshape(SSM_GROUPS, GROUP_COLS, 2 * DIR_STATE)

    pout_r = jnp.stack([pr[0][..., 1:], pr[1][..., q_len:0:-1]])
    pout_i = jnp.stack([pi[0][..., 1:], pi[1][..., q_len:0:-1]])
    ctr, cti = jnp.swapaxes(c_re, -1, -2), jnp.swapaxes(c_im, -1, -2)
    orr, oi = cmul(pout_r[..., :, None], pout_i[..., :, None], ctr[..., None, :], cti[..., None, :])
    wog = jnp.stack([orr, -oi])
    wog = jnp.transpose(wog, (2, 0, 1, 3, 4, 5)).reshape(SSM_GROUPS, 2 * DIR_STATE, GROUP_COLS)

    aq = jnp.stack([jnp.stack([pr[0][..., q_len], pr[1][..., q_len]], axis=1),
                    jnp.stack([pi[0][..., q_len], pi[1][..., q_len]], axis=1)])
    return mg.astype(BF16), wsg.astype(BF16), wog.astype(BF16), aq.reshape(2, STATE_LANES)


def _ssm_state_kernel(x_ref, ws_ref, sre_ref, sim_ref):
    for g in range(GROUPS_PER_BLOCK):
        s = jnp.dot(x_ref[g], ws_ref[g], preferred_element_type=F32)
        lanes = slice(g * DIR_STATE, (g + 1) * DIR_STATE)
        for b in range(BATCH):
            rows = slice(b * N_CHUNKS, (b + 1) * N_CHUNKS)
            sre_ref[b, :, lanes] = s[rows, :DIR_STATE]
            sim_ref[b, :, lanes] = s[rows, DIR_STATE:]


def _ssm_states(xg, wsg):
    out_spec = pl.BlockSpec((BATCH, N_CHUNKS, GROUPS_PER_BLOCK * DIR_STATE), lambda j: (0, 0, j))
    shape = jax.ShapeDtypeStruct((BATCH, N_CHUNKS, STATE_LANES), F32)
    return pl.pallas_call(
        _ssm_state_kernel,
        grid=(GROUP_BLOCKS,),
        in_specs=[
            pl.BlockSpec((GROUPS_PER_BLOCK, BATCH * N_CHUNKS, GROUP_COLS), lambda j: (j, 0, 0)),
            pl.BlockSpec((GROUPS_PER_BLOCK, GROUP_COLS, 2 * DIR_STATE), lambda j: (j, 0, 0)),
        ],
        out_specs=[out_spec, out_spec],
        out_shape=[shape, shape],
        compiler_params=_cparams(("parallel",)),
        name="ssm_states",
    )(xg, wsg)


def _ssm_scan_kernel(sre_ref, sim_ref, aq_ref, hfr_ref, hfi_ref, hbr_ref, hbi_ref):
    ar, ai = aq_ref[0:1], aq_ref[1:2]
    backward = (lax.broadcasted_iota(jnp.int32, (1, SCAN_LANES), 1) & SSM_STATE) != 0

    def body(k, carry):
        hr, hi = carry
        kb = N_CHUNKS - 1 - k
        hfr_ref[0, pl.ds(k, 1), :] = hr
        hfi_ref[0, pl.ds(k, 1), :] = hi
        hbr_ref[0, pl.ds(kb, 1), :] = hr
        hbi_ref[0, pl.ds(kb, 1), :] = hi
        sr = jnp.where(backward, sre_ref[0, pl.ds(kb, 1), :], sre_ref[0, pl.ds(k, 1), :])
        si = jnp.where(backward, sim_ref[0, pl.ds(kb, 1), :], sim_ref[0, pl.ds(k, 1), :])
        return ar * hr - ai * hi + sr, ar * hi + ai * hr + si

    z = jnp.zeros((1, SCAN_LANES), F32)
    lax.fori_loop(0, N_CHUNKS, body, (z, z))


SCAN_LANES = STATE_LANES // 2


def _ssm_scan(sre, sim, aq):
    spec = pl.BlockSpec((1, N_CHUNKS, SCAN_LANES), lambda b, h: (b, 0, h))
    shape = jax.ShapeDtypeStruct(sre.shape, F32)
    return pl.pallas_call(
        _ssm_scan_kernel,
        grid=(BATCH, STATE_LANES // SCAN_LANES),
        in_specs=[spec, spec, pl.BlockSpec((2, SCAN_LANES), lambda b, h: (0, h))],
        out_specs=[spec] * 4,
        out_shape=[shape] * 4,
        compiler_params=_cparams(("parallel", "parallel")),
        name="ssm_scan",
    )(sre, sim, aq)


def _ssm_out_kernel(x_ref, m_ref, wo_ref, hfr_ref, hfi_ref, hbr_ref, hbi_ref, y_ref, y_scr):
    fwd = (lax.broadcasted_iota(jnp.int32, (N_CHUNKS, GROUPS_PER_BLOCK * DIR_STATE), 1) & SSM_STATE) == 0
    h_re = jnp.where(fwd, hfr_ref[0], hbr_ref[0]).astype(BF16)
    h_im = jnp.where(fwd, hfi_ref[0], hbi_ref[0]).astype(BF16)
    accs = []
    for g in range(GROUPS_PER_BLOCK):
        lanes = slice(g * DIR_STATE, (g + 1) * DIR_STATE)
        h = jnp.concatenate([h_re[:, lanes], h_im[:, lanes]], axis=1)
        accs.append(jnp.dot(x_ref[g, 0], m_ref[g], preferred_element_type=F32)
                    + jnp.dot(h, wo_ref[g], preferred_element_type=F32))
    seg = lax.broadcasted_iota(jnp.int32, (N_CHUNKS, LANES), 1) >> GROUP_SHIFT_CH
    for q in range(SSM_CHUNK):
        half, ql = divmod(q, GROUPS_PER_BLOCK)
        piece = jnp.zeros((N_CHUNKS, LANES), F32)
        for g in range(GROUPS_PER_BLOCK):
            src = accs[g][:, half * LANES:(half + 1) * LANES]
            shift = ((g - ql) % GROUPS_PER_BLOCK) * SSM_GROUP_CH
            moved = src if shift == 0 else pltpu.roll(src, shift, axis=1)
            piece = jnp.where(seg == g, moved, piece)
        y_scr[pl.ds(q, N_CHUNKS, stride=SSM_CHUNK), :] = piece
    y_ref[0] = y_scr[...].astype(BF16)


def _ssm_out(xg, mg, wog, hfr, hfi, hbr, hbi):
    hspec = pl.BlockSpec((1, N_CHUNKS, GROUPS_PER_BLOCK * DIR_STATE), lambda j, b: (b, 0, j))
    wspec = pl.BlockSpec((GROUPS_PER_BLOCK, GROUP_COLS, GROUP_COLS), lambda j, b: (j, 0, 0))
    return pl.pallas_call(
        _ssm_out_kernel,
        grid=(GROUP_BLOCKS, BATCH),
        in_specs=[
            pl.BlockSpec((GROUPS_PER_BLOCK, 1, N_CHUNKS, GROUP_COLS), lambda j, b: (j, b, 0, 0)),
            wspec, wspec, hspec, hspec, hspec, hspec,
        ],
        out_specs=pl.BlockSpec((1, SEQ, LANES), lambda j, b: (j, b, 0)),
        out_shape=jax.ShapeDtypeStruct((SSM_LANE_BLOCKS, TOKENS, LANES), BF16),
        scratch_shapes=[pltpu.VMEM((SEQ, LANES), F32)],
        compiler_params=_cparams(("parallel", "parallel")),
        name="ssm_out",
    )(xg.reshape(SSM_GROUPS, BATCH, N_CHUNKS, GROUP_COLS), mg, wog, hfr, hfi, hbr, hbi)


MERGE_TM = 512
GELU_C = math.sqrt(2.0 / math.pi)
PACK_SUB = D_MODEL // LANES


def _split_bf16(v):
    hi = v.astype(BF16)
    lo = (v - hi.astype(F32)).astype(BF16)
    return hi, lo


def _pack_rows(v, out_ref):
    for s in range(PACK_SUB):
        out_ref[pl.ds(s, v.shape[0], stride=PACK_SUB), :] = v[:, s * LANES:(s + 1) * LANES]


def _unpack_rows(buf_ref, start, rows):
    return jnp.concatenate([buf_ref[pl.ds(start + s, rows, stride=PACK_SUB), :] for s in range(PACK_SUB)], axis=1)


def _merge_kernel(x_ref, fm_ref, yc_ref, us_ref, gates_ref, dskip_ref, wf_ref, wglu_ref, ws_ref, wo_ref,
                  ng_ref, wrh_ref, wrl_ref, rb_ref, x1_ref, hp_ref, logit_ref):
    conv = jnp.concatenate([yc_ref[j].astype(F32) for j in range(SSM_LANE_BLOCKS)], axis=-1)
    u = jnp.concatenate([us_ref[j].astype(F32) for j in range(SSM_LANE_BLOCKS)], axis=-1)
    y = conv + dskip_ref[...] * u
    y = 0.5 * y * (1.0 + jnp.tanh(GELU_C * (y + 0.044715 * (y * y * y))))
    glu = jax.nn.sigmoid(jnp.dot(y.astype(BF16), wglu_ref[...], preferred_element_type=F32))
    y_s = jnp.dot((y * glu).astype(BF16), ws_ref[...], preferred_element_type=F32)
    y_f = jnp.dot(fm_ref[...], wf_ref[...], preferred_element_type=F32)
    merged = (gates_ref[:, :D_MODEL].astype(F32) * y_f + gates_ref[:, D_MODEL:].astype(F32) * y_s)
    x1 = x_ref[...] + jnp.dot(merged.astype(BF16), wo_ref[...], preferred_element_type=F32)
    x1_ref[...] = x1
    inv = lax.rsqrt(jnp.mean(x1 * x1, axis=-1, keepdims=True) + RMS_EPS)
    hn = x1 * inv * ng_ref[...]
    _pack_rows(hn, hp_ref)
    hi, lo = _split_bf16(hn)
    logits = (jnp.dot(hi, wrh_ref[...], preferred_element_type=F32)
              + jnp.dot(lo, wrh_ref[...], preferred_element_type=F32)
              + jnp.dot(hi, wrl_ref[...], preferred_element_type=F32))
    logit_ref[...] = logits + rb_ref[...]


def _merge(x, fmix, yconv, us, gates, dskip, wf, wglu, ws, wo, ng, wrh, wrl, rb):
    tm = MERGE_TM
    full = lambda a: pl.BlockSpec(a.shape, lambda i: (0,) * a.ndim)
    return pl.pallas_call(
        _merge_kernel,
        grid=(TOKENS // tm,),
        in_specs=[
            pl.BlockSpec((tm, D_MODEL), lambda i: (i, 0)),
            pl.BlockSpec((tm, FOURIER_WIDTH), lambda i: (i, 0)),
            pl.BlockSpec((SSM_LANE_BLOCKS, tm, LANES), lambda i: (0, i, 0)),
            pl.BlockSpec((SSM_LANE_BLOCKS, tm, LANES), lambda i: (0, i, 0)),
            pl.BlockSpec((tm, 2 * D_MODEL), lambda i: (i, 0)),
            full(dskip), full(wf), full(wglu), full(ws), full(wo), full(ng), full(wrh), full(wrl), full(rb),
        ],
        out_specs=[
            pl.BlockSpec((tm, D_MODEL), lambda i: (i, 0)),
            pl.BlockSpec((tm * PACK_SUB, LANES), lambda i: (i, 0)),
            pl.BlockSpec((tm, ROUTER_COLS), lambda i: (i, 0)),
        ],
        out_shape=[
            jax.ShapeDtypeStruct((TOKENS, D_MODEL), F32),
            jax.ShapeDtypeStruct((TOKENS * PACK_SUB, LANES), F32),
            jax.ShapeDtypeStruct((TOKENS, ROUTER_COLS), F32),
        ],
        compiler_params=_cparams(("parallel",)),
        name="merge",
    )(x, fmix, yconv, us, gates, dskip, wf, wglu, ws, wo, ng, wrh, wrl, rb)


ROUTE_TM = 512
EXPERT_LANE0 = MOE_GROUPS
INFO_EXPERT, INFO_RANK, INFO_GATE = 0, 2, 4
INFO_FIELDS = 8


def _route_kernel(lg_ref, info_ref, fields_ref, cnt_ref, carry):
    @pl.when(pl.program_id(0) == 0)
    def _():
        carry[...] = jnp.zeros_like(carry)

    lg = lg_ref[...]
    tm = lg.shape[0]
    col_i = lax.broadcasted_iota(jnp.int32, lg.shape, 1)
    col = col_i.astype(F32)
    neg = jnp.float32(-jnp.inf)
    none = jnp.float32(ROUTER_COLS)

    def row_max(v):
        return jnp.max(v, axis=-1, keepdims=True)

    def first_at(v, m):
        return jnp.min(jnp.where(v == m, col, none), axis=-1, keepdims=True)

    gl = jnp.where(col_i < MOE_GROUPS, lg, neg)
    gmax = row_max(gl)
    p_g = 1.0 / jnp.sum(jnp.exp(gl - gmax), axis=-1, keepdims=True)
    lo = EXPERT_LANE0 + first_at(gl, gmax) * EXPERTS_PER_GROUP
    el = jnp.where((col >= lo) & (col < lo + EXPERTS_PER_GROUP), lg, neg)
    l1 = row_max(el)
    i1 = first_at(el, l1)
    el2 = jnp.where(col == i1, neg, el)
    l2 = row_max(el2)
    i2 = first_at(el2, l2)
    r = jnp.exp(l2 - l1)
    w1 = p_g / (1.0 + r)
    w2 = w1 * r

    hit1, hit2 = col == i1, col == i2
    onehot = jnp.where(hit1 | hit2, 1.0, 0.0)
    earlier = lax.broadcasted_iota(jnp.int32, (tm, tm), 0) > lax.broadcasted_iota(jnp.int32, (tm, tm), 1)
    before = jnp.dot(jnp.where(earlier, 1.0, 0.0).astype(BF16), onehot.astype(BF16),
                     preferred_element_type=F32) + carry[...]
    rank1 = jnp.sum(jnp.where(hit1, before, 0.0), axis=-1, keepdims=True)
    rank2 = jnp.sum(jnp.where(hit2, before, 0.0), axis=-1, keepdims=True)
    carry[...] += jnp.sum(onehot, axis=0, keepdims=True)
    cnt_ref[...] = carry[...]

    info = jnp.zeros(lg.shape, F32)
    for lane, v in ((INFO_EXPERT, i1 - EXPERT_LANE0), (INFO_EXPERT + 1, i2 - EXPERT_LANE0), (INFO_RANK, rank1),
                    (INFO_RANK + 1, rank2), (INFO_GATE, w1), (INFO_GATE + 1, w2)):
        info = jnp.where(col_i == lane, v, info)
    info_ref[...] = info
    fields_ref[...] = info.T[:INFO_FIELDS]


def _route(logits):
    return pl.pallas_call(
        _route_kernel,
        grid=(TOKENS // ROUTE_TM,),
        in_specs=[pl.BlockSpec((ROUTE_TM, ROUTER_COLS), lambda i: (i, 0))],
        out_specs=[pl.BlockSpec((ROUTE_TM, ROUTER_COLS), lambda i: (i, 0)),
                   pl.BlockSpec((INFO_FIELDS, ROUTE_TM), lambda i: (0, i)),
                   pl.BlockSpec((1, ROUTER_COLS), lambda i: (0, 0))],
        out_shape=[jax.ShapeDtypeStruct((TOKENS, ROUTER_COLS), F32),
                   jax.ShapeDtypeStruct((INFO_FIELDS, TOKENS), F32),
                   jax.ShapeDtypeStruct((1, ROUTER_COLS), F32)],
        scratch_shapes=[pltpu.VMEM((1, ROUTER_COLS), F32)],
        compiler_params=_cparams(("arbitrary",)),
        name="route",
    )(logits)


def _dispatch_plan(fields, counts):
    expert = fields[INFO_EXPERT:INFO_EXPERT + MOE_TOP_K].astype(jnp.int32)
    rank = fields[INFO_RANK:INFO_RANK + MOE_TOP_K].astype(jnp.int32)
    cnt = counts[0, EXPERT_LANE0:EXPERT_LANE0 + N_EXPERTS].astype(jnp.int32)
    padded = ((cnt + MOE_ROWS - 1) // MOE_ROWS) * MOE_ROWS
    pends = jnp.cumsum(padded)
    pstarts = pends - padded
    ids = jnp.arange(N_EXPERTS, dtype=jnp.int32)
    dest = rank + jnp.sum(jnp.where(expert[..., None] == ids, pstarts, 0), axis=-1)
    n_used = pends[-1] // MOE_ROWS
    blocks = jnp.arange(MOE_BLOCKS, dtype=jnp.int32)
    block_e = jnp.sum((pends[None, :] <= (blocks * MOE_ROWS)[:, None]).astype(jnp.int32), axis=1)
    block_e = jnp.minimum(block_e, N_EXPERTS - 1)
    last_e = jnp.sum(jnp.where(blocks == n_used - 1, block_e, 0))
    used = (cnt > 0).astype(jnp.int32)
    ordinal = jnp.cumsum(used) - used
    n_experts_used = jnp.sum(used)
    by_ordinal = jnp.sum(jnp.where((ordinal[None, :] == ids[:, None]) & (used[None, :] > 0), ids[None, :], 0), axis=1)
    block_ord = jnp.sum(jnp.where(block_e[:, None] == ids[None, :], ordinal[None, :], 0), axis=1)
    block_first = (blocks * MOE_ROWS == jnp.sum(jnp.where(block_e[:, None] == ids[None, :], pstarts[None, :], 0), axis=1))
    block_first = (block_first & (blocks < n_used)).astype(jnp.int32)
    meta = jnp.concatenate([n_used.reshape(1), n_experts_used.reshape(1)]).astype(jnp.int32)
    return dest.reshape(MOE_TOP_K * TOKENS), block_ord.astype(jnp.int32), block_first, by_ordinal.astype(jnp.int32), meta, pends


MOE_SLOT_ROWS = MOE_ROWS * PACK_SUB
INVERT_UNROLL = 8


WEIGHT_SLOTS = 3
GATHER_SLOTS = 3
WEIGHT_DMA_PRIORITY = 1


def _moe_kernel(dest_ref, ord_ref, first_ref, eo_ref, meta_ref, pend_ref, hp_ref, wg_hbm, wu_hbm, wd_hbm, y_ref,
                xbuf, wg_buf, wu_buf, wd_buf, tok_ref, sem, wsem):
    i = pl.program_id(0)
    n_used = meta_ref[0]
    n_experts_used = meta_ref[1]

    def slot_rows(slot):
        return xbuf.at[pl.ds(pl.multiple_of(slot * MOE_SLOT_ROWS, MOE_SLOT_ROWS), MOE_SLOT_ROWS), :]

    def weight_copies(ordinal):
        e = eo_ref[ordinal]
        ws = ordinal % WEIGHT_SLOTS
        return [pltpu.make_async_copy(hbm.at[e], buf.at[ws], wsem.at[ws])
                for hbm, buf in ((wg_hbm, wg_buf), (wu_hbm, wu_buf), (wd_hbm, wd_buf))]

    def invert_dispatch():
        def fill_expert(e, c):
            first = jnp.maximum(pend_ref[e] - MOE_ROWS, 0)

            def fill(r, c2):
                tok_ref[first + r] = (first + r) & (TOKENS - 1)
                return c2

            lax.fori_loop(0, MOE_ROWS, fill, 0, unroll=INVERT_UNROLL)
            return c

        lax.fori_loop(0, N_EXPERTS, fill_expert, 0)

        def place(t, c):
            for k in range(MOE_TOP_K):
                tok_ref[dest_ref[k * TOKENS + t]] = t
            return c

        lax.fori_loop(0, TOKENS, place, 0, unroll=INVERT_UNROLL)

    def gather(block, slot):
        base = block * MOE_ROWS
        for r in range(MOE_ROWS):
            src = hp_ref.at[pl.ds(pl.multiple_of(tok_ref[base + r] * PACK_SUB, PACK_SUB), PACK_SUB), :]
            dst = xbuf.at[pl.ds(pl.multiple_of(slot * MOE_SLOT_ROWS + r * PACK_SUB, PACK_SUB), PACK_SUB), :]
            pltpu.make_async_copy(src, dst, sem.at[slot]).start()

    @pl.when(i == 0)
    def _():
        for ahead in range(WEIGHT_SLOTS - 1):
            @pl.when(ahead < n_experts_used)
            def _():
                for cp in weight_copies(ahead):
                    cp.start(priority=WEIGHT_DMA_PRIORITY)
        invert_dispatch()
        for ahead in range(GATHER_SLOTS - 1):
            gather(jnp.minimum(ahead, n_used - 1), ahead)

    @pl.when(i < n_used)
    def _():
        ordinal = ord_ref[i]

        @pl.when(first_ref[i] == 1)
        def _():
            for cp in weight_copies(ordinal):
                cp.wait()

            @pl.when(ordinal + WEIGHT_SLOTS - 1 < n_experts_used)
            def _():
                for cp in weight_copies(ordinal + WEIGHT_SLOTS - 1):
                    cp.start(priority=WEIGHT_DMA_PRIORITY)

        slot = i % GATHER_SLOTS
        ws = ordinal % WEIGHT_SLOTS
        pltpu.make_async_copy(slot_rows(slot), slot_rows(slot), sem.at[slot]).wait()
        ahead = i + GATHER_SLOTS - 1
        xb = _unpack_rows(xbuf, slot * MOE_SLOT_ROWS, MOE_ROWS).astype(BF16)
        gather(jnp.minimum(ahead, n_used - 1), ahead % GATHER_SLOTS)
        a = jnp.dot(xb, wg_buf[ws].astype(BF16), preferred_element_type=F32)
        u = jnp.dot(xb, wu_buf[ws].astype(BF16), preferred_element_type=F32)
        act = (a * jax.nn.sigmoid(a) * u).astype(BF16)
        y = jnp.dot(act, wd_buf[ws].astype(BF16), preferred_element_type=F32)
        _pack_rows(y, y_ref)

        @pl.when(i == n_used - 1)
        def _():
            for extra in range(1, GATHER_SLOTS):
                other = (i + extra) % GATHER_SLOTS
                pltpu.make_async_copy(slot_rows(other), slot_rows(other), sem.at[other]).wait()

    @pl.when(i >= n_used)
    def _():
        y_ref[...] = jnp.zeros_like(y_ref)


def _moe(dest, block_ord, block_first, by_ordinal, meta, pends, hp, w_gate, w_up, w_down):
    hbm = pl.BlockSpec(memory_space=pl.ANY)
    grid_spec = pltpu.PrefetchScalarGridSpec(
        num_scalar_prefetch=6,
        grid=(MOE_BLOCKS,),
        in_specs=[hbm, hbm, hbm, hbm],
        out_specs=pl.BlockSpec((MOE_SLOT_ROWS, LANES), lambda i, *_: (i, 0)),
        scratch_shapes=[pltpu.VMEM((GATHER_SLOTS * MOE_SLOT_ROWS, LANES), F32),
                        pltpu.VMEM((WEIGHT_SLOTS, D_MODEL, D_EXPERT), F32),
                        pltpu.VMEM((WEIGHT_SLOTS, D_MODEL, D_EXPERT), F32),
                        pltpu.VMEM((WEIGHT_SLOTS, D_EXPERT, D_MODEL), F32),
                        pltpu.SMEM((MOE_BLOCKS * MOE_ROWS,), jnp.int32),
                        pltpu.SemaphoreType.DMA((GATHER_SLOTS,)), pltpu.SemaphoreType.DMA((WEIGHT_SLOTS,))],
    )
    return pl.pallas_call(
        _moe_kernel,
        grid_spec=grid_spec,
        out_shape=jax.ShapeDtypeStruct((MOE_BLOCKS * MOE_SLOT_ROWS, LANES), F32),
        compiler_params=_cparams(("arbitrary",)),
        name="moe",
    )(dest, block_ord, block_first, by_ordinal, meta, pends, hp, w_gate, w_up, w_down)


COMBINE_TM = 256


def _combine_kernel(dest_ref, x1_ref, info_ref, g_ref, y_ref, o_ref, ybuf, sem):
    i = pl.program_id(0)
    last = pl.num_programs(0) - 1
    k_rows = COMBINE_TM * PACK_SUB
    slot_rows = MOE_TOP_K * k_rows

    def slot_ref(slot):
        return ybuf.at[pl.ds(pl.multiple_of(slot * slot_rows, slot_rows), slot_rows), :]

    def gather(tile, slot):
        for r in range(COMBINE_TM):
            for k in range(MOE_TOP_K):
                row = dest_ref[k * TOKENS + tile * COMBINE_TM + r]
                src = y_ref.at[pl.ds(pl.multiple_of(row * PACK_SUB, PACK_SUB), PACK_SUB), :]
                at = slot * slot_rows + k * k_rows + r * PACK_SUB
                dst = ybuf.at[pl.ds(pl.multiple_of(at, PACK_SUB), PACK_SUB), :]
                pltpu.make_async_copy(src, dst, sem.at[slot]).start(priority=k)

    @pl.when(i == 0)
    def _():
        for ahead in range(GATHER_SLOTS - 1):
            gather(ahead, ahead)

    slot = i % GATHER_SLOTS
    pltpu.make_async_copy(slot_ref(slot), slot_ref(slot), sem.at[slot]).wait()
    ahead = i + GATHER_SLOTS - 1
    gather(jnp.minimum(ahead, last), ahead % GATHER_SLOTS)
    x2 = x1_ref[...]
    for k in range(MOE_TOP_K):
        yk = _unpack_rows(ybuf, slot * slot_rows + k * k_rows, COMBINE_TM)
        x2 = x2 + info_ref[:, INFO_GATE + k:INFO_GATE + k + 1] * yk
    inv = lax.rsqrt(jnp.mean(x2 * x2, axis=-1, keepdims=True) + RMS_EPS)
    o_ref[...] = x2 * inv * g_ref[...]

    @pl.when(i == last)
    def _():
        for extra in range(1, GATHER_SLOTS):
            other = (i + extra) % GATHER_SLOTS
            pltpu.make_async_copy(slot_ref(other), slot_ref(other), sem.at[other]).wait()


def _combine(dest, x1, info, g, y_pad):
    tm = COMBINE_TM
    grid_spec = pltpu.PrefetchScalarGridSpec(
        num_scalar_prefetch=1,
        grid=(TOKENS // tm,),
        in_specs=[
            pl.BlockSpec((tm, D_MODEL), lambda i, d: (i, 0)),
            pl.BlockSpec((tm, ROUTER_COLS), lambda i, d: (i, 0)),
            pl.BlockSpec((1, D_MODEL), lambda i, d: (0, 0)),
            pl.BlockSpec(memory_space=pl.ANY),
        ],
        out_specs=pl.BlockSpec((tm, D_MODEL), lambda i, d: (i, 0)),
        scratch_shapes=[pltpu.VMEM((GATHER_SLOTS * MOE_TOP_K * tm * PACK_SUB, LANES), F32),
                        pltpu.SemaphoreType.DMA((GATHER_SLOTS,))],
    )
    return pl.pallas_call(
        _combine_kernel,
        grid_spec=grid_spec,
        out_shape=jax.ShapeDtypeStruct((TOKENS, D_MODEL), F32),
        compiler_params=_cparams(("arbitrary",)),
        name="combine",
    )(dest, x1, info, g, y_pad)


def kernel(x, mix_norm_g, w_in, w_fourier_out, ssm_A_re, ssm_A_im, ssm_log_dt, ssm_B_re, ssm_B_im, ssm_C_re,
           ssm_C_im, ssm_D, ssm_w_glu, w_ssm_out, w_out, ffn_norm_g, router_group_w, router_group_b,
           router_expert_w, router_expert_b, expert_w_gate, expert_w_up, expert_w_down, final_norm_g):
    assert x.shape == (BATCH, SEQ, D_MODEL) and w_in.shape[0] == 1
    tw, f2, cdft = _dft_constants()

    vf, us, xg, gates = _inproj(x, mix_norm_g[0][None], w_in[0].astype(BF16), cdft)
    fmix = _dft(vf, tw, f2)

    mg, wsg, wog, aq = _ssm_operators(ssm_A_re[0], ssm_A_im[0], ssm_log_dt[0], ssm_B_re[0], ssm_B_im[0],
                                      ssm_C_re[0], ssm_C_im[0])
    sre, sim = _ssm_states(xg, wsg)
    yconv = _ssm_out(xg, mg, wog, *_ssm_scan(sre, sim, aq))

    w_router = jnp.concatenate([router_group_w[0], router_expert_w[0]], axis=1)
    w_router = jnp.pad(w_router, ((0, 0), (0, ROUTER_COLS - w_router.shape[1])))
    b_router = jnp.concatenate([router_group_b[0], router_expert_b[0]])
    b_router = jnp.pad(b_router, (0, ROUTER_COLS - b_router.shape[0]))[None]
    wr_hi = w_router.astype(BF16)
    wr_lo = (w_router - wr_hi.astype(F32)).astype(BF16)
    x1, hp, logits = _merge(x.reshape(TOKENS, D_MODEL), fmix, yconv, us, gates, ssm_D[0][None],
                            w_fourier_out[0].astype(BF16), ssm_w_glu[0].astype(BF16), w_ssm_out[0].astype(BF16),
                            w_out[0].astype(BF16), ffn_norm_g[0][None], wr_hi, wr_lo, b_router)

    info, fields, counts = _route(logits)
    dest, block_ord, block_first, by_ordinal, meta, pends = _dispatch_plan(fields, counts)
    y_pad = _moe(dest, block_ord, block_first, by_ordinal, meta, pends, hp, expert_w_gate[0], expert_w_up[0],
                 expert_w_down[0])
    out = _combine(dest, x1, info, final_norm_g[None], y_pad)
    return out.reshape(BATCH, SEQ, D_MODEL)
```

```python
import math

import numpy as np
import jax
import jax.numpy as jnp
from jax import lax
from jax.experimental import pallas as pl
from jax.experimental.pallas import tpu as pltpu

F32 = jnp.float32
BF16 = jnp.bfloat16

D_MODEL = 1024
BATCH = 4
SEQ = 4096
TOKENS = BATCH * SEQ
FOURIER_WIDTH = 512
FOURIER_GROUP_CH = 128
FOURIER_GROUPS = 4
SSM_WIDTH = 512
SSM_GROUP_CH = 16
SSM_GROUPS = 32
SSM_STATE = 64
MOE_GROUPS = 8
EXPERTS_PER_GROUP = 8
N_EXPERTS = 64
MOE_TOP_K = 2
D_EXPERT = 512
RMS_EPS = 1e-6

LANES = 128
SSM_CHUNK = 16
SSM_LANE_BLOCKS = SSM_WIDTH // LANES
GROUPS_PER_BLOCK = LANES // SSM_GROUP_CH
N_CHUNKS = SEQ // SSM_CHUNK
GROUP_COLS = SSM_CHUNK * SSM_GROUP_CH
GROUP_SHIFT_CH = 4
MOE_ROWS = 256
MOE_BLOCKS = TOKENS * MOE_TOP_K // MOE_ROWS + N_EXPERTS
ROUTER_COLS = 128
VMEM_LIMIT = 48 * 1024 * 1024


def _cparams(sem, vmem=VMEM_LIMIT):
    return pltpu.CompilerParams(dimension_semantics=sem, vmem_limit_bytes=vmem)


IN_TM = 512


def _inproj_kernel(x_ref, g_ref, w_ref, cdft_ref, vf_ref, us_ref, xg_ref, gates_ref, zs_scr):
    x = x_ref[0]
    inv = lax.rsqrt(jnp.mean(x * x, axis=-1, keepdims=True) + RMS_EPS)
    h = (x * inv * g_ref[...]).astype(BF16)
    zf = jnp.dot(h, w_ref[:, 0:FOURIER_WIDTH], preferred_element_type=F32).astype(BF16)
    cdft = cdft_ref[...].astype(BF16)
    for g in range(FOURIER_GROUPS):
        sl = slice(g * LANES, (g + 1) * LANES)
        v = jnp.dot(zf[:, sl], cdft, preferred_element_type=F32)
        vf_ref[0, 0, :, sl] = v[:, :LANES].astype(BF16)
        vf_ref[0, 1, :, sl] = v[:, LANES:].astype(BF16)
    zs = jnp.dot(h, w_ref[:, FOURIER_WIDTH:FOURIER_WIDTH + SSM_WIDTH], preferred_element_type=F32)
    for j in range(SSM_LANE_BLOCKS):
        us_ref[j] = zs[:, j * LANES:(j + 1) * LANES].astype(BF16)
        zs_scr[j] = zs[:, j * LANES:(j + 1) * LANES]
    seg = lax.broadcasted_iota(jnp.int32, (IN_TM // SSM_CHUNK, LANES), 1) >> GROUP_SHIFT_CH
    for j in range(SSM_LANE_BLOCKS):
        pieces = [zs_scr[j, pl.ds(q, IN_TM // SSM_CHUNK, stride=SSM_CHUNK), :] for q in range(SSM_CHUNK)]
        for g in range(GROUPS_PER_BLOCK):
            for half in range(SSM_CHUNK // GROUPS_PER_BLOCK):
                acc = jnp.zeros((IN_TM // SSM_CHUNK, LANES), F32)
                for ql in range(GROUPS_PER_BLOCK):
                    shift = ((ql - g) % GROUPS_PER_BLOCK) * SSM_GROUP_CH
                    piece = pieces[half * GROUPS_PER_BLOCK + ql]
                    moved = piece if shift == 0 else pltpu.roll(piece, shift, axis=1)
                    acc = jnp.where(seg == ql, moved, acc)
                xg_ref[j * GROUPS_PER_BLOCK + g, :, half * LANES:(half + 1) * LANES] = acc.astype(BF16)
    base = FOURIER_WIDTH + SSM_WIDTH
    for n in range(4):
        zg = jnp.dot(h, w_ref[:, base + n * 512: base + (n + 1) * 512], preferred_element_type=F32)
        gates_ref[:, n * 512:(n + 1) * 512] = jax.nn.sigmoid(zg).astype(BF16)


def _inproj(x, g, w_in, cdft):
    nt = SEQ // IN_TM
    return pl.pallas_call(
        _inproj_kernel,
        grid=(BATCH, nt),
        in_specs=[
            pl.BlockSpec((1, IN_TM, D_MODEL), lambda b, i: (b, i, 0)),
            pl.BlockSpec((1, D_MODEL), lambda b, i: (0, 0)),
            pl.BlockSpec(w_in.shape, lambda b, i: (0, 0)),
            pl.BlockSpec(cdft.shape, lambda b, i: (0, 0)),
        ],
        out_specs=[
            pl.BlockSpec((1, 2, IN_TM, FOURIER_WIDTH), lambda b, i: (b, 0, i, 0)),
            pl.BlockSpec((SSM_LANE_BLOCKS, IN_TM, LANES), lambda b, i: (0, b * nt + i, 0)),
            pl.BlockSpec((SSM_GROUPS, IN_TM // SSM_CHUNK, GROUP_COLS), lambda b, i: (0, b * nt + i, 0)),
            pl.BlockSpec((IN_TM, 2 * D_MODEL), lambda b, i: (b * nt + i, 0)),
        ],
        out_shape=[
            jax.ShapeDtypeStruct((BATCH, 2, SEQ, FOURIER_WIDTH), BF16),
            jax.ShapeDtypeStruct((SSM_LANE_BLOCKS, TOKENS, LANES), BF16),
            jax.ShapeDtypeStruct((SSM_GROUPS, TOKENS // SSM_CHUNK, GROUP_COLS), BF16),
            jax.ShapeDtypeStruct((TOKENS, 2 * D_MODEL), BF16),
        ],
        scratch_shapes=[pltpu.VMEM((SSM_LANE_BLOCKS, IN_TM, LANES), F32)],
        compiler_params=_cparams(("parallel", "parallel")),
        name="inproj",
    )(x, g, w_in, cdft)


DFT_R1 = 8
DFT_R2 = SEQ // DFT_R1
DFT_LANES = 2 * LANES
DFT_ROWS = 16


def _cmul_const(z, w):
    re, im = z
    if abs(w.imag) < 1e-12:
        return (re, im) if abs(w.real - 1.0) < 1e-12 else (re * w.real, im * w.real)
    if abs(w.real) < 1e-12:
        return (im, -re) if abs(w.imag + 1.0) < 1e-12 else (-im * w.imag, re * w.imag)
    return re * w.real - im * w.imag, re * w.imag + im * w.real


def _fft_blocks(xs):
    n = len(xs)
    if n == 1:
        return xs
    even, odd = _fft_blocks(xs[0::2]), _fft_blocks(xs[1::2])
    out = [None] * n
    for k in range(n // 2):
        tr, ti = _cmul_const(odd[k], np.exp(-2j * np.pi * k / n))
        out[k] = (even[k][0] + tr, even[k][1] + ti)
        out[k + n // 2] = (even[k][0] - tr, even[k][1] - ti)
    return out


def _dft_kernel(v_ref, tw_ref, f_ref, o_ref, a_scr, o_scr):
    def tile(i, c):
        r0 = pl.multiple_of(i * DFT_ROWS, DFT_ROWS)
        for slab in range(DFT_LANES // LANES):
            lanes = slice(slab * LANES, (slab + 1) * LANES)
            xs = [(v_ref[0, 0, pl.ds(s1 * DFT_R2 + r0, DFT_ROWS), lanes].astype(F32),
                   v_ref[0, 1, pl.ds(s1 * DFT_R2 + r0, DFT_ROWS), lanes].astype(F32)) for s1 in range(DFT_R1)]
            for t1, (ar, ai) in enumerate(_fft_blocks(xs)):
                tr, ti = tw_ref[0, t1, pl.ds(r0, DFT_ROWS), :], tw_ref[1, t1, pl.ds(r0, DFT_ROWS), :]
                a_scr[t1, pl.ds(r0, DFT_ROWS), lanes] = (ar * tr - ai * ti).astype(BF16)
                a_scr[t1, pl.ds(DFT_R2 + r0, DFT_ROWS), lanes] = (ar * ti + ai * tr).astype(BF16)
        return c

    lax.fori_loop(0, DFT_R2 // DFT_ROWS, tile, 0)

    f2 = f_ref[...].astype(BF16)
    for t1 in range(DFT_R1):
        r = jnp.dot(f2, a_scr[t1], preferred_element_type=F32)
        for slab in range(DFT_LANES // LANES):
            o_scr[slab, pl.ds(t1, DFT_R2, stride=DFT_R1), :] = r[:, slab * LANES:(slab + 1) * LANES]
    for slab in range(DFT_LANES // LANES):
        o_ref[:, slab * LANES:(slab + 1) * LANES] = o_scr[slab].astype(BF16)


def _dft(v, tw, f2):
    nh = FOURIER_WIDTH // DFT_LANES
    return pl.pallas_call(
        _dft_kernel,
        grid=(BATCH, nh),
        in_specs=[
            pl.BlockSpec((1, 2, SEQ, DFT_LANES), lambda b, h: (b, 0, 0, h)),
            pl.BlockSpec(tw.shape, lambda b, h: (0, 0, 0, 0)),
            pl.BlockSpec(f2.shape, lambda b, h: (0, 0)),
        ],
        out_specs=pl.BlockSpec((SEQ, DFT_LANES), lambda b, h: (b, h)),
        out_shape=jax.ShapeDtypeStruct((TOKENS, FOURIER_WIDTH), BF16),
        scratch_shapes=[pltpu.VMEM((DFT_R1, 2 * DFT_R2, DFT_LANES), BF16),
                        pltpu.VMEM((DFT_LANES // LANES, SEQ, LANES), F32)],
        compiler_params=_cparams(("parallel", "parallel")),
        name="dft",
    )(v, tw, f2)


def _dft_constants():
    t1 = np.arange(DFT_R1)
    s2 = np.arange(DFT_R2)
    ang = 2.0 * np.pi * np.outer(t1, s2) / SEQ
    scale = 1.0 / math.sqrt(SEQ)
    tw = np.stack([np.cos(ang) * scale, -np.sin(ang) * scale])
    tw = np.repeat(tw[..., None], LANES, axis=-1)
    ang2 = 2.0 * np.pi * np.outer(s2, s2) / DFT_R2
    f2 = np.concatenate([np.cos(ang2), np.sin(ang2)], axis=1)
    kc = np.arange(FOURIER_GROUP_CH)
    angc = 2.0 * np.pi * np.outer(kc, kc) / FOURIER_GROUP_CH
    cs = 1.0 / math.sqrt(FOURIER_GROUP_CH)
    cdft = np.concatenate([np.cos(angc) * cs, -np.sin(angc) * cs], axis=1)
    return tuple(jnp.asarray(v, F32) for v in (tw, f2, cdft))


GROUP_BLOCKS = SSM_GROUPS // GROUPS_PER_BLOCK
DIR_STATE = 2 * SSM_STATE
STATE_LANES = SSM_GROUPS * DIR_STATE


def _ssm_operators(a_re, a_im, log_dt, b_re, b_im, c_re, c_im):
    q_len = SSM_CHUNK
    hi = lax.Precision.HIGHEST
    dt = jnp.exp(log_dt)[..., None]
    lr, li = a_re * dt, a_im * dt
    steps = jnp.arange(q_len + 1, dtype=F32)
    mag = jnp.exp(lr[..., None] * steps)
    ang = li[..., None] * steps
    pr, pi = mag * jnp.cos(ang), mag * jnp.sin(ang)
    ar, ai = pr[..., 1], pi[..., 1]
    den = a_re * a_re + a_im * a_im
    cr = ((ar - 1.0) * a_re + ai * a_im) / den
    ci = (ai * a_re - (ar - 1.0) * a_im) / den
    bbr = cr[..., None] * b_re - ci[..., None] * b_im
    bbi = cr[..., None] * b_im + ci[..., None] * b_re

    rq = jnp.asarray(np.kron(np.eye(q_len), np.ones((1, SSM_GROUP_CH))), F32)
    rc = jnp.asarray(np.kron(np.ones((1, q_len)), np.eye(SSM_GROUP_CH)), F32)

    def per_group(x):
        return jnp.swapaxes(x, 0, 1).reshape(SSM_GROUPS, DIR_STATE, x.shape[-1])

    def on_cols(x, rep):
        return jnp.einsum('grk,kc->grc', per_group(x), rep, precision=hi)

    def state_operator(pw_r, pw_i, f_r, f_i, negate_im):
        p_r, p_i, q_r, q_i = on_cols(pw_r, rq), on_cols(pw_i, rq), on_cols(f_r, rc), on_cols(f_i, rc)
        w_im = p_r * q_i + p_i * q_r
        return jnp.concatenate([p_r * q_r - p_i * q_i, -w_im if negate_im else w_im], axis=1).astype(BF16)

    def both(p, fwd, bwd):
        return jnp.stack([p[0][..., fwd], p[1][..., bwd]])

    rev = slice(q_len - 1, None, -1)
    wsgt = state_operator(both(pr, rev, slice(0, q_len)), both(pi, rev, slice(0, q_len)), bbr, bbi, False)
    ctr, cti = jnp.swapaxes(c_re, -1, -2), jnp.swapaxes(c_im, -1, -2)
    wog = state_operator(both(pr, slice(1, None), slice(q_len, 0, -1)), both(pi, slice(1, None), slice(q_len, 0, -1)),
                         ctr, cti, True)

    prq, piq = pr[..., :q_len], pi[..., :q_len]
    cpr = jnp.einsum('dgcn,dgnt->dgtcn', c_re, prq) - jnp.einsum('dgcn,dgnt->dgtcn', c_im, piq)
    cpi = jnp.einsum('dgcn,dgnt->dgtcn', c_re, piq) + jnp.einsum('dgcn,dgnt->dgtcn', c_im, prq)
    kern = jnp.einsum('dgtcn,dgne->dgtec', cpr, bbr) - jnp.einsum('dgtcn,dgne->dgtec', cpi, bbi)
    lags = jnp.concatenate([kern[1][:, :0:-1], kern[0][:, :1] + kern[1][:, :1], kern[0][:, 1:]], axis=1)
    lag_cols = jnp.einsum('gtek,kc->gtec', lags, rc, precision=hi)
    qi = np.arange(q_len)
    sel = (qi[None, None, :] - qi[:, None, None] + q_len - 1 == np.arange(2 * q_len - 1)[None, :, None])
    sel = jnp.asarray(np.repeat(sel, SSM_GROUP_CH, axis=2), F32)
    mg = jnp.einsum('ptc,gtec->gpec', sel, lag_cols, precision=hi)
    mg = mg.reshape(SSM_GROUPS, GROUP_COLS, GROUP_COLS).astype(BF16)

    aq = jnp.stack([per_group(pr[..., q_len:]), per_group(pi[..., q_len:])])
    return mg, wsgt, wog, aq.reshape(2, STATE_LANES)


def _ssm_state_kernel(x_ref, ws_ref, flip_ref, sre_ref, sim_ref):
    backward = (lax.broadcasted_iota(jnp.int32, (N_CHUNKS, DIR_STATE), 1) & SSM_STATE) != 0
    nt = (((1,), (1,)), ((), ()))
    for g in range(GROUPS_PER_BLOCK):
        lanes = slice(g * DIR_STATE, (g + 1) * DIR_STATE)
        for b in range(BATCH):
            x = x_ref[g, b * N_CHUNKS:(b + 1) * N_CHUNKS, :]
            x_rev = jnp.dot(flip_ref[...], x, preferred_element_type=F32).astype(BF16)
            s = lax.dot_general(x, ws_ref[g], nt, preferred_element_type=F32)
            s_rev = lax.dot_general(x_rev, ws_ref[g], nt, preferred_element_type=F32)
            sre_ref[b, :, lanes] = jnp.where(backward, s_rev[:, :DIR_STATE], s[:, :DIR_STATE])
            sim_ref[b, :, lanes] = jnp.where(backward, s_rev[:, DIR_STATE:], s[:, DIR_STATE:])


def _ssm_states(xg, wsgt, flip):
    out_spec = pl.BlockSpec((BATCH, N_CHUNKS, GROUPS_PER_BLOCK * DIR_STATE), lambda j: (0, 0, j))
    shape = jax.ShapeDtypeStruct((BATCH, N_CHUNKS, STATE_LANES), F32)
    return pl.pallas_call(
        _ssm_state_kernel,
        grid=(GROUP_BLOCKS,),
        in_specs=[
            pl.BlockSpec((GROUPS_PER_BLOCK, BATCH * N_CHUNKS, GROUP_COLS), lambda j: (j, 0, 0)),
            pl.BlockSpec((GROUPS_PER_BLOCK, 2 * DIR_STATE, GROUP_COLS), lambda j: (j, 0, 0)),
            pl.BlockSpec(flip.shape, lambda j: (0, 0)),
        ],
        out_specs=[out_spec, out_spec],
        out_shape=[shape, shape],
        compiler_params=_cparams(("parallel",)),
        name="ssm_states",
    )(xg, wsgt, flip)


def _ssm_scan_kernel(sre_ref, sim_ref, aq_ref, hre_ref, him_ref):
    ar, ai = aq_ref[0:1], aq_ref[1:2]

    def body(k, carry):
        hr, hi = carry
        hre_ref[0, pl.ds(k, 1), :] = hr
        him_ref[0, pl.ds(k, 1), :] = hi
        sr, si = sre_ref[0, pl.ds(k, 1), :], sim_ref[0, pl.ds(k, 1), :]
        return ar * hr - ai * hi + sr, ar * hi + ai * hr + si

    z = jnp.zeros((1, STATE_LANES), F32)
    lax.fori_loop(0, N_CHUNKS, body, (z, z))


def _ssm_scan(sre, sim, aq):
    spec = pl.BlockSpec((1, N_CHUNKS, STATE_LANES), lambda b: (b, 0, 0))
    shape = jax.ShapeDtypeStruct(sre.shape, F32)
    return pl.pallas_call(
        _ssm_scan_kernel,
        grid=(BATCH,),
        in_specs=[spec, spec, pl.BlockSpec(aq.shape, lambda b: (0, 0))],
        out_specs=[spec, spec],
        out_shape=[shape, shape],
        compiler_params=_cparams(("parallel",)),
        name="ssm_scan",
    )(sre, sim, aq)


def _ssm_out_kernel(x_ref, m_ref, wo_ref, flip_ref, hre_ref, him_ref, y_ref, y_scr):
    fwd = (lax.broadcasted_iota(jnp.int32, (N_CHUNKS, GROUPS_PER_BLOCK * DIR_STATE), 1) & SSM_STATE) == 0

    def in_chunk_order(h_ref):
        h = h_ref[0].astype(BF16)
        return jnp.where(fwd, h, jnp.dot(flip_ref[...], h, preferred_element_type=F32).astype(BF16))

    h_re, h_im = in_chunk_order(hre_ref), in_chunk_order(him_ref)
    accs = []
    for g in range(GROUPS_PER_BLOCK):
        lanes = slice(g * DIR_STATE, (g + 1) * DIR_STATE)
        h = jnp.concatenate([h_re[:, lanes], h_im[:, lanes]], axis=1)
        accs.append(jnp.dot(x_ref[g, 0], m_ref[g], preferred_element_type=F32)
                    + jnp.dot(h, wo_ref[g], preferred_element_type=F32))
    seg = lax.broadcasted_iota(jnp.int32, (N_CHUNKS, LANES), 1) >> GROUP_SHIFT_CH
    for q in range(SSM_CHUNK):
        half, ql = divmod(q, GROUPS_PER_BLOCK)
        piece = jnp.zeros((N_CHUNKS, LANES), F32)
        for g in range(GROUPS_PER_BLOCK):
            src = accs[g][:, half * LANES:(half + 1) * LANES]
            shift = ((g - ql) % GROUPS_PER_BLOCK) * SSM_GROUP_CH
            moved = src if shift == 0 else pltpu.roll(src, shift, axis=1)
            piece = jnp.where(seg == g, moved, piece)
        y_scr[pl.ds(q, N_CHUNKS, stride=SSM_CHUNK), :] = piece
    y_ref[0] = y_scr[...].astype(BF16)


def _ssm_out(xg, mg, wog, flip, hre, him):
    hspec = pl.BlockSpec((1, N_CHUNKS, GROUPS_PER_BLOCK * DIR_STATE), lambda j, b: (b, 0, j))
    wspec = pl.BlockSpec((GROUPS_PER_BLOCK, GROUP_COLS, GROUP_COLS), lambda j, b: (j, 0, 0))
    return pl.pallas_call(
        _ssm_out_kernel,
        grid=(GROUP_BLOCKS, BATCH),
        in_specs=[
            pl.BlockSpec((GROUPS_PER_BLOCK, 1, N_CHUNKS, GROUP_COLS), lambda j, b: (j, b, 0, 0)),
            wspec, wspec, pl.BlockSpec(flip.shape, lambda j, b: (0, 0)), hspec, hspec,
        ],
        out_specs=pl.BlockSpec((1, SEQ, LANES), lambda j, b: (j, b, 0)),
        out_shape=jax.ShapeDtypeStruct((SSM_LANE_BLOCKS, TOKENS, LANES), BF16),
        scratch_shapes=[pltpu.VMEM((SEQ, LANES), F32)],
        compiler_params=_cparams(("parallel", "parallel")),
        name="ssm_out",
    )(xg.reshape(SSM_GROUPS, BATCH, N_CHUNKS, GROUP_COLS), mg, wog, flip, hre, him)


MERGE_TM = 512
GELU_C = math.sqrt(2.0 / math.pi)
PACK_SUB = D_MODEL // LANES


def _split_bf16(v):
    hi = v.astype(BF16)
    lo = (v - hi.astype(F32)).astype(BF16)
    return hi, lo


def _pack_rows(v, out_ref):
    for s in range(PACK_SUB):
        out_ref[pl.ds(s, v.shape[0], stride=PACK_SUB), :] = v[:, s * LANES:(s + 1) * LANES]


def _unpack_rows(buf_ref, start, rows):
    return jnp.concatenate([buf_ref[pl.ds(start + s, rows, stride=PACK_SUB), :] for s in range(PACK_SUB)], axis=1)


def _merge_kernel(x_ref, fm_ref, yc_ref, us_ref, gates_ref, dskip_ref, wf_ref, wglu_ref, ws_ref, wo_ref,
                  ng_ref, wrh_ref, wrl_ref, rb_ref, x1_ref, hp_ref, logit_ref):
    conv = jnp.concatenate([yc_ref[j].astype(F32) for j in range(SSM_LANE_BLOCKS)], axis=-1)
    u = jnp.concatenate([us_ref[j].astype(F32) for j in range(SSM_LANE_BLOCKS)], axis=-1)
    y = conv + dskip_ref[...] * u
    y = 0.5 * y * (1.0 + jnp.tanh(GELU_C * (y + 0.044715 * (y * y * y))))
    glu = jax.nn.sigmoid(jnp.dot(y.astype(BF16), wglu_ref[...], preferred_element_type=F32))
    y_s = jnp.dot((y * glu).astype(BF16), ws_ref[...], preferred_element_type=F32)
    y_f = jnp.dot(fm_ref[...], wf_ref[...], preferred_element_type=F32)
    merged = (gates_ref[:, :D_MODEL].astype(F32) * y_f + gates_ref[:, D_MODEL:].astype(F32) * y_s)
    x1 = x_ref[...] + jnp.dot(merged.astype(BF16), wo_ref[...], preferred_element_type=F32)
    x1_ref[...] = x1
    inv = lax.rsqrt(jnp.mean(x1 * x1, axis=-1, keepdims=True) + RMS_EPS)
    hn = x1 * inv * ng_ref[...]
    _pack_rows(hn, hp_ref)
    hi, lo = _split_bf16(hn)
    logits = (jnp.dot(hi, wrh_ref[...], preferred_element_type=F32)
              + jnp.dot(lo, wrh_ref[...], preferred_element_type=F32)
              + jnp.dot(hi, wrl_ref[...], preferred_element_type=F32))
    logit_ref[...] = logits + rb_ref[...]


def _merge(x, fmix, yconv, us, gates, dskip, wf, wglu, ws, wo, ng, wrh, wrl, rb):
    tm = MERGE_TM
    full = lambda a: pl.BlockSpec(a.shape, lambda i: (0,) * a.ndim)
    return pl.pallas_call(
        _merge_kernel,
        grid=(TOKENS // tm,),
        in_specs=[
            pl.BlockSpec((tm, D_MODEL), lambda i: (i, 0)),
            pl.BlockSpec((tm, FOURIER_WIDTH), lambda i: (i, 0)),
            pl.BlockSpec((SSM_LANE_BLOCKS, tm, LANES), lambda i: (0, i, 0)),
            pl.BlockSpec((SSM_LANE_BLOCKS, tm, LANES), lambda i: (0, i, 0)),
            pl.BlockSpec((tm, 2 * D_MODEL), lambda i: (i, 0)),
            full(dskip), full(wf), full(wglu), full(ws), full(wo), full(ng), full(wrh), full(wrl), full(rb),
        ],
        out_specs=[
            pl.BlockSpec((tm, D_MODEL), lambda i: (i, 0)),
            pl.BlockSpec((tm * PACK_SUB, LANES), lambda i: (i, 0)),
            pl.BlockSpec((tm, ROUTER_COLS), lambda i: (i, 0)),
        ],
        out_shape=[
            jax.ShapeDtypeStruct((TOKENS, D_MODEL), F32),
            jax.ShapeDtypeStruct((TOKENS * PACK_SUB, LANES), F32),
            jax.ShapeDtypeStruct((TOKENS, ROUTER_COLS), F32),
        ],
        compiler_params=_cparams(("parallel",)),
        name="merge",
    )(x, fmix, yconv, us, gates, dskip, wf, wglu, ws, wo, ng, wrh, wrl, rb)


ROUTE_TM = 512
EXPERT_LANE0 = MOE_GROUPS
INFO_EXPERT, INFO_RANK, INFO_GATE = 0, 2, 4
INFO_FIELDS = 8


def _route_kernel(lg_ref, info_ref, fields_ref, cnt_ref, carry):
    @pl.when(pl.program_id(0) == 0)
    def _():
        carry[...] = jnp.zeros_like(carry)

    lg = lg_ref[...]
    tm = lg.shape[0]
    col_i = lax.broadcasted_iota(jnp.int32, lg.shape, 1)
    col = col_i.astype(F32)
    neg = jnp.float32(-jnp.inf)
    none = jnp.float32(ROUTER_COLS)

    def row_max(v):
        return jnp.max(v, axis=-1, keepdims=True)

    def first_at(v, m):
        return jnp.min(jnp.where(v == m, col, none), axis=-1, keepdims=True)

    gl = jnp.where(col_i < MOE_GROUPS, lg, neg)
    gmax = row_max(gl)
    p_g = 1.0 / jnp.sum(jnp.exp(gl - gmax), axis=-1, keepdims=True)
    lo = EXPERT_LANE0 + first_at(gl, gmax) * EXPERTS_PER_GROUP
    el = jnp.where((col >= lo) & (col < lo + EXPERTS_PER_GROUP), lg, neg)
    l1 = row_max(el)
    i1 = first_at(el, l1)
    el2 = jnp.where(col == i1, neg, el)
    l2 = row_max(el2)
    i2 = first_at(el2, l2)
    r = jnp.exp(l2 - l1)
    w1 = p_g / (1.0 + r)
    w2 = w1 * r

    hit1, hit2 = col == i1, col == i2
    onehot = jnp.where(hit1 | hit2, 1.0, 0.0)
    earlier = lax.broadcasted_iota(jnp.int32, (tm, tm), 0) > lax.broadcasted_iota(jnp.int32, (tm, tm), 1)
    before = jnp.dot(jnp.where(earlier, 1.0, 0.0).astype(BF16), onehot.astype(BF16),
                     preferred_element_type=F32) + carry[...]
    rank1 = jnp.sum(jnp.where(hit1, before, 0.0), axis=-1, keepdims=True)
    rank2 = jnp.sum(jnp.where(hit2, before, 0.0), axis=-1, keepdims=True)
    carry[...] += jnp.sum(onehot, axis=0, keepdims=True)
    cnt_ref[...] = carry[...]

    info = jnp.zeros(lg.shape, F32)
    for lane, v in ((INFO_EXPERT, i1 - EXPERT_LANE0), (INFO_EXPERT + 1, i2 - EXPERT_LANE0), (INFO_RANK, rank1),
                    (INFO_RANK + 1, rank2), (INFO_GATE, w1), (INFO_GATE + 1, w2)):
        info = jnp.where(col_i == lane, v, info)
    info_ref[...] = info
    fields_ref[...] = info.T[:INFO_FIELDS]


def _route(logits):
    return pl.pallas_call(
        _route_kernel,
        grid=(TOKENS // ROUTE_TM,),
        in_specs=[pl.BlockSpec((ROUTE_TM, ROUTER_COLS), lambda i: (i, 0))],
        out_specs=[pl.BlockSpec((ROUTE_TM, ROUTER_COLS), lambda i: (i, 0)),
                   pl.BlockSpec((INFO_FIELDS, ROUTE_TM), lambda i: (0, i)),
                   pl.BlockSpec((1, ROUTER_COLS), lambda i: (0, 0))],
        out_shape=[jax.ShapeDtypeStruct((TOKENS, ROUTER_COLS), F32),
                   jax.ShapeDtypeStruct((INFO_FIELDS, TOKENS), F32),
                   jax.ShapeDtypeStruct((1, ROUTER_COLS), F32)],
        scratch_shapes=[pltpu.VMEM((1, ROUTER_COLS), F32)],
        compiler_params=_cparams(("arbitrary",)),
        name="route",
    )(logits)


def _dispatch_plan(fields, counts):
    expert = fields[INFO_EXPERT:INFO_EXPERT + MOE_TOP_K].astype(jnp.int32)
    rank = fields[INFO_RANK:INFO_RANK + MOE_TOP_K].astype(jnp.int32)
    cnt = counts[0, EXPERT_LANE0:EXPERT_LANE0 + N_EXPERTS].astype(jnp.int32)
    padded = ((cnt + MOE_ROWS - 1) // MOE_ROWS) * MOE_ROWS
    pends = jnp.cumsum(padded)
    pstarts = pends - padded
    ids = jnp.arange(N_EXPERTS, dtype=jnp.int32)
    dest = rank + jnp.sum(jnp.where(expert[..., None] == ids, pstarts, 0), axis=-1)
    n_used = pends[-1] // MOE_ROWS
    blocks = jnp.arange(MOE_BLOCKS, dtype=jnp.int32)
    block_e = jnp.sum((pends[None, :] <= (blocks * MOE_ROWS)[:, None]).astype(jnp.int32), axis=1)
    block_e = jnp.minimum(block_e, N_EXPERTS - 1)
    last_e = jnp.sum(jnp.where(blocks == n_used - 1, block_e, 0))
    used = (cnt > 0).astype(jnp.int32)
    ordinal = jnp.cumsum(used) - used
    n_experts_used = jnp.sum(used)
    by_ordinal = jnp.sum(jnp.where((ordinal[None, :] == ids[:, None]) & (used[None, :] > 0), ids[None, :], 0), axis=1)
    block_ord = jnp.sum(jnp.where(block_e[:, None] == ids[None, :], ordinal[None, :], 0), axis=1)
    block_first = (blocks * MOE_ROWS == jnp.sum(jnp.where(block_e[:, None] == ids[None, :], pstarts[None, :], 0), axis=1))
    block_first = (block_first & (blocks < n_used)).astype(jnp.int32)
    meta = jnp.concatenate([n_used.reshape(1), n_experts_used.reshape(1)]).astype(jnp.int32)
    return dest.reshape(MOE_TOP_K * TOKENS), block_ord.astype(jnp.int32), block_first, by_ordinal.astype(jnp.int32), meta, pends


MOE_SLOT_ROWS = MOE_ROWS * PACK_SUB
INVERT_UNROLL = 8


WEIGHT_SLOTS = 3
GATHER_SLOTS = 3
WEIGHT_DMA_PRIORITY = 1


def _moe_kernel(dest_ref, ord_ref, first_ref, eo_ref, meta_ref, pend_ref, hp_ref, wg_hbm, wu_hbm, wd_hbm, y_ref,
                xbuf, wg_buf, wu_buf, wd_buf, tok_ref, sem, wsem):
    i = pl.program_id(0)
    n_used = meta_ref[0]
    n_experts_used = meta_ref[1]

    def slot_rows(slot):
        return xbuf.at[pl.ds(pl.multiple_of(slot * MOE_SLOT_ROWS, MOE_SLOT_ROWS), MOE_SLOT_ROWS), :]

    def weight_copies(ordinal):
        e = eo_ref[ordinal]
        ws = ordinal % WEIGHT_SLOTS
        return [pltpu.make_async_copy(hbm.at[e], buf.at[ws], wsem.at[ws])
                for hbm, buf in ((wg_hbm, wg_buf), (wu_hbm, wu_buf), (wd_hbm, wd_buf))]

    def invert_dispatch():
        def fill_expert(e, c):
            first = jnp.maximum(pend_ref[e] - MOE_ROWS, 0)

            def fill(r, c2):
                tok_ref[first + r] = (first + r) & (TOKENS - 1)
                return c2

            lax.fori_loop(0, MOE_ROWS, fill, 0, unroll=INVERT_UNROLL)
            return c

        lax.fori_loop(0, N_EXPERTS, fill_expert, 0)

        def place(t, c):
            for k in range(MOE_TOP_K):
                tok_ref[dest_ref[k * TOKENS + t]] = t
            return c

        lax.fori_loop(0, TOKENS, place, 0, unroll=INVERT_UNROLL)

    def gather(block, slot):
        base = block * MOE_ROWS
        for r in range(MOE_ROWS):
            src = hp_ref.at[pl.ds(pl.multiple_of(tok_ref[base + r] * PACK_SUB, PACK_SUB), PACK_SUB), :]
            dst = xbuf.at[pl.ds(pl.multiple_of(slot * MOE_SLOT_ROWS + r * PACK_SUB, PACK_SUB), PACK_SUB), :]
            pltpu.make_async_copy(src, dst, sem.at[slot]).start()

    @pl.when(i == 0)
    def _():
        for ahead in range(WEIGHT_SLOTS - 1):
            @pl.when(ahead < n_experts_used)
            def _():
                for cp in weight_copies(ahead):
                    cp.start(priority=WEIGHT_DMA_PRIORITY)
        invert_dispatch()
        for ahead in range(GATHER_SLOTS - 1):
            gather(jnp.minimum(ahead, n_used - 1), ahead)

    @pl.when(i < n_used)
    def _():
        ordinal = ord_ref[i]

        @pl.when(first_ref[i] == 1)
        def _():
            for cp in weight_copies(ordinal):
                cp.wait()

            @pl.when(ordinal + WEIGHT_SLOTS - 1 < n_experts_used)
            def _():
                for cp in weight_copies(ordinal + WEIGHT_SLOTS - 1):
                    cp.start(priority=WEIGHT_DMA_PRIORITY)

        slot = i % GATHER_SLOTS
        ws = ordinal % WEIGHT_SLOTS
        pltpu.make_async_copy(slot_rows(slot), slot_rows(slot), sem.at[slot]).wait()
        ahead = i + GATHER_SLOTS - 1
        xb = _unpack_rows(xbuf, slot * MOE_SLOT_ROWS, MOE_ROWS).astype(BF16)
        gather(jnp.minimum(ahead, n_used - 1), ahead % GATHER_SLOTS)
        a = jnp.dot(xb, wg_buf[ws].astype(BF16), preferred_element_type=F32)
        u = jnp.dot(xb, wu_buf[ws].astype(BF16), preferred_element_type=F32)
        act = (a * jax.nn.sigmoid(a) * u).astype(BF16)
        y = jnp.dot(act, wd_buf[ws].astype(BF16), preferred_element_type=F32)
        _pack_rows(y, y_ref)

        @pl.when(i == n_used - 1)
        def _():
            for extra in range(1, GATHER_SLOTS):
                other = (i + extra) % GATHER_SLOTS
                pltpu.make_async_copy(slot_rows(other), slot_rows(other), sem.at[other]).wait()

    @pl.when(i >= n_used)
    def _():
        y_ref[...] = jnp.zeros_like(y_ref)


def _moe(dest, block_ord, block_first, by_ordinal, meta, pends, hp, w_gate, w_up, w_down):
    hbm = pl.BlockSpec(memory_space=pl.ANY)
    grid_spec = pltpu.PrefetchScalarGridSpec(
        num_scalar_prefetch=6,
        grid=(MOE_BLOCKS,),
        in_specs=[hbm, hbm, hbm, hbm],
        out_specs=pl.BlockSpec((MOE_SLOT_ROWS, LANES), lambda i, *_: (i, 0)),
        scratch_shapes=[pltpu.VMEM((GATHER_SLOTS * MOE_SLOT_ROWS, LANES), F32),
                        pltpu.VMEM((WEIGHT_SLOTS, D_MODEL, D_EXPERT), F32),
                        pltpu.VMEM((WEIGHT_SLOTS, D_MODEL, D_EXPERT), F32),
                        pltpu.VMEM((WEIGHT_SLOTS, D_EXPERT, D_MODEL), F32),
                        pltpu.SMEM((MOE_BLOCKS * MOE_ROWS,), jnp.int32),
                        pltpu.SemaphoreType.DMA((GATHER_SLOTS,)), pltpu.SemaphoreType.DMA((WEIGHT_SLOTS,))],
    )
    return pl.pallas_call(
        _moe_kernel,
        grid_spec=grid_spec,
        out_shape=jax.ShapeDtypeStruct((MOE_BLOCKS * MOE_SLOT_ROWS, LANES), F32),
        compiler_params=_cparams(("arbitrary",)),
        name="moe",
    )(dest, block_ord, block_first, by_ordinal, meta, pends, hp, w_gate, w_up, w_down)


COMBINE_TM = 256


def _combine_kernel(dest_ref, x1_ref, info_ref, g_ref, y_ref, o_ref, ybuf, sem):
    i = pl.program_id(0)
    last = pl.num_programs(0) - 1
    k_rows = COMBINE_TM * PACK_SUB
    slot_rows = MOE_TOP_K * k_rows

    def slot_ref(slot):
        return ybuf.at[pl.ds(pl.multiple_of(slot * slot_rows, slot_rows), slot_rows), :]

    def gather(tile, slot):
        for r in range(COMBINE_TM):
            for k in range(MOE_TOP_K):
                row = dest_ref[k * TOKENS + tile * COMBINE_TM + r]
                src = y_ref.at[pl.ds(pl.multiple_of(row * PACK_SUB, PACK_SUB), PACK_SUB), :]
                at = slot * slot_rows + k * k_rows + r * PACK_SUB
                dst = ybuf.at[pl.ds(pl.multiple_of(at, PACK_SUB), PACK_SUB), :]
                pltpu.make_async_copy(src, dst, sem.at[slot]).start(priority=k)

    @pl.when(i == 0)
    def _():
        for ahead in range(GATHER_SLOTS - 1):
            gather(ahead, ahead)

    slot = i % GATHER_SLOTS
    pltpu.make_async_copy(slot_ref(slot), slot_ref(slot), sem.at[slot]).wait()
    ahead = i + GATHER_SLOTS - 1
    gather(jnp.minimum(ahead, last), ahead % GATHER_SLOTS)
    x2 = x1_ref[...]
    for k in range(MOE_TOP_K):
        yk = _unpack_rows(ybuf, slot * slot_rows + k * k_rows, COMBINE_TM)
        x2 = x2 + info_ref[:, INFO_GATE + k:INFO_GATE + k + 1] * yk
    inv = lax.rsqrt(jnp.mean(x2 * x2, axis=-1, keepdims=True) + RMS_EPS)
    o_ref[...] = x2 * inv * g_ref[...]

    @pl.when(i == last)
    def _():
        for extra in range(1, GATHER_SLOTS):
            other = (i + extra) % GATHER_SLOTS
            pltpu.make_async_copy(slot_ref(other), slot_ref(other), sem.at[other]).wait()


def _combine(dest, x1, info, g, y_pad):
    tm = COMBINE_TM
    grid_spec = pltpu.PrefetchScalarGridSpec(
        num_scalar_prefetch=1,
        grid=(TOKENS // tm,),
        in_specs=[
            pl.BlockSpec((tm, D_MODEL), lambda i, d: (i, 0)),
            pl.BlockSpec((tm, ROUTER_COLS), lambda i, d: (i, 0)),
            pl.BlockSpec((1, D_MODEL), lambda i, d: (0, 0)),
            pl.BlockSpec(memory_space=pl.ANY),
        ],
        out_specs=pl.BlockSpec((tm, D_MODEL), lambda i, d: (i, 0)),
        scratch_shapes=[pltpu.VMEM((GATHER_SLOTS * MOE_TOP_K * tm * PACK_SUB, LANES), F32),
                        pltpu.SemaphoreType.DMA((GATHER_SLOTS,))],
    )
    return pl.pallas_call(
        _combine_kernel,
        grid_spec=grid_spec,
        out_shape=jax.ShapeDtypeStruct((TOKENS, D_MODEL), F32),
        compiler_params=_cparams(("arbitrary",)),
        name="combine",
    )(dest, x1, info, g, y_pad)


def kernel(x, mix_norm_g, w_in, w_fourier_out, ssm_A_re, ssm_A_im, ssm_log_dt, ssm_B_re, ssm_B_im, ssm_C_re,
           ssm_C_im, ssm_D, ssm_w_glu, w_ssm_out, w_out, ffn_norm_g, router_group_w, router_group_b,
           router_expert_w, router_expert_b, expert_w_gate, expert_w_up, expert_w_down, final_norm_g):
    assert x.shape == (BATCH, SEQ, D_MODEL) and w_in.shape[0] == 1
    tw, f2, cdft = _dft_constants()

    vf, us, xg, gates = _inproj(x, mix_norm_g[0][None], w_in[0].astype(BF16), cdft)
    fmix = _dft(vf, tw, f2)

    mg, wsgt, wog, aq = _ssm_operators(ssm_A_re[0], ssm_A_im[0], ssm_log_dt[0], ssm_B_re[0], ssm_B_im[0],
                                       ssm_C_re[0], ssm_C_im[0])
    flip = jnp.asarray(np.eye(N_CHUNKS)[::-1], BF16)
    sre, sim = _ssm_states(xg, wsgt, flip)
    yconv = _ssm_out(xg, mg, wog, flip, *_ssm_scan(sre, sim, aq))

    w_router = jnp.concatenate([router_group_w[0], router_expert_w[0]], axis=1)
    w_router = jnp.pad(w_router, ((0, 0), (0, ROUTER_COLS - w_router.shape[1])))
    b_router = jnp.concatenate([router_group_b[0], router_expert_b[0]])
    b_router = jnp.pad(b_router, (0, ROUTER_COLS - b_router.shape[0]))[None]
    wr_hi = w_router.astype(BF16)
    wr_lo = (w_router - wr_hi.astype(F32)).astype(BF16)
    x1, hp, logits = _merge(x.reshape(TOKENS, D_MODEL), fmix, yconv, us, gates, ssm_D[0][None],
                            w_fourier_out[0].astype(BF16), ssm_w_glu[0].astype(BF16), w_ssm_out[0].astype(BF16),
                            w_out[0].astype(BF16), ffn_norm_g[0][None], wr_hi, wr_lo, b_router)

    info, fields, counts = _route(logits)
    dest, block_ord, block_first, by_ordinal, meta, pends = _dispatch_plan(fields, counts)
    y_pad = _moe(dest, block_ord, block_first, by_ordinal, meta, pends, hp, expert_w_gate[0], expert_w_up[0],
                 expert_w_down[0])
    out = _combine(dest, x1, info, final_norm_g[None], y_pad)
    return out.reshape(BATCH, SEQ, D_MODEL)
```

```python
import math

import numpy as np
import jax
import jax.numpy as jnp
from jax import lax
from jax.experimental import pallas as pl
from jax.experimental.pallas import tpu as pltpu

F32 = jnp.float32
BF16 = jnp.bfloat16

D_MODEL = 1024
BATCH = 4
SEQ = 4096
TOKENS = BATCH * SEQ
FOURIER_WIDTH = 512
FOURIER_GROUP_CH = 128
FOURIER_GROUPS = 4
SSM_WIDTH = 512
SSM_GROUP_CH = 16
SSM_GROUPS = 32
SSM_STATE = 64
MOE_GROUPS = 8
EXPERTS_PER_GROUP = 8
N_EXPERTS = 64
MOE_TOP_K = 2
D_EXPERT = 512
RMS_EPS = 1e-6

LANES = 128
SSM_CHUNK = 16
SSM_LANE_BLOCKS = SSM_WIDTH // LANES
GROUPS_PER_BLOCK = LANES // SSM_GROUP_CH
N_CHUNKS = SEQ // SSM_CHUNK
GROUP_COLS = SSM_CHUNK * SSM_GROUP_CH
GROUP_SHIFT_CH = 4
MOE_ROWS = 256
MOE_BLOCKS = TOKENS * MOE_TOP_K // MOE_ROWS + N_EXPERTS
ROUTER_COLS = 128
VMEM_LIMIT = 48 * 1024 * 1024


def _cparams(sem, vmem=VMEM_LIMIT):
    return pltpu.CompilerParams(dimension_semantics=sem, vmem_limit_bytes=vmem)


IN_TM = 512


def _inproj_kernel(x_ref, g_ref, w_ref, cdft_ref, vf_ref, us_ref, xg_ref, gates_ref, zs_scr):
    x = x_ref[0]
    inv = lax.rsqrt(jnp.mean(x * x, axis=-1, keepdims=True) + RMS_EPS)
    h = (x * inv * g_ref[...]).astype(BF16)
    zf = jnp.dot(h, w_ref[:, 0:FOURIER_WIDTH], preferred_element_type=F32).astype(BF16)
    cdft = cdft_ref[...].astype(BF16)
    for g in range(FOURIER_GROUPS):
        sl = slice(g * LANES, (g + 1) * LANES)
        v = jnp.dot(zf[:, sl], cdft, preferred_element_type=F32)
        vf_ref[0, 0, :, sl] = v[:, :LANES].astype(BF16)
        vf_ref[0, 1, :, sl] = v[:, LANES:].astype(BF16)
    zs = jnp.dot(h, w_ref[:, FOURIER_WIDTH:FOURIER_WIDTH + SSM_WIDTH], preferred_element_type=F32)
    for j in range(SSM_LANE_BLOCKS):
        us_ref[j] = zs[:, j * LANES:(j + 1) * LANES].astype(BF16)
        zs_scr[j] = zs[:, j * LANES:(j + 1) * LANES]
    seg = lax.broadcasted_iota(jnp.int32, (IN_TM // SSM_CHUNK, LANES), 1) >> GROUP_SHIFT_CH
    for j in range(SSM_LANE_BLOCKS):
        pieces = [zs_scr[j, pl.ds(q, IN_TM // SSM_CHUNK, stride=SSM_CHUNK), :] for q in range(SSM_CHUNK)]
        for g in range(GROUPS_PER_BLOCK):
            for half in range(SSM_CHUNK // GROUPS_PER_BLOCK):
                acc = jnp.zeros((IN_TM // SSM_CHUNK, LANES), F32)
                for ql in range(GROUPS_PER_BLOCK):
                    shift = ((ql - g) % GROUPS_PER_BLOCK) * SSM_GROUP_CH
                    piece = pieces[half * GROUPS_PER_BLOCK + ql]
                    moved = piece if shift == 0 else pltpu.roll(piece, shift, axis=1)
                    acc = jnp.where(seg == ql, moved, acc)
                xg_ref[j * GROUPS_PER_BLOCK + g, :, half * LANES:(half + 1) * LANES] = acc.astype(BF16)
    base = FOURIER_WIDTH + SSM_WIDTH
    for n in range(4):
        zg = jnp.dot(h, w_ref[:, base + n * 512: base + (n + 1) * 512], preferred_element_type=F32)
        gates_ref[:, n * 512:(n + 1) * 512] = jax.nn.sigmoid(zg).astype(BF16)


def _inproj(x, g, w_in, cdft):
    nt = SEQ // IN_TM
    return pl.pallas_call(
        _inproj_kernel,
        grid=(BATCH, nt),
        in_specs=[
            pl.BlockSpec((1, IN_TM, D_MODEL), lambda b, i: (b, i, 0)),
            pl.BlockSpec((1, D_MODEL), lambda b, i: (0, 0)),
            pl.BlockSpec(w_in.shape, lambda b, i: (0, 0)),
            pl.BlockSpec(cdft.shape, lambda b, i: (0, 0)),
        ],
        out_specs=[
            pl.BlockSpec((1, 2, IN_TM, FOURIER_WIDTH), lambda b, i: (b, 0, i, 0)),
            pl.BlockSpec((SSM_LANE_BLOCKS, IN_TM, LANES), lambda b, i: (0, b * nt + i, 0)),
            pl.BlockSpec((SSM_GROUPS, IN_TM // SSM_CHUNK, GROUP_COLS), lambda b, i: (0, b * nt + i, 0)),
            pl.BlockSpec((IN_TM, 2 * D_MODEL), lambda b, i: (b * nt + i, 0)),
        ],
        out_shape=[
            jax.ShapeDtypeStruct((BATCH, 2, SEQ, FOURIER_WIDTH), BF16),
            jax.ShapeDtypeStruct((SSM_LANE_BLOCKS, TOKENS, LANES), BF16),
            jax.ShapeDtypeStruct((SSM_GROUPS, TOKENS // SSM_CHUNK, GROUP_COLS), BF16),
            jax.ShapeDtypeStruct((TOKENS, 2 * D_MODEL), BF16),
        ],
        scratch_shapes=[pltpu.VMEM((SSM_LANE_BLOCKS, IN_TM, LANES), F32)],
        compiler_params=_cparams(("parallel", "parallel")),
        name="inproj",
    )(x, g, w_in, cdft)


DFT_R1 = 8
DFT_R2 = SEQ // DFT_R1
DFT_LANES = 2 * LANES
DFT_ROWS = 16


def _cmul_const(z, w):
    re, im = z
    if abs(w.imag) < 1e-12:
        return (re, im) if abs(w.real - 1.0) < 1e-12 else (re * w.real, im * w.real)
    if abs(w.real) < 1e-12:
        return (im, -re) if abs(w.imag + 1.0) < 1e-12 else (-im * w.imag, re * w.imag)
    return re * w.real - im * w.imag, re * w.imag + im * w.real


def _fft_blocks(xs):
    n = len(xs)
    if n == 1:
        return xs
    even, odd = _fft_blocks(xs[0::2]), _fft_blocks(xs[1::2])
    out = [None] * n
    for k in range(n // 2):
        tr, ti = _cmul_const(odd[k], np.exp(-2j * np.pi * k / n))
        out[k] = (even[k][0] + tr, even[k][1] + ti)
        out[k + n // 2] = (even[k][0] - tr, even[k][1] - ti)
    return out


def _dft_kernel(v_ref, tw_ref, f_ref, o_ref, a_scr, o_scr):
    def tile(i, c):
        r0 = pl.multiple_of(i * DFT_ROWS, DFT_ROWS)
        for slab in range(DFT_LANES // LANES):
            lanes = slice(slab * LANES, (slab + 1) * LANES)
            xs = [(v_ref[0, 0, pl.ds(s1 * DFT_R2 + r0, DFT_ROWS), lanes].astype(F32),
                   v_ref[0, 1, pl.ds(s1 * DFT_R2 + r0, DFT_ROWS), lanes].astype(F32)) for s1 in range(DFT_R1)]
            for t1, (ar, ai) in enumerate(_fft_blocks(xs)):
                tr, ti = tw_ref[0, t1, pl.ds(r0, DFT_ROWS), :], tw_ref[1, t1, pl.ds(r0, DFT_ROWS), :]
                a_scr[t1, pl.ds(r0, DFT_ROWS), lanes] = (ar * tr - ai * ti).astype(BF16)
                a_scr[t1, pl.ds(DFT_R2 + r0, DFT_ROWS), lanes] = (ar * ti + ai * tr).astype(BF16)
        return c

    lax.fori_loop(0, DFT_R2 // DFT_ROWS, tile, 0)

    f2 = f_ref[...].astype(BF16)
    for t1 in range(DFT_R1):
        r = jnp.dot(f2, a_scr[t1], preferred_element_type=F32)
        for slab in range(DFT_LANES // LANES):
            o_scr[slab, pl.ds(t1, DFT_R2, stride=DFT_R1), :] = r[:, slab * LANES:(slab + 1) * LANES]
    for slab in range(DFT_LANES // LANES):
        o_ref[:, slab * LANES:(slab + 1) * LANES] = o_scr[slab].astype(BF16)


def _dft(v, tw, f2):
    nh = FOURIER_WIDTH // DFT_LANES
    return pl.pallas_call(
        _dft_kernel,
        grid=(BATCH, nh),
        in_specs=[
            pl.BlockSpec((1, 2, SEQ, DFT_LANES), lambda b, h: (b, 0, 0, h)),
            pl.BlockSpec(tw.shape, lambda b, h: (0, 0, 0, 0)),
            pl.BlockSpec(f2.shape, lambda b, h: (0, 0)),
        ],
        out_specs=pl.BlockSpec((SEQ, DFT_LANES), lambda b, h: (b, h)),
        out_shape=jax.ShapeDtypeStruct((TOKENS, FOURIER_WIDTH), BF16),
        scratch_shapes=[pltpu.VMEM((DFT_R1, 2 * DFT_R2, DFT_LANES), BF16),
                        pltpu.VMEM((DFT_LANES // LANES, SEQ, LANES), F32)],
        compiler_params=_cparams(("parallel", "parallel")),
        name="dft",
    )(v, tw, f2)


def _dft_constants():
    t1 = np.arange(DFT_R1)
    s2 = np.arange(DFT_R2)
    ang = 2.0 * np.pi * np.outer(t1, s2) / SEQ
    scale = 1.0 / math.sqrt(SEQ)
    tw = np.stack([np.cos(ang) * scale, -np.sin(ang) * scale])
    tw = np.repeat(tw[..., None], LANES, axis=-1)
    ang2 = 2.0 * np.pi * np.outer(s2, s2) / DFT_R2
    f2 = np.concatenate([np.cos(ang2), np.sin(ang2)], axis=1)
    kc = np.arange(FOURIER_GROUP_CH)
    angc = 2.0 * np.pi * np.outer(kc, kc) / FOURIER_GROUP_CH
    cs = 1.0 / math.sqrt(FOURIER_GROUP_CH)
    cdft = np.concatenate([np.cos(angc) * cs, -np.sin(angc) * cs], axis=1)
    return tuple(jnp.asarray(v, F32) for v in (tw, f2, cdft))


GROUP_BLOCKS = SSM_GROUPS // GROUPS_PER_BLOCK
DIR_STATE = 2 * SSM_STATE
STATE_LANES = SSM_GROUPS * DIR_STATE


def _ssm_operators(a_re, a_im, log_dt, b_re, b_im, c_re, c_im):
    q_len = SSM_CHUNK
    hi = lax.Precision.HIGHEST
    dt = jnp.exp(log_dt)[..., None]
    lr, li = a_re * dt, a_im * dt
    steps = jnp.arange(q_len + 1, dtype=F32)
    mag = jnp.exp(lr[..., None] * steps)
    ang = li[..., None] * steps
    pr, pi = mag * jnp.cos(ang), mag * jnp.sin(ang)
    ar, ai = pr[..., 1], pi[..., 1]
    den = a_re * a_re + a_im * a_im
    cr = ((ar - 1.0) * a_re + ai * a_im) / den
    ci = (ai * a_re - (ar - 1.0) * a_im) / den
    bbr = cr[..., None] * b_re - ci[..., None] * b_im
    bbi = cr[..., None] * b_im + ci[..., None] * b_re

    rq = jnp.asarray(np.kron(np.eye(q_len), np.ones((1, SSM_GROUP_CH))), F32)
    rc = jnp.asarray(np.kron(np.ones((1, q_len)), np.eye(SSM_GROUP_CH)), F32)

    def per_group(x):
        return jnp.swapaxes(x, 0, 1).reshape(SSM_GROUPS, DIR_STATE, x.shape[-1])

    def on_cols(x, rep):
        return jnp.einsum('grk,kc->grc', per_group(x), rep, precision=hi)

    def state_operator(pw_r, pw_i, f_r, f_i, negate_im):
        p_r, p_i, q_r, q_i = on_cols(pw_r, rq), on_cols(pw_i, rq), on_cols(f_r, rc), on_cols(f_i, rc)
        w_im = p_r * q_i + p_i * q_r
        return jnp.concatenate([p_r * q_r - p_i * q_i, -w_im if negate_im else w_im], axis=1).astype(BF16)

    def both(p, fwd, bwd):
        return jnp.stack([p[0][..., fwd], p[1][..., bwd]])

    rev = slice(q_len - 1, None, -1)
    wsgt = state_operator(both(pr, rev, slice(0, q_len)), both(pi, rev, slice(0, q_len)), bbr, bbi, False)
    ctr, cti = jnp.swapaxes(c_re, -1, -2), jnp.swapaxes(c_im, -1, -2)
    wog = state_operator(both(pr, slice(1, None), slice(q_len, 0, -1)), both(pi, slice(1, None), slice(q_len, 0, -1)),
                         ctr, cti, True)

    prq, piq = pr[..., :q_len], pi[..., :q_len]
    cpr = jnp.einsum('dgcn,dgnt->dgtcn', c_re, prq) - jnp.einsum('dgcn,dgnt->dgtcn', c_im, piq)
    cpi = jnp.einsum('dgcn,dgnt->dgtcn', c_re, piq) + jnp.einsum('dgcn,dgnt->dgtcn', c_im, prq)
    kern = jnp.einsum('dgtcn,dgne->dgtec', cpr, bbr) - jnp.einsum('dgtcn,dgne->dgtec', cpi, bbi)
    lags = jnp.concatenate([kern[1][:, :0:-1], kern[0][:, :1] + kern[1][:, :1], kern[0][:, 1:]], axis=1)
    lag_cols = jnp.einsum('gtek,kc->gtec', lags, rc, precision=hi)
    lane_q = jnp.arange(GROUP_COLS, dtype=jnp.int32) // SSM_GROUP_CH
    rows = []
    for qp in range(q_len):
        acc = jnp.zeros(lag_cols[:, 0].shape, F32)
        for q in range(q_len):
            acc = jnp.where(lane_q == q, lag_cols[:, q - qp + q_len - 1], acc)
        rows.append(acc)
    mg = jnp.stack(rows, axis=1)
    mg = mg.reshape(SSM_GROUPS, GROUP_COLS, GROUP_COLS).astype(BF16)

    aq = jnp.stack([per_group(pr[..., q_len:]), per_group(pi[..., q_len:])])
    return mg, wsgt, wog, aq.reshape(2, STATE_LANES)


def _ssm_state_kernel(x_ref, ws_ref, flip_ref, sre_ref, sim_ref):
    backward = (lax.broadcasted_iota(jnp.int32, (N_CHUNKS, DIR_STATE), 1) & SSM_STATE) != 0
    nt = (((1,), (1,)), ((), ()))
    for g in range(GROUPS_PER_BLOCK):
        lanes = slice(g * DIR_STATE, (g + 1) * DIR_STATE)
        for b in range(BATCH):
            x = x_ref[g, b * N_CHUNKS:(b + 1) * N_CHUNKS, :]
            x_rev = jnp.dot(flip_ref[...], x, preferred_element_type=F32).astype(BF16)
            s = lax.dot_general(x, ws_ref[g], nt, preferred_element_type=F32)
            s_rev = lax.dot_general(x_rev, ws_ref[g], nt, preferred_element_type=F32)
            sre_ref[b, :, lanes] = jnp.where(backward, s_rev[:, :DIR_STATE], s[:, :DIR_STATE])
            sim_ref[b, :, lanes] = jnp.where(backward, s_rev[:, DIR_STATE:], s[:, DIR_STATE:])


def _ssm_states(xg, wsgt, flip):
    out_spec = pl.BlockSpec((BATCH, N_CHUNKS, GROUPS_PER_BLOCK * DIR_STATE), lambda j: (0, 0, j))
    shape = jax.ShapeDtypeStruct((BATCH, N_CHUNKS, STATE_LANES), F32)
    return pl.pallas_call(
        _ssm_state_kernel,
        grid=(GROUP_BLOCKS,),
        in_specs=[
            pl.BlockSpec((GROUPS_PER_BLOCK, BATCH * N_CHUNKS, GROUP_COLS), lambda j: (j, 0, 0)),
            pl.BlockSpec((GROUPS_PER_BLOCK, 2 * DIR_STATE, GROUP_COLS), lambda j: (j, 0, 0)),
            pl.BlockSpec(flip.shape, lambda j: (0, 0)),
        ],
        out_specs=[out_spec, out_spec],
        out_shape=[shape, shape],
        compiler_params=_cparams(("parallel",)),
        name="ssm_states",
    )(xg, wsgt, flip)


def _ssm_scan_kernel(sre_ref, sim_ref, aq_ref, hre_ref, him_ref):
    ar, ai = aq_ref[0:1], aq_ref[1:2]

    def body(k, carry):
        hr, hi = carry
        hre_ref[0, pl.ds(k, 1), :] = hr
        him_ref[0, pl.ds(k, 1), :] = hi
        sr, si = sre_ref[0, pl.ds(k, 1), :], sim_ref[0, pl.ds(k, 1), :]
        return ar * hr - ai * hi + sr, ar * hi + ai * hr + si

    z = jnp.zeros((1, STATE_LANES), F32)
    lax.fori_loop(0, N_CHUNKS, body, (z, z))


def _ssm_scan(sre, sim, aq):
    spec = pl.BlockSpec((1, N_CHUNKS, STATE_LANES), lambda b: (b, 0, 0))
    shape = jax.ShapeDtypeStruct(sre.shape, F32)
    return pl.pallas_call(
        _ssm_scan_kernel,
        grid=(BATCH,),
        in_specs=[spec, spec, pl.BlockSpec(aq.shape, lambda b: (0, 0))],
        out_specs=[spec, spec],
        out_shape=[shape, shape],
        compiler_params=_cparams(("parallel",)),
        name="ssm_scan",
    )(sre, sim, aq)


def _ssm_out_kernel(x_ref, m_ref, wo_ref, flip_ref, hre_ref, him_ref, y_ref, y_scr):
    fwd = (lax.broadcasted_iota(jnp.int32, (N_CHUNKS, GROUPS_PER_BLOCK * DIR_STATE), 1) & SSM_STATE) == 0

    def in_chunk_order(h_ref):
        h = h_ref[0].astype(BF16)
        return jnp.where(fwd, h, jnp.dot(flip_ref[...], h, preferred_element_type=F32).astype(BF16))

    h_re, h_im = in_chunk_order(hre_ref), in_chunk_order(him_ref)
    accs = []
    for g in range(GROUPS_PER_BLOCK):
        lanes = slice(g * DIR_STATE, (g + 1) * DIR_STATE)
        h = jnp.concatenate([h_re[:, lanes], h_im[:, lanes]], axis=1)
        accs.append(jnp.dot(x_ref[g, 0], m_ref[g], preferred_element_type=F32)
                    + jnp.dot(h, wo_ref[g], preferred_element_type=F32))
    seg = lax.broadcasted_iota(jnp.int32, (N_CHUNKS, LANES), 1) >> GROUP_SHIFT_CH
    for q in range(SSM_CHUNK):
        half, ql = divmod(q, GROUPS_PER_BLOCK)
        piece = jnp.zeros((N_CHUNKS, LANES), F32)
        for g in range(GROUPS_PER_BLOCK):
            src = accs[g][:, half * LANES:(half + 1) * LANES]
            shift = ((g - ql) % GROUPS_PER_BLOCK) * SSM_GROUP_CH
            moved = src if shift == 0 else pltpu.roll(src, shift, axis=1)
            piece = jnp.where(seg == g, moved, piece)
        y_scr[pl.ds(q, N_CHUNKS, stride=SSM_CHUNK), :] = piece
    y_ref[0] = y_scr[...].astype(BF16)


def _ssm_out(xg, mg, wog, flip, hre, him):
    hspec = pl.BlockSpec((1, N_CHUNKS, GROUPS_PER_BLOCK * DIR_STATE), lambda j, b: (b, 0, j))
    wspec = pl.BlockSpec((GROUPS_PER_BLOCK, GROUP_COLS, GROUP_COLS), lambda j, b: (j, 0, 0))
    return pl.pallas_call(
        _ssm_out_kernel,
        grid=(GROUP_BLOCKS, BATCH),
        in_specs=[
            pl.BlockSpec((GROUPS_PER_BLOCK, 1, N_CHUNKS, GROUP_COLS), lambda j, b: (j, b, 0, 0)),
            wspec, wspec, pl.BlockSpec(flip.shape, lambda j, b: (0, 0)), hspec, hspec,
        ],
        out_specs=pl.BlockSpec((1, SEQ, LANES), lambda j, b: (j, b, 0)),
        out_shape=jax.ShapeDtypeStruct((SSM_LANE_BLOCKS, TOKENS, LANES), BF16),
        scratch_shapes=[pltpu.VMEM((SEQ, LANES), F32)],
        compiler_params=_cparams(("parallel", "parallel")),
        name="ssm_out",
    )(xg.reshape(SSM_GROUPS, BATCH, N_CHUNKS, GROUP_COLS), mg, wog, flip, hre, him)


MERGE_TM = 512
GELU_C = math.sqrt(2.0 / math.pi)
PACK_SUB = D_MODEL // LANES


def _split_bf16(v):
    hi = v.astype(BF16)
    lo = (v - hi.astype(F32)).astype(BF16)
    return hi, lo


def _pack_rows(v, out_ref):
    for s in range(PACK_SUB):
        out_ref[pl.ds(s, v.shape[0], stride=PACK_SUB), :] = v[:, s * LANES:(s + 1) * LANES]


def _unpack_rows(buf_ref, start, rows):
    return jnp.concatenate([buf_ref[pl.ds(start + s, rows, stride=PACK_SUB), :] for s in range(PACK_SUB)], axis=1)


def _merge_kernel(x_ref, fm_ref, yc_ref, us_ref, gates_ref, dskip_ref, wf_ref, wglu_ref, ws_ref, wo_ref,
                  ng_ref, wrh_ref, wrl_ref, rb_ref, x1_ref, hp_ref, logit_ref):
    conv = jnp.concatenate([yc_ref[j].astype(F32) for j in range(SSM_LANE_BLOCKS)], axis=-1)
    u = jnp.concatenate([us_ref[j].astype(F32) for j in range(SSM_LANE_BLOCKS)], axis=-1)
    y = conv + dskip_ref[...] * u
    y = 0.5 * y * (1.0 + jnp.tanh(GELU_C * (y + 0.044715 * (y * y * y))))
    glu = jax.nn.sigmoid(jnp.dot(y.astype(BF16), wglu_ref[...], preferred_element_type=F32))
    y_s = jnp.dot((y * glu).astype(BF16), ws_ref[...], preferred_element_type=F32)
    y_f = jnp.dot(fm_ref[...], wf_ref[...], preferred_element_type=F32)
    merged = (gates_ref[:, :D_MODEL].astype(F32) * y_f + gates_ref[:, D_MODEL:].astype(F32) * y_s)
    x1 = x_ref[...] + jnp.dot(merged.astype(BF16), wo_ref[...], preferred_element_type=F32)
    x1_ref[...] = x1
    inv = lax.rsqrt(jnp.mean(x1 * x1, axis=-1, keepdims=True) + RMS_EPS)
    hn = x1 * inv * ng_ref[...]
    _pack_rows(hn, hp_ref)
    hi, lo = _split_bf16(hn)
    logits = (jnp.dot(hi, wrh_ref[...], preferred_element_type=F32)
              + jnp.dot(lo, wrh_ref[...], preferred_element_type=F32)
              + jnp.dot(hi, wrl_ref[...], preferred_element_type=F32))
    logit_ref[...] = logits + rb_ref[...]


def _merge(x, fmix, yconv, us, gates, dskip, wf, wglu, ws, wo, ng, wrh, wrl, rb):
    tm = MERGE_TM
    full = lambda a: pl.BlockSpec(a.shape, lambda i: (0,) * a.ndim)
    return pl.pallas_call(
        _merge_kernel,
        grid=(TOKENS // tm,),
        in_specs=[
            pl.BlockSpec((tm, D_MODEL), lambda i: (i, 0)),
            pl.BlockSpec((tm, FOURIER_WIDTH), lambda i: (i, 0)),
            pl.BlockSpec((SSM_LANE_BLOCKS, tm, LANES), lambda i: (0, i, 0)),
            pl.BlockSpec((SSM_LANE_BLOCKS, tm, LANES), lambda i: (0, i, 0)),
            pl.BlockSpec((tm, 2 * D_MODEL), lambda i: (i, 0)),
            full(dskip), full(wf), full(wglu), full(ws), full(wo), full(ng), full(wrh), full(wrl), full(rb),
        ],
        out_specs=[
            pl.BlockSpec((tm, D_MODEL), lambda i: (i, 0)),
            pl.BlockSpec((tm * PACK_SUB, LANES), lambda i: (i, 0)),
            pl.BlockSpec((tm, ROUTER_COLS), lambda i: (i, 0)),
        ],
        out_shape=[
            jax.ShapeDtypeStruct((TOKENS, D_MODEL), F32),
            jax.ShapeDtypeStruct((TOKENS * PACK_SUB, LANES), F32),
            jax.ShapeDtypeStruct((TOKENS, ROUTER_COLS), F32),
        ],
        compiler_params=_cparams(("parallel",)),
        name="merge",
    )(x, fmix, yconv, us, gates, dskip, wf, wglu, ws, wo, ng, wrh, wrl, rb)


ROUTE_TM = 512
EXPERT_LANE0 = MOE_GROUPS
INFO_EXPERT, INFO_RANK, INFO_GATE = 0, 2, 4
INFO_FIELDS = 8


def _route_kernel(lg_ref, info_ref, fields_ref, cnt_ref, carry):
    @pl.when(pl.program_id(0) == 0)
    def _():
        carry[...] = jnp.zeros_like(carry)

    lg = lg_ref[...]
    tm = lg.shape[0]
    col_i = lax.broadcasted_iota(jnp.int32, lg.shape, 1)
    col = col_i.astype(F32)
    neg = jnp.float32(-jnp.inf)
    none = jnp.float32(ROUTER_COLS)

    def row_max(v):
        return jnp.max(v, axis=-1, keepdims=True)

    def first_at(v, m):
        return jnp.min(jnp.where(v == m, col, none), axis=-1, keepdims=True)

    gl = jnp.where(col_i < MOE_GROUPS, lg, neg)
    gmax = row_max(gl)
    p_g = 1.0 / jnp.sum(jnp.exp(gl - gmax), axis=-1, keepdims=True)
    lo = EXPERT_LANE0 + first_at(gl, gmax) * EXPERTS_PER_GROUP
    el = jnp.where((col >= lo) & (col < lo + EXPERTS_PER_GROUP), lg, neg)
    l1 = row_max(el)
    i1 = first_at(el, l1)
    el2 = jnp.where(col == i1, neg, el)
    l2 = row_max(el2)
    i2 = first_at(el2, l2)
    r = jnp.exp(l2 - l1)
    w1 = p_g / (1.0 + r)
    w2 = w1 * r

    hit1, hit2 = col == i1, col == i2
    onehot = jnp.where(hit1 | hit2, 1.0, 0.0)
    earlier = lax.broadcasted_iota(jnp.int32, (tm, tm), 0) > lax.broadcasted_iota(jnp.int32, (tm, tm), 1)
    before = jnp.dot(jnp.where(earlier, 1.0, 0.0).astype(BF16), onehot.astype(BF16),
                     preferred_element_type=F32) + carry[...]
    rank1 = jnp.sum(jnp.where(hit1, before, 0.0), axis=-1, keepdims=True)
    rank2 = jnp.sum(jnp.where(hit2, before, 0.0), axis=-1, keepdims=True)
    carry[...] += jnp.sum(onehot, axis=0, keepdims=True)
    cnt_ref[...] = carry[...]

    info = jnp.zeros(lg.shape, F32)
    for lane, v in ((INFO_EXPERT, i1 - EXPERT_LANE0), (INFO_EXPERT + 1, i2 - EXPERT_LANE0), (INFO_RANK, rank1),
                    (INFO_RANK + 1, rank2), (INFO_GATE, w1), (INFO_GATE + 1, w2)):
        info = jnp.where(col_i == lane, v, info)
    info_ref[...] = info
    fields_ref[...] = info.T[:INFO_FIELDS]


def _route(logits):
    return pl.pallas_call(
        _route_kernel,
        grid=(TOKENS // ROUTE_TM,),
        in_specs=[pl.BlockSpec((ROUTE_TM, ROUTER_COLS), lambda i: (i, 0))],
        out_specs=[pl.BlockSpec((ROUTE_TM, ROUTER_COLS), lambda i: (i, 0)),
                   pl.BlockSpec((INFO_FIELDS, ROUTE_TM), lambda i: (0, i)),
                   pl.BlockSpec((1, ROUTER_COLS), lambda i: (0, 0))],
        out_shape=[jax.ShapeDtypeStruct((TOKENS, ROUTER_COLS), F32),
                   jax.ShapeDtypeStruct((INFO_FIELDS, TOKENS), F32),
                   jax.ShapeDtypeStruct((1, ROUTER_COLS), F32)],
        scratch_shapes=[pltpu.VMEM((1, ROUTER_COLS), F32)],
        compiler_params=_cparams(("arbitrary",)),
        name="route",
    )(logits)


def _dispatch_plan(fields, counts):
    expert = fields[INFO_EXPERT:INFO_EXPERT + MOE_TOP_K].astype(jnp.int32)
    rank = fields[INFO_RANK:INFO_RANK + MOE_TOP_K].astype(jnp.int32)
    cnt = counts[0, EXPERT_LANE0:EXPERT_LANE0 + N_EXPERTS].astype(jnp.int32)
    padded = ((cnt + MOE_ROWS - 1) // MOE_ROWS) * MOE_ROWS
    pends = jnp.cumsum(padded)
    pstarts = pends - padded
    ids = jnp.arange(N_EXPERTS, dtype=jnp.int32)
    dest = rank + jnp.sum(jnp.where(expert[..., None] == ids, pstarts, 0), axis=-1)
    n_used = pends[-1] // MOE_ROWS
    blocks = jnp.arange(MOE_BLOCKS, dtype=jnp.int32)
    block_e = jnp.sum((pends[None, :] <= (blocks * MOE_ROWS)[:, None]).astype(jnp.int32), axis=1)
    block_e = jnp.minimum(block_e, N_EXPERTS - 1)
    last_e = jnp.sum(jnp.where(blocks == n_used - 1, block_e, 0))
    used = (cnt > 0).astype(jnp.int32)
    ordinal = jnp.cumsum(used) - used
    n_experts_used = jnp.sum(used)
    by_ordinal = jnp.sum(jnp.where((ordinal[None, :] == ids[:, None]) & (used[None, :] > 0), ids[None, :], 0), axis=1)
    block_ord = jnp.sum(jnp.where(block_e[:, None] == ids[None, :], ordinal[None, :], 0), axis=1)
    block_first = (blocks * MOE_ROWS == jnp.sum(jnp.where(block_e[:, None] == ids[None, :], pstarts[None, :], 0), axis=1))
    block_first = (block_first & (blocks < n_used)).astype(jnp.int32)
    meta = jnp.concatenate([n_used.reshape(1), n_experts_used.reshape(1)]).astype(jnp.int32)
    return dest.reshape(MOE_TOP_K * TOKENS), block_ord.astype(jnp.int32), block_first, by_ordinal.astype(jnp.int32), meta, pends


MOE_SLOT_ROWS = MOE_ROWS * PACK_SUB
INVERT_UNROLL = 8


WEIGHT_SLOTS = 3
GATHER_SLOTS = 3
WEIGHT_DMA_PRIORITY = 1


def _moe_kernel(dest_ref, ord_ref, first_ref, eo_ref, meta_ref, pend_ref, hp_ref, wg_hbm, wu_hbm, wd_hbm, y_ref,
                xbuf, wg_buf, wu_buf, wd_buf, tok_ref, sem, wsem):
    i = pl.program_id(0)
    n_used = meta_ref[0]
    n_experts_used = meta_ref[1]

    def slot_rows(slot):
        return xbuf.at[pl.ds(pl.multiple_of(slot * MOE_SLOT_ROWS, MOE_SLOT_ROWS), MOE_SLOT_ROWS), :]

    def weight_copies(ordinal):
        e = eo_ref[ordinal]
        ws = ordinal % WEIGHT_SLOTS
        return [pltpu.make_async_copy(hbm.at[e], buf.at[ws], wsem.at[ws])
                for hbm, buf in ((wg_hbm, wg_buf), (wu_hbm, wu_buf), (wd_hbm, wd_buf))]

    def invert_dispatch():
        def fill_expert(e, c):
            first = jnp.maximum(pend_ref[e] - MOE_ROWS, 0)

            def fill(r, c2):
                tok_ref[first + r] = (first + r) & (TOKENS - 1)
                return c2

            lax.fori_loop(0, MOE_ROWS, fill, 0, unroll=INVERT_UNROLL)
            return c

        lax.fori_loop(0, N_EXPERTS, fill_expert, 0)

        def place(t, c):
            for k in range(MOE_TOP_K):
                tok_ref[dest_ref[k * TOKENS + t]] = t
            return c

        lax.fori_loop(0, TOKENS, place, 0, unroll=INVERT_UNROLL)

    def gather(block, slot):
        base = block * MOE_ROWS
        for r in range(MOE_ROWS):
            src = hp_ref.at[pl.ds(pl.multiple_of(tok_ref[base + r] * PACK_SUB, PACK_SUB), PACK_SUB), :]
            dst = xbuf.at[pl.ds(pl.multiple_of(slot * MOE_SLOT_ROWS + r * PACK_SUB, PACK_SUB), PACK_SUB), :]
            pltpu.make_async_copy(src, dst, sem.at[slot]).start()

    @pl.when(i == 0)
    def _():
        for ahead in range(WEIGHT_SLOTS - 1):
            @pl.when(ahead < n_experts_used)
            def _():
                for cp in weight_copies(ahead):
                    cp.start(priority=WEIGHT_DMA_PRIORITY)
        invert_dispatch()
        for ahead in range(GATHER_SLOTS - 1):
            gather(jnp.minimum(ahead, n_used - 1), ahead)

    @pl.when(i < n_used)
    def _():
        ordinal = ord_ref[i]

        @pl.when(first_ref[i] == 1)
        def _():
            for cp in weight_copies(ordinal):
                cp.wait()

            @pl.when(ordinal + WEIGHT_SLOTS - 1 < n_experts_used)
            def _():
                for cp in weight_copies(ordinal + WEIGHT_SLOTS - 1):
                    cp.start(priority=WEIGHT_DMA_PRIORITY)

        slot = i % GATHER_SLOTS
        ws = ordinal % WEIGHT_SLOTS
        pltpu.make_async_copy(slot_rows(slot), slot_rows(slot), sem.at[slot]).wait()
        ahead = i + GATHER_SLOTS - 1
        xb = _unpack_rows(xbuf, slot * MOE_SLOT_ROWS, MOE_ROWS).astype(BF16)
        gather(jnp.minimum(ahead, n_used - 1), ahead % GATHER_SLOTS)
        a = jnp.dot(xb, wg_buf[ws].astype(BF16), preferred_element_type=F32)
        u = jnp.dot(xb, wu_buf[ws].astype(BF16), preferred_element_type=F32)
        act = (a * jax.nn.sigmoid(a) * u).astype(BF16)
        y = jnp.dot(act, wd_buf[ws].astype(BF16), preferred_element_type=F32)
        _pack_rows(y, y_ref)

        @pl.when(i == n_used - 1)
        def _():
            for extra in range(1, GATHER_SLOTS):
                other = (i + extra) % GATHER_SLOTS
                pltpu.make_async_copy(slot_rows(other), slot_rows(other), sem.at[other]).wait()

    @pl.when(i >= n_used)
    def _():
        y_ref[...] = jnp.zeros_like(y_ref)


def _moe(dest, block_ord, block_first, by_ordinal, meta, pends, hp, w_gate, w_up, w_down):
    hbm = pl.BlockSpec(memory_space=pl.ANY)
    grid_spec = pltpu.PrefetchScalarGridSpec(
        num_scalar_prefetch=6,
        grid=(MOE_BLOCKS,),
        in_specs=[hbm, hbm, hbm, hbm],
        out_specs=pl.BlockSpec((MOE_SLOT_ROWS, LANES), lambda i, *_: (i, 0)),
        scratch_shapes=[pltpu.VMEM((GATHER_SLOTS * MOE_SLOT_ROWS, LANES), F32),
                        pltpu.VMEM((WEIGHT_SLOTS, D_MODEL, D_EXPERT), F32),
                        pltpu.VMEM((WEIGHT_SLOTS, D_MODEL, D_EXPERT), F32),
                        pltpu.VMEM((WEIGHT_SLOTS, D_EXPERT, D_MODEL), F32),
                        pltpu.SMEM((MOE_BLOCKS * MOE_ROWS,), jnp.int32),
                        pltpu.SemaphoreType.DMA((GATHER_SLOTS,)), pltpu.SemaphoreType.DMA((WEIGHT_SLOTS,))],
    )
    return pl.pallas_call(
        _moe_kernel,
        grid_spec=grid_spec,
        out_shape=jax.ShapeDtypeStruct((MOE_BLOCKS * MOE_SLOT_ROWS, LANES), F32),
        compiler_params=_cparams(("arbitrary",)),
        name="moe",
    )(dest, block_ord, block_first, by_ordinal, meta, pends, hp, w_gate, w_up, w_down)


COMBINE_TM = 256


def _combine_kernel(dest_ref, x1_ref, info_ref, g_ref, y_ref, o_ref, ybuf, sem):
    i = pl.program_id(0)
    last = pl.num_programs(0) - 1
    k_rows = COMBINE_TM * PACK_SUB
    slot_rows = MOE_TOP_K * k_rows

    def slot_ref(slot):
        return ybuf.at[pl.ds(pl.multiple_of(slot * slot_rows, slot_rows), slot_rows), :]

    def gather(tile, slot):
        for r in range(COMBINE_TM):
            for k in range(MOE_TOP_K):
                row = dest_ref[k * TOKENS + tile * COMBINE_TM + r]
                src = y_ref.at[pl.ds(pl.multiple_of(row * PACK_SUB, PACK_SUB), PACK_SUB), :]
                at = slot * slot_rows + k * k_rows + r * PACK_SUB
                dst = ybuf.at[pl.ds(pl.multiple_of(at, PACK_SUB), PACK_SUB), :]
                pltpu.make_async_copy(src, dst, sem.at[slot]).start(priority=k)

    @pl.when(i == 0)
    def _():
        for ahead in range(GATHER_SLOTS - 1):
            gather(ahead, ahead)

    slot = i % GATHER_SLOTS
    pltpu.make_async_copy(slot_ref(slot), slot_ref(slot), sem.at[slot]).wait()
    ahead = i + GATHER_SLOTS - 1
    gather(jnp.minimum(ahead, last), ahead % GATHER_SLOTS)
    x2 = x1_ref[...]
    for k in range(MOE_TOP_K):
        yk = _unpack_rows(ybuf, slot * slot_rows + k * k_rows, COMBINE_TM)
        x2 = x2 + info_ref[:, INFO_GATE + k:INFO_GATE + k + 1] * yk
    inv = lax.rsqrt(jnp.mean(x2 * x2, axis=-1, keepdims=True) + RMS_EPS)
    o_ref[...] = x2 * inv * g_ref[...]

    @pl.when(i == last)
    def _():
        for extra in range(1, GATHER_SLOTS):
            other = (i + extra) % GATHER_SLOTS
            pltpu.make_async_copy(slot_ref(other), slot_ref(other), sem.at[other]).wait()


def _combine(dest, x1, info, g, y_pad):
    tm = COMBINE_TM
    grid_spec = pltpu.PrefetchScalarGridSpec(
        num_scalar_prefetch=1,
        grid=(TOKENS // tm,),
        in_specs=[
            pl.BlockSpec((tm, D_MODEL), lambda i, d: (i, 0)),
            pl.BlockSpec((tm, ROUTER_COLS), lambda i, d: (i, 0)),
            pl.BlockSpec((1, D_MODEL), lambda i, d: (0, 0)),
            pl.BlockSpec(memory_space=pl.ANY),
        ],
        out_specs=pl.BlockSpec((tm, D_MODEL), lambda i, d: (i, 0)),
        scratch_shapes=[pltpu.VMEM((GATHER_SLOTS * MOE_TOP_K * tm * PACK_SUB, LANES), F32),
                        pltpu.SemaphoreType.DMA((GATHER_SLOTS,))],
    )
    return pl.pallas_call(
        _combine_kernel,
        grid_spec=grid_spec,
        out_shape=jax.ShapeDtypeStruct((TOKENS, D_MODEL), F32),
        compiler_params=_cparams(("arbitrary",)),
        name="combine",
    )(dest, x1, info, g, y_pad)


def kernel(x, mix_norm_g, w_in, w_fourier_out, ssm_A_re, ssm_A_im, ssm_log_dt, ssm_B_re, ssm_B_im, ssm_C_re,
           ssm_C_im, ssm_D, ssm_w_glu, w_ssm_out, w_out, ffn_norm_g, router_group_w, router_group_b,
           router_expert_w, router_expert_b, expert_w_gate, expert_w_up, expert_w_down, final_norm_g):
    assert x.shape == (BATCH, SEQ, D_MODEL) and w_in.shape[0] == 1
    tw, f2, cdft = _dft_constants()

    vf, us, xg, gates = _inproj(x, mix_norm_g[0][None], w_in[0].astype(BF16), cdft)
    fmix = _dft(vf, tw, f2)

    mg, wsgt, wog, aq = _ssm_operators(ssm_A_re[0], ssm_A_im[0], ssm_log_dt[0], ssm_B_re[0], ssm_B_im[0],
                                       ssm_C_re[0], ssm_C_im[0])
    flip = jnp.asarray(np.eye(N_CHUNKS)[::-1], BF16)
    sre, sim = _ssm_states(xg, wsgt, flip)
    yconv = _ssm_out(xg, mg, wog, flip, *_ssm_scan(sre, sim, aq))

    w_router = jnp.concatenate([router_group_w[0], router_expert_w[0]], axis=1)
    w_router = jnp.pad(w_router, ((0, 0), (0, ROUTER_COLS - w_router.shape[1])))
    b_router = jnp.concatenate([router_group_b[0], router_expert_b[0]])
    b_router = jnp.pad(b_router, (0, ROUTER_COLS - b_router.shape[0]))[None]
    wr_hi = w_router.astype(BF16)
    wr_lo = (w_router - wr_hi.astype(F32)).astype(BF16)
    x1, hp, logits = _merge(x.reshape(TOKENS, D_MODEL), fmix, yconv, us, gates, ssm_D[0][None],
                            w_fourier_out[0].astype(BF16), ssm_w_glu[0].astype(BF16), w_ssm_out[0].astype(BF16),
                            w_out[0].astype(BF16), ffn_norm_g[0][None], wr_hi, wr_lo, b_router)

    info, fields, counts = _route(logits)
    dest, block_ord, block_first, by_ordinal, meta, pends = _dispatch_plan(fields, counts)
    y_pad = _moe(dest, block_ord, block_first, by_ordinal, meta, pends, hp, expert_w_gate[0], expert_w_up[0],
                 expert_w_down[0])
    out = _combine(dest, x1, info, final_norm_g[None], y_pad)
    return out.reshape(BATCH, SEQ, D_MODEL)
```

```python
import math

import numpy as np
import jax
import jax.numpy as jnp
from jax import lax
from jax.experimental import pallas as pl
from jax.experimental.pallas import tpu as pltpu

F32 = jnp.float32
BF16 = jnp.bfloat16

D_MODEL = 1024
BATCH = 4
SEQ = 4096
TOKENS = BATCH * SEQ
FOURIER_WIDTH = 512
FOURIER_GROUP_CH = 128
FOURIER_GROUPS = 4
SSM_WIDTH = 512
SSM_GROUP_CH = 16
SSM_GROUPS = 32
SSM_STATE = 64
MOE_GROUPS = 8
EXPERTS_PER_GROUP = 8
N_EXPERTS = 64
MOE_TOP_K = 2
D_EXPERT = 512
RMS_EPS = 1e-6

LANES = 128
SSM_CHUNK = 16
SSM_LANE_BLOCKS = SSM_WIDTH // LANES
GROUPS_PER_BLOCK = LANES // SSM_GROUP_CH
N_CHUNKS = SEQ // SSM_CHUNK
GROUP_COLS = SSM_CHUNK * SSM_GROUP_CH
GROUP_SHIFT_CH = 4
MOE_ROWS = 256
MOE_BLOCKS = TOKENS * MOE_TOP_K // MOE_ROWS + N_EXPERTS
ROUTER_COLS = 128
VMEM_LIMIT = 48 * 1024 * 1024


def _cparams(sem, vmem=VMEM_LIMIT):
    return pltpu.CompilerParams(dimension_semantics=sem, vmem_limit_bytes=vmem)


IN_TM = 512


def _inproj_kernel(x_ref, g_ref, w_ref, cdft_ref, vf_ref, us_ref, xg_ref, gates_ref, zs_scr):
    x = x_ref[0]
    inv = lax.rsqrt(jnp.mean(x * x, axis=-1, keepdims=True) + RMS_EPS)
    h = (x * inv * g_ref[...]).astype(BF16)
    zf = jnp.dot(h, w_ref[:, 0:FOURIER_WIDTH], preferred_element_type=F32).astype(BF16)
    cdft = cdft_ref[...].astype(BF16)
    for g in range(FOURIER_GROUPS):
        sl = slice(g * LANES, (g + 1) * LANES)
        v = jnp.dot(zf[:, sl], cdft, preferred_element_type=F32)
        vf_ref[0, 0, :, sl] = v[:, :LANES].astype(BF16)
        vf_ref[0, 1, :, sl] = v[:, LANES:].astype(BF16)
    zs = jnp.dot(h, w_ref[:, FOURIER_WIDTH:FOURIER_WIDTH + SSM_WIDTH], preferred_element_type=F32)
    for j in range(SSM_LANE_BLOCKS):
        us_ref[j] = zs[:, j * LANES:(j + 1) * LANES].astype(BF16)
        zs_scr[j] = zs[:, j * LANES:(j + 1) * LANES]
    seg = lax.broadcasted_iota(jnp.int32, (IN_TM // SSM_CHUNK, LANES), 1) >> GROUP_SHIFT_CH
    for j in range(SSM_LANE_BLOCKS):
        pieces = [zs_scr[j, pl.ds(q, IN_TM // SSM_CHUNK, stride=SSM_CHUNK), :] for q in range(SSM_CHUNK)]
        for g in range(GROUPS_PER_BLOCK):
            for half in range(SSM_CHUNK // GROUPS_PER_BLOCK):
                acc = jnp.zeros((IN_TM // SSM_CHUNK, LANES), F32)
                for ql in range(GROUPS_PER_BLOCK):
                    shift = ((ql - g) % GROUPS_PER_BLOCK) * SSM_GROUP_CH
                    piece = pieces[half * GROUPS_PER_BLOCK + ql]
                    moved = piece if shift == 0 else pltpu.roll(piece, shift, axis=1)
                    acc = jnp.where(seg == ql, moved, acc)
                xg_ref[j * GROUPS_PER_BLOCK + g, :, half * LANES:(half + 1) * LANES] = acc.astype(BF16)
    base = FOURIER_WIDTH + SSM_WIDTH
    for n in range(4):
        zg = jnp.dot(h, w_ref[:, base + n * 512: base + (n + 1) * 512], preferred_element_type=F32)
        gates_ref[:, n * 512:(n + 1) * 512] = jax.nn.sigmoid(zg).astype(BF16)


def _inproj(x, g, w_in, cdft):
    nt = SEQ // IN_TM
    return pl.pallas_call(
        _inproj_kernel,
        grid=(BATCH, nt),
        in_specs=[
            pl.BlockSpec((1, IN_TM, D_MODEL), lambda b, i: (b, i, 0)),
            pl.BlockSpec((1, D_MODEL), lambda b, i: (0, 0)),
            pl.BlockSpec(w_in.shape, lambda b, i: (0, 0)),
            pl.BlockSpec(cdft.shape, lambda b, i: (0, 0)),
        ],
        out_specs=[
            pl.BlockSpec((1, 2, IN_TM, FOURIER_WIDTH), lambda b, i: (b, 0, i, 0)),
            pl.BlockSpec((SSM_LANE_BLOCKS, IN_TM, LANES), lambda b, i: (0, b * nt + i, 0)),
            pl.BlockSpec((SSM_GROUPS, IN_TM // SSM_CHUNK, GROUP_COLS), lambda b, i: (0, b * nt + i, 0)),
            pl.BlockSpec((IN_TM, 2 * D_MODEL), lambda b, i: (b * nt + i, 0)),
        ],
        out_shape=[
            jax.ShapeDtypeStruct((BATCH, 2, SEQ, FOURIER_WIDTH), BF16),
            jax.ShapeDtypeStruct((SSM_LANE_BLOCKS, TOKENS, LANES), BF16),
            jax.ShapeDtypeStruct((SSM_GROUPS, TOKENS // SSM_CHUNK, GROUP_COLS), BF16),
            jax.ShapeDtypeStruct((TOKENS, 2 * D_MODEL), BF16),
        ],
        scratch_shapes=[pltpu.VMEM((SSM_LANE_BLOCKS, IN_TM, LANES), F32)],
        compiler_params=_cparams(("parallel", "parallel")),
        name="inproj",
    )(x, g, w_in, cdft)


DFT_R1 = 8
DFT_R2 = SEQ // DFT_R1
DFT_LANES = 2 * LANES
DFT_ROWS = 16


def _cmul_const(z, w):
    re, im = z
    if abs(w.imag) < 1e-12:
        return (re, im) if abs(w.real - 1.0) < 1e-12 else (re * w.real, im * w.real)
    if abs(w.real) < 1e-12:
        return (im, -re) if abs(w.imag + 1.0) < 1e-12 else (-im * w.imag, re * w.imag)
    return re * w.real - im * w.imag, re * w.imag + im * w.real


def _fft_blocks(xs):
    n = len(xs)
    if n == 1:
        return xs
    even, odd = _fft_blocks(xs[0::2]), _fft_blocks(xs[1::2])
    out = [None] * n
    for k in range(n // 2):
        tr, ti = _cmul_const(odd[k], np.exp(-2j * np.pi * k / n))
        out[k] = (even[k][0] + tr, even[k][1] + ti)
        out[k + n // 2] = (even[k][0] - tr, even[k][1] - ti)
    return out


def _dft_kernel(v_ref, tw_ref, f_ref, o_ref, a_scr, o_scr):
    def tile(i, c):
        r0 = pl.multiple_of(i * DFT_ROWS, DFT_ROWS)
        for slab in range(DFT_LANES // LANES):
            lanes = slice(slab * LANES, (slab + 1) * LANES)
            xs = [(v_ref[0, 0, pl.ds(s1 * DFT_R2 + r0, DFT_ROWS), lanes].astype(F32),
                   v_ref[0, 1, pl.ds(s1 * DFT_R2 + r0, DFT_ROWS), lanes].astype(F32)) for s1 in range(DFT_R1)]
            for t1, (ar, ai) in enumerate(_fft_blocks(xs)):
                tr, ti = tw_ref[0, t1, pl.ds(r0, DFT_ROWS), :], tw_ref[1, t1, pl.ds(r0, DFT_ROWS), :]
                a_scr[t1, pl.ds(r0, DFT_ROWS), lanes] = (ar * tr - ai * ti).astype(BF16)
                a_scr[t1, pl.ds(DFT_R2 + r0, DFT_ROWS), lanes] = (ar * ti + ai * tr).astype(BF16)
        return c

    lax.fori_loop(0, DFT_R2 // DFT_ROWS, tile, 0)

    f2 = f_ref[...].astype(BF16)
    for t1 in range(DFT_R1):
        r = jnp.dot(f2, a_scr[t1], preferred_element_type=F32)
        for slab in range(DFT_LANES // LANES):
            o_scr[slab, pl.ds(t1, DFT_R2, stride=DFT_R1), :] = r[:, slab * LANES:(slab + 1) * LANES]
    for slab in range(DFT_LANES // LANES):
        o_ref[:, slab * LANES:(slab + 1) * LANES] = o_scr[slab].astype(BF16)


def _dft(v, tw, f2):
    nh = FOURIER_WIDTH // DFT_LANES
    return pl.pallas_call(
        _dft_kernel,
        grid=(BATCH, nh),
        in_specs=[
            pl.BlockSpec((1, 2, SEQ, DFT_LANES), lambda b, h: (b, 0, 0, h)),
            pl.BlockSpec(tw.shape, lambda b, h: (0, 0, 0, 0)),
            pl.BlockSpec(f2.shape, lambda b, h: (0, 0)),
        ],
        out_specs=pl.BlockSpec((SEQ, DFT_LANES), lambda b, h: (b, h)),
        out_shape=jax.ShapeDtypeStruct((TOKENS, FOURIER_WIDTH), BF16),
        scratch_shapes=[pltpu.VMEM((DFT_R1, 2 * DFT_R2, DFT_LANES), BF16),
                        pltpu.VMEM((DFT_LANES // LANES, SEQ, LANES), F32)],
        compiler_params=_cparams(("parallel", "parallel")),
        name="dft",
    )(v, tw, f2)


def _dft_constants():
    t1 = np.arange(DFT_R1)
    s2 = np.arange(DFT_R2)
    ang = 2.0 * np.pi * np.outer(t1, s2) / SEQ
    scale = 1.0 / math.sqrt(SEQ)
    tw = np.stack([np.cos(ang) * scale, -np.sin(ang) * scale])
    tw = np.repeat(tw[..., None], LANES, axis=-1)
    ang2 = 2.0 * np.pi * np.outer(s2, s2) / DFT_R2
    f2 = np.concatenate([np.cos(ang2), np.sin(ang2)], axis=1)
    kc = np.arange(FOURIER_GROUP_CH)
    angc = 2.0 * np.pi * np.outer(kc, kc) / FOURIER_GROUP_CH
    cs = 1.0 / math.sqrt(FOURIER_GROUP_CH)
    cdft = np.concatenate([np.cos(angc) * cs, -np.sin(angc) * cs], axis=1)
    return tuple(jnp.asarray(v, F32) for v in (tw, f2, cdft))


GROUP_BLOCKS = SSM_GROUPS // GROUPS_PER_BLOCK
DIR_STATE = 2 * SSM_STATE
STATE_LANES = SSM_GROUPS * DIR_STATE


def _ssm_operators(a_re, a_im, log_dt, b_re, b_im, c_re, c_im):
    q_len = SSM_CHUNK
    hi = lax.Precision.HIGHEST
    dt = jnp.exp(log_dt)[..., None]
    lr, li = a_re * dt, a_im * dt
    steps = jnp.arange(q_len + 1, dtype=F32)
    mag = jnp.exp(lr[..., None] * steps)
    ang = li[..., None] * steps
    pr, pi = mag * jnp.cos(ang), mag * jnp.sin(ang)
    ar, ai = pr[..., 1], pi[..., 1]
    den = a_re * a_re + a_im * a_im
    cr = ((ar - 1.0) * a_re + ai * a_im) / den
    ci = (ai * a_re - (ar - 1.0) * a_im) / den
    bbr = cr[..., None] * b_re - ci[..., None] * b_im
    bbi = cr[..., None] * b_im + ci[..., None] * b_re

    rq = jnp.asarray(np.kron(np.eye(q_len), np.ones((1, SSM_GROUP_CH))), F32)
    rc = jnp.asarray(np.kron(np.ones((1, q_len)), np.eye(SSM_GROUP_CH)), F32)

    def per_group(x):
        return jnp.swapaxes(x, 0, 1).reshape(SSM_GROUPS, DIR_STATE, x.shape[-1])

    def on_cols(x, rep):
        return jnp.einsum('grk,kc->grc', per_group(x), rep, precision=hi)

    def state_operator(pw_r, pw_i, f_r, f_i, negate_im):
        p_r, p_i, q_r, q_i = on_cols(pw_r, rq), on_cols(pw_i, rq), on_cols(f_r, rc), on_cols(f_i, rc)
        w_im = p_r * q_i + p_i * q_r
        return jnp.concatenate([p_r * q_r - p_i * q_i, -w_im if negate_im else w_im], axis=1).astype(BF16)

    def both(p, fwd, bwd):
        return jnp.stack([p[0][..., fwd], p[1][..., bwd]])

    rev = slice(q_len - 1, None, -1)
    wsgt = state_operator(both(pr, rev, slice(0, q_len)), both(pi, rev, slice(0, q_len)), bbr, bbi, False)
    ctr, cti = jnp.swapaxes(c_re, -1, -2), jnp.swapaxes(c_im, -1, -2)
    wog = state_operator(both(pr, slice(1, None), slice(q_len, 0, -1)), both(pi, slice(1, None), slice(q_len, 0, -1)),
                         ctr, cti, True)

    prq, piq = pr[..., :q_len], pi[..., :q_len]
    cpr = jnp.einsum('dgcn,dgnt->dgtcn', c_re, prq) - jnp.einsum('dgcn,dgnt->dgtcn', c_im, piq)
    cpi = jnp.einsum('dgcn,dgnt->dgtcn', c_re, piq) + jnp.einsum('dgcn,dgnt->dgtcn', c_im, prq)
    kern = jnp.einsum('dgtcn,dgne->dgtec', cpr, bbr) - jnp.einsum('dgtcn,dgne->dgtec', cpi, bbi)
    lags = jnp.concatenate([kern[1][:, :0:-1], kern[0][:, :1] + kern[1][:, :1], kern[0][:, 1:]], axis=1)
    lag_cols = jnp.einsum('gtek,kc->gtec', lags, rc, precision=hi)
    lane_q = jnp.arange(GROUP_COLS, dtype=jnp.int32) // SSM_GROUP_CH
    lag_of = (lane_q[None, :] - jnp.arange(q_len, dtype=jnp.int32)[:, None] + q_len - 1)[None, :, None, :]
    mg = jnp.zeros((SSM_GROUPS, q_len, SSM_GROUP_CH, GROUP_COLS), F32)
    for t in range(2 * q_len - 1):
        mg = jnp.where(lag_of == t, lag_cols[:, t][:, None], mg)
    mg = mg.reshape(SSM_GROUPS, GROUP_COLS, GROUP_COLS).astype(BF16)

    aq = jnp.stack([per_group(pr[..., q_len:]), per_group(pi[..., q_len:])])
    return mg, wsgt, wog, aq.reshape(2, STATE_LANES)


def _ssm_state_kernel(x_ref, ws_ref, flip_ref, sre_ref, sim_ref):
    backward = (lax.broadcasted_iota(jnp.int32, (N_CHUNKS, DIR_STATE), 1) & SSM_STATE) != 0
    nt = (((1,), (1,)), ((), ()))
    for g in range(GROUPS_PER_BLOCK):
        lanes = slice(g * DIR_STATE, (g + 1) * DIR_STATE)
        for b in range(BATCH):
            x = x_ref[g, b * N_CHUNKS:(b + 1) * N_CHUNKS, :]
            x_rev = jnp.dot(flip_ref[...], x, preferred_element_type=F32).astype(BF16)
            s = lax.dot_general(x, ws_ref[g], nt, preferred_element_type=F32)
            s_rev = lax.dot_general(x_rev, ws_ref[g], nt, preferred_element_type=F32)
            sre_ref[b, :, lanes] = jnp.where(backward, s_rev[:, :DIR_STATE], s[:, :DIR_STATE])
            sim_ref[b, :, lanes] = jnp.where(backward, s_rev[:, DIR_STATE:], s[:, DIR_STATE:])


def _ssm_states(xg, wsgt, flip):
    out_spec = pl.BlockSpec((BATCH, N_CHUNKS, GROUPS_PER_BLOCK * DIR_STATE), lambda j: (0, 0, j))
    shape = jax.ShapeDtypeStruct((BATCH, N_CHUNKS, STATE_LANES), F32)
    return pl.pallas_call(
        _ssm_state_kernel,
        grid=(GROUP_BLOCKS,),
        in_specs=[
            pl.BlockSpec((GROUPS_PER_BLOCK, BATCH * N_CHUNKS, GROUP_COLS), lambda j: (j, 0, 0)),
            pl.BlockSpec((GROUPS_PER_BLOCK, 2 * DIR_STATE, GROUP_COLS), lambda j: (j, 0, 0)),
            pl.BlockSpec(flip.shape, lambda j: (0, 0)),
        ],
        out_specs=[out_spec, out_spec],
        out_shape=[shape, shape],
        compiler_params=_cparams(("parallel",)),
        name="ssm_states",
    )(xg, wsgt, flip)


def _ssm_scan_kernel(sre_ref, sim_ref, aq_ref, hre_ref, him_ref):
    ar, ai = aq_ref[0:1], aq_ref[1:2]

    def body(k, carry):
        hr, hi = carry
        hre_ref[0, pl.ds(k, 1), :] = hr
        him_ref[0, pl.ds(k, 1), :] = hi
        sr, si = sre_ref[0, pl.ds(k, 1), :], sim_ref[0, pl.ds(k, 1), :]
        return ar * hr - ai * hi + sr, ar * hi + ai * hr + si

    z = jnp.zeros((1, STATE_LANES), F32)
    lax.fori_loop(0, N_CHUNKS, body, (z, z))


def _ssm_scan(sre, sim, aq):
    spec = pl.BlockSpec((1, N_CHUNKS, STATE_LANES), lambda b: (b, 0, 0))
    shape = jax.ShapeDtypeStruct(sre.shape, F32)
    return pl.pallas_call(
        _ssm_scan_kernel,
        grid=(BATCH,),
        in_specs=[spec, spec, pl.BlockSpec(aq.shape, lambda b: (0, 0))],
        out_specs=[spec, spec],
        out_shape=[shape, shape],
        compiler_params=_cparams(("parallel",)),
        name="ssm_scan",
    )(sre, sim, aq)


def _ssm_out_kernel(x_ref, m_ref, wo_ref, flip_ref, hre_ref, him_ref, y_ref, y_scr):
    fwd = (lax.broadcasted_iota(jnp.int32, (N_CHUNKS, GROUPS_PER_BLOCK * DIR_STATE), 1) & SSM_STATE) == 0

    def in_chunk_order(h_ref):
        h = h_ref[0].astype(BF16)
        return jnp.where(fwd, h, jnp.dot(flip_ref[...], h, preferred_element_type=F32).astype(BF16))

    h_re, h_im = in_chunk_order(hre_ref), in_chunk_order(him_ref)
    accs = []
    for g in range(GROUPS_PER_BLOCK):
        lanes = slice(g * DIR_STATE, (g + 1) * DIR_STATE)
        h = jnp.concatenate([h_re[:, lanes], h_im[:, lanes]], axis=1)
        accs.append(jnp.dot(x_ref[g, 0], m_ref[g], preferred_element_type=F32)
                    + jnp.dot(h, wo_ref[g], preferred_element_type=F32))
    seg = lax.broadcasted_iota(jnp.int32, (N_CHUNKS, LANES), 1) >> GROUP_SHIFT_CH
    for q in range(SSM_CHUNK):
        half, ql = divmod(q, GROUPS_PER_BLOCK)
        piece = jnp.zeros((N_CHUNKS, LANES), F32)
        for g in range(GROUPS_PER_BLOCK):
            src = accs[g][:, half * LANES:(half + 1) * LANES]
            shift = ((g - ql) % GROUPS_PER_BLOCK) * SSM_GROUP_CH
            moved = src if shift == 0 else pltpu.roll(src, shift, axis=1)
            piece = jnp.where(seg == g, moved, piece)
        y_scr[pl.ds(q, N_CHUNKS, stride=SSM_CHUNK), :] = piece
    y_ref[0] = y_scr[...].astype(BF16)


def _ssm_out(xg, mg, wog, flip, hre, him):
    hspec = pl.BlockSpec((1, N_CHUNKS, GROUPS_PER_BLOCK * DIR_STATE), lambda j, b: (b, 0, j))
    wspec = pl.BlockSpec((GROUPS_PER_BLOCK, GROUP_COLS, GROUP_COLS), lambda j, b: (j, 0, 0))
    return pl.pallas_call(
        _ssm_out_kernel,
        grid=(GROUP_BLOCKS, BATCH),
        in_specs=[
            pl.BlockSpec((GROUPS_PER_BLOCK, 1, N_CHUNKS, GROUP_COLS), lambda j, b: (j, b, 0, 0)),
            wspec, wspec, pl.BlockSpec(flip.shape, lambda j, b: (0, 0)), hspec, hspec,
        ],
        out_specs=pl.BlockSpec((1, SEQ, LANES), lambda j, b: (j, b, 0)),
        out_shape=jax.ShapeDtypeStruct((SSM_LANE_BLOCKS, TOKENS, LANES), BF16),
        scratch_shapes=[pltpu.VMEM((SEQ, LANES), F32)],
        compiler_params=_cparams(("parallel", "parallel")),
        name="ssm_out",
    )(xg.reshape(SSM_GROUPS, BATCH, N_CHUNKS, GROUP_COLS), mg, wog, flip, hre, him)


MERGE_TM = 512
GELU_C = math.sqrt(2.0 / math.pi)
PACK_SUB = D_MODEL // LANES


def _split_bf16(v):
    hi = v.astype(BF16)
    lo = (v - hi.astype(F32)).astype(BF16)
    return hi, lo


def _pack_rows(v, out_ref):
    for s in range(PACK_SUB):
        out_ref[pl.ds(s, v.shape[0], stride=PACK_SUB), :] = v[:, s * LANES:(s + 1) * LANES]


def _unpack_rows(buf_ref, start, rows):
    return jnp.concatenate([buf_ref[pl.ds(start + s, rows, stride=PACK_SUB), :] for s in range(PACK_SUB)], axis=1)


def _merge_kernel(x_ref, fm_ref, yc_ref, us_ref, gates_ref, dskip_ref, wf_ref, wglu_ref, ws_ref, wo_ref,
                  ng_ref, wrh_ref, wrl_ref, rb_ref, x1_ref, hp_ref, logit_ref):
    conv = jnp.concatenate([yc_ref[j].astype(F32) for j in range(SSM_LANE_BLOCKS)], axis=-1)
    u = jnp.concatenate([us_ref[j].astype(F32) for j in range(SSM_LANE_BLOCKS)], axis=-1)
    y = conv + dskip_ref[...] * u
    y = 0.5 * y * (1.0 + jnp.tanh(GELU_C * (y + 0.044715 * (y * y * y))))
    glu = jax.nn.sigmoid(jnp.dot(y.astype(BF16), wglu_ref[...], preferred_element_type=F32))
    y_s = jnp.dot((y * glu).astype(BF16), ws_ref[...], preferred_element_type=F32)
    y_f = jnp.dot(fm_ref[...], wf_ref[...], preferred_element_type=F32)
    merged = (gates_ref[:, :D_MODEL].astype(F32) * y_f + gates_ref[:, D_MODEL:].astype(F32) * y_s)
    x1 = x_ref[...] + jnp.dot(merged.astype(BF16), wo_ref[...], preferred_element_type=F32)
    x1_ref[...] = x1
    inv = lax.rsqrt(jnp.mean(x1 * x1, axis=-1, keepdims=True) + RMS_EPS)
    hn = x1 * inv * ng_ref[...]
    _pack_rows(hn, hp_ref)
    hi, lo = _split_bf16(hn)
    logits = (jnp.dot(hi, wrh_ref[...], preferred_element_type=F32)
              + jnp.dot(lo, wrh_ref[...], preferred_element_type=F32)
              + jnp.dot(hi, wrl_ref[...], preferred_element_type=F32))
    logit_ref[...] = logits + rb_ref[...]


def _merge(x, fmix, yconv, us, gates, dskip, wf, wglu, ws, wo, ng, wrh, wrl, rb):
    tm = MERGE_TM
    full = lambda a: pl.BlockSpec(a.shape, lambda i: (0,) * a.ndim)
    return pl.pallas_call(
        _merge_kernel,
        grid=(TOKENS // tm,),
        in_specs=[
            pl.BlockSpec((tm, D_MODEL), lambda i: (i, 0)),
            pl.BlockSpec((tm, FOURIER_WIDTH), lambda i: (i, 0)),
            pl.BlockSpec((SSM_LANE_BLOCKS, tm, LANES), lambda i: (0, i, 0)),
            pl.BlockSpec((SSM_LANE_BLOCKS, tm, LANES), lambda i: (0, i, 0)),
            pl.BlockSpec((tm, 2 * D_MODEL), lambda i: (i, 0)),
            full(dskip), full(wf), full(wglu), full(ws), full(wo), full(ng), full(wrh), full(wrl), full(rb),
        ],
        out_specs=[
            pl.BlockSpec((tm, D_MODEL), lambda i: (i, 0)),
            pl.BlockSpec((tm * PACK_SUB, LANES), lambda i: (i, 0)),
            pl.BlockSpec((tm, ROUTER_COLS), lambda i: (i, 0)),
        ],
        out_shape=[
            jax.ShapeDtypeStruct((TOKENS, D_MODEL), F32),
            jax.ShapeDtypeStruct((TOKENS * PACK_SUB, LANES), F32),
            jax.ShapeDtypeStruct((TOKENS, ROUTER_COLS), F32),
        ],
        compiler_params=_cparams(("parallel",)),
        name="merge",
    )(x, fmix, yconv, us, gates, dskip, wf, wglu, ws, wo, ng, wrh, wrl, rb)


ROUTE_TM = 512
EXPERT_LANE0 = MOE_GROUPS
INFO_EXPERT, INFO_RANK, INFO_GATE = 0, 2, 4
INFO_FIELDS = 8


def _route_kernel(lg_ref, info_ref, fields_ref, cnt_ref, carry):
    @pl.when(pl.program_id(0) == 0)
    def _():
        carry[...] = jnp.zeros_like(carry)

    lg = lg_ref[...]
    tm = lg.shape[0]
    col_i = lax.broadcasted_iota(jnp.int32, lg.shape, 1)
    col = col_i.astype(F32)
    neg = jnp.float32(-jnp.inf)
    none = jnp.float32(ROUTER_COLS)

    def row_max(v):
        return jnp.max(v, axis=-1, keepdims=True)

    def first_at(v, m):
        return jnp.min(jnp.where(v == m, col, none), axis=-1, keepdims=True)

    gl = jnp.where(col_i < MOE_GROUPS, lg, neg)
    gmax = row_max(gl)
    p_g = 1.0 / jnp.sum(jnp.exp(gl - gmax), axis=-1, keepdims=True)
    lo = EXPERT_LANE0 + first_at(gl, gmax) * EXPERTS_PER_GROUP
    el = jnp.where((col >= lo) & (col < lo + EXPERTS_PER_GROUP), lg, neg)
    l1 = row_max(el)
    i1 = first_at(el, l1)
    el2 = jnp.where(col == i1, neg, el)
    l2 = row_max(el2)
    i2 = first_at(el2, l2)
    r = jnp.exp(l2 - l1)
    w1 = p_g / (1.0 + r)
    w2 = w1 * r

    hit1, hit2 = col == i1, col == i2
    onehot = jnp.where(hit1 | hit2, 1.0, 0.0)
    earlier = lax.broadcasted_iota(jnp.int32, (tm, tm), 0) > lax.broadcasted_iota(jnp.int32, (tm, tm), 1)
    before = jnp.dot(jnp.where(earlier, 1.0, 0.0).astype(BF16), onehot.astype(BF16),
                     preferred_element_type=F32) + carry[...]
    rank1 = jnp.sum(jnp.where(hit1, before, 0.0), axis=-1, keepdims=True)
    rank2 = jnp.sum(jnp.where(hit2, before, 0.0), axis=-1, keepdims=True)
    carry[...] += jnp.sum(onehot, axis=0, keepdims=True)
    cnt_ref[...] = carry[...]

    info = jnp.zeros(lg.shape, F32)
    for lane, v in ((INFO_EXPERT, i1 - EXPERT_LANE0), (INFO_EXPERT + 1, i2 - EXPERT_LANE0), (INFO_RANK, rank1),
                    (INFO_RANK + 1, rank2), (INFO_GATE, w1), (INFO_GATE + 1, w2)):
        info = jnp.where(col_i == lane, v, info)
    info_ref[...] = info
    fields_ref[...] = info.T[:INFO_FIELDS]


def _route(logits):
    return pl.pallas_call(
        _route_kernel,
        grid=(TOKENS // ROUTE_TM,),
        in_specs=[pl.BlockSpec((ROUTE_TM, ROUTER_COLS), lambda i: (i, 0))],
        out_specs=[pl.BlockSpec((ROUTE_TM, ROUTER_COLS), lambda i: (i, 0)),
                   pl.BlockSpec((INFO_FIELDS, ROUTE_TM), lambda i: (0, i)),
                   pl.BlockSpec((1, ROUTER_COLS), lambda i: (0, 0))],
        out_shape=[jax.ShapeDtypeStruct((TOKENS, ROUTER_COLS), F32),
                   jax.ShapeDtypeStruct((INFO_FIELDS, TOKENS), F32),
                   jax.ShapeDtypeStruct((1, ROUTER_COLS), F32)],
        scratch_shapes=[pltpu.VMEM((1, ROUTER_COLS), F32)],
        compiler_params=_cparams(("arbitrary",)),
        name="route",
    )(logits)


def _dispatch_plan(fields, counts):
    expert = fields[INFO_EXPERT:INFO_EXPERT + MOE_TOP_K].astype(jnp.int32)
    rank = fields[INFO_RANK:INFO_RANK + MOE_TOP_K].astype(jnp.int32)
    cnt = counts[0, EXPERT_LANE0:EXPERT_LANE0 + N_EXPERTS].astype(jnp.int32)
    padded = ((cnt + MOE_ROWS - 1) // MOE_ROWS) * MOE_ROWS
    pends = jnp.cumsum(padded)
    pstarts = pends - padded
    ids = jnp.arange(N_EXPERTS, dtype=jnp.int32)
    dest = rank + jnp.sum(jnp.where(expert[..., None] == ids, pstarts, 0), axis=-1)
    n_used = pends[-1] // MOE_ROWS
    blocks = jnp.arange(MOE_BLOCKS, dtype=jnp.int32)
    block_e = jnp.sum((pends[None, :] <= (blocks * MOE_ROWS)[:, None]).astype(jnp.int32), axis=1)
    block_e = jnp.minimum(block_e, N_EXPERTS - 1)
    last_e = jnp.sum(jnp.where(blocks == n_used - 1, block_e, 0))
    used = (cnt > 0).astype(jnp.int32)
    ordinal = jnp.cumsum(used) - used
    n_experts_used = jnp.sum(used)
    by_ordinal = jnp.sum(jnp.where((ordinal[None, :] == ids[:, None]) & (used[None, :] > 0), ids[None, :], 0), axis=1)
    block_ord = jnp.sum(jnp.where(block_e[:, None] == ids[None, :], ordinal[None, :], 0), axis=1)
    block_first = (blocks * MOE_ROWS == jnp.sum(jnp.where(block_e[:, None] == ids[None, :], pstarts[None, :], 0), axis=1))
    block_first = (block_first & (blocks < n_used)).astype(jnp.int32)
    meta = jnp.concatenate([n_used.reshape(1), n_experts_used.reshape(1)]).astype(jnp.int32)
    return dest.reshape(MOE_TOP_K * TOKENS), block_ord.astype(jnp.int32), block_first, by_ordinal.astype(jnp.int32), meta, pends


MOE_SLOT_ROWS = MOE_ROWS * PACK_SUB
INVERT_UNROLL = 8


WEIGHT_SLOTS = 3
GATHER_SLOTS = 3
WEIGHT_DMA_PRIORITY = 1


def _moe_kernel(dest_ref, ord_ref, first_ref, eo_ref, meta_ref, pend_ref, hp_ref, wg_hbm, wu_hbm, wd_hbm, y_ref,
                xbuf, wg_buf, wu_buf, wd_buf, tok_ref, sem, wsem):
    i = pl.program_id(0)
    n_used = meta_ref[0]
    n_experts_used = meta_ref[1]

    def slot_rows(slot):
        return xbuf.at[pl.ds(pl.multiple_of(slot * MOE_SLOT_ROWS, MOE_SLOT_ROWS), MOE_SLOT_ROWS), :]

    def weight_copies(ordinal):
        e = eo_ref[ordinal]
        ws = ordinal % WEIGHT_SLOTS
        return [pltpu.make_async_copy(hbm.at[e], buf.at[ws], wsem.at[ws])
                for hbm, buf in ((wg_hbm, wg_buf), (wu_hbm, wu_buf), (wd_hbm, wd_buf))]

    def invert_dispatch():
        def fill_expert(e, c):
            first = jnp.maximum(pend_ref[e] - MOE_ROWS, 0)

            def fill(r, c2):
                tok_ref[first + r] = (first + r) & (TOKENS - 1)
                return c2

            lax.fori_loop(0, MOE_ROWS, fill, 0, unroll=INVERT_UNROLL)
            return c

        lax.fori_loop(0, N_EXPERTS, fill_expert, 0)

        def place(t, c):
            for k in range(MOE_TOP_K):
                tok_ref[dest_ref[k * TOKENS + t]] = t
            return c

        lax.fori_loop(0, TOKENS, place, 0, unroll=INVERT_UNROLL)

    def gather(block, slot):
        base = block * MOE_ROWS
        for r in range(MOE_ROWS):
            src = hp_ref.at[pl.ds(pl.multiple_of(tok_ref[base + r] * PACK_SUB, PACK_SUB), PACK_SUB), :]
            dst = xbuf.at[pl.ds(pl.multiple_of(slot * MOE_SLOT_ROWS + r * PACK_SUB, PACK_SUB), PACK_SUB), :]
            pltpu.make_async_copy(src, dst, sem.at[slot]).start()

    @pl.when(i == 0)
    def _():
        for ahead in range(WEIGHT_SLOTS - 1):
            @pl.when(ahead < n_experts_used)
            def _():
                for cp in weight_copies(ahead):
                    cp.start(priority=WEIGHT_DMA_PRIORITY)
        invert_dispatch()
        for ahead in range(GATHER_SLOTS - 1):
            gather(jnp.minimum(ahead, n_used - 1), ahead)

    @pl.when(i < n_used)
    def _():
        ordinal = ord_ref[i]

        @pl.when(first_ref[i] == 1)
        def _():
            for cp in weight_copies(ordinal):
                cp.wait()

            @pl.when(ordinal + WEIGHT_SLOTS - 1 < n_experts_used)
            def _():
                for cp in weight_copies(ordinal + WEIGHT_SLOTS - 1):
                    cp.start(priority=WEIGHT_DMA_PRIORITY)

        slot = i % GATHER_SLOTS
        ws = ordinal % WEIGHT_SLOTS
        pltpu.make_async_copy(slot_rows(slot), slot_rows(slot), sem.at[slot]).wait()
        ahead = i + GATHER_SLOTS - 1
        xb = _unpack_rows(xbuf, slot * MOE_SLOT_ROWS, MOE_ROWS).astype(BF16)
        gather(jnp.minimum(ahead, n_used - 1), ahead % GATHER_SLOTS)
        a = jnp.dot(xb, wg_buf[ws].astype(BF16), preferred_element_type=F32)
        u = jnp.dot(xb, wu_buf[ws].astype(BF16), preferred_element_type=F32)
        act = (a * jax.nn.sigmoid(a) * u).astype(BF16)
        y = jnp.dot(act, wd_buf[ws].astype(BF16), preferred_element_type=F32)
        _pack_rows(y, y_ref)

        @pl.when(i == n_used - 1)
        def _():
            for extra in range(1, GATHER_SLOTS):
                other = (i + extra) % GATHER_SLOTS
                pltpu.make_async_copy(slot_rows(other), slot_rows(other), sem.at[other]).wait()

    @pl.when(i >= n_used)
    def _():
        y_ref[...] = jnp.zeros_like(y_ref)


def _moe(dest, block_ord, block_first, by_ordinal, meta, pends, hp, w_gate, w_up, w_down):
    hbm = pl.BlockSpec(memory_space=pl.ANY)
    grid_spec = pltpu.PrefetchScalarGridSpec(
        num_scalar_prefetch=6,
        grid=(MOE_BLOCKS,),
        in_specs=[hbm, hbm, hbm, hbm],
        out_specs=pl.BlockSpec((MOE_SLOT_ROWS, LANES), lambda i, *_: (i, 0)),
        scratch_shapes=[pltpu.VMEM((GATHER_SLOTS * MOE_SLOT_ROWS, LANES), F32),
                        pltpu.VMEM((WEIGHT_SLOTS, D_MODEL, D_EXPERT), F32),
                        pltpu.VMEM((WEIGHT_SLOTS, D_MODEL, D_EXPERT), F32),
                        pltpu.VMEM((WEIGHT_SLOTS, D_EXPERT, D_MODEL), F32),
                        pltpu.SMEM((MOE_BLOCKS * MOE_ROWS,), jnp.int32),
                        pltpu.SemaphoreType.DMA((GATHER_SLOTS,)), pltpu.SemaphoreType.DMA((WEIGHT_SLOTS,))],
    )
    return pl.pallas_call(
        _moe_kernel,
        grid_spec=grid_spec,
        out_shape=jax.ShapeDtypeStruct((MOE_BLOCKS * MOE_SLOT_ROWS, LANES), F32),
        compiler_params=_cparams(("arbitrary",)),
        name="moe",
    )(dest, block_ord, block_first, by_ordinal, meta, pends, hp, w_gate, w_up, w_down)


COMBINE_TM = 256


def _combine_kernel(dest_ref, x1_ref, info_ref, g_ref, y_ref, o_ref, ybuf, sem):
    i = pl.program_id(0)
    last = pl.num_programs(0) - 1
    k_rows = COMBINE_TM * PACK_SUB
    slot_rows = MOE_TOP_K * k_rows

    def slot_ref(slot):
        return ybuf.at[pl.ds(pl.multiple_of(slot * slot_rows, slot_rows), slot_rows), :]

    def gather(tile, slot):
        for r in range(COMBINE_TM):
            for k in range(MOE_TOP_K):
                row = dest_ref[k * TOKENS + tile * COMBINE_TM + r]
                src = y_ref.at[pl.ds(pl.multiple_of(row * PACK_SUB, PACK_SUB), PACK_SUB), :]
                at = slot * slot_rows + k * k_rows + r * PACK_SUB
                dst = ybuf.at[pl.ds(pl.multiple_of(at, PACK_SUB), PACK_SUB), :]
                pltpu.make_async_copy(src, dst, sem.at[slot]).start(priority=k)

    @pl.when(i == 0)
    def _():
        for ahead in range(GATHER_SLOTS - 1):
            gather(ahead, ahead)

    slot = i % GATHER_SLOTS
    pltpu.make_async_copy(slot_ref(slot), slot_ref(slot), sem.at[slot]).wait()
    ahead = i + GATHER_SLOTS - 1
    gather(jnp.minimum(ahead, last), ahead % GATHER_SLOTS)
    x2 = x1_ref[...]
    for k in range(MOE_TOP_K):
        yk = _unpack_rows(ybuf, slot * slot_rows + k * k_rows, COMBINE_TM)
        x2 = x2 + info_ref[:, INFO_GATE + k:INFO_GATE + k + 1] * yk
    inv = lax.rsqrt(jnp.mean(x2 * x2, axis=-1, keepdims=True) + RMS_EPS)
    o_ref[...] = x2 * inv * g_ref[...]

    @pl.when(i == last)
    def _():
        for extra in range(1, GATHER_SLOTS):
            other = (i + extra) % GATHER_SLOTS
            pltpu.make_async_copy(slot_ref(other), slot_ref(other), sem.at[other]).wait()


def _combine(dest, x1, info, g, y_pad):
    tm = COMBINE_TM
    grid_spec = pltpu.PrefetchScalarGridSpec(
        num_scalar_prefetch=1,
        grid=(TOKENS // tm,),
        in_specs=[
            pl.BlockSpec((tm, D_MODEL), lambda i, d: (i, 0)),
            pl.BlockSpec((tm, ROUTER_COLS), lambda i, d: (i, 0)),
            pl.BlockSpec((1, D_MODEL), lambda i, d: (0, 0)),
            pl.BlockSpec(memory_space=pl.ANY),
        ],
        out_specs=pl.BlockSpec((tm, D_MODEL), lambda i, d: (i, 0)),
        scratch_shapes=[pltpu.VMEM((GATHER_SLOTS * MOE_TOP_K * tm * PACK_SUB, LANES), F32),
                        pltpu.SemaphoreType.DMA((GATHER_SLOTS,))],
    )
    return pl.pallas_call(
        _combine_kernel,
        grid_spec=grid_spec,
        out_shape=jax.ShapeDtypeStruct((TOKENS, D_MODEL), F32),
        compiler_params=_cparams(("arbitrary",)),
        name="combine",
    )(dest, x1, info, g, y_pad)


def kernel(x, mix_norm_g, w_in, w_fourier_out, ssm_A_re, ssm_A_im, ssm_log_dt, ssm_B_re, ssm_B_im, ssm_C_re,
           ssm_C_im, ssm_D, ssm_w_glu, w_ssm_out, w_out, ffn_norm_g, router_group_w, router_group_b,
           router_expert_w, router_expert_b, expert_w_gate, expert_w_up, expert_w_down, final_norm_g):
    assert x.shape == (BATCH, SEQ, D_MODEL) and w_in.shape[0] == 1
    tw, f2, cdft = _dft_constants()

    vf, us, xg, gates = _inproj(x, mix_norm_g[0][None], w_in[0].astype(BF16), cdft)
    fmix = _dft(vf, tw, f2)

    mg, wsgt, wog, aq = _ssm_operators(ssm_A_re[0], ssm_A_im[0], ssm_log_dt[0], ssm_B_re[0], ssm_B_im[0],
                                       ssm_C_re[0], ssm_C_im[0])
    flip = jnp.asarray(np.eye(N_CHUNKS)[::-1], BF16)
    sre, sim = _ssm_states(xg, wsgt, flip)
    yconv = _ssm_out(xg, mg, wog, flip, *_ssm_scan(sre, sim, aq))

    w_router = jnp.concatenate([router_group_w[0], router_expert_w[0]], axis=1)
    w_router = jnp.pad(w_router, ((0, 0), (0, ROUTER_COLS - w_router.shape[1])))
    b_router = jnp.concatenate([router_group_b[0], router_expert_b[0]])
    b_router = jnp.pad(b_router, (0, ROUTER_COLS - b_router.shape[0]))[None]
    wr_hi = w_router.astype(BF16)
    wr_lo = (w_router - wr_hi.astype(F32)).astype(BF16)
    x1, hp, logits = _merge(x.reshape(TOKENS, D_MODEL), fmix, yconv, us, gates, ssm_D[0][None],
                            w_fourier_out[0].astype(BF16), ssm_w_glu[0].astype(BF16), w_ssm_out[0].astype(BF16),
                            w_out[0].astype(BF16), ffn_norm_g[0][None], wr_hi, wr_lo, b_router)

    info, fields, counts = _route(logits)
    dest, block_ord, block_first, by_ordinal, meta, pends = _dispatch_plan(fields, counts)
    y_pad = _moe(dest, block_ord, block_first, by_ordinal, meta, pends, hp, expert_w_gate[0], expert_w_up[0],
                 expert_w_down[0])
    out = _combine(dest, x1, info, final_norm_g[None], y_pad)
    return out.reshape(BATCH, SEQ, D_MODEL)
```

```python
import math

import numpy as np
import jax
import jax.numpy as jnp
from jax import lax
from jax.experimental import pallas as pl
from jax.experimental.pallas import tpu as pltpu

F32 = jnp.float32
BF16 = jnp.bfloat16

D_MODEL = 1024
BATCH = 4
SEQ = 4096
TOKENS = BATCH * SEQ
FOURIER_WIDTH = 512
FOURIER_GROUP_CH = 128
FOURIER_GROUPS = 4
SSM_WIDTH = 512
SSM_GROUP_CH = 16
SSM_GROUPS = 32
SSM_STATE = 64
MOE_GROUPS = 8
EXPERTS_PER_GROUP = 8
N_EXPERTS = 64
MOE_TOP_K = 2
D_EXPERT = 512
RMS_EPS = 1e-6

LANES = 128
SSM_CHUNK = 16
SSM_LANE_BLOCKS = SSM_WIDTH // LANES
GROUPS_PER_BLOCK = LANES // SSM_GROUP_CH
N_CHUNKS = SEQ // SSM_CHUNK
GROUP_COLS = SSM_CHUNK * SSM_GROUP_CH
GROUP_SHIFT_CH = 4
MOE_ROWS = 256
MOE_BLOCKS = TOKENS * MOE_TOP_K // MOE_ROWS + N_EXPERTS
ROUTER_COLS = 128
VMEM_LIMIT = 48 * 1024 * 1024


def _cparams(sem, vmem=VMEM_LIMIT):
    return pltpu.CompilerParams(dimension_semantics=sem, vmem_limit_bytes=vmem)


IN_TM = 512


def _inproj_kernel(x_ref, g_ref, w_ref, cdft_ref, vf_ref, us_ref, xg_ref, gates_ref, zs_scr):
    x = x_ref[0]
    inv = lax.rsqrt(jnp.mean(x * x, axis=-1, keepdims=True) + RMS_EPS)
    h = (x * inv * g_ref[...]).astype(BF16)
    zf = jnp.dot(h, w_ref[:, 0:FOURIER_WIDTH], preferred_element_type=F32).astype(BF16)
    cdft = cdft_ref[...].astype(BF16)
    for g in range(FOURIER_GROUPS):
        sl = slice(g * LANES, (g + 1) * LANES)
        v = jnp.dot(zf[:, sl], cdft, preferred_element_type=F32)
        vf_ref[0, 0, :, sl] = v[:, :LANES].astype(BF16)
        vf_ref[0, 1, :, sl] = v[:, LANES:].astype(BF16)
    zs = jnp.dot(h, w_ref[:, FOURIER_WIDTH:FOURIER_WIDTH + SSM_WIDTH], preferred_element_type=F32)
    for j in range(SSM_LANE_BLOCKS):
        us_ref[j] = zs[:, j * LANES:(j + 1) * LANES].astype(BF16)
        zs_scr[j] = zs[:, j * LANES:(j + 1) * LANES]
    seg = lax.broadcasted_iota(jnp.int32, (IN_TM // SSM_CHUNK, LANES), 1) >> GROUP_SHIFT_CH
    for j in range(SSM_LANE_BLOCKS):
        pieces = [zs_scr[j, pl.ds(q, IN_TM // SSM_CHUNK, stride=SSM_CHUNK), :] for q in range(SSM_CHUNK)]
        for g in range(GROUPS_PER_BLOCK):
            for half in range(SSM_CHUNK // GROUPS_PER_BLOCK):
                acc = jnp.zeros((IN_TM // SSM_CHUNK, LANES), F32)
                for ql in range(GROUPS_PER_BLOCK):
                    shift = ((ql - g) % GROUPS_PER_BLOCK) * SSM_GROUP_CH
                    piece = pieces[half * GROUPS_PER_BLOCK + ql]
                    moved = piece if shift == 0 else pltpu.roll(piece, shift, axis=1)
                    acc = jnp.where(seg == ql, moved, acc)
                xg_ref[j * GROUPS_PER_BLOCK + g, :, half * LANES:(half + 1) * LANES] = acc.astype(BF16)
    base = FOURIER_WIDTH + SSM_WIDTH
    for n in range(4):
        zg = jnp.dot(h, w_ref[:, base + n * 512: base + (n + 1) * 512], preferred_element_type=F32)
        gates_ref[:, n * 512:(n + 1) * 512] = jax.nn.sigmoid(zg).astype(BF16)


def _inproj(x, g, w_in, cdft):
    nt = SEQ // IN_TM
    return pl.pallas_call(
        _inproj_kernel,
        grid=(BATCH, nt),
        in_specs=[
            pl.BlockSpec((1, IN_TM, D_MODEL), lambda b, i: (b, i, 0)),
            pl.BlockSpec((1, D_MODEL), lambda b, i: (0, 0)),
            pl.BlockSpec(w_in.shape, lambda b, i: (0, 0)),
            pl.BlockSpec(cdft.shape, lambda b, i: (0, 0)),
        ],
        out_specs=[
            pl.BlockSpec((1, 2, IN_TM, FOURIER_WIDTH), lambda b, i: (b, 0, i, 0)),
            pl.BlockSpec((SSM_LANE_BLOCKS, IN_TM, LANES), lambda b, i: (0, b * nt + i, 0)),
            pl.BlockSpec((SSM_GROUPS, IN_TM // SSM_CHUNK, GROUP_COLS), lambda b, i: (0, b * nt + i, 0)),
            pl.BlockSpec((IN_TM, 2 * D_MODEL), lambda b, i: (b * nt + i, 0)),
        ],
        out_shape=[
            jax.ShapeDtypeStruct((BATCH, 2, SEQ, FOURIER_WIDTH), BF16),
            jax.ShapeDtypeStruct((SSM_LANE_BLOCKS, TOKENS, LANES), BF16),
            jax.ShapeDtypeStruct((SSM_GROUPS, TOKENS // SSM_CHUNK, GROUP_COLS), BF16),
            jax.ShapeDtypeStruct((TOKENS, 2 * D_MODEL), BF16),
        ],
        scratch_shapes=[pltpu.VMEM((SSM_LANE_BLOCKS, IN_TM, LANES), F32)],
        compiler_params=_cparams(("parallel", "parallel")),
        name="inproj",
    )(x, g, w_in, cdft)


DFT_R1 = 8
DFT_R2 = SEQ // DFT_R1
DFT_LANES = 2 * LANES
DFT_ROWS = 16


def _cmul_const(z, w):
    re, im = z
    if abs(w.imag) < 1e-12:
        return (re, im) if abs(w.real - 1.0) < 1e-12 else (re * w.real, im * w.real)
    if abs(w.real) < 1e-12:
        return (im, -re) if abs(w.imag + 1.0) < 1e-12 else (-im * w.imag, re * w.imag)
    return re * w.real - im * w.imag, re * w.imag + im * w.real


def _fft_blocks(xs):
    n = len(xs)
    if n == 1:
        return xs
    even, odd = _fft_blocks(xs[0::2]), _fft_blocks(xs[1::2])
    out = [None] * n
    for k in range(n // 2):
        tr, ti = _cmul_const(odd[k], np.exp(-2j * np.pi * k / n))
        out[k] = (even[k][0] + tr, even[k][1] + ti)
        out[k + n // 2] = (even[k][0] - tr, even[k][1] - ti)
    return out


def _dft_kernel(v_ref, tw_ref, f_ref, o_ref, a_scr, o_scr):
    def tile(i, c):
        r0 = pl.multiple_of(i * DFT_ROWS, DFT_ROWS)
        for slab in range(DFT_LANES // LANES):
            lanes = slice(slab * LANES, (slab + 1) * LANES)
            xs = [(v_ref[0, 0, pl.ds(s1 * DFT_R2 + r0, DFT_ROWS), lanes].astype(F32),
                   v_ref[0, 1, pl.ds(s1 * DFT_R2 + r0, DFT_ROWS), lanes].astype(F32)) for s1 in range(DFT_R1)]
            for t1, (ar, ai) in enumerate(_fft_blocks(xs)):
                tr, ti = tw_ref[0, t1, pl.ds(r0, DFT_ROWS), :], tw_ref[1, t1, pl.ds(r0, DFT_ROWS), :]
                a_scr[t1, pl.ds(r0, DFT_ROWS), lanes] = (ar * tr - ai * ti).astype(BF16)
                a_scr[t1, pl.ds(DFT_R2 + r0, DFT_ROWS), lanes] = (ar * ti + ai * tr).astype(BF16)
        return c

    lax.fori_loop(0, DFT_R2 // DFT_ROWS, tile, 0)

    f2 = f_ref[...].astype(BF16)
    for t1 in range(DFT_R1):
        r = jnp.dot(f2, a_scr[t1], preferred_element_type=F32)
        for slab in range(DFT_LANES // LANES):
            o_scr[slab, pl.ds(t1, DFT_R2, stride=DFT_R1), :] = r[:, slab * LANES:(slab + 1) * LANES]
    for slab in range(DFT_LANES // LANES):
        o_ref[:, slab * LANES:(slab + 1) * LANES] = o_scr[slab].astype(BF16)


def _dft(v, tw, f2):
    nh = FOURIER_WIDTH // DFT_LANES
    return pl.pallas_call(
        _dft_kernel,
        grid=(BATCH, nh),
        in_specs=[
            pl.BlockSpec((1, 2, SEQ, DFT_LANES), lambda b, h: (b, 0, 0, h)),
            pl.BlockSpec(tw.shape, lambda b, h: (0, 0, 0, 0)),
            pl.BlockSpec(f2.shape, lambda b, h: (0, 0)),
        ],
        out_specs=pl.BlockSpec((SEQ, DFT_LANES), lambda b, h: (b, h)),
        out_shape=jax.ShapeDtypeStruct((TOKENS, FOURIER_WIDTH), BF16),
        scratch_shapes=[pltpu.VMEM((DFT_R1, 2 * DFT_R2, DFT_LANES), BF16),
                        pltpu.VMEM((DFT_LANES // LANES, SEQ, LANES), F32)],
        compiler_params=_cparams(("parallel", "parallel")),
        name="dft",
    )(v, tw, f2)


def _dft_constants():
    t1 = np.arange(DFT_R1)
    s2 = np.arange(DFT_R2)
    ang = 2.0 * np.pi * np.outer(t1, s2) / SEQ
    scale = 1.0 / math.sqrt(SEQ)
    tw = np.stack([np.cos(ang) * scale, -np.sin(ang) * scale])
    tw = np.repeat(tw[..., None], LANES, axis=-1)
    ang2 = 2.0 * np.pi * np.outer(s2, s2) / DFT_R2
    f2 = np.concatenate([np.cos(ang2), np.sin(ang2)], axis=1)
    kc = np.arange(FOURIER_GROUP_CH)
    angc = 2.0 * np.pi * np.outer(kc, kc) / FOURIER_GROUP_CH
    cs = 1.0 / math.sqrt(FOURIER_GROUP_CH)
    cdft = np.concatenate([np.cos(angc) * cs, -np.sin(angc) * cs], axis=1)
    return tuple(jnp.asarray(v, F32) for v in (tw, f2, cdft))


GROUP_BLOCKS = SSM_GROUPS // GROUPS_PER_BLOCK
DIR_STATE = 2 * SSM_STATE
STATE_LANES = SSM_GROUPS * DIR_STATE


def _ssm_operators(a_re, a_im, log_dt, b_re, b_im, c_re, c_im):
    q_len = SSM_CHUNK
    hi = lax.Precision.HIGHEST
    dt = jnp.exp(log_dt)[..., None]
    lr, li = a_re * dt, a_im * dt
    steps = jnp.arange(q_len + 1, dtype=F32)
    mag = jnp.exp(lr[..., None] * steps)
    ang = li[..., None] * steps
    pr, pi = mag * jnp.cos(ang), mag * jnp.sin(ang)
    ar, ai = pr[..., 1], pi[..., 1]
    den = a_re * a_re + a_im * a_im
    cr = ((ar - 1.0) * a_re + ai * a_im) / den
    ci = (ai * a_re - (ar - 1.0) * a_im) / den
    bbr = cr[..., None] * b_re - ci[..., None] * b_im
    bbi = cr[..., None] * b_im + ci[..., None] * b_re

    rq = jnp.asarray(np.kron(np.eye(q_len), np.ones((1, SSM_GROUP_CH))), F32)
    rc = jnp.asarray(np.kron(np.ones((1, q_len)), np.eye(SSM_GROUP_CH)), F32)

    def per_group(x):
        return jnp.swapaxes(x, 0, 1).reshape(SSM_GROUPS, DIR_STATE, x.shape[-1])

    def on_cols(x, rep):
        return jnp.einsum('grk,kc->grc', per_group(x), rep, precision=hi)

    def state_operator(pw_r, pw_i, f_r, f_i, negate_im):
        p_r, p_i, q_r, q_i = on_cols(pw_r, rq), on_cols(pw_i, rq), on_cols(f_r, rc), on_cols(f_i, rc)
        w_im = p_r * q_i + p_i * q_r
        return jnp.concatenate([p_r * q_r - p_i * q_i, -w_im if negate_im else w_im], axis=1).astype(BF16)

    def both(p, fwd, bwd):
        return jnp.stack([p[0][..., fwd], p[1][..., bwd]])

    rev = slice(q_len - 1, None, -1)
    wsgt = state_operator(both(pr, rev, slice(0, q_len)), both(pi, rev, slice(0, q_len)), bbr, bbi, False)
    ctr, cti = jnp.swapaxes(c_re, -1, -2), jnp.swapaxes(c_im, -1, -2)
    wog = state_operator(both(pr, slice(1, None), slice(q_len, 0, -1)), both(pi, slice(1, None), slice(q_len, 0, -1)),
                         ctr, cti, True)

    prq, piq = pr[..., :q_len], pi[..., :q_len]
    cpr = jnp.einsum('dgcn,dgnt->dgtcn', c_re, prq) - jnp.einsum('dgcn,dgnt->dgtcn', c_im, piq)
    cpi = jnp.einsum('dgcn,dgnt->dgtcn', c_re, piq) + jnp.einsum('dgcn,dgnt->dgtcn', c_im, prq)
    kern = jnp.einsum('dgtcn,dgne->dgtec', cpr, bbr) - jnp.einsum('dgtcn,dgne->dgtec', cpi, bbi)
    lags = jnp.concatenate([kern[1][:, :0:-1], kern[0][:, :1] + kern[1][:, :1], kern[0][:, 1:]], axis=1)
    lag_cols = jnp.einsum('gtek,kc->gtec', lags, rc, precision=hi).astype(BF16)

    aq = jnp.stack([per_group(pr[..., q_len:]), per_group(pi[..., q_len:])])
    return lag_cols, wsgt, wog, aq.reshape(2, STATE_LANES)


def _ssm_state_kernel(x_ref, ws_ref, flip_ref, sre_ref, sim_ref):
    backward = (lax.broadcasted_iota(jnp.int32, (N_CHUNKS, DIR_STATE), 1) & SSM_STATE) != 0
    nt = (((1,), (1,)), ((), ()))
    for g in range(GROUPS_PER_BLOCK):
        lanes = slice(g * DIR_STATE, (g + 1) * DIR_STATE)
        for b in range(BATCH):
            x = x_ref[g, b * N_CHUNKS:(b + 1) * N_CHUNKS, :]
            x_rev = jnp.dot(flip_ref[...], x, preferred_element_type=F32).astype(BF16)
            s = lax.dot_general(x, ws_ref[g], nt, preferred_element_type=F32)
            s_rev = lax.dot_general(x_rev, ws_ref[g], nt, preferred_element_type=F32)
            sre_ref[b, :, lanes] = jnp.where(backward, s_rev[:, :DIR_STATE], s[:, :DIR_STATE])
            sim_ref[b, :, lanes] = jnp.where(backward, s_rev[:, DIR_STATE:], s[:, DIR_STATE:])


def _ssm_states(xg, wsgt, flip):
    out_spec = pl.BlockSpec((BATCH, N_CHUNKS, GROUPS_PER_BLOCK * DIR_STATE), lambda j: (0, 0, j))
    shape = jax.ShapeDtypeStruct((BATCH, N_CHUNKS, STATE_LANES), F32)
    return pl.pallas_call(
        _ssm_state_kernel,
        grid=(GROUP_BLOCKS,),
        in_specs=[
            pl.BlockSpec((GROUPS_PER_BLOCK, BATCH * N_CHUNKS, GROUP_COLS), lambda j: (j, 0, 0)),
            pl.BlockSpec((GROUPS_PER_BLOCK, 2 * DIR_STATE, GROUP_COLS), lambda j: (j, 0, 0)),
            pl.BlockSpec(flip.shape, lambda j: (0, 0)),
        ],
        out_specs=[out_spec, out_spec],
        out_shape=[shape, shape],
        compiler_params=_cparams(("parallel",)),
        name="ssm_states",
    )(xg, wsgt, flip)


def _ssm_scan_kernel(sre_ref, sim_ref, aq_ref, hre_ref, him_ref):
    ar, ai = aq_ref[0:1], aq_ref[1:2]

    def body(k, carry):
        hr, hi = carry
        hre_ref[0, pl.ds(k, 1), :] = hr
        him_ref[0, pl.ds(k, 1), :] = hi
        sr, si = sre_ref[0, pl.ds(k, 1), :], sim_ref[0, pl.ds(k, 1), :]
        return ar * hr - ai * hi + sr, ar * hi + ai * hr + si

    z = jnp.zeros((1, STATE_LANES), F32)
    lax.fori_loop(0, N_CHUNKS, body, (z, z))


def _ssm_scan(sre, sim, aq):
    spec = pl.BlockSpec((1, N_CHUNKS, STATE_LANES), lambda b: (b, 0, 0))
    shape = jax.ShapeDtypeStruct(sre.shape, F32)
    return pl.pallas_call(
        _ssm_scan_kernel,
        grid=(BATCH,),
        in_specs=[spec, spec, pl.BlockSpec(aq.shape, lambda b: (0, 0))],
        out_specs=[spec, spec],
        out_shape=[shape, shape],
        compiler_params=_cparams(("parallel",)),
        name="ssm_scan",
    )(sre, sim, aq)


def _ssm_out_kernel(x_ref, lag_ref, wo_ref, flip_ref, hre_ref, him_ref, y_ref, m_ref, y_scr):
    @pl.when(pl.program_id(1) == 0)
    def _():
        lane_q = lax.broadcasted_iota(jnp.int32, (SSM_GROUP_CH, GROUP_COLS), 1) >> GROUP_SHIFT_CH
        for g in range(GROUPS_PER_BLOCK):
            for qp in range(SSM_CHUNK):
                acc = lag_ref[g, SSM_CHUNK - 1 - qp]
                for q in range(1, SSM_CHUNK):
                    acc = jnp.where(lane_q == q, lag_ref[g, q - qp + SSM_CHUNK - 1], acc)
                m_ref[g, qp * SSM_GROUP_CH:(qp + 1) * SSM_GROUP_CH, :] = acc

    fwd = (lax.broadcasted_iota(jnp.int32, (N_CHUNKS, GROUPS_PER_BLOCK * DIR_STATE), 1) & SSM_STATE) == 0

    def in_chunk_order(h_ref):
        h = h_ref[0].astype(BF16)
        return jnp.where(fwd, h, jnp.dot(flip_ref[...], h, preferred_element_type=F32).astype(BF16))

    h_re, h_im = in_chunk_order(hre_ref), in_chunk_order(him_ref)
    accs = []
    for g in range(GROUPS_PER_BLOCK):
        lanes = slice(g * DIR_STATE, (g + 1) * DIR_STATE)
        h = jnp.concatenate([h_re[:, lanes], h_im[:, lanes]], axis=1)
        accs.append(jnp.dot(x_ref[g, 0], m_ref[g], preferred_element_type=F32)
                    + jnp.dot(h, wo_ref[g], preferred_element_type=F32))
    seg = lax.broadcasted_iota(jnp.int32, (N_CHUNKS, LANES), 1) >> GROUP_SHIFT_CH
    for q in range(SSM_CHUNK):
        half, ql = divmod(q, GROUPS_PER_BLOCK)
        piece = jnp.zeros((N_CHUNKS, LANES), F32)
        for g in range(GROUPS_PER_BLOCK):
            src = accs[g][:, half * LANES:(half + 1) * LANES]
            shift = ((g - ql) % GROUPS_PER_BLOCK) * SSM_GROUP_CH
            moved = src if shift == 0 else pltpu.roll(src, shift, axis=1)
            piece = jnp.where(seg == g, moved, piece)
        y_scr[pl.ds(q, N_CHUNKS, stride=SSM_CHUNK), :] = piece
    y_ref[0] = y_scr[...].astype(BF16)


def _ssm_out(xg, lag_cols, wog, flip, hre, him):
    hspec = pl.BlockSpec((1, N_CHUNKS, GROUPS_PER_BLOCK * DIR_STATE), lambda j, b: (b, 0, j))
    return pl.pallas_call(
        _ssm_out_kernel,
        grid=(GROUP_BLOCKS, BATCH),
        in_specs=[
            pl.BlockSpec((GROUPS_PER_BLOCK, 1, N_CHUNKS, GROUP_COLS), lambda j, b: (j, b, 0, 0)),
            pl.BlockSpec((GROUPS_PER_BLOCK, 2 * SSM_CHUNK - 1, SSM_GROUP_CH, GROUP_COLS), lambda j, b: (j, 0, 0, 0)),
            pl.BlockSpec((GROUPS_PER_BLOCK, GROUP_COLS, GROUP_COLS), lambda j, b: (j, 0, 0)),
            pl.BlockSpec(flip.shape, lambda j, b: (0, 0)), hspec, hspec,
        ],
        out_specs=pl.BlockSpec((1, SEQ, LANES), lambda j, b: (j, b, 0)),
        out_shape=jax.ShapeDtypeStruct((SSM_LANE_BLOCKS, TOKENS, LANES), BF16),
        scratch_shapes=[pltpu.VMEM((GROUPS_PER_BLOCK, GROUP_COLS, GROUP_COLS), BF16), pltpu.VMEM((SEQ, LANES), F32)],
        compiler_params=_cparams(("parallel", "arbitrary")),
        name="ssm_out",
    )(xg.reshape(SSM_GROUPS, BATCH, N_CHUNKS, GROUP_COLS), lag_cols, wog, flip, hre, him)


MERGE_TM = 512
GELU_C = math.sqrt(2.0 / math.pi)
PACK_SUB = D_MODEL // LANES


def _split_bf16(v):
    hi = v.astype(BF16)
    lo = (v - hi.astype(F32)).astype(BF16)
    return hi, lo


def _pack_rows(v, out_ref):
    for s in range(PACK_SUB):
        out_ref[pl.ds(s, v.shape[0], stride=PACK_SUB), :] = v[:, s * LANES:(s + 1) * LANES]


def _unpack_rows(buf_ref, start, rows):
    return jnp.concatenate([buf_ref[pl.ds(start + s, rows, stride=PACK_SUB), :] for s in range(PACK_SUB)], axis=1)


def _merge_kernel(x_ref, fm_ref, yc_ref, us_ref, gates_ref, dskip_ref, wf_ref, wglu_ref, ws_ref, wo_ref,
                  ng_ref, wrh_ref, wrl_ref, rb_ref, x1_ref, hp_ref, logit_ref):
    conv = jnp.concatenate([yc_ref[j].astype(F32) for j in range(SSM_LANE_BLOCKS)], axis=-1)
    u = jnp.concatenate([us_ref[j].astype(F32) for j in range(SSM_LANE_BLOCKS)], axis=-1)
    y = conv + dskip_ref[...] * u
    y = 0.5 * y * (1.0 + jnp.tanh(GELU_C * (y + 0.044715 * (y * y * y))))
    glu = jax.nn.sigmoid(jnp.dot(y.astype(BF16), wglu_ref[...], preferred_element_type=F32))
    y_s = jnp.dot((y * glu).astype(BF16), ws_ref[...], preferred_element_type=F32)
    y_f = jnp.dot(fm_ref[...], wf_ref[...], preferred_element_type=F32)
    merged = (gates_ref[:, :D_MODEL].astype(F32) * y_f + gates_ref[:, D_MODEL:].astype(F32) * y_s)
    x1 = x_ref[...] + jnp.dot(merged.astype(BF16), wo_ref[...], preferred_element_type=F32)
    x1_ref[...] = x1
    inv = lax.rsqrt(jnp.mean(x1 * x1, axis=-1, keepdims=True) + RMS_EPS)
    hn = x1 * inv * ng_ref[...]
    _pack_rows(hn, hp_ref)
    hi, lo = _split_bf16(hn)
    logits = (jnp.dot(hi, wrh_ref[...], preferred_element_type=F32)
              + jnp.dot(lo, wrh_ref[...], preferred_element_type=F32)
              + jnp.dot(hi, wrl_ref[...], preferred_element_type=F32))
    logit_ref[...] = logits + rb_ref[...]


def _merge(x, fmix, yconv, us, gates, dskip, wf, wglu, ws, wo, ng, wrh, wrl, rb):
    tm = MERGE_TM
    full = lambda a: pl.BlockSpec(a.shape, lambda i: (0,) * a.ndim)
    return pl.pallas_call(
        _merge_kernel,
        grid=(TOKENS // tm,),
        in_specs=[
            pl.BlockSpec((tm, D_MODEL), lambda i: (i, 0)),
            pl.BlockSpec((tm, FOURIER_WIDTH), lambda i: (i, 0)),
            pl.BlockSpec((SSM_LANE_BLOCKS, tm, LANES), lambda i: (0, i, 0)),
            pl.BlockSpec((SSM_LANE_BLOCKS, tm, LANES), lambda i: (0, i, 0)),
            pl.BlockSpec((tm, 2 * D_MODEL), lambda i: (i, 0)),
            full(dskip), full(wf), full(wglu), full(ws), full(wo), full(ng), full(wrh), full(wrl), full(rb),
        ],
        out_specs=[
            pl.BlockSpec((tm, D_MODEL), lambda i: (i, 0)),
            pl.BlockSpec((tm * PACK_SUB, LANES), lambda i: (i, 0)),
            pl.BlockSpec((tm, ROUTER_COLS), lambda i: (i, 0)),
        ],
        out_shape=[
            jax.ShapeDtypeStruct((TOKENS, D_MODEL), F32),
            jax.ShapeDtypeStruct((TOKENS * PACK_SUB, LANES), F32),
            jax.ShapeDtypeStruct((TOKENS, ROUTER_COLS), F32),
        ],
        compiler_params=_cparams(("parallel",)),
        name="merge",
    )(x, fmix, yconv, us, gates, dskip, wf, wglu, ws, wo, ng, wrh, wrl, rb)


ROUTE_TM = 512
EXPERT_LANE0 = MOE_GROUPS
INFO_EXPERT, INFO_RANK, INFO_GATE = 0, 2, 4
INFO_FIELDS = 8


def _route_kernel(lg_ref, info_ref, fields_ref, cnt_ref, carry):
    @pl.when(pl.program_id(0) == 0)
    def _():
        carry[...] = jnp.zeros_like(carry)

    lg = lg_ref[...]
    tm = lg.shape[0]
    col_i = lax.broadcasted_iota(jnp.int32, lg.shape, 1)
    col = col_i.astype(F32)
    neg = jnp.float32(-jnp.inf)
    none = jnp.float32(ROUTER_COLS)

    def row_max(v):
        return jnp.max(v, axis=-1, keepdims=True)

    def first_at(v, m):
        return jnp.min(jnp.where(v == m, col, none), axis=-1, keepdims=True)

    gl = jnp.where(col_i < MOE_GROUPS, lg, neg)
    gmax = row_max(gl)
    p_g = 1.0 / jnp.sum(jnp.exp(gl - gmax), axis=-1, keepdims=True)
    lo = EXPERT_LANE0 + first_at(gl, gmax) * EXPERTS_PER_GROUP
    el = jnp.where((col >= lo) & (col < lo + EXPERTS_PER_GROUP), lg, neg)
    l1 = row_max(el)
    i1 = first_at(el, l1)
    el2 = jnp.where(col == i1, neg, el)
    l2 = row_max(el2)
    i2 = first_at(el2, l2)
    r = jnp.exp(l2 - l1)
    w1 = p_g / (1.0 + r)
    w2 = w1 * r

    hit1, hit2 = col == i1, col == i2
    onehot = jnp.where(hit1 | hit2, 1.0, 0.0)
    earlier = lax.broadcasted_iota(jnp.int32, (tm, tm), 0) > lax.broadcasted_iota(jnp.int32, (tm, tm), 1)
    before = jnp.dot(jnp.where(earlier, 1.0, 0.0).astype(BF16), onehot.astype(BF16),
                     preferred_element_type=F32) + carry[...]
    rank1 = jnp.sum(jnp.where(hit1, before, 0.0), axis=-1, keepdims=True)
    rank2 = jnp.sum(jnp.where(hit2, before, 0.0), axis=-1, keepdims=True)
    carry[...] += jnp.sum(onehot, axis=0, keepdims=True)
    cnt_ref[...] = carry[...]

    info = jnp.zeros(lg.shape, F32)
    for lane, v in ((INFO_EXPERT, i1 - EXPERT_LANE0), (INFO_EXPERT + 1, i2 - EXPERT_LANE0), (INFO_RANK, rank1),
                    (INFO_RANK + 1, rank2), (INFO_GATE, w1), (INFO_GATE + 1, w2)):
        info = jnp.where(col_i == lane, v, info)
    info_ref[...] = info
    fields_ref[...] = info.T[:INFO_FIELDS]


def _route(logits):
    return pl.pallas_call(
        _route_kernel,
        grid=(TOKENS // ROUTE_TM,),
        in_specs=[pl.BlockSpec((ROUTE_TM, ROUTER_COLS), lambda i: (i, 0))],
        out_specs=[pl.BlockSpec((ROUTE_TM, ROUTER_COLS), lambda i: (i, 0)),
                   pl.BlockSpec((INFO_FIELDS, ROUTE_TM), lambda i: (0, i)),
                   pl.BlockSpec((1, ROUTER_COLS), lambda i: (0, 0))],
        out_shape=[jax.ShapeDtypeStruct((TOKENS, ROUTER_COLS), F32),
                   jax.ShapeDtypeStruct((INFO_FIELDS, TOKENS), F32),
                   jax.ShapeDtypeStruct((1, ROUTER_COLS), F32)],
        scratch_shapes=[pltpu.VMEM((1, ROUTER_COLS), F32)],
        compiler_params=_cparams(("arbitrary",)),
        name="route",
    )(logits)


def _dispatch_plan(fields, counts):
    expert = fields[INFO_EXPERT:INFO_EXPERT + MOE_TOP_K].astype(jnp.int32)
    rank = fields[INFO_RANK:INFO_RANK + MOE_TOP_K].astype(jnp.int32)
    cnt = counts[0, EXPERT_LANE0:EXPERT_LANE0 + N_EXPERTS].astype(jnp.int32)
    padded = ((cnt + MOE_ROWS - 1) // MOE_ROWS) * MOE_ROWS
    pends = jnp.cumsum(padded)
    pstarts = pends - padded
    ids = jnp.arange(N_EXPERTS, dtype=jnp.int32)
    dest = rank + jnp.sum(jnp.where(expert[..., None] == ids, pstarts, 0), axis=-1)
    n_used = pends[-1] // MOE_ROWS
    blocks = jnp.arange(MOE_BLOCKS, dtype=jnp.int32)
    block_e = jnp.sum((pends[None, :] <= (blocks * MOE_ROWS)[:, None]).astype(jnp.int32), axis=1)
    block_e = jnp.minimum(block_e, N_EXPERTS - 1)
    last_e = jnp.sum(jnp.where(blocks == n_used - 1, block_e, 0))
    used = (cnt > 0).astype(jnp.int32)
    ordinal = jnp.cumsum(used) - used
    n_experts_used = jnp.sum(used)
    by_ordinal = jnp.sum(jnp.where((ordinal[None, :] == ids[:, None]) & (used[None, :] > 0), ids[None, :], 0), axis=1)
    block_ord = jnp.sum(jnp.where(block_e[:, None] == ids[None, :], ordinal[None, :], 0), axis=1)
    block_first = (blocks * MOE_ROWS == jnp.sum(jnp.where(block_e[:, None] == ids[None, :], pstarts[None, :], 0), axis=1))
    block_first = (block_first & (blocks < n_used)).astype(jnp.int32)
    meta = jnp.concatenate([n_used.reshape(1), n_experts_used.reshape(1)]).astype(jnp.int32)
    return dest.reshape(MOE_TOP_K * TOKENS), block_ord.astype(jnp.int32), block_first, by_ordinal.astype(jnp.int32), meta, pends


MOE_SLOT_ROWS = MOE_ROWS * PACK_SUB
INVERT_UNROLL = 8


WEIGHT_SLOTS = 3
GATHER_SLOTS = 3
WEIGHT_DMA_PRIORITY = 1


def _moe_kernel(dest_ref, ord_ref, first_ref, eo_ref, meta_ref, pend_ref, hp_ref, wg_hbm, wu_hbm, wd_hbm, y_ref,
                xbuf, wg_buf, wu_buf, wd_buf, tok_ref, sem, wsem):
    i = pl.program_id(0)
    n_used = meta_ref[0]
    n_experts_used = meta_ref[1]

    def slot_rows(slot):
        return xbuf.at[pl.ds(pl.multiple_of(slot * MOE_SLOT_ROWS, MOE_SLOT_ROWS), MOE_SLOT_ROWS), :]

    def weight_copies(ordinal):
        e = eo_ref[ordinal]
        ws = ordinal % WEIGHT_SLOTS
        return [pltpu.make_async_copy(hbm.at[e], buf.at[ws], wsem.at[ws])
                for hbm, buf in ((wg_hbm, wg_buf), (wu_hbm, wu_buf), (wd_hbm, wd_buf))]

    def invert_dispatch():
        def fill_expert(e, c):
            first = jnp.maximum(pend_ref[e] - MOE_ROWS, 0)

            def fill(r, c2):
                tok_ref[first + r] = (first + r) & (TOKENS - 1)
                return c2

            lax.fori_loop(0, MOE_ROWS, fill, 0, unroll=INVERT_UNROLL)
            return c

        lax.fori_loop(0, N_EXPERTS, fill_expert, 0)

        def place(t, c):
            for k in range(MOE_TOP_K):
                tok_ref[dest_ref[k * TOKENS + t]] = t
            return c

        lax.fori_loop(0, TOKENS, place, 0, unroll=INVERT_UNROLL)

    def gather(block, slot):
        base = block * MOE_ROWS
        for r in range(MOE_ROWS):
            src = hp_ref.at[pl.ds(pl.multiple_of(tok_ref[base + r] * PACK_SUB, PACK_SUB), PACK_SUB), :]
            dst = xbuf.at[pl.ds(pl.multiple_of(slot * MOE_SLOT_ROWS + r * PACK_SUB, PACK_SUB), PACK_SUB), :]
            pltpu.make_async_copy(src, dst, sem.at[slot]).start()

    @pl.when(i == 0)
    def _():
        for ahead in range(WEIGHT_SLOTS - 1):
            @pl.when(ahead < n_experts_used)
            def _():
                for cp in weight_copies(ahead):
                    cp.start(priority=WEIGHT_DMA_PRIORITY)
        invert_dispatch()
        for ahead in range(GATHER_SLOTS - 1):
            gather(jnp.minimum(ahead, n_used - 1), ahead)

    @pl.when(i < n_used)
    def _():
        ordinal = ord_ref[i]

        @pl.when(first_ref[i] == 1)
        def _():
            for cp in weight_copies(ordinal):
                cp.wait()

            @pl.when(ordinal + WEIGHT_SLOTS - 1 < n_experts_used)
            def _():
                for cp in weight_copies(ordinal + WEIGHT_SLOTS - 1):
                    cp.start(priority=WEIGHT_DMA_PRIORITY)

        slot = i % GATHER_SLOTS
        ws = ordinal % WEIGHT_SLOTS
        pltpu.make_async_copy(slot_rows(slot), slot_rows(slot), sem.at[slot]).wait()
        ahead = i + GATHER_SLOTS - 1
        xb = _unpack_rows(xbuf, slot * MOE_SLOT_ROWS, MOE_ROWS).astype(BF16)
        gather(jnp.minimum(ahead, n_used - 1), ahead % GATHER_SLOTS)
        a = jnp.dot(xb, wg_buf[ws].astype(BF16), preferred_element_type=F32)
        u = jnp.dot(xb, wu_buf[ws].astype(BF16), preferred_element_type=F32)
        act = (a * jax.nn.sigmoid(a) * u).astype(BF16)
        y = jnp.dot(act, wd_buf[ws].astype(BF16), preferred_element_type=F32)
        _pack_rows(y, y_ref)

        @pl.when(i == n_used - 1)
        def _():
            for extra in range(1, GATHER_SLOTS):
                other = (i + extra) % GATHER_SLOTS
                pltpu.make_async_copy(slot_rows(other), slot_rows(other), sem.at[other]).wait()

    @pl.when(i >= n_used)
    def _():
        y_ref[...] = jnp.zeros_like(y_ref)


def _moe(dest, block_ord, block_first, by_ordinal, meta, pends, hp, w_gate, w_up, w_down):
    hbm = pl.BlockSpec(memory_space=pl.ANY)
    grid_spec = pltpu.PrefetchScalarGridSpec(
        num_scalar_prefetch=6,
        grid=(MOE_BLOCKS,),
        in_specs=[hbm, hbm, hbm, hbm],
        out_specs=pl.BlockSpec((MOE_SLOT_ROWS, LANES), lambda i, *_: (i, 0)),
        scratch_shapes=[pltpu.VMEM((GATHER_SLOTS * MOE_SLOT_ROWS, LANES), F32),
                        pltpu.VMEM((WEIGHT_SLOTS, D_MODEL, D_EXPERT), F32),
                        pltpu.VMEM((WEIGHT_SLOTS, D_MODEL, D_EXPERT), F32),
                        pltpu.VMEM((WEIGHT_SLOTS, D_EXPERT, D_MODEL), F32),
                        pltpu.SMEM((MOE_BLOCKS * MOE_ROWS,), jnp.int32),
                        pltpu.SemaphoreType.DMA((GATHER_SLOTS,)), pltpu.SemaphoreType.DMA((WEIGHT_SLOTS,))],
    )
    return pl.pallas_call(
        _moe_kernel,
        grid_spec=grid_spec,
        out_shape=jax.ShapeDtypeStruct((MOE_BLOCKS * MOE_SLOT_ROWS, LANES), F32),
        compiler_params=_cparams(("arbitrary",)),
        name="moe",
    )(dest, block_ord, block_first, by_ordinal, meta, pends, hp, w_gate, w_up, w_down)


COMBINE_TM = 256


def _combine_kernel(dest_ref, x1_ref, info_ref, g_ref, y_ref, o_ref, ybuf, sem):
    i = pl.program_id(0)
    last = pl.num_programs(0) - 1
    k_rows = COMBINE_TM * PACK_SUB
    slot_rows = MOE_TOP_K * k_rows

    def slot_ref(slot):
        return ybuf.at[pl.ds(pl.multiple_of(slot * slot_rows, slot_rows), slot_rows), :]

    def gather(tile, slot):
        for r in range(COMBINE_TM):
            for k in range(MOE_TOP_K):
                row = dest_ref[k * TOKENS + tile * COMBINE_TM + r]
                src = y_ref.at[pl.ds(pl.multiple_of(row * PACK_SUB, PACK_SUB), PACK_SUB), :]
                at = slot * slot_rows + k * k_rows + r * PACK_SUB
                dst = ybuf.at[pl.ds(pl.multiple_of(at, PACK_SUB), PACK_SUB), :]
                pltpu.make_async_copy(src, dst, sem.at[slot]).start(priority=k)

    @pl.when(i == 0)
    def _():
        for ahead in range(GATHER_SLOTS - 1):
            gather(ahead, ahead)

    slot = i % GATHER_SLOTS
    pltpu.make_async_copy(slot_ref(slot), slot_ref(slot), sem.at[slot]).wait()
    ahead = i + GATHER_SLOTS - 1
    gather(jnp.minimum(ahead, last), ahead % GATHER_SLOTS)
    x2 = x1_ref[...]
    for k in range(MOE_TOP_K):
        yk = _unpack_rows(ybuf, slot * slot_rows + k * k_rows, COMBINE_TM)
        x2 = x2 + info_ref[:, INFO_GATE + k:INFO_GATE + k + 1] * yk
    inv = lax.rsqrt(jnp.mean(x2 * x2, axis=-1, keepdims=True) + RMS_EPS)
    o_ref[...] = x2 * inv * g_ref[...]

    @pl.when(i == last)
    def _():
        for extra in range(1, GATHER_SLOTS):
            other = (i + extra) % GATHER_SLOTS
            pltpu.make_async_copy(slot_ref(other), slot_ref(other), sem.at[other]).wait()


def _combine(dest, x1, info, g, y_pad):
    tm = COMBINE_TM
    grid_spec = pltpu.PrefetchScalarGridSpec(
        num_scalar_prefetch=1,
        grid=(TOKENS // tm,),
        in_specs=[
            pl.BlockSpec((tm, D_MODEL), lambda i, d: (i, 0)),
            pl.BlockSpec((tm, ROUTER_COLS), lambda i, d: (i, 0)),
            pl.BlockSpec((1, D_MODEL), lambda i, d: (0, 0)),
            pl.BlockSpec(memory_space=pl.ANY),
        ],
        out_specs=pl.BlockSpec((tm, D_MODEL), lambda i, d: (i, 0)),
        scratch_shapes=[pltpu.VMEM((GATHER_SLOTS * MOE_TOP_K * tm * PACK_SUB, LANES), F32),
                        pltpu.SemaphoreType.DMA((GATHER_SLOTS,))],
    )
    return pl.pallas_call(
        _combine_kernel,
        grid_spec=grid_spec,
        out_shape=jax.ShapeDtypeStruct((TOKENS, D_MODEL), F32),
        compiler_params=_cparams(("arbitrary",)),
        name="combine",
    )(dest, x1, info, g, y_pad)


def kernel(x, mix_norm_g, w_in, w_fourier_out, ssm_A_re, ssm_A_im, ssm_log_dt, ssm_B_re, ssm_B_im, ssm_C_re,
           ssm_C_im, ssm_D, ssm_w_glu, w_ssm_out, w_out, ffn_norm_g, router_group_w, router_group_b,
           router_expert_w, router_expert_b, expert_w_gate, expert_w_up, expert_w_down, final_norm_g):
    assert x.shape == (BATCH, SEQ, D_MODEL) and w_in.shape[0] == 1
    tw, f2, cdft = _dft_constants()

    vf, us, xg, gates = _inproj(x, mix_norm_g[0][None], w_in[0].astype(BF16), cdft)
    fmix = _dft(vf, tw, f2)

    lag_cols, wsgt, wog, aq = _ssm_operators(ssm_A_re[0], ssm_A_im[0], ssm_log_dt[0], ssm_B_re[0], ssm_B_im[0],
                                             ssm_C_re[0], ssm_C_im[0])
    flip = jnp.asarray(np.eye(N_CHUNKS)[::-1], BF16)
    sre, sim = _ssm_states(xg, wsgt, flip)
    yconv = _ssm_out(xg, lag_cols, wog, flip, *_ssm_scan(sre, sim, aq))

    w_router = jnp.concatenate([router_group_w[0], router_expert_w[0]], axis=1)
    w_router = jnp.pad(w_router, ((0, 0), (0, ROUTER_COLS - w_router.shape[1])))
    b_router = jnp.concatenate([router_group_b[0], router_expert_b[0]])
    b_router = jnp.pad(b_router, (0, ROUTER_COLS - b_router.shape[0]))[None]
    wr_hi = w_router.astype(BF16)
    wr_lo = (w_router - wr_hi.astype(F32)).astype(BF16)
    x1, hp, logits = _merge(x.reshape(TOKENS, D_MODEL), fmix, yconv, us, gates, ssm_D[0][None],
                            w_fourier_out[0].astype(BF16), ssm_w_glu[0].astype(BF16), w_ssm_out[0].astype(BF16),
                            w_out[0].astype(BF16), ffn_norm_g[0][None], wr_hi, wr_lo, b_router)

    info, fields, counts = _route(logits)
    dest, block_ord, block_first, by_ordinal, meta, pends = _dispatch_plan(fields, counts)
    y_pad = _moe(dest, block_ord, block_first, by_ordinal, meta, pends, hp, expert_w_gate[0], expert_w_up[0],
                 expert_w_down[0])
    out = _combine(dest, x1, info, final_norm_g[None], y_pad)
    return out.reshape(BATCH, SEQ, D_MODEL)
```

```python
import math

import numpy as np
import jax
import jax.numpy as jnp
from jax import lax
from jax.experimental import pallas as pl
from jax.experimental.pallas import tpu as pltpu

F32 = jnp.float32
BF16 = jnp.bfloat16

D_MODEL = 1024
BATCH = 4
SEQ = 4096
TOKENS = BATCH * SEQ
FOURIER_WIDTH = 512
FOURIER_GROUP_CH = 128
FOURIER_GROUPS = 4
SSM_WIDTH = 512
SSM_GROUP_CH = 16
SSM_GROUPS = 32
SSM_STATE = 64
MOE_GROUPS = 8
EXPERTS_PER_GROUP = 8
N_EXPERTS = 64
MOE_TOP_K = 2
D_EXPERT = 512
RMS_EPS = 1e-6

LANES = 128
SSM_CHUNK = 16
SSM_LANE_BLOCKS = SSM_WIDTH // LANES
GROUPS_PER_BLOCK = LANES // SSM_GROUP_CH
N_CHUNKS = SEQ // SSM_CHUNK
GROUP_COLS = SSM_CHUNK * SSM_GROUP_CH
GROUP_SHIFT_CH = 4
MOE_ROWS = 256
MOE_BLOCKS = TOKENS * MOE_TOP_K // MOE_ROWS + N_EXPERTS
ROUTER_COLS = 128
VMEM_LIMIT = 48 * 1024 * 1024


def _cparams(sem, vmem=VMEM_LIMIT):
    return pltpu.CompilerParams(dimension_semantics=sem, vmem_limit_bytes=vmem)


IN_TM = 512


def _inproj_kernel(x_ref, g_ref, w_ref, cdft_ref, vf_ref, us_ref, xg_ref, gates_ref, zs_scr):
    x = x_ref[0]
    inv = lax.rsqrt(jnp.mean(x * x, axis=-1, keepdims=True) + RMS_EPS)
    h = (x * inv * g_ref[...]).astype(BF16)
    zf = jnp.dot(h, w_ref[:, 0:FOURIER_WIDTH], preferred_element_type=F32).astype(BF16)
    cdft = cdft_ref[...].astype(BF16)
    for g in range(FOURIER_GROUPS):
        sl = slice(g * LANES, (g + 1) * LANES)
        v = jnp.dot(zf[:, sl], cdft, preferred_element_type=F32)
        vf_ref[0, 0, :, sl] = v[:, :LANES].astype(BF16)
        vf_ref[0, 1, :, sl] = v[:, LANES:].astype(BF16)
    zs = jnp.dot(h, w_ref[:, FOURIER_WIDTH:FOURIER_WIDTH + SSM_WIDTH], preferred_element_type=F32)
    for j in range(SSM_LANE_BLOCKS):
        us_ref[j] = zs[:, j * LANES:(j + 1) * LANES].astype(BF16)
        zs_scr[j] = zs[:, j * LANES:(j + 1) * LANES]
    seg = lax.broadcasted_iota(jnp.int32, (IN_TM // SSM_CHUNK, LANES), 1) >> GROUP_SHIFT_CH
    for j in range(SSM_LANE_BLOCKS):
        pieces = [zs_scr[j, pl.ds(q, IN_TM // SSM_CHUNK, stride=SSM_CHUNK), :] for q in range(SSM_CHUNK)]
        for g in range(GROUPS_PER_BLOCK):
            for half in range(SSM_CHUNK // GROUPS_PER_BLOCK):
                acc = jnp.zeros((IN_TM // SSM_CHUNK, LANES), F32)
                for ql in range(GROUPS_PER_BLOCK):
                    shift = ((ql - g) % GROUPS_PER_BLOCK) * SSM_GROUP_CH
                    piece = pieces[half * GROUPS_PER_BLOCK + ql]
                    moved = piece if shift == 0 else pltpu.roll(piece, shift, axis=1)
                    acc = jnp.where(seg == ql, moved, acc)
                xg_ref[j * GROUPS_PER_BLOCK + g, :, half * LANES:(half + 1) * LANES] = acc.astype(BF16)
    base = FOURIER_WIDTH + SSM_WIDTH
    for n in range(4):
        zg = jnp.dot(h, w_ref[:, base + n * 512: base + (n + 1) * 512], preferred_element_type=F32)
        gates_ref[:, n * 512:(n + 1) * 512] = jax.nn.sigmoid(zg).astype(BF16)


def _inproj(x, g, w_in, cdft):
    nt = SEQ // IN_TM
    return pl.pallas_call(
        _inproj_kernel,
        grid=(BATCH, nt),
        in_specs=[
            pl.BlockSpec((1, IN_TM, D_MODEL), lambda b, i: (b, i, 0)),
            pl.BlockSpec((1, D_MODEL), lambda b, i: (0, 0)),
            pl.BlockSpec(w_in.shape, lambda b, i: (0, 0)),
            pl.BlockSpec(cdft.shape, lambda b, i: (0, 0)),
        ],
        out_specs=[
            pl.BlockSpec((1, 2, IN_TM, FOURIER_WIDTH), lambda b, i: (b, 0, i, 0)),
            pl.BlockSpec((SSM_LANE_BLOCKS, IN_TM, LANES), lambda b, i: (0, b * nt + i, 0)),
            pl.BlockSpec((SSM_GROUPS, IN_TM // SSM_CHUNK, GROUP_COLS), lambda b, i: (0, b * nt + i, 0)),
            pl.BlockSpec((IN_TM, 2 * D_MODEL), lambda b, i: (b * nt + i, 0)),
        ],
        out_shape=[
            jax.ShapeDtypeStruct((BATCH, 2, SEQ, FOURIER_WIDTH), BF16),
            jax.ShapeDtypeStruct((SSM_LANE_BLOCKS, TOKENS, LANES), BF16),
            jax.ShapeDtypeStruct((SSM_GROUPS, TOKENS // SSM_CHUNK, GROUP_COLS), BF16),
            jax.ShapeDtypeStruct((TOKENS, 2 * D_MODEL), BF16),
        ],
        scratch_shapes=[pltpu.VMEM((SSM_LANE_BLOCKS, IN_TM, LANES), F32)],
        compiler_params=_cparams(("parallel", "parallel")),
        name="inproj",
    )(x, g, w_in, cdft)


DFT_R1 = 8
DFT_R2 = SEQ // DFT_R1
DFT_LANES = 2 * LANES
DFT_ROWS = 16


def _cmul_const(z, w):
    re, im = z
    if abs(w.imag) < 1e-12:
        return (re, im) if abs(w.real - 1.0) < 1e-12 else (re * w.real, im * w.real)
    if abs(w.real) < 1e-12:
        return (im, -re) if abs(w.imag + 1.0) < 1e-12 else (-im * w.imag, re * w.imag)
    return re * w.real - im * w.imag, re * w.imag + im * w.real


def _fft_blocks(xs):
    n = len(xs)
    if n == 1:
        return xs
    even, odd = _fft_blocks(xs[0::2]), _fft_blocks(xs[1::2])
    out = [None] * n
    for k in range(n // 2):
        tr, ti = _cmul_const(odd[k], np.exp(-2j * np.pi * k / n))
        out[k] = (even[k][0] + tr, even[k][1] + ti)
        out[k + n // 2] = (even[k][0] - tr, even[k][1] - ti)
    return out


def _dft_kernel(v_ref, tw_ref, f_ref, o_ref, a_scr, o_scr):
    def tile(i, c):
        r0 = pl.multiple_of(i * DFT_ROWS, DFT_ROWS)
        for slab in range(DFT_LANES // LANES):
            lanes = slice(slab * LANES, (slab + 1) * LANES)
            xs = [(v_ref[0, 0, pl.ds(s1 * DFT_R2 + r0, DFT_ROWS), lanes].astype(F32),
                   v_ref[0, 1, pl.ds(s1 * DFT_R2 + r0, DFT_ROWS), lanes].astype(F32)) for s1 in range(DFT_R1)]
            for t1, (ar, ai) in enumerate(_fft_blocks(xs)):
                tr, ti = tw_ref[0, t1, pl.ds(r0, DFT_ROWS), :], tw_ref[1, t1, pl.ds(r0, DFT_ROWS), :]
                a_scr[t1, pl.ds(r0, DFT_ROWS), lanes] = (ar * tr - ai * ti).astype(BF16)
                a_scr[t1, pl.ds(DFT_R2 + r0, DFT_ROWS), lanes] = (ar * ti + ai * tr).astype(BF16)
        return c

    lax.fori_loop(0, DFT_R2 // DFT_ROWS, tile, 0)

    f2 = f_ref[...].astype(BF16)
    for t1 in range(DFT_R1):
        r = jnp.dot(f2, a_scr[t1], preferred_element_type=F32)
        for slab in range(DFT_LANES // LANES):
            o_scr[slab, pl.ds(t1, DFT_R2, stride=DFT_R1), :] = r[:, slab * LANES:(slab + 1) * LANES]
    for slab in range(DFT_LANES // LANES):
        o_ref[:, slab * LANES:(slab + 1) * LANES] = o_scr[slab].astype(BF16)


def _dft(v, tw, f2):
    nh = FOURIER_WIDTH // DFT_LANES
    return pl.pallas_call(
        _dft_kernel,
        grid=(BATCH, nh),
        in_specs=[
            pl.BlockSpec((1, 2, SEQ, DFT_LANES), lambda b, h: (b, 0, 0, h)),
            pl.BlockSpec(tw.shape, lambda b, h: (0, 0, 0, 0)),
            pl.BlockSpec(f2.shape, lambda b, h: (0, 0)),
        ],
        out_specs=pl.BlockSpec((SEQ, DFT_LANES), lambda b, h: (b, h)),
        out_shape=jax.ShapeDtypeStruct((TOKENS, FOURIER_WIDTH), BF16),
        scratch_shapes=[pltpu.VMEM((DFT_R1, 2 * DFT_R2, DFT_LANES), BF16),
                        pltpu.VMEM((DFT_LANES // LANES, SEQ, LANES), F32)],
        compiler_params=_cparams(("parallel", "parallel")),
        name="dft",
    )(v, tw, f2)


def _dft_constants():
    t1 = np.arange(DFT_R1)
    s2 = np.arange(DFT_R2)
    ang = 2.0 * np.pi * np.outer(t1, s2) / SEQ
    scale = 1.0 / math.sqrt(SEQ)
    tw = np.stack([np.cos(ang) * scale, -np.sin(ang) * scale])
    tw = np.repeat(tw[..., None], LANES, axis=-1)
    ang2 = 2.0 * np.pi * np.outer(s2, s2) / DFT_R2
    f2 = np.concatenate([np.cos(ang2), np.sin(ang2)], axis=1)
    kc = np.arange(FOURIER_GROUP_CH)
    angc = 2.0 * np.pi * np.outer(kc, kc) / FOURIER_GROUP_CH
    cs = 1.0 / math.sqrt(FOURIER_GROUP_CH)
    cdft = np.concatenate([np.cos(angc) * cs, -np.sin(angc) * cs], axis=1)
    return tuple(jnp.asarray(v, F32) for v in (tw, f2, cdft))


GROUP_BLOCKS = SSM_GROUPS // GROUPS_PER_BLOCK
DIR_STATE = 2 * SSM_STATE
STATE_LANES = SSM_GROUPS * DIR_STATE


def _ssm_operators(a_re, a_im, log_dt, b_re, b_im, c_re, c_im):
    q_len = SSM_CHUNK
    hi = lax.Precision.HIGHEST
    dt = jnp.exp(log_dt)[..., None]
    lr, li = a_re * dt, a_im * dt
    steps = jnp.arange(q_len + 1, dtype=F32)
    mag = jnp.exp(lr[..., None] * steps)
    ang = li[..., None] * steps
    pr, pi = mag * jnp.cos(ang), mag * jnp.sin(ang)
    ar, ai = pr[..., 1], pi[..., 1]
    den = a_re * a_re + a_im * a_im
    cr = ((ar - 1.0) * a_re + ai * a_im) / den
    ci = (ai * a_re - (ar - 1.0) * a_im) / den
    bbr = cr[..., None] * b_re - ci[..., None] * b_im
    bbi = cr[..., None] * b_im + ci[..., None] * b_re

    rq = jnp.asarray(np.kron(np.eye(q_len), np.ones((1, SSM_GROUP_CH))), F32)
    rc = jnp.asarray(np.kron(np.ones((1, q_len)), np.eye(SSM_GROUP_CH)), F32)

    def per_group(x):
        return jnp.swapaxes(x, 0, 1).reshape(SSM_GROUPS, DIR_STATE, x.shape[-1])

    def on_cols(x, rep):
        return jnp.einsum('grk,kc->grc', per_group(x), rep, precision=hi)

    def state_operator(pw_r, pw_i, f_r, f_i, negate_im):
        p_r, p_i, q_r, q_i = on_cols(pw_r, rq), on_cols(pw_i, rq), on_cols(f_r, rc), on_cols(f_i, rc)
        w_im = p_r * q_i + p_i * q_r
        return jnp.concatenate([p_r * q_r - p_i * q_i, -w_im if negate_im else w_im], axis=1).astype(BF16)

    def both(p, fwd, bwd):
        return jnp.stack([p[0][..., fwd], p[1][..., bwd]])

    rev = slice(q_len - 1, None, -1)
    wsgt = state_operator(both(pr, rev, slice(0, q_len)), both(pi, rev, slice(0, q_len)), bbr, bbi, False)
    ctr, cti = jnp.swapaxes(c_re, -1, -2), jnp.swapaxes(c_im, -1, -2)
    wog = state_operator(both(pr, slice(1, None), slice(q_len, 0, -1)), both(pi, slice(1, None), slice(q_len, 0, -1)),
                         ctr, cti, True)

    prq, piq = pr[..., :q_len], pi[..., :q_len]
    cpr = jnp.einsum('dgcn,dgnt->dgtcn', c_re, prq) - jnp.einsum('dgcn,dgnt->dgtcn', c_im, piq)
    cpi = jnp.einsum('dgcn,dgnt->dgtcn', c_re, piq) + jnp.einsum('dgcn,dgnt->dgtcn', c_im, prq)
    kern = jnp.einsum('dgtcn,dgne->dgtec', cpr, bbr) - jnp.einsum('dgtcn,dgne->dgtec', cpi, bbi)
    lags = jnp.concatenate([kern[1][:, :0:-1], kern[0][:, :1] + kern[1][:, :1], kern[0][:, 1:]], axis=1)
    lag_cols = jnp.einsum('gtek,kc->gtec', lags, rc, precision=hi)
    lane_q = jnp.arange(GROUP_COLS, dtype=jnp.int32) // SSM_GROUP_CH
    rows = []
    for qp in range(q_len):
        acc = jnp.zeros(lag_cols[:, 0].shape, F32)
        for q in range(q_len):
            acc = jnp.where(lane_q == q, lag_cols[:, q - qp + q_len - 1], acc)
        rows.append(acc)
    mg = jnp.stack(rows, axis=1)
    mg = mg.reshape(SSM_GROUPS, GROUP_COLS, GROUP_COLS).astype(BF16)

    aq = jnp.stack([per_group(pr[..., q_len:]), per_group(pi[..., q_len:])])
    return mg, wsgt, wog, aq.reshape(2, STATE_LANES)


def _ssm_state_kernel(x_ref, ws_ref, flip_ref, sre_ref, sim_ref):
    backward = (lax.broadcasted_iota(jnp.int32, (N_CHUNKS, DIR_STATE), 1) & SSM_STATE) != 0
    nt = (((1,), (1,)), ((), ()))
    for g in range(GROUPS_PER_BLOCK):
        lanes = slice(g * DIR_STATE, (g + 1) * DIR_STATE)
        for b in range(BATCH):
            x = x_ref[g, b * N_CHUNKS:(b + 1) * N_CHUNKS, :]
            x_rev = jnp.dot(flip_ref[...], x, preferred_element_type=F32).astype(BF16)
            s = lax.dot_general(x, ws_ref[g], nt, preferred_element_type=F32)
            s_rev = lax.dot_general(x_rev, ws_ref[g], nt, preferred_element_type=F32)
            sre_ref[b, :, lanes] = jnp.where(backward, s_rev[:, :DIR_STATE], s[:, :DIR_STATE])
            sim_ref[b, :, lanes] = jnp.where(backward, s_rev[:, DIR_STATE:], s[:, DIR_STATE:])


def _ssm_states(xg, wsgt, flip):
    out_spec = pl.BlockSpec((BATCH, N_CHUNKS, GROUPS_PER_BLOCK * DIR_STATE), lambda j: (0, 0, j))
    shape = jax.ShapeDtypeStruct((BATCH, N_CHUNKS, STATE_LANES), F32)
    return pl.pallas_call(
        _ssm_state_kernel,
        grid=(GROUP_BLOCKS,),
        in_specs=[
            pl.BlockSpec((GROUPS_PER_BLOCK, BATCH * N_CHUNKS, GROUP_COLS), lambda j: (j, 0, 0)),
            pl.BlockSpec((GROUPS_PER_BLOCK, 2 * DIR_STATE, GROUP_COLS), lambda j: (j, 0, 0)),
            pl.BlockSpec(flip.shape, lambda j: (0, 0)),
        ],
        out_specs=[out_spec, out_spec],
        out_shape=[shape, shape],
        compiler_params=_cparams(("parallel",)),
        name="ssm_states",
    )(xg, wsgt, flip)


def _ssm_scan_kernel(sre_ref, sim_ref, aq_ref, hre_ref, him_ref):
    ar, ai = aq_ref[0:1], aq_ref[1:2]

    def body(k, carry):
        hr, hi = carry
        hre_ref[0, pl.ds(k, 1), :] = hr
        him_ref[0, pl.ds(k, 1), :] = hi
        sr, si = sre_ref[0, pl.ds(k, 1), :], sim_ref[0, pl.ds(k, 1), :]
        return ar * hr - ai * hi + sr, ar * hi + ai * hr + si

    z = jnp.zeros((1, STATE_LANES), F32)
    lax.fori_loop(0, N_CHUNKS, body, (z, z))


def _ssm_scan(sre, sim, aq):
    spec = pl.BlockSpec((1, N_CHUNKS, STATE_LANES), lambda b: (b, 0, 0))
    shape = jax.ShapeDtypeStruct(sre.shape, F32)
    return pl.pallas_call(
        _ssm_scan_kernel,
        grid=(BATCH,),
        in_specs=[spec, spec, pl.BlockSpec(aq.shape, lambda b: (0, 0))],
        out_specs=[spec, spec],
        out_shape=[shape, shape],
        compiler_params=_cparams(("parallel",)),
        name="ssm_scan",
    )(sre, sim, aq)


def _ssm_out_kernel(x_ref, m_ref, wo_ref, flip_ref, hre_ref, him_ref, y_ref, y_scr):
    fwd = (lax.broadcasted_iota(jnp.int32, (N_CHUNKS, GROUPS_PER_BLOCK * DIR_STATE), 1) & SSM_STATE) == 0

    def in_chunk_order(h_ref):
        h = h_ref[0].astype(BF16)
        return jnp.where(fwd, h, jnp.dot(flip_ref[...], h, preferred_element_type=F32).astype(BF16))

    h_re, h_im = in_chunk_order(hre_ref), in_chunk_order(him_ref)
    accs = []
    for g in range(GROUPS_PER_BLOCK):
        lanes = slice(g * DIR_STATE, (g + 1) * DIR_STATE)
        h = jnp.concatenate([h_re[:, lanes], h_im[:, lanes]], axis=1)
        accs.append(jnp.dot(x_ref[g, 0], m_ref[g], preferred_element_type=F32)
                    + jnp.dot(h, wo_ref[g], preferred_element_type=F32))
    seg = lax.broadcasted_iota(jnp.int32, (N_CHUNKS, LANES), 1) >> GROUP_SHIFT_CH
    for q in range(SSM_CHUNK):
        half, ql = divmod(q, GROUPS_PER_BLOCK)
        piece = jnp.zeros((N_CHUNKS, LANES), F32)
        for g in range(GROUPS_PER_BLOCK):
            src = accs[g][:, half * LANES:(half + 1) * LANES]
            shift = ((g - ql) % GROUPS_PER_BLOCK) * SSM_GROUP_CH
            moved = src if shift == 0 else pltpu.roll(src, shift, axis=1)
            piece = jnp.where(seg == g, moved, piece)
        y_scr[pl.ds(q, N_CHUNKS, stride=SSM_CHUNK), :] = piece
    y_ref[0] = y_scr[...].astype(BF16)


def _ssm_out(xg, mg, wog, flip, hre, him):
    hspec = pl.BlockSpec((1, N_CHUNKS, GROUPS_PER_BLOCK * DIR_STATE), lambda j, b: (b, 0, j))
    wspec = pl.BlockSpec((GROUPS_PER_BLOCK, GROUP_COLS, GROUP_COLS), lambda j, b: (j, 0, 0))
    return pl.pallas_call(
        _ssm_out_kernel,
        grid=(GROUP_BLOCKS, BATCH),
        in_specs=[
            pl.BlockSpec((GROUPS_PER_BLOCK, 1, N_CHUNKS, GROUP_COLS), lambda j, b: (j, b, 0, 0)),
            wspec, wspec, pl.BlockSpec(flip.shape, lambda j, b: (0, 0)), hspec, hspec,
        ],
        out_specs=pl.BlockSpec((1, SEQ, LANES), lambda j, b: (j, b, 0)),
        out_shape=jax.ShapeDtypeStruct((SSM_LANE_BLOCKS, TOKENS, LANES), BF16),
        scratch_shapes=[pltpu.VMEM((SEQ, LANES), F32)],
        compiler_params=_cparams(("parallel", "parallel")),
        name="ssm_out",
    )(xg.reshape(SSM_GROUPS, BATCH, N_CHUNKS, GROUP_COLS), mg, wog, flip, hre, him)


MERGE_TM = 512
GELU_C = math.sqrt(2.0 / math.pi)
PACK_SUB = D_MODEL // LANES


def _split_bf16(v):
    hi = v.astype(BF16)
    lo = (v - hi.astype(F32)).astype(BF16)
    return hi, lo


def _pack_rows(v, out_ref):
    for s in range(PACK_SUB):
        out_ref[pl.ds(s, v.shape[0], stride=PACK_SUB), :] = v[:, s * LANES:(s + 1) * LANES]


def _unpack_rows(buf_ref, start, rows):
    return jnp.concatenate([buf_ref[pl.ds(start + s, rows, stride=PACK_SUB), :] for s in range(PACK_SUB)], axis=1)


def _merge_kernel(x_ref, fm_ref, yc_ref, us_ref, gates_ref, dskip_ref, wf_ref, wglu_ref, ws_ref, wo_ref,
                  ng_ref, wrh_ref, wrl_ref, rb_ref, x1_ref, hp_ref, info_ref, fields_ref, cnt_ref, carry):
    @pl.when(pl.program_id(0) == 0)
    def _():
        carry[...] = jnp.zeros_like(carry)

    conv = jnp.concatenate([yc_ref[j].astype(F32) for j in range(SSM_LANE_BLOCKS)], axis=-1)
    u = jnp.concatenate([us_ref[j].astype(F32) for j in range(SSM_LANE_BLOCKS)], axis=-1)
    y = conv + dskip_ref[...] * u
    y = 0.5 * y * (1.0 + jnp.tanh(GELU_C * (y + 0.044715 * (y * y * y))))
    glu = jax.nn.sigmoid(jnp.dot(y.astype(BF16), wglu_ref[...], preferred_element_type=F32))
    y_s = jnp.dot((y * glu).astype(BF16), ws_ref[...], preferred_element_type=F32)
    y_f = jnp.dot(fm_ref[...], wf_ref[...], preferred_element_type=F32)
    merged = (gates_ref[:, :D_MODEL].astype(F32) * y_f + gates_ref[:, D_MODEL:].astype(F32) * y_s)
    x1 = x_ref[...] + jnp.dot(merged.astype(BF16), wo_ref[...], preferred_element_type=F32)
    x1_ref[...] = x1
    inv = lax.rsqrt(jnp.mean(x1 * x1, axis=-1, keepdims=True) + RMS_EPS)
    hn = x1 * inv * ng_ref[...]
    _pack_rows(hn, hp_ref)
    hi, lo = _split_bf16(hn)
    logits = (jnp.dot(hi, wrh_ref[...], preferred_element_type=F32)
              + jnp.dot(lo, wrh_ref[...], preferred_element_type=F32)
              + jnp.dot(hi, wrl_ref[...], preferred_element_type=F32))
    _route_tile(logits + rb_ref[...], info_ref, fields_ref, cnt_ref, carry)


def _merge(x, fmix, yconv, us, gates, dskip, wf, wglu, ws, wo, ng, wrh, wrl, rb):
    tm = MERGE_TM
    full = lambda a: pl.BlockSpec(a.shape, lambda i: (0,) * a.ndim)
    return pl.pallas_call(
        _merge_kernel,
        grid=(TOKENS // tm,),
        in_specs=[
            pl.BlockSpec((tm, D_MODEL), lambda i: (i, 0)),
            pl.BlockSpec((tm, FOURIER_WIDTH), lambda i: (i, 0)),
            pl.BlockSpec((SSM_LANE_BLOCKS, tm, LANES), lambda i: (0, i, 0)),
            pl.BlockSpec((SSM_LANE_BLOCKS, tm, LANES), lambda i: (0, i, 0)),
            pl.BlockSpec((tm, 2 * D_MODEL), lambda i: (i, 0)),
            full(dskip), full(wf), full(wglu), full(ws), full(wo), full(ng), full(wrh), full(wrl), full(rb),
        ],
        out_specs=[
            pl.BlockSpec((tm, D_MODEL), lambda i: (i, 0)),
            pl.BlockSpec((tm * PACK_SUB, LANES), lambda i: (i, 0)),
            pl.BlockSpec((tm, ROUTER_COLS), lambda i: (i, 0)),
            pl.BlockSpec((INFO_FIELDS, tm), lambda i: (0, i)),
            pl.BlockSpec((1, ROUTER_COLS), lambda i: (0, 0)),
        ],
        out_shape=[
            jax.ShapeDtypeStruct((TOKENS, D_MODEL), F32),
            jax.ShapeDtypeStruct((TOKENS * PACK_SUB, LANES), F32),
            jax.ShapeDtypeStruct((TOKENS, ROUTER_COLS), F32),
            jax.ShapeDtypeStruct((INFO_FIELDS, TOKENS), F32),
            jax.ShapeDtypeStruct((1, ROUTER_COLS), F32),
        ],
        scratch_shapes=[pltpu.VMEM((1, ROUTER_COLS), F32)],
        compiler_params=_cparams(("arbitrary",)),
        name="merge",
    )(x, fmix, yconv, us, gates, dskip, wf, wglu, ws, wo, ng, wrh, wrl, rb)


EXPERT_LANE0 = MOE_GROUPS
INFO_EXPERT, INFO_RANK, INFO_GATE = 0, 2, 4
INFO_FIELDS = 8


def _route_tile(lg, info_ref, fields_ref, cnt_ref, carry):
    tm = lg.shape[0]
    col_i = lax.broadcasted_iota(jnp.int32, lg.shape, 1)
    col = col_i.astype(F32)
    neg = jnp.float32(-jnp.inf)
    none = jnp.float32(ROUTER_COLS)

    def row_max(v):
        return jnp.max(v, axis=-1, keepdims=True)

    def first_at(v, m):
        return jnp.min(jnp.where(v == m, col, none), axis=-1, keepdims=True)

    gl = jnp.where(col_i < MOE_GROUPS, lg, neg)
    gmax = row_max(gl)
    p_g = 1.0 / jnp.sum(jnp.exp(gl - gmax), axis=-1, keepdims=True)
    lo = EXPERT_LANE0 + first_at(gl, gmax) * EXPERTS_PER_GROUP
    el = jnp.where((col >= lo) & (col < lo + EXPERTS_PER_GROUP), lg, neg)
    l1 = row_max(el)
    i1 = first_at(el, l1)
    el2 = jnp.where(col == i1, neg, el)
    l2 = row_max(el2)
    i2 = first_at(el2, l2)
    r = jnp.exp(l2 - l1)
    w1 = p_g / (1.0 + r)
    w2 = w1 * r

    hit1, hit2 = col == i1, col == i2
    onehot = jnp.where(hit1 | hit2, 1.0, 0.0)
    earlier = lax.broadcasted_iota(jnp.int32, (tm, tm), 0) > lax.broadcasted_iota(jnp.int32, (tm, tm), 1)
    before = jnp.dot(jnp.where(earlier, 1.0, 0.0).astype(BF16), onehot.astype(BF16),
                     preferred_element_type=F32) + carry[...]
    rank1 = jnp.sum(jnp.where(hit1, before, 0.0), axis=-1, keepdims=True)
    rank2 = jnp.sum(jnp.where(hit2, before, 0.0), axis=-1, keepdims=True)
    carry[...] += jnp.sum(onehot, axis=0, keepdims=True)
    cnt_ref[...] = carry[...]

    info = jnp.zeros(lg.shape, F32)
    for lane, v in ((INFO_EXPERT, i1 - EXPERT_LANE0), (INFO_EXPERT + 1, i2 - EXPERT_LANE0), (INFO_RANK, rank1),
                    (INFO_RANK + 1, rank2), (INFO_GATE, w1), (INFO_GATE + 1, w2)):
        info = jnp.where(col_i == lane, v, info)
    info_ref[...] = info
    fields_ref[...] = info.T[:INFO_FIELDS]


def _dispatch_plan(fields, counts):
    expert = fields[INFO_EXPERT:INFO_EXPERT + MOE_TOP_K].astype(jnp.int32)
    rank = fields[INFO_RANK:INFO_RANK + MOE_TOP_K].astype(jnp.int32)
    cnt = counts[0, EXPERT_LANE0:EXPERT_LANE0 + N_EXPERTS].astype(jnp.int32)
    padded = ((cnt + MOE_ROWS - 1) // MOE_ROWS) * MOE_ROWS
    pends = jnp.cumsum(padded)
    pstarts = pends - padded
    ids = jnp.arange(N_EXPERTS, dtype=jnp.int32)
    dest = rank + jnp.sum(jnp.where(expert[..., None] == ids, pstarts, 0), axis=-1)
    n_used = pends[-1] // MOE_ROWS
    blocks = jnp.arange(MOE_BLOCKS, dtype=jnp.int32)
    block_e = jnp.sum((pends[None, :] <= (blocks * MOE_ROWS)[:, None]).astype(jnp.int32), axis=1)
    block_e = jnp.minimum(block_e, N_EXPERTS - 1)
    last_e = jnp.sum(jnp.where(blocks == n_used - 1, block_e, 0))
    used = (cnt > 0).astype(jnp.int32)
    ordinal = jnp.cumsum(used) - used
    n_experts_used = jnp.sum(used)
    by_ordinal = jnp.sum(jnp.where((ordinal[None, :] == ids[:, None]) & (used[None, :] > 0), ids[None, :], 0), axis=1)
    block_ord = jnp.sum(jnp.where(block_e[:, None] == ids[None, :], ordinal[None, :], 0), axis=1)
    block_first = (blocks * MOE_ROWS == jnp.sum(jnp.where(block_e[:, None] == ids[None, :], pstarts[None, :], 0), axis=1))
    block_first = (block_first & (blocks < n_used)).astype(jnp.int32)
    meta = jnp.concatenate([n_used.reshape(1), n_experts_used.reshape(1)]).astype(jnp.int32)
    return dest.reshape(MOE_TOP_K * TOKENS), block_ord.astype(jnp.int32), block_first, by_ordinal.astype(jnp.int32), meta, pends


MOE_SLOT_ROWS = MOE_ROWS * PACK_SUB
INVERT_UNROLL = 8


WEIGHT_SLOTS = 3
GATHER_SLOTS = 3
WEIGHT_DMA_PRIORITY = 1


def _moe_kernel(dest_ref, ord_ref, first_ref, eo_ref, meta_ref, pend_ref, hp_ref, wg_hbm, wu_hbm, wd_hbm, y_ref,
                xbuf, wg_buf, wu_buf, wd_buf, tok_ref, sem, wsem):
    i = pl.program_id(0)
    n_used = meta_ref[0]
    n_experts_used = meta_ref[1]

    def slot_rows(slot):
        return xbuf.at[pl.ds(pl.multiple_of(slot * MOE_SLOT_ROWS, MOE_SLOT_ROWS), MOE_SLOT_ROWS), :]

    def weight_copies(ordinal):
        e = eo_ref[ordinal]
        ws = ordinal % WEIGHT_SLOTS
        return [pltpu.make_async_copy(hbm.at[e], buf.at[ws], wsem.at[ws])
                for hbm, buf in ((wg_hbm, wg_buf), (wu_hbm, wu_buf), (wd_hbm, wd_buf))]

    def invert_dispatch():
        def fill_expert(e, c):
            first = jnp.maximum(pend_ref[e] - MOE_ROWS, 0)

            def fill(r, c2):
                tok_ref[first + r] = (first + r) & (TOKENS - 1)
                return c2

            lax.fori_loop(0, MOE_ROWS, fill, 0, unroll=INVERT_UNROLL)
            return c

        lax.fori_loop(0, N_EXPERTS, fill_expert, 0)

        def place(t, c):
            for k in range(MOE_TOP_K):
                tok_ref[dest_ref[k * TOKENS + t]] = t
            return c

        lax.fori_loop(0, TOKENS, place, 0, unroll=INVERT_UNROLL)

    def gather(block, slot):
        base = block * MOE_ROWS
        for r in range(MOE_ROWS):
            src = hp_ref.at[pl.ds(pl.multiple_of(tok_ref[base + r] * PACK_SUB, PACK_SUB), PACK_SUB), :]
            dst = xbuf.at[pl.ds(pl.multiple_of(slot * MOE_SLOT_ROWS + r * PACK_SUB, PACK_SUB), PACK_SUB), :]
            pltpu.make_async_copy(src, dst, sem.at[slot]).start()

    @pl.when(i == 0)
    def _():
        for ahead in range(WEIGHT_SLOTS - 1):
            @pl.when(ahead < n_experts_used)
            def _():
                for cp in weight_copies(ahead):
                    cp.start(priority=WEIGHT_DMA_PRIORITY)
        invert_dispatch()
        for ahead in range(GATHER_SLOTS - 1):
            gather(jnp.minimum(ahead, n_used - 1), ahead)

    @pl.when(i < n_used)
    def _():
        ordinal = ord_ref[i]

        @pl.when(first_ref[i] == 1)
        def _():
            for cp in weight_copies(ordinal):
                cp.wait()

            @pl.when(ordinal + WEIGHT_SLOTS - 1 < n_experts_used)
            def _():
                for cp in weight_copies(ordinal + WEIGHT_SLOTS - 1):
                    cp.start(priority=WEIGHT_DMA_PRIORITY)

        slot = i % GATHER_SLOTS
        ws = ordinal % WEIGHT_SLOTS
        pltpu.make_async_copy(slot_rows(slot), slot_rows(slot), sem.at[slot]).wait()
        ahead = i + GATHER_SLOTS - 1
        xb = _unpack_rows(xbuf, slot * MOE_SLOT_ROWS, MOE_ROWS).astype(BF16)
        gather(jnp.minimum(ahead, n_used - 1), ahead % GATHER_SLOTS)
        a = jnp.dot(xb, wg_buf[ws].astype(BF16), preferred_element_type=F32)
        u = jnp.dot(xb, wu_buf[ws].astype(BF16), preferred_element_type=F32)
        act = (a * jax.nn.sigmoid(a) * u).astype(BF16)
        y = jnp.dot(act, wd_buf[ws].astype(BF16), preferred_element_type=F32)
        _pack_rows(y, y_ref)

        @pl.when(i == n_used - 1)
        def _():
            for extra in range(1, GATHER_SLOTS):
                other = (i + extra) % GATHER_SLOTS
                pltpu.make_async_copy(slot_rows(other), slot_rows(other), sem.at[other]).wait()

    @pl.when(i >= n_used)
    def _():
        y_ref[...] = jnp.zeros_like(y_ref)


def _moe(dest, block_ord, block_first, by_ordinal, meta, pends, hp, w_gate, w_up, w_down):
    hbm = pl.BlockSpec(memory_space=pl.ANY)
    grid_spec = pltpu.PrefetchScalarGridSpec(
        num_scalar_prefetch=6,
        grid=(MOE_BLOCKS,),
        in_specs=[hbm, hbm, hbm, hbm],
        out_specs=pl.BlockSpec((MOE_SLOT_ROWS, LANES), lambda i, *_: (i, 0)),
        scratch_shapes=[pltpu.VMEM((GATHER_SLOTS * MOE_SLOT_ROWS, LANES), F32),
                        pltpu.VMEM((WEIGHT_SLOTS, D_MODEL, D_EXPERT), F32),
                        pltpu.VMEM((WEIGHT_SLOTS, D_MODEL, D_EXPERT), F32),
                        pltpu.VMEM((WEIGHT_SLOTS, D_EXPERT, D_MODEL), F32),
                        pltpu.SMEM((MOE_BLOCKS * MOE_ROWS,), jnp.int32),
                        pltpu.SemaphoreType.DMA((GATHER_SLOTS,)), pltpu.SemaphoreType.DMA((WEIGHT_SLOTS,))],
    )
    return pl.pallas_call(
        _moe_kernel,
        grid_spec=grid_spec,
        out_shape=jax.ShapeDtypeStruct((MOE_BLOCKS * MOE_SLOT_ROWS, LANES), F32),
        compiler_params=_cparams(("arbitrary",)),
        name="moe",
    )(dest, block_ord, block_first, by_ordinal, meta, pends, hp, w_gate, w_up, w_down)


COMBINE_TM = 256


def _combine_kernel(dest_ref, x1_ref, info_ref, g_ref, y_ref, o_ref, ybuf, sem):
    i = pl.program_id(0)
    last = pl.num_programs(0) - 1
    k_rows = COMBINE_TM * PACK_SUB
    slot_rows = MOE_TOP_K * k_rows

    def slot_ref(slot):
        return ybuf.at[pl.ds(pl.multiple_of(slot * slot_rows, slot_rows), slot_rows), :]

    def gather(tile, slot):
        for r in range(COMBINE_TM):
            for k in range(MOE_TOP_K):
                row = dest_ref[k * TOKENS + tile * COMBINE_TM + r]
                src = y_ref.at[pl.ds(pl.multiple_of(row * PACK_SUB, PACK_SUB), PACK_SUB), :]
                at = slot * slot_rows + k * k_rows + r * PACK_SUB
                dst = ybuf.at[pl.ds(pl.multiple_of(at, PACK_SUB), PACK_SUB), :]
                pltpu.make_async_copy(src, dst, sem.at[slot]).start(priority=k)

    @pl.when(i == 0)
    def _():
        for ahead in range(GATHER_SLOTS - 1):
            gather(ahead, ahead)

    slot = i % GATHER_SLOTS
    pltpu.make_async_copy(slot_ref(slot), slot_ref(slot), sem.at[slot]).wait()
    ahead = i + GATHER_SLOTS - 1
    gather(jnp.minimum(ahead, last), ahead % GATHER_SLOTS)
    x2 = x1_ref[...]
    for k in range(MOE_TOP_K):
        yk = _unpack_rows(ybuf, slot * slot_rows + k * k_rows, COMBINE_TM)
        x2 = x2 + info_ref[:, INFO_GATE + k:INFO_GATE + k + 1] * yk
    inv = lax.rsqrt(jnp.mean(x2 * x2, axis=-1, keepdims=True) + RMS_EPS)
    o_ref[...] = x2 * inv * g_ref[...]

    @pl.when(i == last)
    def _():
        for extra in range(1, GATHER_SLOTS):
            other = (i + extra) % GATHER_SLOTS
            pltpu.make_async_copy(slot_ref(other), slot_ref(other), sem.at[other]).wait()


def _combine(dest, x1, info, g, y_pad):
    tm = COMBINE_TM
    grid_spec = pltpu.PrefetchScalarGridSpec(
        num_scalar_prefetch=1,
        grid=(TOKENS // tm,),
        in_specs=[
            pl.BlockSpec((tm, D_MODEL), lambda i, d: (i, 0)),
            pl.BlockSpec((tm, ROUTER_COLS), lambda i, d: (i, 0)),
            pl.BlockSpec((1, D_MODEL), lambda i, d: (0, 0)),
            pl.BlockSpec(memory_space=pl.ANY),
        ],
        out_specs=pl.BlockSpec((tm, D_MODEL), lambda i, d: (i, 0)),
        scratch_shapes=[pltpu.VMEM((GATHER_SLOTS * MOE_TOP_K * tm * PACK_SUB, LANES), F32),
                        pltpu.SemaphoreType.DMA((GATHER_SLOTS,))],
    )
    return pl.pallas_call(
        _combine_kernel,
        grid_spec=grid_spec,
        out_shape=jax.ShapeDtypeStruct((TOKENS, D_MODEL), F32),
        compiler_params=_cparams(("arbitrary",)),
        name="combine",
    )(dest, x1, info, g, y_pad)


def kernel(x, mix_norm_g, w_in, w_fourier_out, ssm_A_re, ssm_A_im, ssm_log_dt, ssm_B_re, ssm_B_im, ssm_C_re,
           ssm_C_im, ssm_D, ssm_w_glu, w_ssm_out, w_out, ffn_norm_g, router_group_w, router_group_b,
           router_expert_w, router_expert_b, expert_w_gate, expert_w_up, expert_w_down, final_norm_g):
    assert x.shape == (BATCH, SEQ, D_MODEL) and w_in.shape[0] == 1
    tw, f2, cdft = _dft_constants()

    vf, us, xg, gates = _inproj(x, mix_norm_g[0][None], w_in[0].astype(BF16), cdft)
    fmix = _dft(vf, tw, f2)

    mg, wsgt, wog, aq = _ssm_operators(ssm_A_re[0], ssm_A_im[0], ssm_log_dt[0], ssm_B_re[0], ssm_B_im[0],
                                       ssm_C_re[0], ssm_C_im[0])
    flip = jnp.asarray(np.eye(N_CHUNKS)[::-1], BF16)
    sre, sim = _ssm_states(xg, wsgt, flip)
    yconv = _ssm_out(xg, mg, wog, flip, *_ssm_scan(sre, sim, aq))

    w_router = jnp.concatenate([router_group_w[0], router_expert_w[0]], axis=1)
    w_router = jnp.pad(w_router, ((0, 0), (0, ROUTER_COLS - w_router.shape[1])))
    b_router = jnp.concatenate([router_group_b[0], router_expert_b[0]])
    b_router = jnp.pad(b_router, (0, ROUTER_COLS - b_router.shape[0]))[None]
    wr_hi = w_router.astype(BF16)
    wr_lo = (w_router - wr_hi.astype(F32)).astype(BF16)
    x1, hp, info, fields, counts = _merge(
        x.reshape(TOKENS, D_MODEL), fmix, yconv, us, gates, ssm_D[0][None], w_fourier_out[0].astype(BF16),
        ssm_w_glu[0].astype(BF16), w_ssm_out[0].astype(BF16), w_out[0].astype(BF16), ffn_norm_g[0][None],
        wr_hi, wr_lo, b_router)

    dest, block_ord, block_first, by_ordinal, meta, pends = _dispatch_plan(fields, counts)
    y_pad = _moe(dest, block_ord, block_first, by_ordinal, meta, pends, hp, expert_w_gate[0], expert_w_up[0],
                 expert_w_down[0])
    out = _combine(dest, x1, info, final_norm_g[None], y_pad)
    return out.reshape(BATCH, SEQ, D_MODEL)
```

```python
import math

import numpy as np
import jax
import jax.numpy as jnp
from jax import lax
from jax.experimental import pallas as pl
from jax.experimental.pallas import tpu as pltpu

F32 = jnp.float32
BF16 = jnp.bfloat16

D_MODEL = 1024
BATCH = 4
SEQ = 4096
TOKENS = BATCH * SEQ
FOURIER_WIDTH = 512
FOURIER_GROUP_CH = 128
FOURIER_GROUPS = 4
SSM_WIDTH = 512
SSM_GROUP_CH = 16
SSM_GROUPS = 32
SSM_STATE = 64
MOE_GROUPS = 8
EXPERTS_PER_GROUP = 8
N_EXPERTS = 64
MOE_TOP_K = 2
D_EXPERT = 512
RMS_EPS = 1e-6

LANES = 128
SSM_CHUNK = 16
SSM_LANE_BLOCKS = SSM_WIDTH // LANES
GROUPS_PER_BLOCK = LANES // SSM_GROUP_CH
N_CHUNKS = SEQ // SSM_CHUNK
GROUP_COLS = SSM_CHUNK * SSM_GROUP_CH
GROUP_SHIFT_CH = 4
MOE_ROWS = 256
MOE_BLOCKS = TOKENS * MOE_TOP_K // MOE_ROWS + N_EXPERTS
ROUTER_COLS = 128
VMEM_LIMIT = 48 * 1024 * 1024


def _cparams(sem, vmem=VMEM_LIMIT):
    return pltpu.CompilerParams(dimension_semantics=sem, vmem_limit_bytes=vmem)


IN_TM = 512


def _inproj_kernel(x_ref, g_ref, w_ref, cdft_ref, vf_ref, us_ref, xg_ref, gates_ref, zs_scr):
    x = x_ref[0]
    inv = lax.rsqrt(jnp.mean(x * x, axis=-1, keepdims=True) + RMS_EPS)
    h = (x * inv * g_ref[...]).astype(BF16)
    zf = jnp.dot(h, w_ref[:, 0:FOURIER_WIDTH], preferred_element_type=F32).astype(BF16)
    cdft = cdft_ref[...].astype(BF16)
    for g in range(FOURIER_GROUPS):
        sl = slice(g * LANES, (g + 1) * LANES)
        v = jnp.dot(zf[:, sl], cdft, preferred_element_type=F32)
        vf_ref[0, 0, :, sl] = v[:, :LANES].astype(BF16)
        vf_ref[0, 1, :, sl] = v[:, LANES:].astype(BF16)
    zs = jnp.dot(h, w_ref[:, FOURIER_WIDTH:FOURIER_WIDTH + SSM_WIDTH], preferred_element_type=F32)
    for j in range(SSM_LANE_BLOCKS):
        us_ref[j] = zs[:, j * LANES:(j + 1) * LANES].astype(BF16)
        zs_scr[j] = zs[:, j * LANES:(j + 1) * LANES]
    seg = lax.broadcasted_iota(jnp.int32, (IN_TM // SSM_CHUNK, LANES), 1) >> GROUP_SHIFT_CH
    for j in range(SSM_LANE_BLOCKS):
        pieces = [zs_scr[j, pl.ds(q, IN_TM // SSM_CHUNK, stride=SSM_CHUNK), :] for q in range(SSM_CHUNK)]
        for g in range(GROUPS_PER_BLOCK):
            for half in range(SSM_CHUNK // GROUPS_PER_BLOCK):
                acc = jnp.zeros((IN_TM // SSM_CHUNK, LANES), F32)
                for ql in range(GROUPS_PER_BLOCK):
                    shift = ((ql - g) % GROUPS_PER_BLOCK) * SSM_GROUP_CH
                    piece = pieces[half * GROUPS_PER_BLOCK + ql]
                    moved = piece if shift == 0 else pltpu.roll(piece, shift, axis=1)
                    acc = jnp.where(seg == ql, moved, acc)
                xg_ref[j * GROUPS_PER_BLOCK + g, :, half * LANES:(half + 1) * LANES] = acc.astype(BF16)
    base = FOURIER_WIDTH + SSM_WIDTH
    for n in range(4):
        zg = jnp.dot(h, w_ref[:, base + n * 512: base + (n + 1) * 512], preferred_element_type=F32)
        gates_ref[:, n * 512:(n + 1) * 512] = jax.nn.sigmoid(zg).astype(BF16)


def _inproj(x, g, w_in, cdft):
    nt = SEQ // IN_TM
    return pl.pallas_call(
        _inproj_kernel,
        grid=(BATCH, nt),
        in_specs=[
            pl.BlockSpec((1, IN_TM, D_MODEL), lambda b, i: (b, i, 0)),
            pl.BlockSpec((1, D_MODEL), lambda b, i: (0, 0)),
            pl.BlockSpec(w_in.shape, lambda b, i: (0, 0)),
            pl.BlockSpec(cdft.shape, lambda b, i: (0, 0)),
        ],
        out_specs=[
            pl.BlockSpec((1, 2, IN_TM, FOURIER_WIDTH), lambda b, i: (b, 0, i, 0)),
            pl.BlockSpec((SSM_LANE_BLOCKS, IN_TM, LANES), lambda b, i: (0, b * nt + i, 0)),
            pl.BlockSpec((SSM_GROUPS, IN_TM // SSM_CHUNK, GROUP_COLS), lambda b, i: (0, b * nt + i, 0)),
            pl.BlockSpec((IN_TM, 2 * D_MODEL), lambda b, i: (b * nt + i, 0)),
        ],
        out_shape=[
            jax.ShapeDtypeStruct((BATCH, 2, SEQ, FOURIER_WIDTH), BF16),
            jax.ShapeDtypeStruct((SSM_LANE_BLOCKS, TOKENS, LANES), BF16),
            jax.ShapeDtypeStruct((SSM_GROUPS, TOKENS // SSM_CHUNK, GROUP_COLS), BF16),
            jax.ShapeDtypeStruct((TOKENS, 2 * D_MODEL), BF16),
        ],
        scratch_shapes=[pltpu.VMEM((SSM_LANE_BLOCKS, IN_TM, LANES), F32)],
        compiler_params=_cparams(("parallel", "parallel")),
        name="inproj",
    )(x, g, w_in, cdft)


DFT_R1 = 8
DFT_R2 = SEQ // DFT_R1
DFT_LANES = 2 * LANES
DFT_ROWS = 16


def _cmul_const(z, w):
    re, im = z
    if abs(w.imag) < 1e-12:
        return (re, im) if abs(w.real - 1.0) < 1e-12 else (re * w.real, im * w.real)
    if abs(w.real) < 1e-12:
        return (im, -re) if abs(w.imag + 1.0) < 1e-12 else (-im * w.imag, re * w.imag)
    return re * w.real - im * w.imag, re * w.imag + im * w.real


def _fft_blocks(xs):
    n = len(xs)
    if n == 1:
        return xs
    even, odd = _fft_blocks(xs[0::2]), _fft_blocks(xs[1::2])
    out = [None] * n
    for k in range(n // 2):
        tr, ti = _cmul_const(odd[k], np.exp(-2j * np.pi * k / n))
        out[k] = (even[k][0] + tr, even[k][1] + ti)
        out[k + n // 2] = (even[k][0] - tr, even[k][1] - ti)
    return out


def _dft_kernel(v_ref, tw_ref, f_ref, o_ref, a_scr, o_scr):
    def tile(i, c):
        r0 = pl.multiple_of(i * DFT_ROWS, DFT_ROWS)
        for slab in range(DFT_LANES // LANES):
            lanes = slice(slab * LANES, (slab + 1) * LANES)
            xs = [(v_ref[0, 0, pl.ds(s1 * DFT_R2 + r0, DFT_ROWS), lanes].astype(F32),
                   v_ref[0, 1, pl.ds(s1 * DFT_R2 + r0, DFT_ROWS), lanes].astype(F32)) for s1 in range(DFT_R1)]
            for t1, (ar, ai) in enumerate(_fft_blocks(xs)):
                tr, ti = tw_ref[0, t1, pl.ds(r0, DFT_ROWS), :], tw_ref[1, t1, pl.ds(r0, DFT_ROWS), :]
                a_scr[t1, pl.ds(r0, DFT_ROWS), lanes] = (ar * tr - ai * ti).astype(BF16)
                a_scr[t1, pl.ds(DFT_R2 + r0, DFT_ROWS), lanes] = (ar * ti + ai * tr).astype(BF16)
        return c

    lax.fori_loop(0, DFT_R2 // DFT_ROWS, tile, 0)

    f2 = f_ref[...].astype(BF16)
    for t1 in range(DFT_R1):
        r = jnp.dot(f2, a_scr[t1], preferred_element_type=F32)
        for slab in range(DFT_LANES // LANES):
            o_scr[slab, pl.ds(t1, DFT_R2, stride=DFT_R1), :] = r[:, slab * LANES:(slab + 1) * LANES]
    for slab in range(DFT_LANES // LANES):
        o_ref[:, slab * LANES:(slab + 1) * LANES] = o_scr[slab].astype(BF16)


def _dft(v, tw, f2):
    nh = FOURIER_WIDTH // DFT_LANES
    return pl.pallas_call(
        _dft_kernel,
        grid=(BATCH, nh),
        in_specs=[
            pl.BlockSpec((1, 2, SEQ, DFT_LANES), lambda b, h: (b, 0, 0, h)),
            pl.BlockSpec(tw.shape, lambda b, h: (0, 0, 0, 0)),
            pl.BlockSpec(f2.shape, lambda b, h: (0, 0)),
        ],
        out_specs=pl.BlockSpec((SEQ, DFT_LANES), lambda b, h: (b, h)),
        out_shape=jax.ShapeDtypeStruct((TOKENS, FOURIER_WIDTH), BF16),
        scratch_shapes=[pltpu.VMEM((DFT_R1, 2 * DFT_R2, DFT_LANES), BF16),
                        pltpu.VMEM((DFT_LANES // LANES, SEQ, LANES), F32)],
        compiler_params=_cparams(("parallel", "parallel")),
        name="dft",
    )(v, tw, f2)


def _dft_constants():
    t1 = np.arange(DFT_R1)
    s2 = np.arange(DFT_R2)
    ang = 2.0 * np.pi * np.outer(t1, s2) / SEQ
    scale = 1.0 / math.sqrt(SEQ)
    tw = np.stack([np.cos(ang) * scale, -np.sin(ang) * scale])
    tw = np.repeat(tw[..., None], LANES, axis=-1)
    ang2 = 2.0 * np.pi * np.outer(s2, s2) / DFT_R2
    f2 = np.concatenate([np.cos(ang2), np.sin(ang2)], axis=1)
    kc = np.arange(FOURIER_GROUP_CH)
    angc = 2.0 * np.pi * np.outer(kc, kc) / FOURIER_GROUP_CH
    cs = 1.0 / math.sqrt(FOURIER_GROUP_CH)
    cdft = np.concatenate([np.cos(angc) * cs, -np.sin(angc) * cs], axis=1)
    return tuple(jnp.asarray(v, F32) for v in (tw, f2, cdft))


GROUP_BLOCKS = SSM_GROUPS // GROUPS_PER_BLOCK
DIR_STATE = 2 * SSM_STATE
STATE_LANES = SSM_GROUPS * DIR_STATE


def _ssm_operators(a_re, a_im, log_dt, b_re, b_im, c_re, c_im):
    q_len = SSM_CHUNK
    hi = lax.Precision.HIGHEST
    dt = jnp.exp(log_dt)[..., None]
    lr, li = a_re * dt, a_im * dt
    steps = jnp.arange(q_len + 1, dtype=F32)
    mag = jnp.exp(lr[..., None] * steps)
    ang = li[..., None] * steps
    pr, pi = mag * jnp.cos(ang), mag * jnp.sin(ang)
    ar, ai = pr[..., 1], pi[..., 1]
    den = a_re * a_re + a_im * a_im
    cr = ((ar - 1.0) * a_re + ai * a_im) / den
    ci = (ai * a_re - (ar - 1.0) * a_im) / den
    bbr = cr[..., None] * b_re - ci[..., None] * b_im
    bbi = cr[..., None] * b_im + ci[..., None] * b_re

    rq = jnp.asarray(np.kron(np.eye(q_len), np.ones((1, SSM_GROUP_CH))), F32)
    rc = jnp.asarray(np.kron(np.ones((1, q_len)), np.eye(SSM_GROUP_CH)), F32)

    def per_group(x):
        return jnp.swapaxes(x, 0, 1).reshape(SSM_GROUPS, DIR_STATE, x.shape[-1])

    def on_cols(x, rep):
        return jnp.einsum('grk,kc->grc', per_group(x), rep, precision=hi)

    def state_operator(pw_r, pw_i, f_r, f_i, negate_im):
        p_r, p_i, q_r, q_i = on_cols(pw_r, rq), on_cols(pw_i, rq), on_cols(f_r, rc), on_cols(f_i, rc)
        w_im = p_r * q_i + p_i * q_r
        return jnp.concatenate([p_r * q_r - p_i * q_i, -w_im if negate_im else w_im], axis=1).astype(BF16)

    def both(p, fwd, bwd):
        return jnp.stack([p[0][..., fwd], p[1][..., bwd]])

    rev = slice(q_len - 1, None, -1)
    wsgt = state_operator(both(pr, rev, slice(0, q_len)), both(pi, rev, slice(0, q_len)), bbr, bbi, False)
    ctr, cti = jnp.swapaxes(c_re, -1, -2), jnp.swapaxes(c_im, -1, -2)
    wog = state_operator(both(pr, slice(1, None), slice(q_len, 0, -1)), both(pi, slice(1, None), slice(q_len, 0, -1)),
                         ctr, cti, True)

    prq, piq = pr[..., :q_len], pi[..., :q_len]
    cpr = jnp.einsum('dgcn,dgnt->dgtcn', c_re, prq) - jnp.einsum('dgcn,dgnt->dgtcn', c_im, piq)
    cpi = jnp.einsum('dgcn,dgnt->dgtcn', c_re, piq) + jnp.einsum('dgcn,dgnt->dgtcn', c_im, prq)
    kern = jnp.einsum('dgtcn,dgne->dgtec', cpr, bbr) - jnp.einsum('dgtcn,dgne->dgtec', cpi, bbi)
    lags = jnp.concatenate([kern[1][:, :0:-1], kern[0][:, :1] + kern[1][:, :1], kern[0][:, 1:]], axis=1)
    lag_cols = jnp.einsum('gtek,kc->gtec', lags, rc, precision=hi)
    lane_q = jnp.arange(GROUP_COLS, dtype=jnp.int32) // SSM_GROUP_CH
    rows = []
    for qp in range(q_len):
        acc = jnp.zeros(lag_cols[:, 0].shape, F32)
        for q in range(q_len):
            acc = jnp.where(lane_q == q, lag_cols[:, q - qp + q_len - 1], acc)
        rows.append(acc)
    mg = jnp.stack(rows, axis=1)
    mg = mg.reshape(SSM_GROUPS, GROUP_COLS, GROUP_COLS).astype(BF16)

    aq = jnp.stack([per_group(pr[..., q_len:]), per_group(pi[..., q_len:])])
    return mg, wsgt, wog, aq.reshape(2, STATE_LANES)


def _ssm_state_kernel(x_ref, ws_ref, flip_ref, sre_ref, sim_ref):
    backward = (lax.broadcasted_iota(jnp.int32, (N_CHUNKS, DIR_STATE), 1) & SSM_STATE) != 0
    nt = (((1,), (1,)), ((), ()))
    for g in range(GROUPS_PER_BLOCK):
        lanes = slice(g * DIR_STATE, (g + 1) * DIR_STATE)
        for b in range(BATCH):
            x = x_ref[g, b * N_CHUNKS:(b + 1) * N_CHUNKS, :]
            x_rev = jnp.dot(flip_ref[...], x, preferred_element_type=F32).astype(BF16)
            s = lax.dot_general(x, ws_ref[g], nt, preferred_element_type=F32)
            s_rev = lax.dot_general(x_rev, ws_ref[g], nt, preferred_element_type=F32)
            sre_ref[b, :, lanes] = jnp.where(backward, s_rev[:, :DIR_STATE], s[:, :DIR_STATE])
            sim_ref[b, :, lanes] = jnp.where(backward, s_rev[:, DIR_STATE:], s[:, DIR_STATE:])


def _ssm_states(xg, wsgt, flip):
    out_spec = pl.BlockSpec((BATCH, N_CHUNKS, GROUPS_PER_BLOCK * DIR_STATE), lambda j: (0, 0, j))
    shape = jax.ShapeDtypeStruct((BATCH, N_CHUNKS, STATE_LANES), F32)
    return pl.pallas_call(
        _ssm_state_kernel,
        grid=(GROUP_BLOCKS,),
        in_specs=[
            pl.BlockSpec((GROUPS_PER_BLOCK, BATCH * N_CHUNKS, GROUP_COLS), lambda j: (j, 0, 0)),
            pl.BlockSpec((GROUPS_PER_BLOCK, 2 * DIR_STATE, GROUP_COLS), lambda j: (j, 0, 0)),
            pl.BlockSpec(flip.shape, lambda j: (0, 0)),
        ],
        out_specs=[out_spec, out_spec],
        out_shape=[shape, shape],
        compiler_params=_cparams(("parallel",)),
        name="ssm_states",
    )(xg, wsgt, flip)


def _ssm_scan_kernel(sre_ref, sim_ref, aq_ref, hre_ref, him_ref):
    ar, ai = aq_ref[0:1], aq_ref[1:2]

    def body(k, carry):
        hr, hi = carry
        hre_ref[0, pl.ds(k, 1), :] = hr
        him_ref[0, pl.ds(k, 1), :] = hi
        sr, si = sre_ref[0, pl.ds(k, 1), :], sim_ref[0, pl.ds(k, 1), :]
        return ar * hr - ai * hi + sr, ar * hi + ai * hr + si

    z = jnp.zeros((1, STATE_LANES), F32)
    lax.fori_loop(0, N_CHUNKS, body, (z, z))


def _ssm_scan(sre, sim, aq):
    spec = pl.BlockSpec((1, N_CHUNKS, STATE_LANES), lambda b: (b, 0, 0))
    shape = jax.ShapeDtypeStruct(sre.shape, F32)
    return pl.pallas_call(
        _ssm_scan_kernel,
        grid=(BATCH,),
        in_specs=[spec, spec, pl.BlockSpec(aq.shape, lambda b: (0, 0))],
        out_specs=[spec, spec],
        out_shape=[shape, shape],
        compiler_params=_cparams(("parallel",)),
        name="ssm_scan",
    )(sre, sim, aq)


def _ssm_out_kernel(x_ref, m_ref, wo_ref, flip_ref, hre_ref, him_ref, y_ref, y_scr):
    fwd = (lax.broadcasted_iota(jnp.int32, (N_CHUNKS, GROUPS_PER_BLOCK * DIR_STATE), 1) & SSM_STATE) == 0

    def in_chunk_order(h_ref):
        h = h_ref[0].astype(BF16)
        return jnp.where(fwd, h, jnp.dot(flip_ref[...], h, preferred_element_type=F32).astype(BF16))

    h_re, h_im = in_chunk_order(hre_ref), in_chunk_order(him_ref)
    accs = []
    for g in range(GROUPS_PER_BLOCK):
        lanes = slice(g * DIR_STATE, (g + 1) * DIR_STATE)
        h = jnp.concatenate([h_re[:, lanes], h_im[:, lanes]], axis=1)
        accs.append(jnp.dot(x_ref[g, 0], m_ref[g], preferred_element_type=F32)
                    + jnp.dot(h, wo_ref[g], preferred_element_type=F32))
    seg = lax.broadcasted_iota(jnp.int32, (N_CHUNKS, LANES), 1) >> GROUP_SHIFT_CH
    for q in range(SSM_CHUNK):
        half, ql = divmod(q, GROUPS_PER_BLOCK)
        piece = jnp.zeros((N_CHUNKS, LANES), F32)
        for g in range(GROUPS_PER_BLOCK):
            src = accs[g][:, half * LANES:(half + 1) * LANES]
            shift = ((g - ql) % GROUPS_PER_BLOCK) * SSM_GROUP_CH
            moved = src if shift == 0 else pltpu.roll(src, shift, axis=1)
            piece = jnp.where(seg == g, moved, piece)
        y_scr[pl.ds(q, N_CHUNKS, stride=SSM_CHUNK), :] = piece
    y_ref[0] = y_scr[...].astype(BF16)


def _ssm_out(xg, mg, wog, flip, hre, him):
    hspec = pl.BlockSpec((1, N_CHUNKS, GROUPS_PER_BLOCK * DIR_STATE), lambda j, b: (b, 0, j))
    wspec = pl.BlockSpec((GROUPS_PER_BLOCK, GROUP_COLS, GROUP_COLS), lambda j, b: (j, 0, 0))
    return pl.pallas_call(
        _ssm_out_kernel,
        grid=(GROUP_BLOCKS, BATCH),
        in_specs=[
            pl.BlockSpec((GROUPS_PER_BLOCK, 1, N_CHUNKS, GROUP_COLS), lambda j, b: (j, b, 0, 0)),
            wspec, wspec, pl.BlockSpec(flip.shape, lambda j, b: (0, 0)), hspec, hspec,
        ],
        out_specs=pl.BlockSpec((1, SEQ, LANES), lambda j, b: (j, b, 0)),
        out_shape=jax.ShapeDtypeStruct((SSM_LANE_BLOCKS, TOKENS, LANES), BF16),
        scratch_shapes=[pltpu.VMEM((SEQ, LANES), F32)],
        compiler_params=_cparams(("parallel", "parallel")),
        name="ssm_out",
    )(xg.reshape(SSM_GROUPS, BATCH, N_CHUNKS, GROUP_COLS), mg, wog, flip, hre, him)


MERGE_TM = 512
GELU_C = math.sqrt(2.0 / math.pi)
PACK_SUB = D_MODEL // LANES


def _split_bf16(v):
    hi = v.astype(BF16)
    lo = (v - hi.astype(F32)).astype(BF16)
    return hi, lo


def _pack_rows(v, out_ref):
    for s in range(PACK_SUB):
        out_ref[pl.ds(s, v.shape[0], stride=PACK_SUB), :] = v[:, s * LANES:(s + 1) * LANES]


def _unpack_rows(buf_ref, start, rows):
    return jnp.concatenate([buf_ref[pl.ds(start + s, rows, stride=PACK_SUB), :] for s in range(PACK_SUB)], axis=1)


def _merge_kernel(x_ref, fm_ref, yc_ref, us_ref, gates_ref, dskip_ref, wf_ref, wglu_ref, ws_ref, wo_ref,
                  ng_ref, wrh_ref, wrl_ref, rb_ref, x1_ref, hp_ref, info_ref, fields_ref, cnt_ref, carry, lg_scr):
    i = pl.program_id(0)

    @pl.when(i == 0)
    def _():
        carry[...] = jnp.zeros_like(carry)
        lg_scr[...] = jnp.zeros_like(lg_scr)

    conv = jnp.concatenate([yc_ref[j].astype(F32) for j in range(SSM_LANE_BLOCKS)], axis=-1)
    u = jnp.concatenate([us_ref[j].astype(F32) for j in range(SSM_LANE_BLOCKS)], axis=-1)
    y = conv + dskip_ref[...] * u
    y = 0.5 * y * (1.0 + jnp.tanh(GELU_C * (y + 0.044715 * (y * y * y))))
    glu = jax.nn.sigmoid(jnp.dot(y.astype(BF16), wglu_ref[...], preferred_element_type=F32))
    y_s = jnp.dot((y * glu).astype(BF16), ws_ref[...], preferred_element_type=F32)
    y_f = jnp.dot(fm_ref[...], wf_ref[...], preferred_element_type=F32)
    merged = (gates_ref[:, :D_MODEL].astype(F32) * y_f + gates_ref[:, D_MODEL:].astype(F32) * y_s)
    x1 = x_ref[...] + jnp.dot(merged.astype(BF16), wo_ref[...], preferred_element_type=F32)
    x1_ref[...] = x1
    inv = lax.rsqrt(jnp.mean(x1 * x1, axis=-1, keepdims=True) + RMS_EPS)
    hn = x1 * inv * ng_ref[...]
    _pack_rows(hn, hp_ref)
    hi, lo = _split_bf16(hn)
    logits = (jnp.dot(hi, wrh_ref[...], preferred_element_type=F32)
              + jnp.dot(lo, wrh_ref[...], preferred_element_type=F32)
              + jnp.dot(hi, wrl_ref[...], preferred_element_type=F32))
    _route_tile(lg_scr[(i + 1) % 2], info_ref, fields_ref, cnt_ref, carry, jnp.where(i > 0, 1.0, 0.0))
    lg_scr[i % 2] = logits + rb_ref[...]


def _merge(x, fmix, yconv, us, gates, dskip, wf, wglu, ws, wo, ng, wrh, wrl, rb):
    tm = MERGE_TM
    last = TOKENS // tm - 1
    full = lambda a: pl.BlockSpec(a.shape, lambda i: (0,) * a.ndim)
    cur = lambda i: jnp.minimum(i, last)
    routed = lambda i: jnp.maximum(i - 1, 0)
    return pl.pallas_call(
        _merge_kernel,
        grid=(last + 2,),
        in_specs=[
            pl.BlockSpec((tm, D_MODEL), lambda i: (cur(i), 0)),
            pl.BlockSpec((tm, FOURIER_WIDTH), lambda i: (cur(i), 0)),
            pl.BlockSpec((SSM_LANE_BLOCKS, tm, LANES), lambda i: (0, cur(i), 0)),
            pl.BlockSpec((SSM_LANE_BLOCKS, tm, LANES), lambda i: (0, cur(i), 0)),
            pl.BlockSpec((tm, 2 * D_MODEL), lambda i: (cur(i), 0)),
            full(dskip), full(wf), full(wglu), full(ws), full(wo), full(ng), full(wrh), full(wrl), full(rb),
        ],
        out_specs=[
            pl.BlockSpec((tm, D_MODEL), lambda i: (cur(i), 0)),
            pl.BlockSpec((tm * PACK_SUB, LANES), lambda i: (cur(i), 0)),
            pl.BlockSpec((tm, ROUTER_COLS), lambda i: (routed(i), 0)),
            pl.BlockSpec((INFO_FIELDS, tm), lambda i: (0, routed(i))),
            pl.BlockSpec((1, ROUTER_COLS), lambda i: (0, 0)),
        ],
        out_shape=[
            jax.ShapeDtypeStruct((TOKENS, D_MODEL), F32),
            jax.ShapeDtypeStruct((TOKENS * PACK_SUB, LANES), F32),
            jax.ShapeDtypeStruct((TOKENS, ROUTER_COLS), F32),
            jax.ShapeDtypeStruct((INFO_FIELDS, TOKENS), F32),
            jax.ShapeDtypeStruct((1, ROUTER_COLS), F32),
        ],
        scratch_shapes=[pltpu.VMEM((1, ROUTER_COLS), F32),
                        pltpu.VMEM((2, tm, ROUTER_COLS), F32)],
        compiler_params=_cparams(("arbitrary",)),
        name="merge",
    )(x, fmix, yconv, us, gates, dskip, wf, wglu, ws, wo, ng, wrh, wrl, rb)


EXPERT_LANE0 = MOE_GROUPS
INFO_EXPERT, INFO_RANK, INFO_GATE = 0, 2, 4
INFO_FIELDS = 8


def _route_tile(lg, info_ref, fields_ref, cnt_ref, carry, weight):
    tm = lg.shape[0]
    col_i = lax.broadcasted_iota(jnp.int32, lg.shape, 1)
    col = col_i.astype(F32)
    neg = jnp.float32(-jnp.inf)
    none = jnp.float32(ROUTER_COLS)

    def row_max(v):
        return jnp.max(v, axis=-1, keepdims=True)

    def first_at(v, m):
        return jnp.min(jnp.where(v == m, col, none), axis=-1, keepdims=True)

    gl = jnp.where(col_i < MOE_GROUPS, lg, neg)
    gmax = row_max(gl)
    p_g = 1.0 / jnp.sum(jnp.exp(gl - gmax), axis=-1, keepdims=True)
    lo = EXPERT_LANE0 + first_at(gl, gmax) * EXPERTS_PER_GROUP
    el = jnp.where((col >= lo) & (col < lo + EXPERTS_PER_GROUP), lg, neg)
    l1 = row_max(el)
    i1 = first_at(el, l1)
    el2 = jnp.where(col == i1, neg, el)
    l2 = row_max(el2)
    i2 = first_at(el2, l2)
    r = jnp.exp(l2 - l1)
    w1 = p_g / (1.0 + r)
    w2 = w1 * r

    hit1, hit2 = col == i1, col == i2
    onehot = jnp.where(hit1 | hit2, 1.0, 0.0)
    earlier = lax.broadcasted_iota(jnp.int32, (tm, tm), 0) > lax.broadcasted_iota(jnp.int32, (tm, tm), 1)
    before = jnp.dot(jnp.where(earlier, 1.0, 0.0).astype(BF16), onehot.astype(BF16),
                     preferred_element_type=F32) + carry[...]
    rank1 = jnp.sum(jnp.where(hit1, before, 0.0), axis=-1, keepdims=True)
    rank2 = jnp.sum(jnp.where(hit2, before, 0.0), axis=-1, keepdims=True)
    carry[...] += weight * jnp.sum(onehot, axis=0, keepdims=True)
    cnt_ref[...] = carry[...]

    info = jnp.zeros(lg.shape, F32)
    for lane, v in ((INFO_EXPERT, i1 - EXPERT_LANE0), (INFO_EXPERT + 1, i2 - EXPERT_LANE0), (INFO_RANK, rank1),
                    (INFO_RANK + 1, rank2), (INFO_GATE, w1), (INFO_GATE + 1, w2)):
        info = jnp.where(col_i == lane, v, info)
    info_ref[...] = info
    fields_ref[...] = info.T[:INFO_FIELDS]


def _dispatch_plan(fields, counts):
    expert = fields[INFO_EXPERT:INFO_EXPERT + MOE_TOP_K].astype(jnp.int32)
    rank = fields[INFO_RANK:INFO_RANK + MOE_TOP_K].astype(jnp.int32)
    cnt = counts[0, EXPERT_LANE0:EXPERT_LANE0 + N_EXPERTS].astype(jnp.int32)
    padded = ((cnt + MOE_ROWS - 1) // MOE_ROWS) * MOE_ROWS
    pends = jnp.cumsum(padded)
    pstarts = pends - padded
    ids = jnp.arange(N_EXPERTS, dtype=jnp.int32)
    dest = rank + jnp.sum(jnp.where(expert[..., None] == ids, pstarts, 0), axis=-1)
    n_used = pends[-1] // MOE_ROWS
    blocks = jnp.arange(MOE_BLOCKS, dtype=jnp.int32)
    block_e = jnp.sum((pends[None, :] <= (blocks * MOE_ROWS)[:, None]).astype(jnp.int32), axis=1)
    block_e = jnp.minimum(block_e, N_EXPERTS - 1)
    last_e = jnp.sum(jnp.where(blocks == n_used - 1, block_e, 0))
    used = (cnt > 0).astype(jnp.int32)
    ordinal = jnp.cumsum(used) - used
    n_experts_used = jnp.sum(used)
    by_ordinal = jnp.sum(jnp.where((ordinal[None, :] == ids[:, None]) & (used[None, :] > 0), ids[None, :], 0), axis=1)
    block_ord = jnp.sum(jnp.where(block_e[:, None] == ids[None, :], ordinal[None, :], 0), axis=1)
    block_first = (blocks * MOE_ROWS == jnp.sum(jnp.where(block_e[:, None] == ids[None, :], pstarts[None, :], 0), axis=1))
    block_first = (block_first & (blocks < n_used)).astype(jnp.int32)
    meta = jnp.concatenate([n_used.reshape(1), n_experts_used.reshape(1)]).astype(jnp.int32)
    return dest.reshape(MOE_TOP_K * TOKENS), block_ord.astype(jnp.int32), block_first, by_ordinal.astype(jnp.int32), meta, pends


MOE_SLOT_ROWS = MOE_ROWS * PACK_SUB
INVERT_UNROLL = 8


WEIGHT_SLOTS = 3
GATHER_SLOTS = 3
WEIGHT_DMA_PRIORITY = 1


def _moe_kernel(dest_ref, ord_ref, first_ref, eo_ref, meta_ref, pend_ref, hp_ref, wg_hbm, wu_hbm, wd_hbm, y_ref,
                xbuf, wg_buf, wu_buf, wd_buf, tok_ref, sem, wsem):
    i = pl.program_id(0)
    n_used = meta_ref[0]
    n_experts_used = meta_ref[1]

    def slot_rows(slot):
        return xbuf.at[pl.ds(pl.multiple_of(slot * MOE_SLOT_ROWS, MOE_SLOT_ROWS), MOE_SLOT_ROWS), :]

    def weight_copies(ordinal):
        e = eo_ref[ordinal]
        ws = ordinal % WEIGHT_SLOTS
        return [pltpu.make_async_copy(hbm.at[e], buf.at[ws], wsem.at[ws])
                for hbm, buf in ((wg_hbm, wg_buf), (wu_hbm, wu_buf), (wd_hbm, wd_buf))]

    def invert_dispatch():
        def fill_expert(e, c):
            first = jnp.maximum(pend_ref[e] - MOE_ROWS, 0)

            def fill(r, c2):
                tok_ref[first + r] = (first + r) & (TOKENS - 1)
                return c2

            lax.fori_loop(0, MOE_ROWS, fill, 0, unroll=INVERT_UNROLL)
            return c

        lax.fori_loop(0, N_EXPERTS, fill_expert, 0)

        def place(t, c):
            for k in range(MOE_TOP_K):
                tok_ref[dest_ref[k * TOKENS + t]] = t
            return c

        lax.fori_loop(0, TOKENS, place, 0, unroll=INVERT_UNROLL)

    def gather(block, slot):
        base = block * MOE_ROWS
        for r in range(MOE_ROWS):
            src = hp_ref.at[pl.ds(pl.multiple_of(tok_ref[base + r] * PACK_SUB, PACK_SUB), PACK_SUB), :]
            dst = xbuf.at[pl.ds(pl.multiple_of(slot * MOE_SLOT_ROWS + r * PACK_SUB, PACK_SUB), PACK_SUB), :]
            pltpu.make_async_copy(src, dst, sem.at[slot]).start()

    @pl.when(i == 0)
    def _():
        for ahead in range(WEIGHT_SLOTS - 1):
            @pl.when(ahead < n_experts_used)
            def _():
                for cp in weight_copies(ahead):
                    cp.start(priority=WEIGHT_DMA_PRIORITY)
        invert_dispatch()
        for ahead in range(GATHER_SLOTS - 1):
            gather(jnp.minimum(ahead, n_used - 1), ahead)

    @pl.when(i < n_used)
    def _():
        ordinal = ord_ref[i]

        @pl.when(first_ref[i] == 1)
        def _():
            for cp in weight_copies(ordinal):
                cp.wait()

            @pl.when(ordinal + WEIGHT_SLOTS - 1 < n_experts_used)
            def _():
                for cp in weight_copies(ordinal + WEIGHT_SLOTS - 1):
                    cp.start(priority=WEIGHT_DMA_PRIORITY)

        slot = i % GATHER_SLOTS
        ws = ordinal % WEIGHT_SLOTS
        pltpu.make_async_copy(slot_rows(slot), slot_rows(slot), sem.at[slot]).wait()
        ahead = i + GATHER_SLOTS - 1
        xb = _unpack_rows(xbuf, slot * MOE_SLOT_ROWS, MOE_ROWS).astype(BF16)
        gather(jnp.minimum(ahead, n_used - 1), ahead % GATHER_SLOTS)
        a = jnp.dot(xb, wg_buf[ws].astype(BF16), preferred_element_type=F32)
        u = jnp.dot(xb, wu_buf[ws].astype(BF16), preferred_element_type=F32)
        act = (a * jax.nn.sigmoid(a) * u).astype(BF16)
        y = jnp.dot(act, wd_buf[ws].astype(BF16), preferred_element_type=F32)
        _pack_rows(y, y_ref)

        @pl.when(i == n_used - 1)
        def _():
            for extra in range(1, GATHER_SLOTS):
                other = (i + extra) % GATHER_SLOTS
                pltpu.make_async_copy(slot_rows(other), slot_rows(other), sem.at[other]).wait()

    @pl.when(i >= n_used)
    def _():
        y_ref[...] = jnp.zeros_like(y_ref)


def _moe(dest, block_ord, block_first, by_ordinal, meta, pends, hp, w_gate, w_up, w_down):
    hbm = pl.BlockSpec(memory_space=pl.ANY)
    grid_spec = pltpu.PrefetchScalarGridSpec(
        num_scalar_prefetch=6,
        grid=(MOE_BLOCKS,),
        in_specs=[hbm, hbm, hbm, hbm],
        out_specs=pl.BlockSpec((MOE_SLOT_ROWS, LANES), lambda i, *_: (i, 0)),
        scratch_shapes=[pltpu.VMEM((GATHER_SLOTS * MOE_SLOT_ROWS, LANES), F32),
                        pltpu.VMEM((WEIGHT_SLOTS, D_MODEL, D_EXPERT), F32),
                        pltpu.VMEM((WEIGHT_SLOTS, D_MODEL, D_EXPERT), F32),
                        pltpu.VMEM((WEIGHT_SLOTS, D_EXPERT, D_MODEL), F32),
                        pltpu.SMEM((MOE_BLOCKS * MOE_ROWS,), jnp.int32),
                        pltpu.SemaphoreType.DMA((GATHER_SLOTS,)), pltpu.SemaphoreType.DMA((WEIGHT_SLOTS,))],
    )
    return pl.pallas_call(
        _moe_kernel,
        grid_spec=grid_spec,
        out_shape=jax.ShapeDtypeStruct((MOE_BLOCKS * MOE_SLOT_ROWS, LANES), F32),
        compiler_params=_cparams(("arbitrary",)),
        name="moe",
    )(dest, block_ord, block_first, by_ordinal, meta, pends, hp, w_gate, w_up, w_down)


COMBINE_TM = 256


def _combine_kernel(dest_ref, x1_ref, info_ref, g_ref, y_ref, o_ref, ybuf, sem):
    i = pl.program_id(0)
    last = pl.num_programs(0) - 1
    k_rows = COMBINE_TM * PACK_SUB
    slot_rows = MOE_TOP_K * k_rows

    def slot_ref(slot):
        return ybuf.at[pl.ds(pl.multiple_of(slot * slot_rows, slot_rows), slot_rows), :]

    def gather(tile, slot):
        for r in range(COMBINE_TM):
            for k in range(MOE_TOP_K):
                row = dest_ref[k * TOKENS + tile * COMBINE_TM + r]
                src = y_ref.at[pl.ds(pl.multiple_of(row * PACK_SUB, PACK_SUB), PACK_SUB), :]
                at = slot * slot_rows + k * k_rows + r * PACK_SUB
                dst = ybuf.at[pl.ds(pl.multiple_of(at, PACK_SUB), PACK_SUB), :]
                pltpu.make_async_copy(src, dst, sem.at[slot]).start(priority=k)

    @pl.when(i == 0)
    def _():
        for ahead in range(GATHER_SLOTS - 1):
            gather(ahead, ahead)

    slot = i % GATHER_SLOTS
    pltpu.make_async_copy(slot_ref(slot), slot_ref(slot), sem.at[slot]).wait()
    ahead = i + GATHER_SLOTS - 1
    gather(jnp.minimum(ahead, last), ahead % GATHER_SLOTS)
    x2 = x1_ref[...]
    for k in range(MOE_TOP_K):
        yk = _unpack_rows(ybuf, slot * slot_rows + k * k_rows, COMBINE_TM)
        x2 = x2 + info_ref[:, INFO_GATE + k:INFO_GATE + k + 1] * yk
    inv = lax.rsqrt(jnp.mean(x2 * x2, axis=-1, keepdims=True) + RMS_EPS)
    o_ref[...] = x2 * inv * g_ref[...]

    @pl.when(i == last)
    def _():
        for extra in range(1, GATHER_SLOTS):
            other = (i + extra) % GATHER_SLOTS
            pltpu.make_async_copy(slot_ref(other), slot_ref(other), sem.at[other]).wait()


def _combine(dest, x1, info, g, y_pad):
    tm = COMBINE_TM
    grid_spec = pltpu.PrefetchScalarGridSpec(
        num_scalar_prefetch=1,
        grid=(TOKENS // tm,),
        in_specs=[
            pl.BlockSpec((tm, D_MODEL), lambda i, d: (i, 0)),
            pl.BlockSpec((tm, ROUTER_COLS), lambda i, d: (i, 0)),
            pl.BlockSpec((1, D_MODEL), lambda i, d: (0, 0)),
            pl.BlockSpec(memory_space=pl.ANY),
        ],
        out_specs=pl.BlockSpec((tm, D_MODEL), lambda i, d: (i, 0)),
        scratch_shapes=[pltpu.VMEM((GATHER_SLOTS * MOE_TOP_K * tm * PACK_SUB, LANES), F32),
                        pltpu.SemaphoreType.DMA((GATHER_SLOTS,))],
    )
    return pl.pallas_call(
        _combine_kernel,
        grid_spec=grid_spec,
        out_shape=jax.ShapeDtypeStruct((TOKENS, D_MODEL), F32),
        compiler_params=_cparams(("arbitrary",)),
        name="combine",
    )(dest, x1, info, g, y_pad)


def kernel(x, mix_norm_g, w_in, w_fourier_out, ssm_A_re, ssm_A_im, ssm_log_dt, ssm_B_re, ssm_B_im, ssm_C_re,
           ssm_C_im, ssm_D, ssm_w_glu, w_ssm_out, w_out, ffn_norm_g, router_group_w, router_group_b,
           router_expert_w, router_expert_b, expert_w_gate, expert_w_up, expert_w_down, final_norm_g):
    assert x.shape == (BATCH, SEQ, D_MODEL) and w_in.shape[0] == 1
    tw, f2, cdft = _dft_constants()

    vf, us, xg, gates = _inproj(x, mix_norm_g[0][None], w_in[0].astype(BF16), cdft)
    fmix = _dft(vf, tw, f2)

    mg, wsgt, wog, aq = _ssm_operators(ssm_A_re[0], ssm_A_im[0], ssm_log_dt[0], ssm_B_re[0], ssm_B_im[0],
                                       ssm_C_re[0], ssm_C_im[0])
    flip = jnp.asarray(np.eye(N_CHUNKS)[::-1], BF16)
    sre, sim = _ssm_states(xg, wsgt, flip)
    yconv = _ssm_out(xg, mg, wog, flip, *_ssm_scan(sre, sim, aq))

    w_router = jnp.concatenate([router_group_w[0], router_expert_w[0]], axis=1)
    w_router = jnp.pad(w_router, ((0, 0), (0, ROUTER_COLS - w_router.shape[1])))
    b_router = jnp.concatenate([router_group_b[0], router_expert_b[0]])
    b_router = jnp.pad(b_router, (0, ROUTER_COLS - b_router.shape[0]))[None]
    wr_hi = w_router.astype(BF16)
    wr_lo = (w_router - wr_hi.astype(F32)).astype(BF16)
    x1, hp, info, fields, counts = _merge(
        x.reshape(TOKENS, D_MODEL), fmix, yconv, us, gates, ssm_D[0][None], w_fourier_out[0].astype(BF16),
        ssm_w_glu[0].astype(BF16), w_ssm_out[0].astype(BF16), w_out[0].astype(BF16), ffn_norm_g[0][None],
        wr_hi, wr_lo, b_router)

    dest, block_ord, block_first, by_ordinal, meta, pends = _dispatch_plan(fields, counts)
    y_pad = _moe(dest, block_ord, block_first, by_ordinal, meta, pends, hp, expert_w_gate[0], expert_w_up[0],
                 expert_w_down[0])
    out = _combine(dest, x1, info, final_norm_g[None], y_pad)
    return out.reshape(BATCH, SEQ, D_MODEL)
```

```python
import math

import numpy as np
import jax
import jax.numpy as jnp
from jax import lax
from jax.experimental import pallas as pl
from jax.experimental.pallas import tpu as pltpu

F32 = jnp.float32
BF16 = jnp.bfloat16

D_MODEL = 1024
BATCH = 4
SEQ = 4096
TOKENS = BATCH * SEQ
FOURIER_WIDTH = 512
FOURIER_GROUP_CH = 128
FOURIER_GROUPS = 4
SSM_WIDTH = 512
SSM_GROUP_CH = 16
SSM_GROUPS = 32
SSM_STATE = 64
MOE_GROUPS = 8
EXPERTS_PER_GROUP = 8
N_EXPERTS = 64
MOE_TOP_K = 2
D_EXPERT = 512
RMS_EPS = 1e-6

LANES = 128
SSM_CHUNK = 16
SSM_LANE_BLOCKS = SSM_WIDTH // LANES
GROUPS_PER_BLOCK = LANES // SSM_GROUP_CH
N_CHUNKS = SEQ // SSM_CHUNK
GROUP_COLS = SSM_CHUNK * SSM_GROUP_CH
GROUP_SHIFT_CH = 4
MOE_ROWS = 256
MOE_BLOCKS = TOKENS * MOE_TOP_K // MOE_ROWS + N_EXPERTS
ROUTER_COLS = 128
VMEM_LIMIT = 48 * 1024 * 1024


def _cparams(sem, vmem=VMEM_LIMIT):
    return pltpu.CompilerParams(dimension_semantics=sem, vmem_limit_bytes=vmem)


IN_TM = 512


def _inproj_kernel(x_ref, g_ref, w_ref, cdft_ref, vf_ref, us_ref, xg_ref, gates_ref, zs_scr):
    x = x_ref[0]
    inv = lax.rsqrt(jnp.mean(x * x, axis=-1, keepdims=True) + RMS_EPS)
    h = (x * inv * g_ref[...]).astype(BF16)
    zf = jnp.dot(h, w_ref[:, 0:FOURIER_WIDTH], preferred_element_type=F32).astype(BF16)
    cdft = cdft_ref[...].astype(BF16)
    for g in range(FOURIER_GROUPS):
        sl = slice(g * LANES, (g + 1) * LANES)
        v = jnp.dot(zf[:, sl], cdft, preferred_element_type=F32)
        vf_ref[0, 0, :, sl] = v[:, :LANES].astype(BF16)
        vf_ref[0, 1, :, sl] = v[:, LANES:].astype(BF16)
    zs = jnp.dot(h, w_ref[:, FOURIER_WIDTH:FOURIER_WIDTH + SSM_WIDTH], preferred_element_type=F32)
    for j in range(SSM_LANE_BLOCKS):
        us_ref[j] = zs[:, j * LANES:(j + 1) * LANES].astype(BF16)
        zs_scr[j] = zs[:, j * LANES:(j + 1) * LANES]
    seg = lax.broadcasted_iota(jnp.int32, (IN_TM // SSM_CHUNK, LANES), 1) >> GROUP_SHIFT_CH
    for j in range(SSM_LANE_BLOCKS):
        pieces = [zs_scr[j, pl.ds(q, IN_TM // SSM_CHUNK, stride=SSM_CHUNK), :] for q in range(SSM_CHUNK)]
        for g in range(GROUPS_PER_BLOCK):
            for half in range(SSM_CHUNK // GROUPS_PER_BLOCK):
                acc = jnp.zeros((IN_TM // SSM_CHUNK, LANES), F32)
                for ql in range(GROUPS_PER_BLOCK):
                    shift = ((ql - g) % GROUPS_PER_BLOCK) * SSM_GROUP_CH
                    piece = pieces[half * GROUPS_PER_BLOCK + ql]
                    moved = piece if shift == 0 else pltpu.roll(piece, shift, axis=1)
                    acc = jnp.where(seg == ql, moved, acc)
                xg_ref[j * GROUPS_PER_BLOCK + g, :, half * LANES:(half + 1) * LANES] = acc.astype(BF16)
    base = FOURIER_WIDTH + SSM_WIDTH
    for n in range(4):
        zg = jnp.dot(h, w_ref[:, base + n * 512: base + (n + 1) * 512], preferred_element_type=F32)
        gates_ref[:, n * 512:(n + 1) * 512] = jax.nn.sigmoid(zg).astype(BF16)


def _inproj(x, g, w_in, cdft):
    nt = SEQ // IN_TM
    return pl.pallas_call(
        _inproj_kernel,
        grid=(BATCH, nt),
        in_specs=[
            pl.BlockSpec((1, IN_TM, D_MODEL), lambda b, i: (b, i, 0)),
            pl.BlockSpec((1, D_MODEL), lambda b, i: (0, 0)),
            pl.BlockSpec(w_in.shape, lambda b, i: (0, 0)),
            pl.BlockSpec(cdft.shape, lambda b, i: (0, 0)),
        ],
        out_specs=[
            pl.BlockSpec((1, 2, IN_TM, FOURIER_WIDTH), lambda b, i: (b, 0, i, 0)),
            pl.BlockSpec((SSM_LANE_BLOCKS, IN_TM, LANES), lambda b, i: (0, b * nt + i, 0)),
            pl.BlockSpec((SSM_GROUPS, IN_TM // SSM_CHUNK, GROUP_COLS), lambda b, i: (0, b * nt + i, 0)),
            pl.BlockSpec((IN_TM, 2 * D_MODEL), lambda b, i: (b * nt + i, 0)),
        ],
        out_shape=[
            jax.ShapeDtypeStruct((BATCH, 2, SEQ, FOURIER_WIDTH), BF16),
            jax.ShapeDtypeStruct((SSM_LANE_BLOCKS, TOKENS, LANES), BF16),
            jax.ShapeDtypeStruct((SSM_GROUPS, TOKENS // SSM_CHUNK, GROUP_COLS), BF16),
            jax.ShapeDtypeStruct((TOKENS, 2 * D_MODEL), BF16),
        ],
        scratch_shapes=[pltpu.VMEM((SSM_LANE_BLOCKS, IN_TM, LANES), F32)],
        compiler_params=_cparams(("parallel", "parallel")),
        name="inproj",
    )(x, g, w_in, cdft)


DFT_R1 = 8
DFT_R2 = SEQ // DFT_R1
DFT_LANES = 2 * LANES
DFT_ROWS = 16


def _cmul_const(z, w):
    re, im = z
    if abs(w.imag) < 1e-12:
        return (re, im) if abs(w.real - 1.0) < 1e-12 else (re * w.real, im * w.real)
    if abs(w.real) < 1e-12:
        return (im, -re) if abs(w.imag + 1.0) < 1e-12 else (-im * w.imag, re * w.imag)
    return re * w.real - im * w.imag, re * w.imag + im * w.real


def _fft_blocks(xs):
    n = len(xs)
    if n == 1:
        return xs
    even, odd = _fft_blocks(xs[0::2]), _fft_blocks(xs[1::2])
    out = [None] * n
    for k in range(n // 2):
        tr, ti = _cmul_const(odd[k], np.exp(-2j * np.pi * k / n))
        out[k] = (even[k][0] + tr, even[k][1] + ti)
        out[k + n // 2] = (even[k][0] - tr, even[k][1] - ti)
    return out


def _dft_kernel(v_ref, tw_ref, f_ref, o_ref, a_scr, o_scr):
    def tile(i, c):
        r0 = pl.multiple_of(i * DFT_ROWS, DFT_ROWS)
        for slab in range(DFT_LANES // LANES):
            lanes = slice(slab * LANES, (slab + 1) * LANES)
            xs = [(v_ref[0, 0, pl.ds(s1 * DFT_R2 + r0, DFT_ROWS), lanes].astype(F32),
                   v_ref[0, 1, pl.ds(s1 * DFT_R2 + r0, DFT_ROWS), lanes].astype(F32)) for s1 in range(DFT_R1)]
            for t1, (ar, ai) in enumerate(_fft_blocks(xs)):
                tr, ti = tw_ref[0, t1, pl.ds(r0, DFT_ROWS), :], tw_ref[1, t1, pl.ds(r0, DFT_ROWS), :]
                a_scr[t1, pl.ds(r0, DFT_ROWS), lanes] = (ar * tr - ai * ti).astype(BF16)
                a_scr[t1, pl.ds(DFT_R2 + r0, DFT_ROWS), lanes] = (ar * ti + ai * tr).astype(BF16)
        return c

    lax.fori_loop(0, DFT_R2 // DFT_ROWS, tile, 0)

    f2 = f_ref[...].astype(BF16)
    for t1 in range(DFT_R1):
        r = jnp.dot(f2, a_scr[t1], preferred_element_type=F32)
        for slab in range(DFT_LANES // LANES):
            o_scr[slab, pl.ds(t1, DFT_R2, stride=DFT_R1), :] = r[:, slab * LANES:(slab + 1) * LANES]
    for slab in range(DFT_LANES // LANES):
        o_ref[:, slab * LANES:(slab + 1) * LANES] = o_scr[slab].astype(BF16)


def _dft(v, tw, f2):
    nh = FOURIER_WIDTH // DFT_LANES
    return pl.pallas_call(
        _dft_kernel,
        grid=(BATCH, nh),
        in_specs=[
            pl.BlockSpec((1, 2, SEQ, DFT_LANES), lambda b, h: (b, 0, 0, h)),
            pl.BlockSpec(tw.shape, lambda b, h: (0, 0, 0, 0)),
            pl.BlockSpec(f2.shape, lambda b, h: (0, 0)),
        ],
        out_specs=pl.BlockSpec((SEQ, DFT_LANES), lambda b, h: (b, h)),
        out_shape=jax.ShapeDtypeStruct((TOKENS, FOURIER_WIDTH), BF16),
        scratch_shapes=[pltpu.VMEM((DFT_R1, 2 * DFT_R2, DFT_LANES), BF16),
                        pltpu.VMEM((DFT_LANES // LANES, SEQ, LANES), F32)],
        compiler_params=_cparams(("parallel", "parallel")),
        name="dft",
    )(v, tw, f2)


def _dft_constants():
    t1 = np.arange(DFT_R1)
    s2 = np.arange(DFT_R2)
    ang = 2.0 * np.pi * np.outer(t1, s2) / SEQ
    scale = 1.0 / math.sqrt(SEQ)
    tw = np.stack([np.cos(ang) * scale, -np.sin(ang) * scale])
    tw = np.repeat(tw[..., None], LANES, axis=-1)
    ang2 = 2.0 * np.pi * np.outer(s2, s2) / DFT_R2
    f2 = np.concatenate([np.cos(ang2), np.sin(ang2)], axis=1)
    kc = np.arange(FOURIER_GROUP_CH)
    angc = 2.0 * np.pi * np.outer(kc, kc) / FOURIER_GROUP_CH
    cs = 1.0 / math.sqrt(FOURIER_GROUP_CH)
    cdft = np.concatenate([np.cos(angc) * cs, -np.sin(angc) * cs], axis=1)
    return tuple(jnp.asarray(v, F32) for v in (tw, f2, cdft))


GROUP_BLOCKS = SSM_GROUPS // GROUPS_PER_BLOCK
DIR_STATE = 2 * SSM_STATE
STATE_LANES = SSM_GROUPS * DIR_STATE


def _ssm_operators(a_re, a_im, log_dt, b_re, b_im, c_re, c_im):
    q_len = SSM_CHUNK
    hi = lax.Precision.HIGHEST
    dt = jnp.exp(log_dt)[..., None]
    lr, li = a_re * dt, a_im * dt
    steps = jnp.arange(q_len + 1, dtype=F32)
    mag = jnp.exp(lr[..., None] * steps)
    ang = li[..., None] * steps
    pr, pi = mag * jnp.cos(ang), mag * jnp.sin(ang)
    ar, ai = pr[..., 1], pi[..., 1]
    den = a_re * a_re + a_im * a_im
    cr = ((ar - 1.0) * a_re + ai * a_im) / den
    ci = (ai * a_re - (ar - 1.0) * a_im) / den
    bbr = cr[..., None] * b_re - ci[..., None] * b_im
    bbi = cr[..., None] * b_im + ci[..., None] * b_re

    rq = jnp.asarray(np.kron(np.eye(q_len), np.ones((1, SSM_GROUP_CH))), F32)
    rc = jnp.asarray(np.kron(np.ones((1, q_len)), np.eye(SSM_GROUP_CH)), F32)

    def per_group(x):
        return jnp.swapaxes(x, 0, 1).reshape(SSM_GROUPS, DIR_STATE, x.shape[-1])

    def on_cols(x, rep):
        return jnp.einsum('grk,kc->grc', per_group(x), rep, precision=hi)

    def state_operator(pw_r, pw_i, f_r, f_i, negate_im):
        p_r, p_i, q_r, q_i = on_cols(pw_r, rq), on_cols(pw_i, rq), on_cols(f_r, rc), on_cols(f_i, rc)
        w_im = p_r * q_i + p_i * q_r
        return jnp.concatenate([p_r * q_r - p_i * q_i, -w_im if negate_im else w_im], axis=1).astype(BF16)

    def both(p, fwd, bwd):
        return jnp.stack([p[0][..., fwd], p[1][..., bwd]])

    rev = slice(q_len - 1, None, -1)
    wsgt = state_operator(both(pr, rev, slice(0, q_len)), both(pi, rev, slice(0, q_len)), bbr, bbi, False)
    ctr, cti = jnp.swapaxes(c_re, -1, -2), jnp.swapaxes(c_im, -1, -2)
    wog = state_operator(both(pr, slice(1, None), slice(q_len, 0, -1)), both(pi, slice(1, None), slice(q_len, 0, -1)),
                         ctr, cti, True)

    prq, piq = pr[..., :q_len], pi[..., :q_len]
    cpr = jnp.einsum('dgcn,dgnt->dgtcn', c_re, prq) - jnp.einsum('dgcn,dgnt->dgtcn', c_im, piq)
    cpi = jnp.einsum('dgcn,dgnt->dgtcn', c_re, piq) + jnp.einsum('dgcn,dgnt->dgtcn', c_im, prq)
    kern = jnp.einsum('dgtcn,dgne->dgtec', cpr, bbr) - jnp.einsum('dgtcn,dgne->dgtec', cpi, bbi)
    lags = jnp.concatenate([kern[1][:, :0:-1], kern[0][:, :1] + kern[1][:, :1], kern[0][:, 1:]], axis=1)
    lag_cols = jnp.einsum('gtek,kc->gtec', lags, rc, precision=hi)
    lane_q = jnp.arange(GROUP_COLS, dtype=jnp.int32) // SSM_GROUP_CH
    rows = []
    for qp in range(q_len):
        acc = jnp.zeros(lag_cols[:, 0].shape, F32)
        for q in range(q_len):
            acc = jnp.where(lane_q == q, lag_cols[:, q - qp + q_len - 1], acc)
        rows.append(acc)
    mg = jnp.stack(rows, axis=1)
    mg = mg.reshape(SSM_GROUPS, GROUP_COLS, GROUP_COLS).astype(BF16)

    aq = jnp.stack([per_group(pr[..., q_len:]), per_group(pi[..., q_len:])])
    return mg, wsgt, wog, aq.reshape(2, STATE_LANES)


def _ssm_state_kernel(x_ref, ws_ref, flip_ref, sre_ref, sim_ref):
    backward = (lax.broadcasted_iota(jnp.int32, (N_CHUNKS, DIR_STATE), 1) & SSM_STATE) != 0
    nt = (((1,), (1,)), ((), ()))
    for g in range(GROUPS_PER_BLOCK):
        lanes = slice(g * DIR_STATE, (g + 1) * DIR_STATE)
        for b in range(BATCH):
            x = x_ref[g, b * N_CHUNKS:(b + 1) * N_CHUNKS, :]
            x_rev = jnp.dot(flip_ref[...], x, preferred_element_type=F32).astype(BF16)
            s = lax.dot_general(x, ws_ref[g], nt, preferred_element_type=F32)
            s_rev = lax.dot_general(x_rev, ws_ref[g], nt, preferred_element_type=F32)
            sre_ref[b, :, lanes] = jnp.where(backward, s_rev[:, :DIR_STATE], s[:, :DIR_STATE])
            sim_ref[b, :, lanes] = jnp.where(backward, s_rev[:, DIR_STATE:], s[:, DIR_STATE:])


def _ssm_states(xg, wsgt, flip):
    out_spec = pl.BlockSpec((BATCH, N_CHUNKS, GROUPS_PER_BLOCK * DIR_STATE), lambda j: (0, 0, j))
    shape = jax.ShapeDtypeStruct((BATCH, N_CHUNKS, STATE_LANES), F32)
    return pl.pallas_call(
        _ssm_state_kernel,
        grid=(GROUP_BLOCKS,),
        in_specs=[
            pl.BlockSpec((GROUPS_PER_BLOCK, BATCH * N_CHUNKS, GROUP_COLS), lambda j: (j, 0, 0)),
            pl.BlockSpec((GROUPS_PER_BLOCK, 2 * DIR_STATE, GROUP_COLS), lambda j: (j, 0, 0)),
            pl.BlockSpec(flip.shape, lambda j: (0, 0)),
        ],
        out_specs=[out_spec, out_spec],
        out_shape=[shape, shape],
        compiler_params=_cparams(("parallel",)),
        name="ssm_states",
    )(xg, wsgt, flip)


def _ssm_scan_kernel(sre_ref, sim_ref, aq_ref, hre_ref, him_ref):
    ar, ai = aq_ref[0:1], aq_ref[1:2]

    def body(k, carry):
        hr, hi = carry
        hre_ref[0, pl.ds(k, 1), :] = hr
        him_ref[0, pl.ds(k, 1), :] = hi
        sr, si = sre_ref[0, pl.ds(k, 1), :], sim_ref[0, pl.ds(k, 1), :]
        return ar * hr - ai * hi + sr, ar * hi + ai * hr + si

    z = jnp.zeros((1, STATE_LANES), F32)
    lax.fori_loop(0, N_CHUNKS, body, (z, z))


def _ssm_scan(sre, sim, aq):
    spec = pl.BlockSpec((1, N_CHUNKS, STATE_LANES), lambda b: (b, 0, 0))
    shape = jax.ShapeDtypeStruct(sre.shape, F32)
    return pl.pallas_call(
        _ssm_scan_kernel,
        grid=(BATCH,),
        in_specs=[spec, spec, pl.BlockSpec(aq.shape, lambda b: (0, 0))],
        out_specs=[spec, spec],
        out_shape=[shape, shape],
        compiler_params=_cparams(("parallel",)),
        name="ssm_scan",
    )(sre, sim, aq)


def _ssm_out_kernel(x_ref, m_ref, wo_ref, flip_ref, hre_ref, him_ref, y_ref, y_scr):
    fwd = (lax.broadcasted_iota(jnp.int32, (N_CHUNKS, GROUPS_PER_BLOCK * DIR_STATE), 1) & SSM_STATE) == 0

    def in_chunk_order(h_ref):
        h = h_ref[0].astype(BF16)
        return jnp.where(fwd, h, jnp.dot(flip_ref[...], h, preferred_element_type=F32).astype(BF16))

    h_re, h_im = in_chunk_order(hre_ref), in_chunk_order(him_ref)
    accs = []
    for g in range(GROUPS_PER_BLOCK):
        lanes = slice(g * DIR_STATE, (g + 1) * DIR_STATE)
        h = jnp.concatenate([h_re[:, lanes], h_im[:, lanes]], axis=1)
        accs.append(jnp.dot(x_ref[g, 0], m_ref[g], preferred_element_type=F32)
                    + jnp.dot(h, wo_ref[g], preferred_element_type=F32))
    seg = lax.broadcasted_iota(jnp.int32, (N_CHUNKS, LANES), 1) >> GROUP_SHIFT_CH
    for q in range(SSM_CHUNK):
        half, ql = divmod(q, GROUPS_PER_BLOCK)
        piece = jnp.zeros((N_CHUNKS, LANES), F32)
        for g in range(GROUPS_PER_BLOCK):
            src = accs[g][:, half * LANES:(half + 1) * LANES]
            shift = ((g - ql) % GROUPS_PER_BLOCK) * SSM_GROUP_CH
            moved = src if shift == 0 else pltpu.roll(src, shift, axis=1)
            piece = jnp.where(seg == g, moved, piece)
        y_scr[pl.ds(q, N_CHUNKS, stride=SSM_CHUNK), :] = piece
    y_ref[0] = y_scr[...].astype(BF16)


def _ssm_out(xg, mg, wog, flip, hre, him):
    hspec = pl.BlockSpec((1, N_CHUNKS, GROUPS_PER_BLOCK * DIR_STATE), lambda j, b: (b, 0, j))
    wspec = pl.BlockSpec((GROUPS_PER_BLOCK, GROUP_COLS, GROUP_COLS), lambda j, b: (j, 0, 0))
    return pl.pallas_call(
        _ssm_out_kernel,
        grid=(GROUP_BLOCKS, BATCH),
        in_specs=[
            pl.BlockSpec((GROUPS_PER_BLOCK, 1, N_CHUNKS, GROUP_COLS), lambda j, b: (j, b, 0, 0)),
            wspec, wspec, pl.BlockSpec(flip.shape, lambda j, b: (0, 0)), hspec, hspec,
        ],
        out_specs=pl.BlockSpec((1, SEQ, LANES), lambda j, b: (j, b, 0)),
        out_shape=jax.ShapeDtypeStruct((SSM_LANE_BLOCKS, TOKENS, LANES), BF16),
        scratch_shapes=[pltpu.VMEM((SEQ, LANES), F32)],
        compiler_params=_cparams(("parallel", "parallel")),
        name="ssm_out",
    )(xg.reshape(SSM_GROUPS, BATCH, N_CHUNKS, GROUP_COLS), mg, wog, flip, hre, him)


MERGE_TM = 512
GELU_C = math.sqrt(2.0 / math.pi)
PACK_SUB = D_MODEL // LANES


def _split_bf16(v):
    hi = v.astype(BF16)
    lo = (v - hi.astype(F32)).astype(BF16)
    return hi, lo


def _pack_rows(v, out_ref):
    for s in range(PACK_SUB):
        out_ref[pl.ds(s, v.shape[0], stride=PACK_SUB), :] = v[:, s * LANES:(s + 1) * LANES]


def _unpack_rows(buf_ref, start, rows):
    return jnp.concatenate([buf_ref[pl.ds(start + s, rows, stride=PACK_SUB), :] for s in range(PACK_SUB)], axis=1)


def _merge_kernel(x_ref, fm_ref, yc_ref, us_ref, gates_ref, dskip_ref, wf_ref, wglu_ref, ws_ref, wo_ref,
                  ng_ref, wrh_ref, wrl_ref, rb_ref, x1_ref, hp_ref, info_ref, fields_ref, cnt_ref, carry, lg_scr):
    i = pl.program_id(0)

    @pl.when(i == 0)
    def _():
        carry[...] = jnp.zeros_like(carry)
        lg_scr[...] = jnp.zeros_like(lg_scr)

    conv = jnp.concatenate([yc_ref[j].astype(F32) for j in range(SSM_LANE_BLOCKS)], axis=-1)
    u = jnp.concatenate([us_ref[j].astype(F32) for j in range(SSM_LANE_BLOCKS)], axis=-1)
    y = conv + dskip_ref[...] * u
    y = 0.5 * y * (1.0 + jnp.tanh(GELU_C * (y + 0.044715 * (y * y * y))))
    glu = jax.nn.sigmoid(jnp.dot(y.astype(BF16), wglu_ref[...], preferred_element_type=F32))
    y_s = jnp.dot((y * glu).astype(BF16), ws_ref[...], preferred_element_type=F32)
    y_f = jnp.dot(fm_ref[...], wf_ref[...], preferred_element_type=F32)
    merged = (gates_ref[:, :D_MODEL].astype(F32) * y_f + gates_ref[:, D_MODEL:].astype(F32) * y_s)
    x1 = x_ref[...] + jnp.dot(merged.astype(BF16), wo_ref[...], preferred_element_type=F32)
    x1_ref[...] = x1
    inv = lax.rsqrt(jnp.mean(x1 * x1, axis=-1, keepdims=True) + RMS_EPS)
    hn = x1 * inv * ng_ref[...]
    _pack_rows(hn, hp_ref)
    hi, lo = _split_bf16(hn)
    logits = (jnp.dot(hi, wrh_ref[...], preferred_element_type=F32)
              + jnp.dot(lo, wrh_ref[...], preferred_element_type=F32)
              + jnp.dot(hi, wrl_ref[...], preferred_element_type=F32))
    _route_tile(lg_scr[(i + 1) % 2], info_ref, fields_ref, cnt_ref, carry, jnp.where(i > 0, 1.0, 0.0))
    lg_scr[i % 2] = logits + rb_ref[...]


def _merge(x, fmix, yconv, us, gates, dskip, wf, wglu, ws, wo, ng, wrh, wrl, rb):
    tm = MERGE_TM
    last = TOKENS // tm - 1
    full = lambda a: pl.BlockSpec(a.shape, lambda i: (0,) * a.ndim)
    cur = lambda i: jnp.minimum(i, last)
    routed = lambda i: jnp.maximum(i - 1, 0)
    return pl.pallas_call(
        _merge_kernel,
        grid=(last + 2,),
        in_specs=[
            pl.BlockSpec((tm, D_MODEL), lambda i: (cur(i), 0)),
            pl.BlockSpec((tm, FOURIER_WIDTH), lambda i: (cur(i), 0)),
            pl.BlockSpec((SSM_LANE_BLOCKS, tm, LANES), lambda i: (0, cur(i), 0)),
            pl.BlockSpec((SSM_LANE_BLOCKS, tm, LANES), lambda i: (0, cur(i), 0)),
            pl.BlockSpec((tm, 2 * D_MODEL), lambda i: (cur(i), 0)),
            full(dskip), full(wf), full(wglu), full(ws), full(wo), full(ng), full(wrh), full(wrl), full(rb),
        ],
        out_specs=[
            pl.BlockSpec((tm, D_MODEL), lambda i: (cur(i), 0)),
            pl.BlockSpec((tm * PACK_SUB, LANES), lambda i: (cur(i), 0)),
            pl.BlockSpec((tm, ROUTER_COLS), lambda i: (routed(i), 0)),
            pl.BlockSpec((INFO_FIELDS, tm), lambda i: (0, routed(i))),
            pl.BlockSpec((1, ROUTER_COLS), lambda i: (0, 0)),
        ],
        out_shape=[
            jax.ShapeDtypeStruct((TOKENS, D_MODEL), F32),
            jax.ShapeDtypeStruct((TOKENS * PACK_SUB, LANES), F32),
            jax.ShapeDtypeStruct((TOKENS, ROUTER_COLS), F32),
            jax.ShapeDtypeStruct((INFO_FIELDS, TOKENS), F32),
            jax.ShapeDtypeStruct((1, ROUTER_COLS), F32),
        ],
        scratch_shapes=[pltpu.VMEM((1, ROUTER_COLS), F32),
                        pltpu.VMEM((2, tm, ROUTER_COLS), F32)],
        compiler_params=_cparams(("arbitrary",)),
        name="merge",
    )(x, fmix, yconv, us, gates, dskip, wf, wglu, ws, wo, ng, wrh, wrl, rb)


EXPERT_LANE0 = MOE_GROUPS
INFO_EXPERT, INFO_RANK, INFO_GATE = 0, 2, 4
INFO_FIELDS = 8


def _route_tile(lg, info_ref, fields_ref, cnt_ref, carry, weight):
    tm = lg.shape[0]
    col_i = lax.broadcasted_iota(jnp.int32, lg.shape, 1)
    col = col_i.astype(F32)
    neg = jnp.float32(-jnp.inf)
    none = jnp.float32(ROUTER_COLS)

    def row_max(v):
        return jnp.max(v, axis=-1, keepdims=True)

    def first_at(v, m):
        return jnp.min(jnp.where(v == m, col, none), axis=-1, keepdims=True)

    gl = jnp.where(col_i < MOE_GROUPS, lg, neg)
    gmax = row_max(gl)
    p_g = 1.0 / jnp.sum(jnp.exp(gl - gmax), axis=-1, keepdims=True)
    lo = EXPERT_LANE0 + first_at(gl, gmax) * EXPERTS_PER_GROUP
    el = jnp.where((col >= lo) & (col < lo + EXPERTS_PER_GROUP), lg, neg)
    l1 = row_max(el)
    i1 = first_at(el, l1)
    el2 = jnp.where(col == i1, neg, el)
    l2 = row_max(el2)
    i2 = first_at(el2, l2)
    r = jnp.exp(l2 - l1)
    w1 = p_g / (1.0 + r)
    w2 = w1 * r

    hit1, hit2 = col == i1, col == i2
    onehot = jnp.where(hit1 | hit2, 1.0, 0.0)
    earlier = lax.broadcasted_iota(jnp.int32, (tm, tm), 0) > lax.broadcasted_iota(jnp.int32, (tm, tm), 1)
    before = jnp.dot(jnp.where(earlier, 1.0, 0.0).astype(BF16), onehot.astype(BF16),
                     preferred_element_type=F32) + carry[...]
    rank1 = jnp.sum(jnp.where(hit1, before, 0.0), axis=-1, keepdims=True)
    rank2 = jnp.sum(jnp.where(hit2, before, 0.0), axis=-1, keepdims=True)
    carry[...] += weight * jnp.sum(onehot, axis=0, keepdims=True)
    cnt_ref[...] = carry[...]

    info = jnp.zeros(lg.shape, F32)
    for lane, v in ((INFO_EXPERT, i1 - EXPERT_LANE0), (INFO_EXPERT + 1, i2 - EXPERT_LANE0), (INFO_RANK, rank1),
                    (INFO_RANK + 1, rank2), (INFO_GATE, w1), (INFO_GATE + 1, w2)):
        info = jnp.where(col_i == lane, v, info)
    info_ref[...] = info
    fields_ref[...] = info.T[:INFO_FIELDS]


def _dispatch_plan(fields, counts):
    expert = fields[INFO_EXPERT:INFO_EXPERT + MOE_TOP_K].astype(jnp.int32)
    rank = fields[INFO_RANK:INFO_RANK + MOE_TOP_K].astype(jnp.int32)
    cnt = counts[0, EXPERT_LANE0:EXPERT_LANE0 + N_EXPERTS].astype(jnp.int32)
    padded = ((cnt + MOE_ROWS - 1) // MOE_ROWS) * MOE_ROWS
    pends = jnp.cumsum(padded)
    pstarts = pends - padded
    ids = jnp.arange(N_EXPERTS, dtype=jnp.int32)
    dest = rank + jnp.sum(jnp.where(expert[..., None] == ids, pstarts, 0), axis=-1)
    n_used = pends[-1] // MOE_ROWS
    blocks = jnp.arange(MOE_BLOCKS, dtype=jnp.int32)
    block_e = jnp.sum((pends[None, :] <= (blocks * MOE_ROWS)[:, None]).astype(jnp.int32), axis=1)
    block_e = jnp.minimum(block_e, N_EXPERTS - 1)
    last_e = jnp.sum(jnp.where(blocks == n_used - 1, block_e, 0))
    used = (cnt > 0).astype(jnp.int32)
    ordinal = jnp.cumsum(used) - used
    n_experts_used = jnp.sum(used)
    by_ordinal = jnp.sum(jnp.where((ordinal[None, :] == ids[:, None]) & (used[None, :] > 0), ids[None, :], 0), axis=1)
    block_ord = jnp.sum(jnp.where(block_e[:, None] == ids[None, :], ordinal[None, :], 0), axis=1)
    block_first = (blocks * MOE_ROWS == jnp.sum(jnp.where(block_e[:, None] == ids[None, :], pstarts[None, :], 0), axis=1))
    block_first = (block_first & (blocks < n_used)).astype(jnp.int32)
    meta = jnp.concatenate([n_used.reshape(1), n_experts_used.reshape(1)]).astype(jnp.int32)
    return dest.reshape(MOE_TOP_K * TOKENS), block_ord.astype(jnp.int32), block_first, by_ordinal.astype(jnp.int32), meta, pends


MOE_SLOT_ROWS = MOE_ROWS * PACK_SUB
INVERT_UNROLL = 32


WEIGHT_SLOTS = 3
GATHER_SLOTS = 3
WEIGHT_DMA_PRIORITY = 1


def _moe_kernel(dest_ref, ord_ref, first_ref, eo_ref, meta_ref, pend_ref, hp_ref, wg_hbm, wu_hbm, wd_hbm, y_ref,
                xbuf, wg_buf, wu_buf, wd_buf, tok_ref, sem, wsem):
    i = pl.program_id(0)
    n_used = meta_ref[0]
    n_experts_used = meta_ref[1]

    def slot_rows(slot):
        return xbuf.at[pl.ds(pl.multiple_of(slot * MOE_SLOT_ROWS, MOE_SLOT_ROWS), MOE_SLOT_ROWS), :]

    def weight_copies(ordinal):
        e = eo_ref[ordinal]
        ws = ordinal % WEIGHT_SLOTS
        return [pltpu.make_async_copy(hbm.at[e], buf.at[ws], wsem.at[ws])
                for hbm, buf in ((wg_hbm, wg_buf), (wu_hbm, wu_buf), (wd_hbm, wd_buf))]

    def invert_dispatch():
        def fill_expert(e, c):
            first = jnp.maximum(pend_ref[e] - MOE_ROWS, 0)

            def fill(r, c2):
                tok_ref[first + r] = (first + r) & (TOKENS - 1)
                return c2

            lax.fori_loop(0, MOE_ROWS, fill, 0, unroll=INVERT_UNROLL)
            return c

        lax.fori_loop(0, N_EXPERTS, fill_expert, 0)

        def place(t, c):
            for k in range(MOE_TOP_K):
                tok_ref[dest_ref[k * TOKENS + t]] = t
            return c

        lax.fori_loop(0, TOKENS, place, 0, unroll=INVERT_UNROLL)

    def gather(block, slot):
        base = block * MOE_ROWS
        for r in range(MOE_ROWS):
            src = hp_ref.at[pl.ds(pl.multiple_of(tok_ref[base + r] * PACK_SUB, PACK_SUB), PACK_SUB), :]
            dst = xbuf.at[pl.ds(pl.multiple_of(slot * MOE_SLOT_ROWS + r * PACK_SUB, PACK_SUB), PACK_SUB), :]
            pltpu.make_async_copy(src, dst, sem.at[slot]).start()

    @pl.when(i == 0)
    def _():
        for ahead in range(WEIGHT_SLOTS - 1):
            @pl.when(ahead < n_experts_used)
            def _():
                for cp in weight_copies(ahead):
                    cp.start(priority=WEIGHT_DMA_PRIORITY)
        invert_dispatch()
        for ahead in range(GATHER_SLOTS - 1):
            gather(jnp.minimum(ahead, n_used - 1), ahead)

    @pl.when(i < n_used)
    def _():
        ordinal = ord_ref[i]

        @pl.when(first_ref[i] == 1)
        def _():
            for cp in weight_copies(ordinal):
                cp.wait()

            @pl.when(ordinal + WEIGHT_SLOTS - 1 < n_experts_used)
            def _():
                for cp in weight_copies(ordinal + WEIGHT_SLOTS - 1):
                    cp.start(priority=WEIGHT_DMA_PRIORITY)

        slot = i % GATHER_SLOTS
        ws = ordinal % WEIGHT_SLOTS
        pltpu.make_async_copy(slot_rows(slot), slot_rows(slot), sem.at[slot]).wait()
        ahead = i + GATHER_SLOTS - 1
        xb = _unpack_rows(xbuf, slot * MOE_SLOT_ROWS, MOE_ROWS).astype(BF16)
        gather(jnp.minimum(ahead, n_used - 1), ahead % GATHER_SLOTS)
        a = jnp.dot(xb, wg_buf[ws].astype(BF16), preferred_element_type=F32)
        u = jnp.dot(xb, wu_buf[ws].astype(BF16), preferred_element_type=F32)
        act = (a * jax.nn.sigmoid(a) * u).astype(BF16)
        y = jnp.dot(act, wd_buf[ws].astype(BF16), preferred_element_type=F32)
        _pack_rows(y, y_ref)

        @pl.when(i == n_used - 1)
        def _():
            for extra in range(1, GATHER_SLOTS):
                other = (i + extra) % GATHER_SLOTS
                pltpu.make_async_copy(slot_rows(other), slot_rows(other), sem.at[other]).wait()

    @pl.when(i >= n_used)
    def _():
        y_ref[...] = jnp.zeros_like(y_ref)


def _moe(dest, block_ord, block_first, by_ordinal, meta, pends, hp, w_gate, w_up, w_down):
    hbm = pl.BlockSpec(memory_space=pl.ANY)
    grid_spec = pltpu.PrefetchScalarGridSpec(
        num_scalar_prefetch=6,
        grid=(MOE_BLOCKS,),
        in_specs=[hbm, hbm, hbm, hbm],
        out_specs=pl.BlockSpec((MOE_SLOT_ROWS, LANES), lambda i, *_: (i, 0)),
        scratch_shapes=[pltpu.VMEM((GATHER_SLOTS * MOE_SLOT_ROWS, LANES), F32),
                        pltpu.VMEM((WEIGHT_SLOTS, D_MODEL, D_EXPERT), F32),
                        pltpu.VMEM((WEIGHT_SLOTS, D_MODEL, D_EXPERT), F32),
                        pltpu.VMEM((WEIGHT_SLOTS, D_EXPERT, D_MODEL), F32),
                        pltpu.SMEM((MOE_BLOCKS * MOE_ROWS,), jnp.int32),
                        pltpu.SemaphoreType.DMA((GATHER_SLOTS,)), pltpu.SemaphoreType.DMA((WEIGHT_SLOTS,))],
    )
    return pl.pallas_call(
        _moe_kernel,
        grid_spec=grid_spec,
        out_shape=jax.ShapeDtypeStruct((MOE_BLOCKS * MOE_SLOT_ROWS, LANES), F32),
        compiler_params=_cparams(("arbitrary",)),
        name="moe",
    )(dest, block_ord, block_first, by_ordinal, meta, pends, hp, w_gate, w_up, w_down)


COMBINE_TM = 256


def _combine_kernel(dest_ref, x1_ref, info_ref, g_ref, y_ref, o_ref, ybuf, sem):
    i = pl.program_id(0)
    last = pl.num_programs(0) - 1
    k_rows = COMBINE_TM * PACK_SUB
    slot_rows = MOE_TOP_K * k_rows

    def slot_ref(slot):
        return ybuf.at[pl.ds(pl.multiple_of(slot * slot_rows, slot_rows), slot_rows), :]

    def gather(tile, slot):
        for r in range(COMBINE_TM):
            for k in range(MOE_TOP_K):
                row = dest_ref[k * TOKENS + tile * COMBINE_TM + r]
                src = y_ref.at[pl.ds(pl.multiple_of(row * PACK_SUB, PACK_SUB), PACK_SUB), :]
                at = slot * slot_rows + k * k_rows + r * PACK_SUB
                dst = ybuf.at[pl.ds(pl.multiple_of(at, PACK_SUB), PACK_SUB), :]
                pltpu.make_async_copy(src, dst, sem.at[slot]).start(priority=k)

    @pl.when(i == 0)
    def _():
        for ahead in range(GATHER_SLOTS - 1):
            gather(ahead, ahead)

    slot = i % GATHER_SLOTS
    pltpu.make_async_copy(slot_ref(slot), slot_ref(slot), sem.at[slot]).wait()
    ahead = i + GATHER_SLOTS - 1
    gather(jnp.minimum(ahead, last), ahead % GATHER_SLOTS)
    x2 = x1_ref[...]
    for k in range(MOE_TOP_K):
        yk = _unpack_rows(ybuf, slot * slot_rows + k * k_rows, COMBINE_TM)
        x2 = x2 + info_ref[:, INFO_GATE + k:INFO_GATE + k + 1] * yk
    inv = lax.rsqrt(jnp.mean(x2 * x2, axis=-1, keepdims=True) + RMS_EPS)
    o_ref[...] = x2 * inv * g_ref[...]

    @pl.when(i == last)
    def _():
        for extra in range(1, GATHER_SLOTS):
            other = (i + extra) % GATHER_SLOTS
            pltpu.make_async_copy(slot_ref(other), slot_ref(other), sem.at[other]).wait()


def _combine(dest, x1, info, g, y_pad):
    tm = COMBINE_TM
    grid_spec = pltpu.PrefetchScalarGridSpec(
        num_scalar_prefetch=1,
        grid=(TOKENS // tm,),
        in_specs=[
            pl.BlockSpec((tm, D_MODEL), lambda i, d: (i, 0)),
            pl.BlockSpec((tm, ROUTER_COLS), lambda i, d: (i, 0)),
            pl.BlockSpec((1, D_MODEL), lambda i, d: (0, 0)),
            pl.BlockSpec(memory_space=pl.ANY),
        ],
        out_specs=pl.BlockSpec((tm, D_MODEL), lambda i, d: (i, 0)),
        scratch_shapes=[pltpu.VMEM((GATHER_SLOTS * MOE_TOP_K * tm * PACK_SUB, LANES), F32),
                        pltpu.SemaphoreType.DMA((GATHER_SLOTS,))],
    )
    return pl.pallas_call(
        _combine_kernel,
        grid_spec=grid_spec,
        out_shape=jax.ShapeDtypeStruct((TOKENS, D_MODEL), F32),
        compiler_params=_cparams(("arbitrary",)),
        name="combine",
    )(dest, x1, info, g, y_pad)


def kernel(x, mix_norm_g, w_in, w_fourier_out, ssm_A_re, ssm_A_im, ssm_log_dt, ssm_B_re, ssm_B_im, ssm_C_re,
           ssm_C_im, ssm_D, ssm_w_glu, w_ssm_out, w_out, ffn_norm_g, router_group_w, router_group_b,
           router_expert_w, router_expert_b, expert_w_gate, expert_w_up, expert_w_down, final_norm_g):
    assert x.shape == (BATCH, SEQ, D_MODEL) and w_in.shape[0] == 1
    tw, f2, cdft = _dft_constants()

    vf, us, xg, gates = _inproj(x, mix_norm_g[0][None], w_in[0].astype(BF16), cdft)
    fmix = _dft(vf, tw, f2)

    mg, wsgt, wog, aq = _ssm_operators(ssm_A_re[0], ssm_A_im[0], ssm_log_dt[0], ssm_B_re[0], ssm_B_im[0],
                                       ssm_C_re[0], ssm_C_im[0])
    flip = jnp.asarray(np.eye(N_CHUNKS)[::-1], BF16)
    sre, sim = _ssm_states(xg, wsgt, flip)
    yconv = _ssm_out(xg, mg, wog, flip, *_ssm_scan(sre, sim, aq))

    w_router = jnp.concatenate([router_group_w[0], router_expert_w[0]], axis=1)
    w_router = jnp.pad(w_router, ((0, 0), (0, ROUTER_COLS - w_router.shape[1])))
    b_router = jnp.concatenate([router_group_b[0], router_expert_b[0]])
    b_router = jnp.pad(b_router, (0, ROUTER_COLS - b_router.shape[0]))[None]
    wr_hi = w_router.astype(BF16)
    wr_lo = (w_router - wr_hi.astype(F32)).astype(BF16)
    x1, hp, info, fields, counts = _merge(
        x.reshape(TOKENS, D_MODEL), fmix, yconv, us, gates, ssm_D[0][None], w_fourier_out[0].astype(BF16),
        ssm_w_glu[0].astype(BF16), w_ssm_out[0].astype(BF16), w_out[0].astype(BF16), ffn_norm_g[0][None],
        wr_hi, wr_lo, b_router)

    dest, block_ord, block_first, by_ordinal, meta, pends = _dispatch_plan(fields, counts)
    y_pad = _moe(dest, block_ord, block_first, by_ordinal, meta, pends, hp, expert_w_gate[0], expert_w_up[0],
                 expert_w_down[0])
    out = _combine(dest, x1, info, final_norm_g[None], y_pad)
    return out.reshape(BATCH, SEQ, D_MODEL)
```

```python
import math

import numpy as np
import jax
import jax.numpy as jnp
from jax import lax
from jax.experimental import pallas as pl
from jax.experimental.pallas import tpu as pltpu

F32 = jnp.float32
BF16 = jnp.bfloat16

D_MODEL = 1024
BATCH = 4
SEQ = 4096
TOKENS = BATCH * SEQ
FOURIER_WIDTH = 512
FOURIER_GROUP_CH = 128
FOURIER_GROUPS = 4
SSM_WIDTH = 512
SSM_GROUP_CH = 16
SSM_GROUPS = 32
SSM_STATE = 64
MOE_GROUPS = 8
EXPERTS_PER_GROUP = 8
N_EXPERTS = 64
MOE_TOP_K = 2
D_EXPERT = 512
RMS_EPS = 1e-6

LANES = 128
SSM_CHUNK = 16
SSM_LANE_BLOCKS = SSM_WIDTH // LANES
GROUPS_PER_BLOCK = LANES // SSM_GROUP_CH
N_CHUNKS = SEQ // SSM_CHUNK
GROUP_COLS = SSM_CHUNK * SSM_GROUP_CH
GROUP_SHIFT_CH = 4
MOE_ROWS = 256
MOE_BLOCKS = TOKENS * MOE_TOP_K // MOE_ROWS + N_EXPERTS
ROUTER_COLS = 128
VMEM_LIMIT = 48 * 1024 * 1024


def _cparams(sem, vmem=VMEM_LIMIT):
    return pltpu.CompilerParams(dimension_semantics=sem, vmem_limit_bytes=vmem)


IN_TM = 512


def _inproj_kernel(x_ref, g_ref, w_ref, cdft_ref, vf_ref, us_ref, xg_ref, gates_ref, zs_scr):
    x = x_ref[0]
    inv = lax.rsqrt(jnp.mean(x * x, axis=-1, keepdims=True) + RMS_EPS)
    h = (x * inv * g_ref[...]).astype(BF16)
    zf = jnp.dot(h, w_ref[:, 0:FOURIER_WIDTH], preferred_element_type=F32).astype(BF16)
    cdft = cdft_ref[...].astype(BF16)
    for g in range(FOURIER_GROUPS):
        sl = slice(g * LANES, (g + 1) * LANES)
        v = jnp.dot(zf[:, sl], cdft, preferred_element_type=F32)
        vf_ref[0, 0, :, sl] = v[:, :LANES].astype(BF16)
        vf_ref[0, 1, :, sl] = v[:, LANES:].astype(BF16)
    zs = jnp.dot(h, w_ref[:, FOURIER_WIDTH:FOURIER_WIDTH + SSM_WIDTH], preferred_element_type=F32)
    for j in range(SSM_LANE_BLOCKS):
        us_ref[j] = zs[:, j * LANES:(j + 1) * LANES].astype(BF16)
        zs_scr[j] = zs[:, j * LANES:(j + 1) * LANES]
    seg = lax.broadcasted_iota(jnp.int32, (IN_TM // SSM_CHUNK, LANES), 1) >> GROUP_SHIFT_CH
    for j in range(SSM_LANE_BLOCKS):
        pieces = [zs_scr[j, pl.ds(q, IN_TM // SSM_CHUNK, stride=SSM_CHUNK), :] for q in range(SSM_CHUNK)]
        for g in range(GROUPS_PER_BLOCK):
            for half in range(SSM_CHUNK // GROUPS_PER_BLOCK):
                acc = jnp.zeros((IN_TM // SSM_CHUNK, LANES), F32)
                for ql in range(GROUPS_PER_BLOCK):
                    shift = ((ql - g) % GROUPS_PER_BLOCK) * SSM_GROUP_CH
                    piece = pieces[half * GROUPS_PER_BLOCK + ql]
                    moved = piece if shift == 0 else pltpu.roll(piece, shift, axis=1)
                    acc = jnp.where(seg == ql, moved, acc)
                xg_ref[j * GROUPS_PER_BLOCK + g, :, half * LANES:(half + 1) * LANES] = acc.astype(BF16)
    base = FOURIER_WIDTH + SSM_WIDTH
    for n in range(4):
        zg = jnp.dot(h, w_ref[:, base + n * 512: base + (n + 1) * 512], preferred_element_type=F32)
        gates_ref[:, n * 512:(n + 1) * 512] = jax.nn.sigmoid(zg).astype(BF16)


def _inproj(x, g, w_in, cdft):
    nt = SEQ // IN_TM
    return pl.pallas_call(
        _inproj_kernel,
        grid=(BATCH, nt),
        in_specs=[
            pl.BlockSpec((1, IN_TM, D_MODEL), lambda b, i: (b, i, 0)),
            pl.BlockSpec((1, D_MODEL), lambda b, i: (0, 0)),
            pl.BlockSpec(w_in.shape, lambda b, i: (0, 0)),
            pl.BlockSpec(cdft.shape, lambda b, i: (0, 0)),
        ],
        out_specs=[
            pl.BlockSpec((1, 2, IN_TM, FOURIER_WIDTH), lambda b, i: (b, 0, i, 0)),
            pl.BlockSpec((SSM_LANE_BLOCKS, IN_TM, LANES), lambda b, i: (0, b * nt + i, 0)),
            pl.BlockSpec((SSM_GROUPS, IN_TM // SSM_CHUNK, GROUP_COLS), lambda b, i: (0, b * nt + i, 0)),
            pl.BlockSpec((IN_TM, 2 * D_MODEL), lambda b, i: (b * nt + i, 0)),
        ],
        out_shape=[
            jax.ShapeDtypeStruct((BATCH, 2, SEQ, FOURIER_WIDTH), BF16),
            jax.ShapeDtypeStruct((SSM_LANE_BLOCKS, TOKENS, LANES), BF16),
            jax.ShapeDtypeStruct((SSM_GROUPS, TOKENS // SSM_CHUNK, GROUP_COLS), BF16),
            jax.ShapeDtypeStruct((TOKENS, 2 * D_MODEL), BF16),
        ],
        scratch_shapes=[pltpu.VMEM((SSM_LANE_BLOCKS, IN_TM, LANES), F32)],
        compiler_params=_cparams(("parallel", "parallel")),
        name="inproj",
    )(x, g, w_in, cdft)


DFT_R1 = 8
DFT_R2 = SEQ // DFT_R1
DFT_LANES = 2 * LANES
DFT_ROWS = 16


def _cmul_const(z, w):
    re, im = z
    if abs(w.imag) < 1e-12:
        return (re, im) if abs(w.real - 1.0) < 1e-12 else (re * w.real, im * w.real)
    if abs(w.real) < 1e-12:
        return (im, -re) if abs(w.imag + 1.0) < 1e-12 else (-im * w.imag, re * w.imag)
    return re * w.real - im * w.imag, re * w.imag + im * w.real


def _fft_blocks(xs):
    n = len(xs)
    if n == 1:
        return xs
    even, odd = _fft_blocks(xs[0::2]), _fft_blocks(xs[1::2])
    out = [None] * n
    for k in range(n // 2):
        tr, ti = _cmul_const(odd[k], np.exp(-2j * np.pi * k / n))
        out[k] = (even[k][0] + tr, even[k][1] + ti)
        out[k + n // 2] = (even[k][0] - tr, even[k][1] - ti)
    return out


def _dft_kernel(v_ref, tw_ref, f_ref, o_ref, a_scr, o_scr):
    def tile(i, c):
        r0 = pl.multiple_of(i * DFT_ROWS, DFT_ROWS)
        for slab in range(DFT_LANES // LANES):
            lanes = slice(slab * LANES, (slab + 1) * LANES)
            xs = [(v_ref[0, 0, pl.ds(s1 * DFT_R2 + r0, DFT_ROWS), lanes].astype(F32),
                   v_ref[0, 1, pl.ds(s1 * DFT_R2 + r0, DFT_ROWS), lanes].astype(F32)) for s1 in range(DFT_R1)]
            for t1, (ar, ai) in enumerate(_fft_blocks(xs)):
                tr, ti = tw_ref[0, t1, pl.ds(r0, DFT_ROWS), :], tw_ref[1, t1, pl.ds(r0, DFT_ROWS), :]
                a_scr[t1, pl.ds(r0, DFT_ROWS), lanes] = (ar * tr - ai * ti).astype(BF16)
                a_scr[t1, pl.ds(DFT_R2 + r0, DFT_ROWS), lanes] = (ar * ti + ai * tr).astype(BF16)
        return c

    lax.fori_loop(0, DFT_R2 // DFT_ROWS, tile, 0)

    f2 = f_ref[...].astype(BF16)
    for t1 in range(DFT_R1):
        r = jnp.dot(f2, a_scr[t1], preferred_element_type=F32)
        for slab in range(DFT_LANES // LANES):
            o_scr[slab, pl.ds(t1, DFT_R2, stride=DFT_R1), :] = r[:, slab * LANES:(slab + 1) * LANES]
    for slab in range(DFT_LANES // LANES):
        o_ref[:, slab * LANES:(slab + 1) * LANES] = o_scr[slab].astype(BF16)


def _dft(v, tw, f2):
    nh = FOURIER_WIDTH // DFT_LANES
    return pl.pallas_call(
        _dft_kernel,
        grid=(BATCH, nh),
        in_specs=[
            pl.BlockSpec((1, 2, SEQ, DFT_LANES), lambda b, h: (b, 0, 0, h)),
            pl.BlockSpec(tw.shape, lambda b, h: (0, 0, 0, 0)),
            pl.BlockSpec(f2.shape, lambda b, h: (0, 0)),
        ],
        out_specs=pl.BlockSpec((SEQ, DFT_LANES), lambda b, h: (b, h)),
        out_shape=jax.ShapeDtypeStruct((TOKENS, FOURIER_WIDTH), BF16),
        scratch_shapes=[pltpu.VMEM((DFT_R1, 2 * DFT_R2, DFT_LANES), BF16),
                        pltpu.VMEM((DFT_LANES // LANES, SEQ, LANES), F32)],
        compiler_params=_cparams(("parallel", "parallel")),
        name="dft",
    )(v, tw, f2)


def _dft_constants():
    t1 = np.arange(DFT_R1)
    s2 = np.arange(DFT_R2)
    ang = 2.0 * np.pi * np.outer(t1, s2) / SEQ
    scale = 1.0 / math.sqrt(SEQ)
    tw = np.stack([np.cos(ang) * scale, -np.sin(ang) * scale])
    tw = np.repeat(tw[..., None], LANES, axis=-1)
    ang2 = 2.0 * np.pi * np.outer(s2, s2) / DFT_R2
    f2 = np.concatenate([np.cos(ang2), np.sin(ang2)], axis=1)
    kc = np.arange(FOURIER_GROUP_CH)
    angc = 2.0 * np.pi * np.outer(kc, kc) / FOURIER_GROUP_CH
    cs = 1.0 / math.sqrt(FOURIER_GROUP_CH)
    cdft = np.concatenate([np.cos(angc) * cs, -np.sin(angc) * cs], axis=1)
    return tuple(jnp.asarray(v, F32) for v in (tw, f2, cdft))


GROUP_BLOCKS = SSM_GROUPS // GROUPS_PER_BLOCK
DIR_STATE = 2 * SSM_STATE
STATE_LANES = SSM_GROUPS * DIR_STATE


def _ssm_operators(a_re, a_im, log_dt, b_re, b_im, c_re, c_im):
    q_len = SSM_CHUNK
    hi = lax.Precision.HIGHEST
    dt = jnp.exp(log_dt)[..., None]
    lr, li = a_re * dt, a_im * dt
    steps = jnp.arange(q_len + 1, dtype=F32)
    mag = jnp.exp(lr[..., None] * steps)
    ang = li[..., None] * steps
    pr, pi = mag * jnp.cos(ang), mag * jnp.sin(ang)
    ar, ai = pr[..., 1], pi[..., 1]
    den = a_re * a_re + a_im * a_im
    cr = ((ar - 1.0) * a_re + ai * a_im) / den
    ci = (ai * a_re - (ar - 1.0) * a_im) / den
    bbr = cr[..., None] * b_re - ci[..., None] * b_im
    bbi = cr[..., None] * b_im + ci[..., None] * b_re

    rq = jnp.asarray(np.kron(np.eye(q_len), np.ones((1, SSM_GROUP_CH))), F32)
    rc = jnp.asarray(np.kron(np.ones((1, q_len)), np.eye(SSM_GROUP_CH)), F32)

    def per_group(x):
        return jnp.swapaxes(x, 0, 1).reshape(SSM_GROUPS, DIR_STATE, x.shape[-1])

    def on_cols(x, rep):
        return jnp.einsum('grk,kc->grc', per_group(x), rep, precision=hi)

    def state_operator(pw_r, pw_i, f_r, f_i, negate_im):
        p_r, p_i, q_r, q_i = on_cols(pw_r, rq), on_cols(pw_i, rq), on_cols(f_r, rc), on_cols(f_i, rc)
        w_im = p_r * q_i + p_i * q_r
        return jnp.concatenate([p_r * q_r - p_i * q_i, -w_im if negate_im else w_im], axis=1).astype(BF16)

    def both(p, fwd, bwd):
        return jnp.stack([p[0][..., fwd], p[1][..., bwd]])

    rev = slice(q_len - 1, None, -1)
    wsgt = state_operator(both(pr, rev, slice(0, q_len)), both(pi, rev, slice(0, q_len)), bbr, bbi, False)
    ctr, cti = jnp.swapaxes(c_re, -1, -2), jnp.swapaxes(c_im, -1, -2)
    wog = state_operator(both(pr, slice(1, None), slice(q_len, 0, -1)), both(pi, slice(1, None), slice(q_len, 0, -1)),
                         ctr, cti, True)

    prq, piq = pr[..., :q_len], pi[..., :q_len]
    cpr = jnp.einsum('dgcn,dgnt->dgtcn', c_re, prq) - jnp.einsum('dgcn,dgnt->dgtcn', c_im, piq)
    cpi = jnp.einsum('dgcn,dgnt->dgtcn', c_re, piq) + jnp.einsum('dgcn,dgnt->dgtcn', c_im, prq)
    kern = jnp.einsum('dgtcn,dgne->dgtec', cpr, bbr) - jnp.einsum('dgtcn,dgne->dgtec', cpi, bbi)
    lags = jnp.concatenate([kern[1][:, :0:-1], kern[0][:, :1] + kern[1][:, :1], kern[0][:, 1:]], axis=1)
    lag_cols = jnp.einsum('gtek,kc->gtec', lags, rc, precision=hi)
    lane_q = jnp.arange(GROUP_COLS, dtype=jnp.int32) // SSM_GROUP_CH
    rows = []
    for qp in range(q_len):
        acc = jnp.zeros(lag_cols[:, 0].shape, F32)
        for q in range(q_len):
            acc = jnp.where(lane_q == q, lag_cols[:, q - qp + q_len - 1], acc)
        rows.append(acc)
    mg = jnp.stack(rows, axis=1)
    mg = mg.reshape(SSM_GROUPS, GROUP_COLS, GROUP_COLS).astype(BF16)

    aq = jnp.stack([per_group(pr[..., q_len:]), per_group(pi[..., q_len:])])
    return mg, wsgt, wog, aq.reshape(2, STATE_LANES)


def _ssm_state_kernel(x_ref, ws_ref, flip_ref, sre_ref, sim_ref):
    backward = (lax.broadcasted_iota(jnp.int32, (N_CHUNKS, DIR_STATE), 1) & SSM_STATE) != 0
    nt = (((1,), (1,)), ((), ()))
    for g in range(GROUPS_PER_BLOCK):
        lanes = slice(g * DIR_STATE, (g + 1) * DIR_STATE)
        for b in range(BATCH):
            x = x_ref[g, b * N_CHUNKS:(b + 1) * N_CHUNKS, :]
            x_rev = jnp.dot(flip_ref[...], x, preferred_element_type=F32).astype(BF16)
            s = lax.dot_general(x, ws_ref[g], nt, preferred_element_type=F32)
            s_rev = lax.dot_general(x_rev, ws_ref[g], nt, preferred_element_type=F32)
            sre_ref[b, :, lanes] = jnp.where(backward, s_rev[:, :DIR_STATE], s[:, :DIR_STATE])
            sim_ref[b, :, lanes] = jnp.where(backward, s_rev[:, DIR_STATE:], s[:, DIR_STATE:])


def _ssm_states(xg, wsgt, flip):
    out_spec = pl.BlockSpec((BATCH, N_CHUNKS, GROUPS_PER_BLOCK * DIR_STATE), lambda j: (0, 0, j))
    shape = jax.ShapeDtypeStruct((BATCH, N_CHUNKS, STATE_LANES), F32)
    return pl.pallas_call(
        _ssm_state_kernel,
        grid=(GROUP_BLOCKS,),
        in_specs=[
            pl.BlockSpec((GROUPS_PER_BLOCK, BATCH * N_CHUNKS, GROUP_COLS), lambda j: (j, 0, 0)),
            pl.BlockSpec((GROUPS_PER_BLOCK, 2 * DIR_STATE, GROUP_COLS), lambda j: (j, 0, 0)),
            pl.BlockSpec(flip.shape, lambda j: (0, 0)),
        ],
        out_specs=[out_spec, out_spec],
        out_shape=[shape, shape],
        compiler_params=_cparams(("parallel",)),
        name="ssm_states",
    )(xg, wsgt, flip)


SCAN_UNROLL = 4


def _ssm_scan_kernel(sre_ref, sim_ref, aq_ref, hre_ref, him_ref):
    ar, ai = aq_ref[0:1], aq_ref[1:2]

    def body(k, carry):
        hr, hi = carry
        hre_ref[0, pl.ds(k, 1), :] = hr
        him_ref[0, pl.ds(k, 1), :] = hi
        sr, si = sre_ref[0, pl.ds(k, 1), :], sim_ref[0, pl.ds(k, 1), :]
        return ar * hr - ai * hi + sr, ar * hi + ai * hr + si

    z = jnp.zeros((1, STATE_LANES), F32)
    lax.fori_loop(0, N_CHUNKS, body, (z, z), unroll=SCAN_UNROLL)


def _ssm_scan(sre, sim, aq):
    spec = pl.BlockSpec((1, N_CHUNKS, STATE_LANES), lambda b: (b, 0, 0))
    shape = jax.ShapeDtypeStruct(sre.shape, F32)
    return pl.pallas_call(
        _ssm_scan_kernel,
        grid=(BATCH,),
        in_specs=[spec, spec, pl.BlockSpec(aq.shape, lambda b: (0, 0))],
        out_specs=[spec, spec],
        out_shape=[shape, shape],
        compiler_params=_cparams(("parallel",)),
        name="ssm_scan",
    )(sre, sim, aq)


def _ssm_out_kernel(x_ref, m_ref, wo_ref, flip_ref, hre_ref, him_ref, y_ref, y_scr):
    fwd = (lax.broadcasted_iota(jnp.int32, (N_CHUNKS, GROUPS_PER_BLOCK * DIR_STATE), 1) & SSM_STATE) == 0

    def in_chunk_order(h_ref):
        h = h_ref[0].astype(BF16)
        return jnp.where(fwd, h, jnp.dot(flip_ref[...], h, preferred_element_type=F32).astype(BF16))

    h_re, h_im = in_chunk_order(hre_ref), in_chunk_order(him_ref)
    accs = []
    for g in range(GROUPS_PER_BLOCK):
        lanes = slice(g * DIR_STATE, (g + 1) * DIR_STATE)
        h = jnp.concatenate([h_re[:, lanes], h_im[:, lanes]], axis=1)
        accs.append(jnp.dot(x_ref[g, 0], m_ref[g], preferred_element_type=F32)
                    + jnp.dot(h, wo_ref[g], preferred_element_type=F32))
    seg = lax.broadcasted_iota(jnp.int32, (N_CHUNKS, LANES), 1) >> GROUP_SHIFT_CH
    for q in range(SSM_CHUNK):
        half, ql = divmod(q, GROUPS_PER_BLOCK)
        piece = jnp.zeros((N_CHUNKS, LANES), F32)
        for g in range(GROUPS_PER_BLOCK):
            src = accs[g][:, half * LANES:(half + 1) * LANES]
            shift = ((g - ql) % GROUPS_PER_BLOCK) * SSM_GROUP_CH
            moved = src if shift == 0 else pltpu.roll(src, shift, axis=1)
            piece = jnp.where(seg == g, moved, piece)
        y_scr[pl.ds(q, N_CHUNKS, stride=SSM_CHUNK), :] = piece
    y_ref[0] = y_scr[...].astype(BF16)


def _ssm_out(xg, mg, wog, flip, hre, him):
    hspec = pl.BlockSpec((1, N_CHUNKS, GROUPS_PER_BLOCK * DIR_STATE), lambda j, b: (b, 0, j))
    wspec = pl.BlockSpec((GROUPS_PER_BLOCK, GROUP_COLS, GROUP_COLS), lambda j, b: (j, 0, 0))
    return pl.pallas_call(
        _ssm_out_kernel,
        grid=(GROUP_BLOCKS, BATCH),
        in_specs=[
            pl.BlockSpec((GROUPS_PER_BLOCK, 1, N_CHUNKS, GROUP_COLS), lambda j, b: (j, b, 0, 0)),
            wspec, wspec, pl.BlockSpec(flip.shape, lambda j, b: (0, 0)), hspec, hspec,
        ],
        out_specs=pl.BlockSpec((1, SEQ, LANES), lambda j, b: (j, b, 0)),
        out_shape=jax.ShapeDtypeStruct((SSM_LANE_BLOCKS, TOKENS, LANES), BF16),
        scratch_shapes=[pltpu.VMEM((SEQ, LANES), F32)],
        compiler_params=_cparams(("parallel", "parallel")),
        name="ssm_out",
    )(xg.reshape(SSM_GROUPS, BATCH, N_CHUNKS, GROUP_COLS), mg, wog, flip, hre, him)


MERGE_TM = 512
GELU_C = math.sqrt(2.0 / math.pi)
PACK_SUB = D_MODEL // LANES


def _split_bf16(v):
    hi = v.astype(BF16)
    lo = (v - hi.astype(F32)).astype(BF16)
    return hi, lo


def _pack_rows(v, out_ref):
    for s in range(PACK_SUB):
        out_ref[pl.ds(s, v.shape[0], stride=PACK_SUB), :] = v[:, s * LANES:(s + 1) * LANES]


def _unpack_rows(buf_ref, start, rows):
    return jnp.concatenate([buf_ref[pl.ds(start + s, rows, stride=PACK_SUB), :] for s in range(PACK_SUB)], axis=1)


def _merge_kernel(x_ref, fm_ref, yc_ref, us_ref, gates_ref, dskip_ref, wf_ref, wglu_ref, ws_ref, wo_ref,
                  ng_ref, wrh_ref, wrl_ref, rb_ref, x1_ref, hp_ref, info_ref, fields_ref, cnt_ref, carry, lg_scr):
    i = pl.program_id(0)

    @pl.when(i == 0)
    def _():
        carry[...] = jnp.zeros_like(carry)
        lg_scr[...] = jnp.zeros_like(lg_scr)

    conv = jnp.concatenate([yc_ref[j].astype(F32) for j in range(SSM_LANE_BLOCKS)], axis=-1)
    u = jnp.concatenate([us_ref[j].astype(F32) for j in range(SSM_LANE_BLOCKS)], axis=-1)
    y = conv + dskip_ref[...] * u
    y = 0.5 * y * (1.0 + jnp.tanh(GELU_C * (y + 0.044715 * (y * y * y))))
    glu = jax.nn.sigmoid(jnp.dot(y.astype(BF16), wglu_ref[...], preferred_element_type=F32))
    y_s = jnp.dot((y * glu).astype(BF16), ws_ref[...], preferred_element_type=F32)
    y_f = jnp.dot(fm_ref[...], wf_ref[...], preferred_element_type=F32)
    merged = (gates_ref[:, :D_MODEL].astype(F32) * y_f + gates_ref[:, D_MODEL:].astype(F32) * y_s)
    x1 = x_ref[...] + jnp.dot(merged.astype(BF16), wo_ref[...], preferred_element_type=F32)
    x1_ref[...] = x1
    inv = lax.rsqrt(jnp.mean(x1 * x1, axis=-1, keepdims=True) + RMS_EPS)
    hn = x1 * inv * ng_ref[...]
    _pack_rows(hn, hp_ref)
    hi, lo = _split_bf16(hn)
    logits = (jnp.dot(hi, wrh_ref[...], preferred_element_type=F32)
              + jnp.dot(lo, wrh_ref[...], preferred_element_type=F32)
              + jnp.dot(hi, wrl_ref[...], preferred_element_type=F32))
    _route_tile(lg_scr[(i + 1) % 2], info_ref, fields_ref, cnt_ref, carry, jnp.where(i > 0, 1.0, 0.0))
    lg_scr[i % 2] = logits + rb_ref[...]


def _merge(x, fmix, yconv, us, gates, dskip, wf, wglu, ws, wo, ng, wrh, wrl, rb):
    tm = MERGE_TM
    last = TOKENS // tm - 1
    full = lambda a: pl.BlockSpec(a.shape, lambda i: (0,) * a.ndim)
    cur = lambda i: jnp.minimum(i, last)
    routed = lambda i: jnp.maximum(i - 1, 0)
    return pl.pallas_call(
        _merge_kernel,
        grid=(last + 2,),
        in_specs=[
            pl.BlockSpec((tm, D_MODEL), lambda i: (cur(i), 0)),
            pl.BlockSpec((tm, FOURIER_WIDTH), lambda i: (cur(i), 0)),
            pl.BlockSpec((SSM_LANE_BLOCKS, tm, LANES), lambda i: (0, cur(i), 0)),
            pl.BlockSpec((SSM_LANE_BLOCKS, tm, LANES), lambda i: (0, cur(i), 0)),
            pl.BlockSpec((tm, 2 * D_MODEL), lambda i: (cur(i), 0)),
            full(dskip), full(wf), full(wglu), full(ws), full(wo), full(ng), full(wrh), full(wrl), full(rb),
        ],
        out_specs=[
            pl.BlockSpec((tm, D_MODEL), lambda i: (cur(i), 0)),
            pl.BlockSpec((tm * PACK_SUB, LANES), lambda i: (cur(i), 0)),
            pl.BlockSpec((tm, ROUTER_COLS), lambda i: (routed(i), 0)),
            pl.BlockSpec((INFO_FIELDS, tm), lambda i: (0, routed(i))),
            pl.BlockSpec((1, ROUTER_COLS), lambda i: (0, 0)),
        ],
        out_shape=[
            jax.ShapeDtypeStruct((TOKENS, D_MODEL), F32),
            jax.ShapeDtypeStruct((TOKENS * PACK_SUB, LANES), F32),
            jax.ShapeDtypeStruct((TOKENS, ROUTER_COLS), F32),
            jax.ShapeDtypeStruct((INFO_FIELDS, TOKENS), F32),
            jax.ShapeDtypeStruct((1, ROUTER_COLS), F32),
        ],
        scratch_shapes=[pltpu.VMEM((1, ROUTER_COLS), F32),
                        pltpu.VMEM((2, tm, ROUTER_COLS), F32)],
        compiler_params=_cparams(("arbitrary",)),
        name="merge",
    )(x, fmix, yconv, us, gates, dskip, wf, wglu, ws, wo, ng, wrh, wrl, rb)


EXPERT_LANE0 = MOE_GROUPS
INFO_EXPERT, INFO_RANK, INFO_GATE = 0, 2, 4
INFO_FIELDS = 8


def _route_tile(lg, info_ref, fields_ref, cnt_ref, carry, weight):
    tm = lg.shape[0]
    col_i = lax.broadcasted_iota(jnp.int32, lg.shape, 1)
    col = col_i.astype(F32)
    neg = jnp.float32(-jnp.inf)
    none = jnp.float32(ROUTER_COLS)

    def row_max(v):
        return jnp.max(v, axis=-1, keepdims=True)

    def first_at(v, m):
        return jnp.min(jnp.where(v == m, col, none), axis=-1, keepdims=True)

    gl = jnp.where(col_i < MOE_GROUPS, lg, neg)
    gmax = row_max(gl)
    p_g = 1.0 / jnp.sum(jnp.exp(gl - gmax), axis=-1, keepdims=True)
    lo = EXPERT_LANE0 + first_at(gl, gmax) * EXPERTS_PER_GROUP
    el = jnp.where((col >= lo) & (col < lo + EXPERTS_PER_GROUP), lg, neg)
    l1 = row_max(el)
    i1 = first_at(el, l1)
    el2 = jnp.where(col == i1, neg, el)
    l2 = row_max(el2)
    i2 = first_at(el2, l2)
    r = jnp.exp(l2 - l1)
    w1 = p_g / (1.0 + r)
    w2 = w1 * r

    hit1, hit2 = col == i1, col == i2
    onehot = jnp.where(hit1 | hit2, 1.0, 0.0)
    earlier = lax.broadcasted_iota(jnp.int32, (tm, tm), 0) > lax.broadcasted_iota(jnp.int32, (tm, tm), 1)
    before = jnp.dot(jnp.where(earlier, 1.0, 0.0).astype(BF16), onehot.astype(BF16),
                     preferred_element_type=F32) + carry[...]
    rank1 = jnp.sum(jnp.where(hit1, before, 0.0), axis=-1, keepdims=True)
    rank2 = jnp.sum(jnp.where(hit2, before, 0.0), axis=-1, keepdims=True)
    carry[...] += weight * jnp.sum(onehot, axis=0, keepdims=True)
    cnt_ref[...] = carry[...]

    info = jnp.zeros(lg.shape, F32)
    for lane, v in ((INFO_EXPERT, i1 - EXPERT_LANE0), (INFO_EXPERT + 1, i2 - EXPERT_LANE0), (INFO_RANK, rank1),
                    (INFO_RANK + 1, rank2), (INFO_GATE, w1), (INFO_GATE + 1, w2)):
        info = jnp.where(col_i == lane, v, info)
    info_ref[...] = info
    fields_ref[...] = info.T[:INFO_FIELDS]


def _dispatch_plan(fields, counts):
    expert = fields[INFO_EXPERT:INFO_EXPERT + MOE_TOP_K].astype(jnp.int32)
    rank = fields[INFO_RANK:INFO_RANK + MOE_TOP_K].astype(jnp.int32)
    cnt = counts[0, EXPERT_LANE0:EXPERT_LANE0 + N_EXPERTS].astype(jnp.int32)
    padded = ((cnt + MOE_ROWS - 1) // MOE_ROWS) * MOE_ROWS
    pends = jnp.cumsum(padded)
    pstarts = pends - padded
    ids = jnp.arange(N_EXPERTS, dtype=jnp.int32)
    dest = rank + jnp.sum(jnp.where(expert[..., None] == ids, pstarts, 0), axis=-1)
    n_used = pends[-1] // MOE_ROWS
    blocks = jnp.arange(MOE_BLOCKS, dtype=jnp.int32)
    block_e = jnp.sum((pends[None, :] <= (blocks * MOE_ROWS)[:, None]).astype(jnp.int32), axis=1)
    block_e = jnp.minimum(block_e, N_EXPERTS - 1)
    last_e = jnp.sum(jnp.where(blocks == n_used - 1, block_e, 0))
    used = (cnt > 0).astype(jnp.int32)
    ordinal = jnp.cumsum(used) - used
    n_experts_used = jnp.sum(used)
    by_ordinal = jnp.sum(jnp.where((ordinal[None, :] == ids[:, None]) & (used[None, :] > 0), ids[None, :], 0), axis=1)
    block_ord = jnp.sum(jnp.where(block_e[:, None] == ids[None, :], ordinal[None, :], 0), axis=1)
    block_first = (blocks * MOE_ROWS == jnp.sum(jnp.where(block_e[:, None] == ids[None, :], pstarts[None, :], 0), axis=1))
    block_first = (block_first & (blocks < n_used)).astype(jnp.int32)
    meta = jnp.concatenate([n_used.reshape(1), n_experts_used.reshape(1)]).astype(jnp.int32)
    return dest.reshape(MOE_TOP_K * TOKENS), block_ord.astype(jnp.int32), block_first, by_ordinal.astype(jnp.int32), meta, pends


MOE_SLOT_ROWS = MOE_ROWS * PACK_SUB
INVERT_UNROLL = 32


WEIGHT_SLOTS = 3
GATHER_SLOTS = 3
WEIGHT_DMA_PRIORITY = 1


def _moe_kernel(dest_ref, ord_ref, first_ref, eo_ref, meta_ref, pend_ref, hp_ref, wg_hbm, wu_hbm, wd_hbm, y_ref,
                xbuf, wg_buf, wu_buf, wd_buf, tok_ref, sem, wsem):
    i = pl.program_id(0)
    n_used = meta_ref[0]
    n_experts_used = meta_ref[1]

    def slot_rows(slot):
        return xbuf.at[pl.ds(pl.multiple_of(slot * MOE_SLOT_ROWS, MOE_SLOT_ROWS), MOE_SLOT_ROWS), :]

    def weight_copies(ordinal):
        e = eo_ref[ordinal]
        ws = ordinal % WEIGHT_SLOTS
        return [pltpu.make_async_copy(hbm.at[e], buf.at[ws], wsem.at[ws])
                for hbm, buf in ((wg_hbm, wg_buf), (wu_hbm, wu_buf), (wd_hbm, wd_buf))]

    def invert_dispatch():
        def fill_expert(e, c):
            first = jnp.maximum(pend_ref[e] - MOE_ROWS, 0)

            def fill(r, c2):
                tok_ref[first + r] = (first + r) & (TOKENS - 1)
                return c2

            lax.fori_loop(0, MOE_ROWS, fill, 0, unroll=INVERT_UNROLL)
            return c

        lax.fori_loop(0, N_EXPERTS, fill_expert, 0)

        def place(t, c):
            for k in range(MOE_TOP_K):
                tok_ref[dest_ref[k * TOKENS + t]] = t
            return c

        lax.fori_loop(0, TOKENS, place, 0, unroll=INVERT_UNROLL)

    def gather(block, slot):
        base = block * MOE_ROWS
        for r in range(MOE_ROWS):
            src = hp_ref.at[pl.ds(pl.multiple_of(tok_ref[base + r] * PACK_SUB, PACK_SUB), PACK_SUB), :]
            dst = xbuf.at[pl.ds(pl.multiple_of(slot * MOE_SLOT_ROWS + r * PACK_SUB, PACK_SUB), PACK_SUB), :]
            pltpu.make_async_copy(src, dst, sem.at[slot]).start()

    @pl.when(i == 0)
    def _():
        for ahead in range(WEIGHT_SLOTS - 1):
            @pl.when(ahead < n_experts_used)
            def _():
                for cp in weight_copies(ahead):
                    cp.start(priority=WEIGHT_DMA_PRIORITY)
        invert_dispatch()
        for ahead in range(GATHER_SLOTS - 1):
            gather(jnp.minimum(ahead, n_used - 1), ahead)

    @pl.when(i < n_used)
    def _():
        ordinal = ord_ref[i]

        @pl.when(first_ref[i] == 1)
        def _():
            for cp in weight_copies(ordinal):
                cp.wait()

            @pl.when(ordinal + WEIGHT_SLOTS - 1 < n_experts_used)
            def _():
                for cp in weight_copies(ordinal + WEIGHT_SLOTS - 1):
                    cp.start(priority=WEIGHT_DMA_PRIORITY)

        slot = i % GATHER_SLOTS
        ws = ordinal % WEIGHT_SLOTS
        pltpu.make_async_copy(slot_rows(slot), slot_rows(slot), sem.at[slot]).wait()
        ahead = i + GATHER_SLOTS - 1
        xb = _unpack_rows(xbuf, slot * MOE_SLOT_ROWS, MOE_ROWS).astype(BF16)
        gather(jnp.minimum(ahead, n_used - 1), ahead % GATHER_SLOTS)
        a = jnp.dot(xb, wg_buf[ws].astype(BF16), preferred_element_type=F32)
        u = jnp.dot(xb, wu_buf[ws].astype(BF16), preferred_element_type=F32)
        act = (a * jax.nn.sigmoid(a) * u).astype(BF16)
        y = jnp.dot(act, wd_buf[ws].astype(BF16), preferred_element_type=F32)
        _pack_rows(y, y_ref)

        @pl.when(i == n_used - 1)
        def _():
            for extra in range(1, GATHER_SLOTS):
                other = (i + extra) % GATHER_SLOTS
                pltpu.make_async_copy(slot_rows(other), slot_rows(other), sem.at[other]).wait()

    @pl.when(i >= n_used)
    def _():
        y_ref[...] = jnp.zeros_like(y_ref)


def _moe(dest, block_ord, block_first, by_ordinal, meta, pends, hp, w_gate, w_up, w_down):
    hbm = pl.BlockSpec(memory_space=pl.ANY)
    grid_spec = pltpu.PrefetchScalarGridSpec(
        num_scalar_prefetch=6,
        grid=(MOE_BLOCKS,),
        in_specs=[hbm, hbm, hbm, hbm],
        out_specs=pl.BlockSpec((MOE_SLOT_ROWS, LANES), lambda i, *_: (i, 0)),
        scratch_shapes=[pltpu.VMEM((GATHER_SLOTS * MOE_SLOT_ROWS, LANES), F32),
                        pltpu.VMEM((WEIGHT_SLOTS, D_MODEL, D_EXPERT), F32),
                        pltpu.VMEM((WEIGHT_SLOTS, D_MODEL, D_EXPERT), F32),
                        pltpu.VMEM((WEIGHT_SLOTS, D_EXPERT, D_MODEL), F32),
                        pltpu.SMEM((MOE_BLOCKS * MOE_ROWS,), jnp.int32),
                        pltpu.SemaphoreType.DMA((GATHER_SLOTS,)), pltpu.SemaphoreType.DMA((WEIGHT_SLOTS,))],
    )
    return pl.pallas_call(
        _moe_kernel,
        grid_spec=grid_spec,
        out_shape=jax.ShapeDtypeStruct((MOE_BLOCKS * MOE_SLOT_ROWS, LANES), F32),
        compiler_params=_cparams(("arbitrary",)),
        name="moe",
    )(dest, block_ord, block_first, by_ordinal, meta, pends, hp, w_gate, w_up, w_down)


COMBINE_TM = 256


def _combine_kernel(dest_ref, x1_ref, info_ref, g_ref, y_ref, o_ref, ybuf, sem):
    i = pl.program_id(0)
    last = pl.num_programs(0) - 1
    k_rows = COMBINE_TM * PACK_SUB
    slot_rows = MOE_TOP_K * k_rows

    def slot_ref(slot):
        return ybuf.at[pl.ds(pl.multiple_of(slot * slot_rows, slot_rows), slot_rows), :]

    def gather(tile, slot):
        for r in range(COMBINE_TM):
            for k in range(MOE_TOP_K):
                row = dest_ref[k * TOKENS + tile * COMBINE_TM + r]
                src = y_ref.at[pl.ds(pl.multiple_of(row * PACK_SUB, PACK_SUB), PACK_SUB), :]
                at = slot * slot_rows + k * k_rows + r * PACK_SUB
                dst = ybuf.at[pl.ds(pl.multiple_of(at, PACK_SUB), PACK_SUB), :]
                pltpu.make_async_copy(src, dst, sem.at[slot]).start(priority=k)

    @pl.when(i == 0)
    def _():
        for ahead in range(GATHER_SLOTS - 1):
            gather(ahead, ahead)

    slot = i % GATHER_SLOTS
    pltpu.make_async_copy(slot_ref(slot), slot_ref(slot), sem.at[slot]).wait()
    ahead = i + GATHER_SLOTS - 1
    gather(jnp.minimum(ahead, last), ahead % GATHER_SLOTS)
    x2 = x1_ref[...]
    for k in range(MOE_TOP_K):
        yk = _unpack_rows(ybuf, slot * slot_rows + k * k_rows, COMBINE_TM)
        x2 = x2 + info_ref[:, INFO_GATE + k:INFO_GATE + k + 1] * yk
    inv = lax.rsqrt(jnp.mean(x2 * x2, axis=-1, keepdims=True) + RMS_EPS)
    o_ref[...] = x2 * inv * g_ref[...]

    @pl.when(i == last)
    def _():
        for extra in range(1, GATHER_SLOTS):
            other = (i + extra) % GATHER_SLOTS
            pltpu.make_async_copy(slot_ref(other), slot_ref(other), sem.at[other]).wait()


def _combine(dest, x1, info, g, y_pad):
    tm = COMBINE_TM
    grid_spec = pltpu.PrefetchScalarGridSpec(
        num_scalar_prefetch=1,
        grid=(TOKENS // tm,),
        in_specs=[
            pl.BlockSpec((tm, D_MODEL), lambda i, d: (i, 0)),
            pl.BlockSpec((tm, ROUTER_COLS), lambda i, d: (i, 0)),
            pl.BlockSpec((1, D_MODEL), lambda i, d: (0, 0)),
            pl.BlockSpec(memory_space=pl.ANY),
        ],
        out_specs=pl.BlockSpec((tm, D_MODEL), lambda i, d: (i, 0)),
        scratch_shapes=[pltpu.VMEM((GATHER_SLOTS * MOE_TOP_K * tm * PACK_SUB, LANES), F32),
                        pltpu.SemaphoreType.DMA((GATHER_SLOTS,))],
    )
    return pl.pallas_call(
        _combine_kernel,
        grid_spec=grid_spec,
        out_shape=jax.ShapeDtypeStruct((TOKENS, D_MODEL), F32),
        compiler_params=_cparams(("arbitrary",)),
        name="combine",
    )(dest, x1, info, g, y_pad)


def kernel(x, mix_norm_g, w_in, w_fourier_out, ssm_A_re, ssm_A_im, ssm_log_dt, ssm_B_re, ssm_B_im, ssm_C_re,
           ssm_C_im, ssm_D, ssm_w_glu, w_ssm_out, w_out, ffn_norm_g, router_group_w, router_group_b,
           router_expert_w, router_expert_b, expert_w_gate, expert_w_up, expert_w_down, final_norm_g):
    assert x.shape == (BATCH, SEQ, D_MODEL) and w_in.shape[0] == 1
    tw, f2, cdft = _dft_constants()

    vf, us, xg, gates = _inproj(x, mix_norm_g[0][None], w_in[0].astype(BF16), cdft)
    fmix = _dft(vf, tw, f2)

    mg, wsgt, wog, aq = _ssm_operators(ssm_A_re[0], ssm_A_im[0], ssm_log_dt[0], ssm_B_re[0], ssm_B_im[0],
                                       ssm_C_re[0], ssm_C_im[0])
    flip = jnp.asarray(np.eye(N_CHUNKS)[::-1], BF16)
    sre, sim = _ssm_states(xg, wsgt, flip)
    yconv = _ssm_out(xg, mg, wog, flip, *_ssm_scan(sre, sim, aq))

    w_router = jnp.concatenate([router_group_w[0], router_expert_w[0]], axis=1)
    w_router = jnp.pad(w_router, ((0, 0), (0, ROUTER_COLS - w_router.shape[1])))
    b_router = jnp.concatenate([router_group_b[0], router_expert_b[0]])
    b_router = jnp.pad(b_router, (0, ROUTER_COLS - b_router.shape[0]))[None]
    wr_hi = w_router.astype(BF16)
    wr_lo = (w_router - wr_hi.astype(F32)).astype(BF16)
    x1, hp, info, fields, counts = _merge(
        x.reshape(TOKENS, D_MODEL), fmix, yconv, us, gates, ssm_D[0][None], w_fourier_out[0].astype(BF16),
        ssm_w_glu[0].astype(BF16), w_ssm_out[0].astype(BF16), w_out[0].astype(BF16), ffn_norm_g[0][None],
        wr_hi, wr_lo, b_router)

    dest, block_ord, block_first, by_ordinal, meta, pends = _dispatch_plan(fields, counts)
    y_pad = _moe(dest, block_ord, block_first, by_ordinal, meta, pends, hp, expert_w_gate[0], expert_w_up[0],
                 expert_w_down[0])
    out = _combine(dest, x1, info, final_norm_g[None], y_pad)
    return out.reshape(BATCH, SEQ, D_MODEL)
```

```python
import math

import numpy as np
import jax
import jax.numpy as jnp
from jax import lax
from jax.experimental import pallas as pl
from jax.experimental.pallas import tpu as pltpu

F32 = jnp.float32
BF16 = jnp.bfloat16

D_MODEL = 1024
BATCH = 4
SEQ = 4096
TOKENS = BATCH * SEQ
FOURIER_WIDTH = 512
FOURIER_GROUP_CH = 128
FOURIER_GROUPS = 4
SSM_WIDTH = 512
SSM_GROUP_CH = 16
SSM_GROUPS = 32
SSM_STATE = 64
MOE_GROUPS = 8
EXPERTS_PER_GROUP = 8
N_EXPERTS = 64
MOE_TOP_K = 2
D_EXPERT = 512
RMS_EPS = 1e-6

LANES = 128
SSM_CHUNK = 16
SSM_LANE_BLOCKS = SSM_WIDTH // LANES
GROUPS_PER_BLOCK = LANES // SSM_GROUP_CH
N_CHUNKS = SEQ // SSM_CHUNK
GROUP_COLS = SSM_CHUNK * SSM_GROUP_CH
GROUP_SHIFT_CH = 4
MOE_ROWS = 256
MOE_BLOCKS = TOKENS * MOE_TOP_K // MOE_ROWS + N_EXPERTS
ROUTER_COLS = 128
VMEM_LIMIT = 48 * 1024 * 1024


def _cparams(sem, vmem=VMEM_LIMIT):
    return pltpu.CompilerParams(dimension_semantics=sem, vmem_limit_bytes=vmem)


IN_TM = 512


def _inproj_kernel(x_ref, g_ref, w_ref, cdft_ref, vf_ref, us_ref, xg_ref, gates_ref, zs_scr):
    x = x_ref[0]
    inv = lax.rsqrt(jnp.mean(x * x, axis=-1, keepdims=True) + RMS_EPS)
    h = (x * inv * g_ref[...]).astype(BF16)
    zf = jnp.dot(h, w_ref[:, 0:FOURIER_WIDTH], preferred_element_type=F32).astype(BF16)
    cdft = cdft_ref[...].astype(BF16)
    for g in range(FOURIER_GROUPS):
        sl = slice(g * LANES, (g + 1) * LANES)
        v = jnp.dot(zf[:, sl], cdft, preferred_element_type=F32)
        vf_ref[0, 0, :, sl] = v[:, :LANES].astype(BF16)
        vf_ref[0, 1, :, sl] = v[:, LANES:].astype(BF16)
    zs = jnp.dot(h, w_ref[:, FOURIER_WIDTH:FOURIER_WIDTH + SSM_WIDTH], preferred_element_type=F32)
    for j in range(SSM_LANE_BLOCKS):
        us_ref[j] = zs[:, j * LANES:(j + 1) * LANES].astype(BF16)
        zs_scr[j] = zs[:, j * LANES:(j + 1) * LANES]
    seg = lax.broadcasted_iota(jnp.int32, (IN_TM // SSM_CHUNK, LANES), 1) >> GROUP_SHIFT_CH
    for j in range(SSM_LANE_BLOCKS):
        pieces = [zs_scr[j, pl.ds(q, IN_TM // SSM_CHUNK, stride=SSM_CHUNK), :] for q in range(SSM_CHUNK)]
        for g in range(GROUPS_PER_BLOCK):
            for half in range(SSM_CHUNK // GROUPS_PER_BLOCK):
                acc = jnp.zeros((IN_TM // SSM_CHUNK, LANES), F32)
                for ql in range(GROUPS_PER_BLOCK):
                    shift = ((ql - g) % GROUPS_PER_BLOCK) * SSM_GROUP_CH
                    piece = pieces[half * GROUPS_PER_BLOCK + ql]
                    moved = piece if shift == 0 else pltpu.roll(piece, shift, axis=1)
                    acc = jnp.where(seg == ql, moved, acc)
                xg_ref[j * GROUPS_PER_BLOCK + g, :, half * LANES:(half + 1) * LANES] = acc.astype(BF16)
    base = FOURIER_WIDTH + SSM_WIDTH
    for n in range(4):
        zg = jnp.dot(h, w_ref[:, base + n * 512: base + (n + 1) * 512], preferred_element_type=F32)
        gates_ref[:, n * 512:(n + 1) * 512] = jax.nn.sigmoid(zg).astype(BF16)


def _inproj(x, g, w_in, cdft):
    nt = SEQ // IN_TM
    return pl.pallas_call(
        _inproj_kernel,
        grid=(BATCH, nt),
        in_specs=[
            pl.BlockSpec((1, IN_TM, D_MODEL), lambda b, i: (b, i, 0)),
            pl.BlockSpec((1, D_MODEL), lambda b, i: (0, 0)),
            pl.BlockSpec(w_in.shape, lambda b, i: (0, 0)),
            pl.BlockSpec(cdft.shape, lambda b, i: (0, 0)),
        ],
        out_specs=[
            pl.BlockSpec((1, 2, IN_TM, FOURIER_WIDTH), lambda b, i: (b, 0, i, 0)),
            pl.BlockSpec((SSM_LANE_BLOCKS, IN_TM, LANES), lambda b, i: (0, b * nt + i, 0)),
            pl.BlockSpec((SSM_GROUPS, IN_TM // SSM_CHUNK, GROUP_COLS), lambda b, i: (0, b * nt + i, 0)),
            pl.BlockSpec((IN_TM, 2 * D_MODEL), lambda b, i: (b * nt + i, 0)),
        ],
        out_shape=[
            jax.ShapeDtypeStruct((BATCH, 2, SEQ, FOURIER_WIDTH), BF16),
            jax.ShapeDtypeStruct((SSM_LANE_BLOCKS, TOKENS, LANES), BF16),
            jax.ShapeDtypeStruct((SSM_GROUPS, TOKENS // SSM_CHUNK, GROUP_COLS), BF16),
            jax.ShapeDtypeStruct((TOKENS, 2 * D_MODEL), BF16),
        ],
        scratch_shapes=[pltpu.VMEM((SSM_LANE_BLOCKS, IN_TM, LANES), F32)],
        compiler_params=_cparams(("parallel", "parallel")),
        name="inproj",
    )(x, g, w_in, cdft)


DFT_R1 = 8
DFT_R2 = SEQ // DFT_R1
DFT_LANES = 2 * LANES
DFT_ROWS = 16


def _cmul_const(z, w):
    re, im = z
    if abs(w.imag) < 1e-12:
        return (re, im) if abs(w.real - 1.0) < 1e-12 else (re * w.real, im * w.real)
    if abs(w.real) < 1e-12:
        return (im, -re) if abs(w.imag + 1.0) < 1e-12 else (-im * w.imag, re * w.imag)
    return re * w.real - im * w.imag, re * w.imag + im * w.real


def _fft_blocks(xs):
    n = len(xs)
    if n == 1:
        return xs
    even, odd = _fft_blocks(xs[0::2]), _fft_blocks(xs[1::2])
    out = [None] * n
    for k in range(n // 2):
        tr, ti = _cmul_const(odd[k], np.exp(-2j * np.pi * k / n))
        out[k] = (even[k][0] + tr, even[k][1] + ti)
        out[k + n // 2] = (even[k][0] - tr, even[k][1] - ti)
    return out


def _dft_kernel(v_ref, tw_ref, f_ref, o_ref, a_scr, o_scr):
    def tile(i, c):
        r0 = pl.multiple_of(i * DFT_ROWS, DFT_ROWS)
        for slab in range(DFT_LANES // LANES):
            lanes = slice(slab * LANES, (slab + 1) * LANES)
            xs = [(v_ref[0, 0, pl.ds(s1 * DFT_R2 + r0, DFT_ROWS), lanes].astype(F32),
                   v_ref[0, 1, pl.ds(s1 * DFT_R2 + r0, DFT_ROWS), lanes].astype(F32)) for s1 in range(DFT_R1)]
            for t1, (ar, ai) in enumerate(_fft_blocks(xs)):
                tr, ti = tw_ref[0, t1, pl.ds(r0, DFT_ROWS), :], tw_ref[1, t1, pl.ds(r0, DFT_ROWS), :]
                a_scr[t1, pl.ds(r0, DFT_ROWS), lanes] = (ar * tr - ai * ti).astype(BF16)
                a_scr[t1, pl.ds(DFT_R2 + r0, DFT_ROWS), lanes] = (ar * ti + ai * tr).astype(BF16)
        return c

    lax.fori_loop(0, DFT_R2 // DFT_ROWS, tile, 0)

    f2 = f_ref[...].astype(BF16)
    for t1 in range(DFT_R1):
        r = jnp.dot(f2, a_scr[t1], preferred_element_type=F32)
        for slab in range(DFT_LANES // LANES):
            o_scr[slab, pl.ds(t1, DFT_R2, stride=DFT_R1), :] = r[:, slab * LANES:(slab + 1) * LANES]
    for slab in range(DFT_LANES // LANES):
        o_ref[:, slab * LANES:(slab + 1) * LANES] = o_scr[slab].astype(BF16)


def _dft(v, tw, f2):
    nh = FOURIER_WIDTH // DFT_LANES
    return pl.pallas_call(
        _dft_kernel,
        grid=(BATCH, nh),
        in_specs=[
            pl.BlockSpec((1, 2, SEQ, DFT_LANES), lambda b, h: (b, 0, 0, h)),
            pl.BlockSpec(tw.shape, lambda b, h: (0, 0, 0, 0)),
            pl.BlockSpec(f2.shape, lambda b, h: (0, 0)),
        ],
        out_specs=pl.BlockSpec((SEQ, DFT_LANES), lambda b, h: (b, h)),
        out_shape=jax.ShapeDtypeStruct((TOKENS, FOURIER_WIDTH), BF16),
        scratch_shapes=[pltpu.VMEM((DFT_R1, 2 * DFT_R2, DFT_LANES), BF16),
                        pltpu.VMEM((DFT_LANES // LANES, SEQ, LANES), F32)],
        compiler_params=_cparams(("parallel", "parallel")),
        name="dft",
    )(v, tw, f2)


def _dft_constants():
    t1 = np.arange(DFT_R1)
    s2 = np.arange(DFT_R2)
    ang = 2.0 * np.pi * np.outer(t1, s2) / SEQ
    scale = 1.0 / math.sqrt(SEQ)
    tw = np.stack([np.cos(ang) * scale, -np.sin(ang) * scale])
    tw = np.repeat(tw[..., None], LANES, axis=-1)
    ang2 = 2.0 * np.pi * np.outer(s2, s2) / DFT_R2
    f2 = np.concatenate([np.cos(ang2), np.sin(ang2)], axis=1)
    kc = np.arange(FOURIER_GROUP_CH)
    angc = 2.0 * np.pi * np.outer(kc, kc) / FOURIER_GROUP_CH
    cs = 1.0 / math.sqrt(FOURIER_GROUP_CH)
    cdft = np.concatenate([np.cos(angc) * cs, -np.sin(angc) * cs], axis=1)
    return tuple(jnp.asarray(v, F32) for v in (tw, f2, cdft))


GROUP_BLOCKS = SSM_GROUPS // GROUPS_PER_BLOCK
DIR_STATE = 2 * SSM_STATE
STATE_LANES = SSM_GROUPS * DIR_STATE


def _ssm_operators(a_re, a_im, log_dt, b_re, b_im, c_re, c_im):
    q_len = SSM_CHUNK
    hi = lax.Precision.HIGHEST
    dt = jnp.exp(log_dt)[..., None]
    lr, li = a_re * dt, a_im * dt
    steps = jnp.arange(q_len + 1, dtype=F32)
    mag = jnp.exp(lr[..., None] * steps)
    ang = li[..., None] * steps
    pr, pi = mag * jnp.cos(ang), mag * jnp.sin(ang)
    ar, ai = pr[..., 1], pi[..., 1]
    den = a_re * a_re + a_im * a_im
    cr = ((ar - 1.0) * a_re + ai * a_im) / den
    ci = (ai * a_re - (ar - 1.0) * a_im) / den
    bbr = cr[..., None] * b_re - ci[..., None] * b_im
    bbi = cr[..., None] * b_im + ci[..., None] * b_re

    rq = jnp.asarray(np.kron(np.eye(q_len), np.ones((1, SSM_GROUP_CH))), F32)
    rc = jnp.asarray(np.kron(np.ones((1, q_len)), np.eye(SSM_GROUP_CH)), F32)

    def per_group(x):
        return jnp.swapaxes(x, 0, 1).reshape(SSM_GROUPS, DIR_STATE, x.shape[-1])

    def on_cols(x, rep):
        return jnp.einsum('grk,kc->grc', per_group(x), rep, precision=hi)

    def state_operator(pw_r, pw_i, f_r, f_i, negate_im):
        p_r, p_i, q_r, q_i = on_cols(pw_r, rq), on_cols(pw_i, rq), on_cols(f_r, rc), on_cols(f_i, rc)
        w_im = p_r * q_i + p_i * q_r
        return jnp.concatenate([p_r * q_r - p_i * q_i, -w_im if negate_im else w_im], axis=1).astype(BF16)

    def both(p, fwd, bwd):
        return jnp.stack([p[0][..., fwd], p[1][..., bwd]])

    rev = slice(q_len - 1, None, -1)
    wsgt = state_operator(both(pr, rev, slice(0, q_len)), both(pi, rev, slice(0, q_len)), bbr, bbi, False)
    ctr, cti = jnp.swapaxes(c_re, -1, -2), jnp.swapaxes(c_im, -1, -2)
    wog = state_operator(both(pr, slice(1, None), slice(q_len, 0, -1)), both(pi, slice(1, None), slice(q_len, 0, -1)),
                         ctr, cti, True)

    prq, piq = pr[..., :q_len], pi[..., :q_len]
    cpr = jnp.einsum('dgcn,dgnt->dgtcn', c_re, prq) - jnp.einsum('dgcn,dgnt->dgtcn', c_im, piq)
    cpi = jnp.einsum('dgcn,dgnt->dgtcn', c_re, piq) + jnp.einsum('dgcn,dgnt->dgtcn', c_im, prq)
    kern = jnp.einsum('dgtcn,dgne->dgtec', cpr, bbr) - jnp.einsum('dgtcn,dgne->dgtec', cpi, bbi)
    lags = jnp.concatenate([kern[1][:, :0:-1], kern[0][:, :1] + kern[1][:, :1], kern[0][:, 1:]], axis=1)
    lag_cols = jnp.einsum('gtek,kc->gtec', lags, rc, precision=hi)
    lane_q = jnp.arange(GROUP_COLS, dtype=jnp.int32) // SSM_GROUP_CH
    rows = []
    for qp in range(q_len):
        acc = jnp.zeros(lag_cols[:, 0].shape, F32)
        for q in range(q_len):
            acc = jnp.where(lane_q == q, lag_cols[:, q - qp + q_len - 1], acc)
        rows.append(acc)
    mg = jnp.stack(rows, axis=1)
    mg = mg.reshape(SSM_GROUPS, GROUP_COLS, GROUP_COLS).astype(BF16)

    aq = jnp.stack([per_group(pr[..., q_len:]), per_group(pi[..., q_len:])])
    return mg, wsgt, wog, aq.reshape(2, STATE_LANES)


def _ssm_state_kernel(x_ref, ws_ref, flip_ref, sre_ref, sim_ref):
    backward = (lax.broadcasted_iota(jnp.int32, (N_CHUNKS, DIR_STATE), 1) & SSM_STATE) != 0
    nt = (((1,), (1,)), ((), ()))
    for g in range(GROUPS_PER_BLOCK):
        lanes = slice(g * DIR_STATE, (g + 1) * DIR_STATE)
        for b in range(BATCH):
            x = x_ref[g, b * N_CHUNKS:(b + 1) * N_CHUNKS, :]
            x_rev = jnp.dot(flip_ref[...], x, preferred_element_type=F32).astype(BF16)
            s = lax.dot_general(x, ws_ref[g], nt, preferred_element_type=F32)
            s_rev = lax.dot_general(x_rev, ws_ref[g], nt, preferred_element_type=F32)
            sre_ref[b, :, lanes] = jnp.where(backward, s_rev[:, :DIR_STATE], s[:, :DIR_STATE])
            sim_ref[b, :, lanes] = jnp.where(backward, s_rev[:, DIR_STATE:], s[:, DIR_STATE:])


def _ssm_states(xg, wsgt, flip):
    out_spec = pl.BlockSpec((BATCH, N_CHUNKS, GROUPS_PER_BLOCK * DIR_STATE), lambda j: (0, 0, j))
    shape = jax.ShapeDtypeStruct((BATCH, N_CHUNKS, STATE_LANES), F32)
    return pl.pallas_call(
        _ssm_state_kernel,
        grid=(GROUP_BLOCKS,),
        in_specs=[
            pl.BlockSpec((GROUPS_PER_BLOCK, BATCH * N_CHUNKS, GROUP_COLS), lambda j: (j, 0, 0)),
            pl.BlockSpec((GROUPS_PER_BLOCK, 2 * DIR_STATE, GROUP_COLS), lambda j: (j, 0, 0)),
            pl.BlockSpec(flip.shape, lambda j: (0, 0)),
        ],
        out_specs=[out_spec, out_spec],
        out_shape=[shape, shape],
        compiler_params=_cparams(("parallel",)),
        name="ssm_states",
    )(xg, wsgt, flip)


SCAN_UNROLL = 4


def _ssm_scan_kernel(sre_ref, sim_ref, aq_ref, hre_ref, him_ref):
    ar, ai = aq_ref[0:1], aq_ref[1:2]

    def body(k, carry):
        hr, hi = carry
        hre_ref[0, pl.ds(k, 1), :] = hr
        him_ref[0, pl.ds(k, 1), :] = hi
        sr, si = sre_ref[0, pl.ds(k, 1), :], sim_ref[0, pl.ds(k, 1), :]
        return ar * hr - ai * hi + sr, ar * hi + ai * hr + si

    z = jnp.zeros((1, STATE_LANES), F32)
    lax.fori_loop(0, N_CHUNKS, body, (z, z), unroll=SCAN_UNROLL)


def _ssm_scan(sre, sim, aq):
    spec = pl.BlockSpec((1, N_CHUNKS, STATE_LANES), lambda b: (b, 0, 0))
    shape = jax.ShapeDtypeStruct(sre.shape, F32)
    return pl.pallas_call(
        _ssm_scan_kernel,
        grid=(BATCH,),
        in_specs=[spec, spec, pl.BlockSpec(aq.shape, lambda b: (0, 0))],
        out_specs=[spec, spec],
        out_shape=[shape, shape],
        compiler_params=_cparams(("parallel",)),
        name="ssm_scan",
    )(sre, sim, aq)


def _ssm_out_kernel(x_ref, m_ref, wo_ref, flip_ref, hre_ref, him_ref, y_ref, y_scr):
    fwd = (lax.broadcasted_iota(jnp.int32, (N_CHUNKS, GROUPS_PER_BLOCK * DIR_STATE), 1) & SSM_STATE) == 0

    def in_chunk_order(h_ref):
        h = h_ref[0].astype(BF16)
        return jnp.where(fwd, h, jnp.dot(flip_ref[...], h, preferred_element_type=F32).astype(BF16))

    h_re, h_im = in_chunk_order(hre_ref), in_chunk_order(him_ref)
    accs = []
    for g in range(GROUPS_PER_BLOCK):
        lanes = slice(g * DIR_STATE, (g + 1) * DIR_STATE)
        h = jnp.concatenate([h_re[:, lanes], h_im[:, lanes]], axis=1)
        accs.append(jnp.dot(x_ref[g, 0], m_ref[g], preferred_element_type=F32)
                    + jnp.dot(h, wo_ref[g], preferred_element_type=F32))
    seg = lax.broadcasted_iota(jnp.int32, (N_CHUNKS, LANES), 1) >> GROUP_SHIFT_CH
    for q in range(SSM_CHUNK):
        half, ql = divmod(q, GROUPS_PER_BLOCK)
        piece = jnp.zeros((N_CHUNKS, LANES), F32)
        for g in range(GROUPS_PER_BLOCK):
            src = accs[g][:, half * LANES:(half + 1) * LANES]
            shift = ((g - ql) % GROUPS_PER_BLOCK) * SSM_GROUP_CH
            moved = src if shift == 0 else pltpu.roll(src, shift, axis=1)
            piece = jnp.where(seg == g, moved, piece)
        y_scr[pl.ds(q, N_CHUNKS, stride=SSM_CHUNK), :] = piece
    y_ref[0] = y_scr[...].astype(BF16)


def _ssm_out(xg, mg, wog, flip, hre, him):
    hspec = pl.BlockSpec((1, N_CHUNKS, GROUPS_PER_BLOCK * DIR_STATE), lambda j, b: (b, 0, j))
    wspec = pl.BlockSpec((GROUPS_PER_BLOCK, GROUP_COLS, GROUP_COLS), lambda j, b: (j, 0, 0))
    return pl.pallas_call(
        _ssm_out_kernel,
        grid=(GROUP_BLOCKS, BATCH),
        in_specs=[
            pl.BlockSpec((GROUPS_PER_BLOCK, 1, N_CHUNKS, GROUP_COLS), lambda j, b: (j, b, 0, 0)),
            wspec, wspec, pl.BlockSpec(flip.shape, lambda j, b: (0, 0)), hspec, hspec,
        ],
        out_specs=pl.BlockSpec((1, SEQ, LANES), lambda j, b: (j, b, 0)),
        out_shape=jax.ShapeDtypeStruct((SSM_LANE_BLOCKS, TOKENS, LANES), BF16),
        scratch_shapes=[pltpu.VMEM((SEQ, LANES), F32)],
        compiler_params=_cparams(("parallel", "parallel")),
        name="ssm_out",
    )(xg.reshape(SSM_GROUPS, BATCH, N_CHUNKS, GROUP_COLS), mg, wog, flip, hre, him)


MERGE_TM = 512
GELU_C = math.sqrt(2.0 / math.pi)
PACK_SUB = D_MODEL // LANES


def _split_bf16(v):
    hi = v.astype(BF16)
    lo = (v - hi.astype(F32)).astype(BF16)
    return hi, lo


def _pack_rows(v, out_ref):
    for s in range(PACK_SUB):
        out_ref[pl.ds(s, v.shape[0], stride=PACK_SUB), :] = v[:, s * LANES:(s + 1) * LANES]


def _unpack_rows(buf_ref, start, rows):
    return jnp.concatenate([buf_ref[pl.ds(start + s, rows, stride=PACK_SUB), :] for s in range(PACK_SUB)], axis=1)


def _merge_kernel(x_ref, fm_ref, yc_ref, us_ref, gates_ref, dskip_ref, wf_ref, wglu_ref, ws_ref, wo_ref,
                  ng_ref, wrh_ref, wrl_ref, rb_ref, x1_ref, hp_ref, info_ref, fields_ref, cnt_ref, carry, lg_scr):
    i = pl.program_id(0)

    @pl.when(i == 0)
    def _():
        carry[...] = jnp.zeros_like(carry)
        lg_scr[...] = jnp.zeros_like(lg_scr)

    conv = jnp.concatenate([yc_ref[j].astype(F32) for j in range(SSM_LANE_BLOCKS)], axis=-1)
    u = jnp.concatenate([us_ref[j].astype(F32) for j in range(SSM_LANE_BLOCKS)], axis=-1)
    y = conv + dskip_ref[...] * u
    y = 0.5 * y * (1.0 + jnp.tanh(GELU_C * (y + 0.044715 * (y * y * y))))
    glu = jax.nn.sigmoid(jnp.dot(y.astype(BF16), wglu_ref[...], preferred_element_type=F32))
    y_s = jnp.dot((y * glu).astype(BF16), ws_ref[...], preferred_element_type=F32)
    y_f = jnp.dot(fm_ref[...], wf_ref[...], preferred_element_type=F32)
    merged = (gates_ref[:, :D_MODEL].astype(F32) * y_f + gates_ref[:, D_MODEL:].astype(F32) * y_s)
    x1 = x_ref[...] + jnp.dot(merged.astype(BF16), wo_ref[...], preferred_element_type=F32)
    x1_ref[...] = x1
    inv = lax.rsqrt(jnp.mean(x1 * x1, axis=-1, keepdims=True) + RMS_EPS)
    hn = x1 * inv * ng_ref[...]
    _pack_rows(hn, hp_ref)
    hi, lo = _split_bf16(hn)
    logits = (jnp.dot(hi, wrh_ref[...], preferred_element_type=F32)
              + jnp.dot(lo, wrh_ref[...], preferred_element_type=F32)
              + jnp.dot(hi, wrl_ref[...], preferred_element_type=F32))
    _route_tile(lg_scr[(i + 1) % 2], info_ref, fields_ref, cnt_ref, carry, jnp.where(i > 0, 1.0, 0.0))
    lg_scr[i % 2] = logits + rb_ref[...]


def _merge(x, fmix, yconv, us, gates, dskip, wf, wglu, ws, wo, ng, wrh, wrl, rb):
    tm = MERGE_TM
    last = TOKENS // tm - 1
    full = lambda a: pl.BlockSpec(a.shape, lambda i: (0,) * a.ndim)
    cur = lambda i: jnp.minimum(i, last)
    routed = lambda i: jnp.maximum(i - 1, 0)
    return pl.pallas_call(
        _merge_kernel,
        grid=(last + 2,),
        in_specs=[
            pl.BlockSpec((tm, D_MODEL), lambda i: (cur(i), 0)),
            pl.BlockSpec((tm, FOURIER_WIDTH), lambda i: (cur(i), 0)),
            pl.BlockSpec((SSM_LANE_BLOCKS, tm, LANES), lambda i: (0, cur(i), 0)),
            pl.BlockSpec((SSM_LANE_BLOCKS, tm, LANES), lambda i: (0, cur(i), 0)),
            pl.BlockSpec((tm, 2 * D_MODEL), lambda i: (cur(i), 0)),
            full(dskip), full(wf), full(wglu), full(ws), full(wo), full(ng), full(wrh), full(wrl), full(rb),
        ],
        out_specs=[
            pl.BlockSpec((tm, D_MODEL), lambda i: (cur(i), 0)),
            pl.BlockSpec((tm * PACK_SUB, LANES), lambda i: (cur(i), 0)),
            pl.BlockSpec((tm, ROUTER_COLS), lambda i: (routed(i), 0)),
            pl.BlockSpec((INFO_FIELDS, tm), lambda i: (0, routed(i))),
            pl.BlockSpec((1, ROUTER_COLS), lambda i: (0, 0)),
        ],
        out_shape=[
            jax.ShapeDtypeStruct((TOKENS, D_MODEL), F32),
            jax.ShapeDtypeStruct((TOKENS * PACK_SUB, LANES), F32),
            jax.ShapeDtypeStruct((TOKENS, ROUTER_COLS), F32),
            jax.ShapeDtypeStruct((INFO_FIELDS, TOKENS), F32),
            jax.ShapeDtypeStruct((1, ROUTER_COLS), F32),
        ],
        scratch_shapes=[pltpu.VMEM((1, ROUTER_COLS), F32),
                        pltpu.VMEM((2, tm, ROUTER_COLS), F32)],
        compiler_params=_cparams(("arbitrary",)),
        name="merge",
    )(x, fmix, yconv, us, gates, dskip, wf, wglu, ws, wo, ng, wrh, wrl, rb)


EXPERT_LANE0 = MOE_GROUPS
INFO_EXPERT, INFO_RANK, INFO_GATE = 0, 2, 4
INFO_FIELDS = 8


def _route_tile(lg, info_ref, fields_ref, cnt_ref, carry, weight):
    tm = lg.shape[0]
    col_i = lax.broadcasted_iota(jnp.int32, lg.shape, 1)
    col = col_i.astype(F32)
    neg = jnp.float32(-jnp.inf)
    none = jnp.float32(ROUTER_COLS)

    def row_max(v):
        return jnp.max(v, axis=-1, keepdims=True)

    def first_at(v, m):
        return jnp.min(jnp.where(v == m, col, none), axis=-1, keepdims=True)

    gl = jnp.where(col_i < MOE_GROUPS, lg, neg)
    gmax = row_max(gl)
    p_g = 1.0 / jnp.sum(jnp.exp(gl - gmax), axis=-1, keepdims=True)
    lo = EXPERT_LANE0 + first_at(gl, gmax) * EXPERTS_PER_GROUP
    el = jnp.where((col >= lo) & (col < lo + EXPERTS_PER_GROUP), lg, neg)
    l1 = row_max(el)
    i1 = first_at(el, l1)
    el2 = jnp.where(col == i1, neg, el)
    l2 = row_max(el2)
    i2 = first_at(el2, l2)
    r = jnp.exp(l2 - l1)
    w1 = p_g / (1.0 + r)
    w2 = w1 * r

    hit1, hit2 = col == i1, col == i2
    onehot = jnp.where(hit1 | hit2, 1.0, 0.0)
    earlier = lax.broadcasted_iota(jnp.int32, (tm, tm), 0) > lax.broadcasted_iota(jnp.int32, (tm, tm), 1)
    before = jnp.dot(jnp.where(earlier, 1.0, 0.0).astype(BF16), onehot.astype(BF16),
                     preferred_element_type=F32) + carry[...]
    rank1 = jnp.sum(jnp.where(hit1, before, 0.0), axis=-1, keepdims=True)
    rank2 = jnp.sum(jnp.where(hit2, before, 0.0), axis=-1, keepdims=True)
    carry[...] += weight * jnp.sum(onehot, axis=0, keepdims=True)
    cnt_ref[...] = carry[...]

    info = jnp.zeros(lg.shape, F32)
    for lane, v in ((INFO_EXPERT, i1 - EXPERT_LANE0), (INFO_EXPERT + 1, i2 - EXPERT_LANE0), (INFO_RANK, rank1),
                    (INFO_RANK + 1, rank2), (INFO_GATE, w1), (INFO_GATE + 1, w2)):
        info = jnp.where(col_i == lane, v, info)
    info_ref[...] = info
    fields_ref[...] = info.T[:INFO_FIELDS]


def _dispatch_plan(fields, counts):
    expert = fields[INFO_EXPERT:INFO_EXPERT + MOE_TOP_K].astype(jnp.int32)
    rank = fields[INFO_RANK:INFO_RANK + MOE_TOP_K].astype(jnp.int32)
    cnt = counts[0, EXPERT_LANE0:EXPERT_LANE0 + N_EXPERTS].astype(jnp.int32)
    padded = ((cnt + MOE_ROWS - 1) // MOE_ROWS) * MOE_ROWS
    pends = jnp.cumsum(padded)
    pstarts = pends - padded
    ids = jnp.arange(N_EXPERTS, dtype=jnp.int32)
    dest = rank + jnp.sum(jnp.where(expert[..., None] == ids, pstarts, 0), axis=-1)
    n_used = pends[-1] // MOE_ROWS
    blocks = jnp.arange(MOE_BLOCKS, dtype=jnp.int32)
    block_e = jnp.sum((pends[None, :] <= (blocks * MOE_ROWS)[:, None]).astype(jnp.int32), axis=1)
    block_e = jnp.minimum(block_e, N_EXPERTS - 1)
    last_e = jnp.sum(jnp.where(blocks == n_used - 1, block_e, 0))
    used = (cnt > 0).astype(jnp.int32)
    ordinal = jnp.cumsum(used) - used
    n_experts_used = jnp.sum(used)
    by_ordinal = jnp.sum(jnp.where((ordinal[None, :] == ids[:, None]) & (used[None, :] > 0), ids[None, :], 0), axis=1)
    block_ord = jnp.sum(jnp.where(block_e[:, None] == ids[None, :], ordinal[None, :], 0), axis=1)
    block_first = (blocks * MOE_ROWS == jnp.sum(jnp.where(block_e[:, None] == ids[None, :], pstarts[None, :], 0), axis=1))
    block_first = (block_first & (blocks < n_used)).astype(jnp.int32)
    meta = jnp.concatenate([n_used.reshape(1), n_experts_used.reshape(1)]).astype(jnp.int32)
    return dest.reshape(MOE_TOP_K * TOKENS), block_ord.astype(jnp.int32), block_first, by_ordinal.astype(jnp.int32), meta, pends


MOE_SLOT_ROWS = MOE_ROWS * PACK_SUB
INVERT_UNROLL = 32


WEIGHT_SLOTS = 3
GATHER_SLOTS = 3
WEIGHT_DMA_PRIORITY = 1


def _moe_kernel(dest_ref, ord_ref, first_ref, eo_ref, meta_ref, pend_ref, hp_ref, wg_hbm, wu_hbm, wd_hbm, y_ref,
                xbuf, wg_buf, wu_buf, wd_buf, tok_ref, sem, wsem):
    i = pl.program_id(0)
    n_used = meta_ref[0]
    n_experts_used = meta_ref[1]

    def slot_rows(slot):
        return xbuf.at[pl.ds(pl.multiple_of(slot * MOE_SLOT_ROWS, MOE_SLOT_ROWS), MOE_SLOT_ROWS), :]

    def weight_copies(ordinal):
        e = eo_ref[ordinal]
        ws = ordinal % WEIGHT_SLOTS
        return [pltpu.make_async_copy(hbm.at[e], buf.at[ws], wsem.at[ws])
                for hbm, buf in ((wg_hbm, wg_buf), (wu_hbm, wu_buf), (wd_hbm, wd_buf))]

    def invert_dispatch():
        def fill_expert(e, c):
            first = jnp.maximum(pend_ref[e] - MOE_ROWS, 0)

            def fill(r, c2):
                tok_ref[first + r] = ((first + r) & (TOKENS - 1)) * PACK_SUB
                return c2

            lax.fori_loop(0, MOE_ROWS, fill, 0, unroll=INVERT_UNROLL)
            return c

        lax.fori_loop(0, N_EXPERTS, fill_expert, 0)

        def place(t, c):
            for k in range(MOE_TOP_K):
                tok_ref[dest_ref[k * TOKENS + t]] = t * PACK_SUB
            return c

        lax.fori_loop(0, TOKENS, place, 0, unroll=INVERT_UNROLL)

    def gather(block, slot):
        base = block * MOE_ROWS
        for r in range(MOE_ROWS):
            src = hp_ref.at[pl.ds(pl.multiple_of(tok_ref[base + r], PACK_SUB), PACK_SUB), :]
            dst = xbuf.at[pl.ds(pl.multiple_of(slot * MOE_SLOT_ROWS + r * PACK_SUB, PACK_SUB), PACK_SUB), :]
            pltpu.make_async_copy(src, dst, sem.at[slot]).start()

    @pl.when(i == 0)
    def _():
        for ahead in range(WEIGHT_SLOTS - 1):
            @pl.when(ahead < n_experts_used)
            def _():
                for cp in weight_copies(ahead):
                    cp.start(priority=WEIGHT_DMA_PRIORITY)
        invert_dispatch()
        for ahead in range(GATHER_SLOTS - 1):
            gather(jnp.minimum(ahead, n_used - 1), ahead)

    @pl.when(i < n_used)
    def _():
        ordinal = ord_ref[i]

        @pl.when(first_ref[i] == 1)
        def _():
            for cp in weight_copies(ordinal):
                cp.wait()

            @pl.when(ordinal + WEIGHT_SLOTS - 1 < n_experts_used)
            def _():
                for cp in weight_copies(ordinal + WEIGHT_SLOTS - 1):
                    cp.start(priority=WEIGHT_DMA_PRIORITY)

        slot = i % GATHER_SLOTS
        ws = ordinal % WEIGHT_SLOTS
        pltpu.make_async_copy(slot_rows(slot), slot_rows(slot), sem.at[slot]).wait()
        ahead = i + GATHER_SLOTS - 1
        xb = _unpack_rows(xbuf, slot * MOE_SLOT_ROWS, MOE_ROWS).astype(BF16)
        gather(jnp.minimum(ahead, n_used - 1), ahead % GATHER_SLOTS)
        a = jnp.dot(xb, wg_buf[ws].astype(BF16), preferred_element_type=F32)
        u = jnp.dot(xb, wu_buf[ws].astype(BF16), preferred_element_type=F32)
        act = (a * jax.nn.sigmoid(a) * u).astype(BF16)
        y = jnp.dot(act, wd_buf[ws].astype(BF16), preferred_element_type=F32)
        _pack_rows(y, y_ref)

        @pl.when(i == n_used - 1)
        def _():
            for extra in range(1, GATHER_SLOTS):
                other = (i + extra) % GATHER_SLOTS
                pltpu.make_async_copy(slot_rows(other), slot_rows(other), sem.at[other]).wait()

    @pl.when(i >= n_used)
    def _():
        y_ref[...] = jnp.zeros_like(y_ref)


def _moe(dest, block_ord, block_first, by_ordinal, meta, pends, hp, w_gate, w_up, w_down):
    hbm = pl.BlockSpec(memory_space=pl.ANY)
    grid_spec = pltpu.PrefetchScalarGridSpec(
        num_scalar_prefetch=6,
        grid=(MOE_BLOCKS,),
        in_specs=[hbm, hbm, hbm, hbm],
        out_specs=pl.BlockSpec((MOE_SLOT_ROWS, LANES), lambda i, *_: (i, 0)),
        scratch_shapes=[pltpu.VMEM((GATHER_SLOTS * MOE_SLOT_ROWS, LANES), F32),
                        pltpu.VMEM((WEIGHT_SLOTS, D_MODEL, D_EXPERT), F32),
                        pltpu.VMEM((WEIGHT_SLOTS, D_MODEL, D_EXPERT), F32),
                        pltpu.VMEM((WEIGHT_SLOTS, D_EXPERT, D_MODEL), F32),
                        pltpu.SMEM((MOE_BLOCKS * MOE_ROWS,), jnp.int32),
                        pltpu.SemaphoreType.DMA((GATHER_SLOTS,)), pltpu.SemaphoreType.DMA((WEIGHT_SLOTS,))],
    )
    return pl.pallas_call(
        _moe_kernel,
        grid_spec=grid_spec,
        out_shape=jax.ShapeDtypeStruct((MOE_BLOCKS * MOE_SLOT_ROWS, LANES), F32),
        compiler_params=_cparams(("arbitrary",)),
        name="moe",
    )(dest, block_ord, block_first, by_ordinal, meta, pends, hp, w_gate, w_up, w_down)


COMBINE_TM = 256


def _combine_kernel(dest_ref, x1_ref, info_ref, g_ref, y_ref, o_ref, ybuf, sem):
    i = pl.program_id(0)
    last = pl.num_programs(0) - 1
    k_rows = COMBINE_TM * PACK_SUB
    slot_rows = MOE_TOP_K * k_rows

    def slot_ref(slot):
        return ybuf.at[pl.ds(pl.multiple_of(slot * slot_rows, slot_rows), slot_rows), :]

    def gather(tile, slot):
        for r in range(COMBINE_TM):
            for k in range(MOE_TOP_K):
                row = dest_ref[k * TOKENS + tile * COMBINE_TM + r]
                src = y_ref.at[pl.ds(pl.multiple_of(row * PACK_SUB, PACK_SUB), PACK_SUB), :]
                at = slot * slot_rows + k * k_rows + r * PACK_SUB
                dst = ybuf.at[pl.ds(pl.multiple_of(at, PACK_SUB), PACK_SUB), :]
                pltpu.make_async_copy(src, dst, sem.at[slot]).start(priority=k)

    @pl.when(i == 0)
    def _():
        for ahead in range(GATHER_SLOTS - 1):
            gather(ahead, ahead)

    slot = i % GATHER_SLOTS
    pltpu.make_async_copy(slot_ref(slot), slot_ref(slot), sem.at[slot]).wait()
    ahead = i + GATHER_SLOTS - 1
    gather(jnp.minimum(ahead, last), ahead % GATHER_SLOTS)
    x2 = x1_ref[...]
    for k in range(MOE_TOP_K):
        yk = _unpack_rows(ybuf, slot * slot_rows + k * k_rows, COMBINE_TM)
        x2 = x2 + info_ref[:, INFO_GATE + k:INFO_GATE + k + 1] * yk
    inv = lax.rsqrt(jnp.mean(x2 * x2, axis=-1, keepdims=True) + RMS_EPS)
    o_ref[...] = x2 * inv * g_ref[...]

    @pl.when(i == last)
    def _():
        for extra in range(1, GATHER_SLOTS):
            other = (i + extra) % GATHER_SLOTS
            pltpu.make_async_copy(slot_ref(other), slot_ref(other), sem.at[other]).wait()


def _combine(dest, x1, info, g, y_pad):
    tm = COMBINE_TM
    grid_spec = pltpu.PrefetchScalarGridSpec(
        num_scalar_prefetch=1,
        grid=(TOKENS // tm,),
        in_specs=[
            pl.BlockSpec((tm, D_MODEL), lambda i, d: (i, 0)),
            pl.BlockSpec((tm, ROUTER_COLS), lambda i, d: (i, 0)),
            pl.BlockSpec((1, D_MODEL), lambda i, d: (0, 0)),
            pl.BlockSpec(memory_space=pl.ANY),
        ],
        out_specs=pl.BlockSpec((tm, D_MODEL), lambda i, d: (i, 0)),
        scratch_shapes=[pltpu.VMEM((GATHER_SLOTS * MOE_TOP_K * tm * PACK_SUB, LANES), F32),
                        pltpu.SemaphoreType.DMA((GATHER_SLOTS,))],
    )
    return pl.pallas_call(
        _combine_kernel,
        grid_spec=grid_spec,
        out_shape=jax.ShapeDtypeStruct((TOKENS, D_MODEL), F32),
        compiler_params=_cparams(("arbitrary",)),
        name="combine",
    )(dest, x1, info, g, y_pad)


def kernel(x, mix_norm_g, w_in, w_fourier_out, ssm_A_re, ssm_A_im, ssm_log_dt, ssm_B_re, ssm_B_im, ssm_C_re,
           ssm_C_im, ssm_D, ssm_w_glu, w_ssm_out, w_out, ffn_norm_g, router_group_w, router_group_b,
           router_expert_w, router_expert_b, expert_w_gate, expert_w_up, expert_w_down, final_norm_g):
    assert x.shape == (BATCH, SEQ, D_MODEL) and w_in.shape[0] == 1
    tw, f2, cdft = _dft_constants()

    vf, us, xg, gates = _inproj(x, mix_norm_g[0][None], w_in[0].astype(BF16), cdft)
    fmix = _dft(vf, tw, f2)

    mg, wsgt, wog, aq = _ssm_operators(ssm_A_re[0], ssm_A_im[0], ssm_log_dt[0], ssm_B_re[0], ssm_B_im[0],
                                       ssm_C_re[0], ssm_C_im[0])
    flip = jnp.asarray(np.eye(N_CHUNKS)[::-1], BF16)
    sre, sim = _ssm_states(xg, wsgt, flip)
    yconv = _ssm_out(xg, mg, wog, flip, *_ssm_scan(sre, sim, aq))

    w_router = jnp.concatenate([router_group_w[0], router_expert_w[0]], axis=1)
    w_router = jnp.pad(w_router, ((0, 0), (0, ROUTER_COLS - w_router.shape[1])))
    b_router = jnp.concatenate([router_group_b[0], router_expert_b[0]])
    b_router = jnp.pad(b_router, (0, ROUTER_COLS - b_router.shape[0]))[None]
    wr_hi = w_router.astype(BF16)
    wr_lo = (w_router - wr_hi.astype(F32)).astype(BF16)
    x1, hp, info, fields, counts = _merge(
        x.reshape(TOKENS, D_MODEL), fmix, yconv, us, gates, ssm_D[0][None], w_fourier_out[0].astype(BF16),
        ssm_w_glu[0].astype(BF16), w_ssm_out[0].astype(BF16), w_out[0].astype(BF16), ffn_norm_g[0][None],
        wr_hi, wr_lo, b_router)

    dest, block_ord, block_first, by_ordinal, meta, pends = _dispatch_plan(fields, counts)
    y_pad = _moe(dest, block_ord, block_first, by_ordinal, meta, pends, hp, expert_w_gate[0], expert_w_up[0],
                 expert_w_down[0])
    out = _combine(dest, x1, info, final_norm_g[None], y_pad)
    return out.reshape(BATCH, SEQ, D_MODEL)
```
